```python
import math
import jax, jax.numpy as jnp
from jax import lax
import numpy as np

D_MODEL = 1024
BATCH = 2
SEQ = 16384
DEPTH = 2

CHUNK = 128
A_GROUPS = 4
A_GROUP_DIM = 128
A_WIDTH = A_GROUPS * A_GROUP_DIM
B_HEADS = 4
B_HEAD_DIM = 128
B_WIDTH = B_HEADS * B_HEAD_DIM
B_ROT = B_HEAD_DIM // 4
IDX_HEADS = 4
IDX_DIM = 64
IDX_ROT = IDX_DIM // 4
TOPK_MAX = 256
Q_BLOCK = 128
C_WIDTH = 512
CONV_WIDTH = 3
N_BRANCH = 3
D_IN = 2 * A_WIDTH + 3 * B_WIDTH + IDX_HEADS * IDX_DIM + IDX_DIM + IDX_HEADS + 3 * C_WIDTH + N_BRANCH * D_MODEL
N_GROUPS = 4
EXP_PER_GROUP = 8
N_EXPERTS = N_GROUPS * EXP_PER_GROUP
TOP_K_EXP = 2
D_FF_EXPERT = 512
MOE_BLOCK = 128

ROPE_THETA = 500000.0
EPS = 1e-6
NEG = -1e30

kernel_name = "hybrid_gmlp_dsa_shortconv_hiermoe"


def rms_norm(x, g):
    xf = x.astype(jnp.float32)
    y = xf * lax.rsqrt(jnp.mean(xf * xf, axis=-1, keepdims=True) + EPS)
    return (y * g.astype(jnp.float32)).astype(x.dtype)


def layer_norm_noaffine(x):
    xf = x.astype(jnp.float32)
    mu = jnp.mean(xf, axis=-1, keepdims=True)
    var = jnp.mean(jnp.square(xf - mu), axis=-1, keepdims=True)
    return ((xf - mu) * lax.rsqrt(var + EPS)).astype(x.dtype)


def partial_rope(x, positions, rot_dim):
    half = rot_dim // 2
    inv_freq = jnp.float32(ROPE_THETA) ** (-jnp.arange(half, dtype=jnp.float32) * 2.0 / rot_dim)
    ang = positions.astype(jnp.float32)[..., None] * inv_freq
    cos = jnp.cos(ang)[:, :, None, :].astype(x.dtype)
    sin = jnp.sin(ang)[:, :, None, :].astype(x.dtype)
    x1 = x[..., :half]
    x2 = x[..., half:rot_dim]
    rest = x[..., rot_dim:]
    return jnp.concatenate([x1 * cos - x2 * sin, x1 * sin + x2 * cos, rest], axis=-1)


def gmlp_spatial(u, v, ws, bs):
    bsz, s, _ = u.shape
    n = s // CHUNK
    u = jax.nn.gelu(u)
    v = layer_norm_noaffine(jax.nn.gelu(v))
    vc = v.reshape(bsz, n, CHUNK, A_GROUPS, A_GROUP_DIM)
    causal = jnp.tril(jnp.ones((CHUNK, CHUNK), dtype=bool))
    wsm = jnp.where(causal[None], ws, jnp.zeros_like(ws))
    vm = jnp.einsum('gts,bnsgc->bntgc', wsm, vc) + bs.T[None, None, :, :, None]
    return u * vm.reshape(bsz, s, A_WIDTH)


def dsa_attention(q, k, v, qi, ki, wi, positions):
    bsz, s, _ = q.shape
    nblk = s // Q_BLOCK
    n_sel = min(TOPK_MAX, s // 4)
    q = partial_rope(q.reshape(bsz, s, B_HEADS, B_HEAD_DIM), positions, B_ROT)
    k = partial_rope(k.reshape(bsz, s, B_HEADS, B_HEAD_DIM), positions, B_ROT)
    v = v.reshape(bsz, s, B_HEADS, B_HEAD_DIM)
    qi = partial_rope(qi.reshape(bsz, s, IDX_HEADS, IDX_DIM), positions, IDX_ROT)
    ki = partial_rope(ki.reshape(bsz, s, 1, IDX_DIM), positions, IDX_ROT)[:, :, 0]
    wi = wi * (IDX_HEADS ** -0.5)
    key_pos = jnp.arange(s, dtype=jnp.int32)

    def to_blocks(a):
        return a.reshape(bsz, nblk, Q_BLOCK, *a.shape[2:]).swapaxes(0, 1)

    def block(args):
        qb, qib, wib, blk = args
        t = blk * Q_BLOCK + jnp.arange(Q_BLOCK, dtype=jnp.int32)
        rel = jax.nn.relu(jnp.einsum('bthd,bsd->bths', qib, ki).astype(jnp.float32) * (IDX_DIM ** -0.5))
        score = jnp.einsum('bths,bth->bts', rel, wib.astype(jnp.float32))
        causal = key_pos[None, :] <= t[:, None]
        score = jnp.where(causal[None], score, NEG)
        _, sel = lax.top_k(score, n_sel)
        valid = sel <= t[None, :, None]
        kg = jax.vmap(lambda a, i: a[i])(k, sel)
        vg = jax.vmap(lambda a, i: a[i])(v, sel)
        logit = jnp.einsum('bthd,btkhd->bhtk', qb, kg).astype(jnp.float32) * (B_HEAD_DIM ** -0.5)
        logit = jnp.where(valid[:, None], logit, NEG)
        p = jax.nn.softmax(logit, axis=-1).astype(vg.dtype)
        return jnp.einsum('bhtk,btkhd->bthd', p, vg)

    out = lax.map(block, (to_blocks(q), to_blocks(qi), to_blocks(wi), jnp.arange(nblk, dtype=jnp.int32)))
    return out.swapaxes(0, 1).reshape(bsz, s, B_WIDTH)


def short_gated_conv(cb, cc, cx, conv_w):
    z = cc * cx
    s = z.shape[1]
    zp = jnp.pad(z, ((0, 0), (CONV_WIDTH - 1, 0), (0, 0)))
    y = conv_w[0] * zp[:, 0:s]
    for i in range(1, CONV_WIDTH):
        y = y + conv_w[i] * zp[:, i:i + s]
    return cb * y


def token_mixers(h, positions, w_in, ws, bs, conv_w, w_proj_a, w_proj_b, w_proj_c, w_out):
    bsz, s, _ = h.shape
    proj = h @ w_in
    sizes = [A_WIDTH, A_WIDTH, B_WIDTH, B_WIDTH, B_WIDTH, IDX_HEADS * IDX_DIM, IDX_DIM, IDX_HEADS,
             C_WIDTH, C_WIDTH, C_WIDTH, N_BRANCH * D_MODEL]
    cuts = [int(c) for c in np.cumsum(sizes)[:-1]]
    u, va, q, k, vb, qi, ki, wi, cb, cc, cx, gates = jnp.split(proj, cuts, axis=-1)
    ya = gmlp_spatial(u, va, ws, bs)
    yb = dsa_attention(q, k, vb, qi, ki, wi, positions)
    yc = short_gated_conv(cb, cc, cx, conv_w)
    g = jax.nn.sigmoid(gates.astype(jnp.float32)).astype(h.dtype).reshape(bsz, s, N_BRANCH, D_MODEL)
    merged = g[:, :, 0] * (ya @ w_proj_a) + g[:, :, 1] * (yb @ w_proj_b) + g[:, :, 2] * (yc @ w_proj_c)
    return merged @ w_out


def hier_moe(h, rg_w, rg_b, re_w, re_b, w_gate, w_up, w_down):
    bsz, s, d = h.shape
    t_tok = bsz * s
    xt = h.reshape(t_tok, d)
    glog = (xt @ rg_w + rg_b).astype(jnp.float32)
    gprob = jax.nn.softmax(glog, axis=-1)
    grp = jnp.argmax(glog, axis=-1).astype(jnp.int32)
    gw = jnp.take_along_axis(gprob, grp[:, None], axis=1)[:, 0]
    elog = (xt @ re_w + re_b).astype(jnp.float32).reshape(t_tok, N_GROUPS, EXP_PER_GROUP)
    el = jnp.take_along_axis(elog, grp[:, None, None], axis=1)[:, 0]
    tv, ti = lax.top_k(el, TOP_K_EXP)
    tw = jax.nn.softmax(tv, axis=-1) * gw[:, None]
    n_assign = t_tok * TOP_K_EXP
    eid = (grp[:, None] * EXP_PER_GROUP + ti).reshape(n_assign)
    wts = tw.reshape(n_assign)
    tok = jnp.repeat(jnp.arange(t_tok, dtype=jnp.int32), TOP_K_EXP)
    order = jnp.argsort(eid)
    eid_s, wts_s, tok_s = eid[order], wts[order], tok[order]
    counts = jnp.bincount(eid, length=N_EXPERTS)
    start = jnp.cumsum(counts) - counts
    padded = ((counts + MOE_BLOCK - 1) // MOE_BLOCK) * MOE_BLOCK
    pend = jnp.cumsum(padded)
    pstart = pend - padded
    dest = pstart[eid_s] + (jnp.arange(n_assign, dtype=jnp.int32) - start[eid_s])
    n_blocks = -(-n_assign // MOE_BLOCK) + N_EXPERTS
    n_rows = n_blocks * MOE_BLOCK
    buf_tok = jnp.zeros((n_rows,), jnp.int32).at[dest].set(tok_s)
    buf_w = jnp.zeros((n_rows,), jnp.float32).at[dest].set(wts_s)
    block_exp = jnp.minimum(jnp.searchsorted(pend, jnp.arange(n_blocks, dtype=jnp.int32) * MOE_BLOCK, side='right'),
                            N_EXPERTS - 1).astype(jnp.int32)

    def expert_block(args):
        tok_b, e = args
        xb = xt[tok_b]
        hid = jax.nn.silu(xb @ w_gate[e]) * (xb @ w_up[e])
        return hid @ w_down[e]

    yb = lax.map(expert_block, (buf_tok.reshape(n_blocks, MOE_BLOCK), block_exp))
    yb = yb.reshape(n_rows, d) * buf_w[:, None].astype(xt.dtype)
    out = jnp.zeros((t_tok, d), xt.dtype).at[buf_tok].add(yb)
    return out.reshape(bsz, s, d)


def setup_inputs(seed: int = 0) -> dict:
    key = jax.random.key(seed)
    ks = jax.random.split(key, 20)
    f32 = jnp.float32
    nrm = lambda k, shape, scale: jax.random.normal(k, shape, f32) * scale
    return {
        "x": nrm(ks[0], (BATCH, SEQ, D_MODEL), 1.0),
        "positions": jnp.broadcast_to(jnp.arange(SEQ, dtype=jnp.int32), (BATCH, SEQ)),
        "norm_mix_g": 1.0 + nrm(ks[1], (DEPTH, D_MODEL), 0.02),
        "norm_ffn_g": 1.0 + nrm(ks[2], (DEPTH, D_MODEL), 0.02),
        "norm_final_g": 1.0 + nrm(ks[3], (D_MODEL,), 0.02),
        "w_in": nrm(ks[4], (DEPTH, D_MODEL, D_IN), D_MODEL ** -0.5),
        "gmlp_ws": nrm(ks[5], (DEPTH, A_GROUPS, CHUNK, CHUNK), CHUNK ** -0.5),
        "gmlp_b": 1.0 + nrm(ks[6], (DEPTH, A_GROUPS, CHUNK), 0.02),
        "conv_w": nrm(ks[7], (DEPTH, CONV_WIDTH, C_WIDTH), CONV_WIDTH ** -0.5),
        "w_proj_a": nrm(ks[8], (DEPTH, A_WIDTH, D_MODEL), A_WIDTH ** -0.5),
        "w_proj_b": nrm(ks[9], (DEPTH, B_WIDTH, D_MODEL), B_WIDTH ** -0.5),
        "w_proj_c": nrm(ks[10], (DEPTH, C_WIDTH, D_MODEL), C_WIDTH ** -0.5),
        "w_out": nrm(ks[11], (DEPTH, D_MODEL, D_MODEL), D_MODEL ** -0.5),
        "router_group_w": nrm(ks[12], (DEPTH, D_MODEL, N_GROUPS), D_MODEL ** -0.5),
        "router_group_b": nrm(ks[13], (DEPTH, N_GROUPS), 0.01),
        "router_expert_w": nrm(ks[14], (DEPTH, D_MODEL, N_EXPERTS), D_MODEL ** -0.5),
        "router_expert_b": nrm(ks[15], (DEPTH, N_EXPERTS), 0.01),
        "expert_w_gate": nrm(ks[16], (DEPTH, N_EXPERTS, D_MODEL, D_FF_EXPERT), D_MODEL ** -0.5),
        "expert_w_up": nrm(ks[17], (DEPTH, N_EXPERTS, D_MODEL, D_FF_EXPERT), D_MODEL ** -0.5),
        "expert_w_down": nrm(ks[18], (DEPTH, N_EXPERTS, D_FF_EXPERT, D_MODEL), D_FF_EXPERT ** -0.5),
    }


def reference(x, positions, norm_mix_g, norm_ffn_g, norm_final_g, w_in, gmlp_ws, gmlp_b, conv_w,
              w_proj_a, w_proj_b, w_proj_c, w_out, router_group_w, router_group_b,
              router_expert_w, router_expert_b, expert_w_gate, expert_w_up, expert_w_down):
    for l in range(DEPTH):
        h = rms_norm(x, norm_mix_g[l])
        x = x + token_mixers(h, positions, w_in[l], gmlp_ws[l], gmlp_b[l], conv_w[l],
                             w_proj_a[l], w_proj_b[l], w_proj_c[l], w_out[l])
        h = rms_norm(x, norm_ffn_g[l])
        x = x + hier_moe(h, router_group_w[l], router_group_b[l], router_expert_w[l], router_expert_b[l],
                         expert_w_gate[l], expert_w_up[l], expert_w_down[l])
    return rms_norm(x, norm_final_g)
```

```python
import functools

import numpy as np
import jax
import jax.numpy as jnp
from jax import lax
from jax.experimental import pallas as pl
from jax.experimental.pallas import tpu as pltpu

D_MODEL = 1024
CHUNK = 128
A_GROUPS = 4
A_GROUP_DIM = 128
A_WIDTH = A_GROUPS * A_GROUP_DIM
B_HEADS = 4
B_HEAD_DIM = 128
B_WIDTH = B_HEADS * B_HEAD_DIM
B_ROT = B_HEAD_DIM // 4
IDX_HEADS = 4
IDX_DIM = 64
IDX_ROT = IDX_DIM // 4
TOPK_MAX = 256
C_WIDTH = 512
CONV_WIDTH = 3
N_BRANCH = 3
N_GROUPS = 4
EXP_PER_GROUP = 8
N_EXPERTS = N_GROUPS * EXP_PER_GROUP
D_FF_EXPERT = 512
ROPE_THETA = 500000.0
EPS = 1e-6
NEG = -1e30

LANES = 128
SUBLANES = 8
VMEM_LIMIT = 56 * 1024 * 1024

TM_PROJ = 512
TM_MERGE = 256
TM_ROUTE = 512
TM_RANK = 256
TM_MOVE = 256
EXP_BLOCK = 256
DSA_TQ = 128
DSA_KC = 256
DSA_RB = 256

F32 = jnp.float32
BF16 = jnp.bfloat16
I32 = jnp.int32
INT_MIN = -2 ** 31


def _monotone_key_of(value):
    bits = int(np.array(value, np.float32).view(np.int32))
    return bits ^ ((bits >> 31) & 0x7FFFFFFF)


NEG_KEY = _monotone_key_of(NEG)


def _params(n_axes=1, semantics=None):
    return pltpu.CompilerParams(
        dimension_semantics=semantics or ("arbitrary",) * n_axes,
        vmem_limit_bytes=VMEM_LIMIT)


def _rms(x, g):
    ms = jnp.mean(x * x, axis=-1, keepdims=True)
    return x * lax.rsqrt(ms + EPS) * g


def _full(shape):
    nd = len(shape)
    return pl.BlockSpec(shape, lambda *_: (0,) * nd)


def _rope_table_kernel(pos_ref, invf_ref, mrot_ref, ma_ref, mb_ref, c_ref, sa_ref, sb_ref):
    ang = pos_ref[...] * invf_ref[...]
    c = jnp.cos(ang)
    s = jnp.sin(ang)
    c_ref[...] = jnp.where(mrot_ref[...] > 0, c, 1.0)
    sa_ref[...] = jnp.where(ma_ref[...] > 0, -s, 0.0)
    sb_ref[...] = jnp.where(mb_ref[...] > 0, s, 0.0)


def _rope_tables(pos_col, head_dim, rot):
    t = pos_col.shape[0]
    half = rot // 2
    inv_freq = jnp.float32(ROPE_THETA) ** (-jnp.arange(half, dtype=F32) * 2.0 / rot)
    lane = np.arange(LANES) % head_dim
    mrot = lane < rot
    ma = lane < half
    mb = (lane >= half) & (lane < rot)
    invf = jnp.where(jnp.asarray(mrot), inv_freq[np.where(mrot, lane % half, 0)], 0.0)[None, :]
    row = lambda m: jnp.asarray(m.astype(np.float32))[None, :]
    tm = min(1024, t)
    spec_row = _full((1, LANES))
    out_spec = pl.BlockSpec((tm, LANES), lambda i: (i, 0))
    return pl.pallas_call(
        _rope_table_kernel,
        grid=(t // tm,),
        in_specs=[pl.BlockSpec((tm, 1), lambda i: (i, 0)), spec_row, spec_row, spec_row, spec_row],
        out_specs=[out_spec] * 3,
        out_shape=[jax.ShapeDtypeStruct((t, LANES), F32)] * 3,
        compiler_params=_params(),
        name="rope_tables",
    )(pos_col, invf, row(mrot), row(ma), row(mb))


def _rope(x, c, sa, sb, half):
    parts = []
    for j in range(x.shape[1] // LANES):
        xj = x[:, j * LANES:(j + 1) * LANES]
        parts.append(xj * c + pltpu.roll(xj, LANES - half, 1) * sa + pltpu.roll(xj, half, 1) * sb)
    return parts


def _proj_b_kernel(x_ref, g_ref, w_ref, cb_ref, sab_ref, sbb_ref, ci_ref, sai_ref, sbi_ref,
                   q_ref, k_ref, v_ref, qi_ref, ki_ref, wi_ref):
    h = _rms(x_ref[...], g_ref[...]).astype(BF16)
    p = jnp.dot(h, w_ref[...], preferred_element_type=F32)
    tb = (cb_ref[...], sab_ref[...], sbb_ref[...])
    ti = (ci_ref[...], sai_ref[...], sbi_ref[...])
    o = 0
    qs = _rope(p[:, o:o + B_WIDTH], *tb, B_ROT // 2)
    for j, qj in enumerate(qs):
        q_ref[:, j * LANES:(j + 1) * LANES] = (qj * (B_HEAD_DIM ** -0.5)).astype(BF16)
    o += B_WIDTH
    for j, kj in enumerate(_rope(p[:, o:o + B_WIDTH], *tb, B_ROT // 2)):
        k_ref[:, j * LANES:(j + 1) * LANES] = kj.astype(BF16)
    o += B_WIDTH
    v_ref[...] = p[:, o:o + B_WIDTH].astype(BF16)
    o += B_WIDTH
    for j, qj in enumerate(_rope(p[:, o:o + IDX_HEADS * IDX_DIM], *ti, IDX_ROT // 2)):
        qi_ref[:, j * LANES:(j + 1) * LANES] = qj.astype(BF16)
    o += IDX_HEADS * IDX_DIM
    ki_ref[...] = _rope(p[:, o:o + LANES], *ti, IDX_ROT // 2)[0].astype(BF16)
    o += LANES
    wi_ref[...] = p[:, o:o + LANES] * ((IDX_HEADS ** -0.5) * (IDX_DIM ** -0.5))


def _proj_b(x2, g, w, tabs_b, tabs_i):
    t = x2.shape[0]
    tm = min(TM_PROJ, t)
    tok = lambda n: pl.BlockSpec((tm, n), lambda i: (i, 0))
    n_out = w.shape[1]
    outs = [(B_WIDTH, BF16), (B_WIDTH, BF16), (B_WIDTH, BF16), (IDX_HEADS * IDX_DIM, BF16),
            (LANES, BF16), (LANES, F32)]
    return pl.pallas_call(
        _proj_b_kernel,
        grid=(t // tm,),
        in_specs=[tok(D_MODEL), _full((1, D_MODEL)), _full((D_MODEL, n_out))] + [tok(LANES)] * 6,
        out_specs=[tok(n) for n, _ in outs],
        out_shape=[jax.ShapeDtypeStruct((t, n), d) for n, d in outs],
        compiler_params=_params(),
        name="proj_b",
    )(x2, g, w, *tabs_b, *tabs_i)


def _dsa_kernel(qit_ref, wit_ref, ki_ref, qt_ref, k_ref, vt_ref, o_ref, keys_ref, acc_ref,
                *, seq, n_sel, tq, kc, rb):
    i = pl.program_id(1)
    q0 = i * tq
    n_chunk = (q0 + tq + kc - 1) // kc
    n_rows = n_chunk * kc
    n_unproc = seq - n_rows
    qpos = q0 + lax.broadcasted_iota(I32, (kc, tq), 1)
    krow = lax.broadcasted_iota(I32, (kc, tq), 0)
    wit = wit_ref[...]

    def score_body(c, _):
        r0 = pl.multiple_of(c * kc, kc)
        kic = ki_ref[pl.ds(r0, kc), :]
        sc = jnp.zeros((kc, tq), F32)
        for h in range(IDX_HEADS):
            a = jnp.dot(kic, qit_ref[h * IDX_DIM:(h + 1) * IDX_DIM, :], preferred_element_type=F32)
            sc = sc + jnp.maximum(a, 0.0) * wit[h:h + 1, :]
        sc = jnp.where(krow + r0 <= qpos, sc, NEG)
        bits = lax.bitcast_convert_type(sc, I32)
        key = bits ^ ((bits >> 31) & 0x7FFFFFFF)
        key = jnp.where(key == -1, 0, key)
        keys_ref[pl.ds(r0, kc), :] = key
        return 0

    lax.fori_loop(0, n_chunk, score_body, 0)

    acc_rows = 4 * SUBLANES

    def count(cand, strict):
        def body(r, acc):
            r0 = pl.multiple_of(r * rb, rb)
            blk = keys_ref[pl.ds(r0, rb), :]
            for j in range(rb // acc_rows):
                part = blk[j * acc_rows:(j + 1) * acc_rows, :]
                hit = (part > cand) if strict else (part >= cand)
                acc = acc + jnp.where(hit, 1, 0)
            return acc

        acc = lax.fori_loop(0, n_rows // rb, body, jnp.zeros((acc_rows, tq), I32))
        cnt = jnp.sum(acc.astype(F32), axis=0, keepdims=True).astype(I32)
        pad_hit = (NEG_KEY > cand) if strict else (NEG_KEY >= cand)
        return cnt + jnp.where(pad_hit, n_unproc, 0)

    def bit_body(b, carry):
        ans_u, = carry
        bit = jnp.left_shift(jnp.int32(1), 31 - b)
        cand_u = ans_u | bit
        ok = count(cand_u ^ INT_MIN, False) >= n_sel
        return (jnp.where(ok, cand_u, ans_u),)

    ans_u, = lax.fori_loop(0, 32, bit_body, (jnp.zeros((1, tq), I32),))
    thr = ans_u ^ INT_MIN
    need = n_sel - count(thr, True)

    acc_ref[...] = jnp.zeros_like(acc_ref)
    lower = (lax.broadcasted_iota(I32, (kc, kc), 1) < lax.broadcasted_iota(I32, (kc, kc), 0))
    lower = jnp.where(lower, 1.0, 0.0).astype(BF16)

    def attn_body(c, carry):
        ms, ls, tie_base = carry
        r0 = pl.multiple_of(c * kc, kc)
        key = keys_ref[pl.ds(r0, kc), :]
        causal = krow + r0 <= qpos
        tie = (key == thr) & causal
        tie_f = jnp.where(tie, 1.0, 0.0)
        before = jnp.dot(lower, tie_f.astype(BF16), preferred_element_type=F32) + tie_base
        sel = causal & ((key > thr) | (tie & (before < need.astype(F32))))
        tie_base = tie_base + jnp.sum(tie_f, axis=0, keepdims=True)
        new_ms, new_ls = [], []
        for h in range(B_HEADS):
            hs = slice(h * B_HEAD_DIM, (h + 1) * B_HEAD_DIM)
            lg = jnp.dot(k_ref[pl.ds(r0, kc), hs], qt_ref[hs, :], preferred_element_type=F32)
            lg = jnp.where(sel, lg, NEG)
            m_new = jnp.maximum(ms[h], jnp.max(lg, axis=0, keepdims=True))
            alpha = jnp.exp(ms[h] - m_new)
            p = jnp.exp(lg - m_new)
            new_ls.append(alpha * ls[h] + jnp.sum(p, axis=0, keepdims=True))
            pv = jnp.dot(vt_ref[hs, pl.ds(r0, kc)], p.astype(BF16), preferred_element_type=F32)
            acc_ref[hs, :] = acc_ref[hs, :] * alpha + pv
            new_ms.append(m_new)
        return tuple(new_ms), tuple(new_ls), tie_base

    row = lambda v: jnp.full((1, tq), v, F32)
    init = (tuple(row(NEG) for _ in range(B_HEADS)), tuple(row(0.0) for _ in range(B_HEADS)), row(0.0))
    _, ls, _ = lax.fori_loop(0, n_chunk, attn_body, init)
    for h in range(B_HEADS):
        hs = slice(h * B_HEAD_DIM, (h + 1) * B_HEAD_DIM)
        o_ref[hs, :] = (acc_ref[hs, :] / ls[h]).astype(o_ref.dtype)


def _dsa(qit, wit, ki, qt, k, vt, n_sel):
    bsz, _, seq = qt.shape
    tq = min(DSA_TQ, seq)
    kc = min(DSA_KC, seq)
    rb = min(DSA_RB, kc)
    once = pl.Buffered(1)
    kern = functools.partial(_dsa_kernel, seq=seq, n_sel=n_sel, tq=tq, kc=kc, rb=rb)
    return pl.pallas_call(
        kern,
        grid=(bsz, seq // tq),
        in_specs=[
            pl.BlockSpec((None, IDX_HEADS * IDX_DIM, tq), lambda b, i: (b, 0, i)),
            pl.BlockSpec((None, IDX_HEADS, tq), lambda b, i: (b, 0, i)),
            pl.BlockSpec((None, seq, IDX_DIM), lambda b, i: (b, 0, 0), pipeline_mode=once),
            pl.BlockSpec((None, B_WIDTH, tq), lambda b, i: (b, 0, i)),
            pl.BlockSpec((None, seq, B_WIDTH), lambda b, i: (b, 0, 0), pipeline_mode=once),
            pl.BlockSpec((None, B_WIDTH, seq), lambda b, i: (b, 0, 0), pipeline_mode=once),
        ],
        out_specs=pl.BlockSpec((None, B_WIDTH, tq), lambda b, i: (b, 0, i)),
        out_shape=jax.ShapeDtypeStruct((bsz, B_WIDTH, seq), BF16),
        scratch_shapes=[pltpu.VMEM((seq, tq), I32), pltpu.VMEM((B_WIDTH, tq), F32)],
        compiler_params=_params(2),
        name="dsa",
    )(qit, wit, ki, qt, k, vt)


def _proj_a_kernel(x_ref, g_ref, w_ref, ws_ref, bs_ref, ya_ref, *, tm):
    h = _rms(x_ref[...], g_ref[...]).astype(BF16)
    p = jnp.dot(h, w_ref[...], preferred_element_type=F32)
    u = jax.nn.gelu(p[:, :A_WIDTH])
    v = jax.nn.gelu(p[:, A_WIDTH:])
    mu = jnp.mean(v, axis=-1, keepdims=True)
    var = jnp.mean(jnp.square(v - mu), axis=-1, keepdims=True)
    v = ((v - mu) * lax.rsqrt(var + EPS)).astype(BF16)
    causal = (lax.broadcasted_iota(I32, (CHUNK, CHUNK), 1) <= lax.broadcasted_iota(I32, (CHUNK, CHUNK), 0))
    bs = bs_ref[...]
    for gidx in range(A_GROUPS):
        wsm = jnp.where(causal, ws_ref[gidx], 0.0).astype(BF16)
        cs = slice(gidx * A_GROUP_DIM, (gidx + 1) * A_GROUP_DIM)
        for n in range(tm // CHUNK):
            rs = slice(n * CHUNK, (n + 1) * CHUNK)
            vm = jnp.dot(wsm, v[rs, cs], preferred_element_type=F32) + bs[:, gidx:gidx + 1]
            ya_ref[rs, cs] = (u[rs, cs] * vm).astype(ya_ref.dtype)


def _proj_a(x2, g, w, ws, bs_t):
    t = x2.shape[0]
    tm = min(TM_PROJ, t)
    return pl.pallas_call(
        functools.partial(_proj_a_kernel, tm=tm),
        grid=(t // tm,),
        in_specs=[pl.BlockSpec((tm, D_MODEL), lambda i: (i, 0)), _full((1, D_MODEL)),
                  _full((D_MODEL, 2 * A_WIDTH)), _full((A_GROUPS, CHUNK, CHUNK)), _full((CHUNK, A_GROUPS))],
        out_specs=pl.BlockSpec((tm, A_WIDTH), lambda i: (i, 0)),
        out_shape=jax.ShapeDtypeStruct((t, A_WIDTH), BF16),
        compiler_params=_params(),
        name="proj_a",
    )(x2, g, w, ws, bs_t)


def _proj_c_kernel(x_ref, xh_ref, g_ref, w_ref, cw_ref, yc_ref, *, tm, tiles_per_seq):
    g = g_ref[...]
    w = w_ref[...]
    p = jnp.dot(_rms(x_ref[...], g).astype(BF16), w, preferred_element_type=F32)
    ph = jnp.dot(_rms(xh_ref[...], g).astype(BF16), w[:, C_WIDTH:], preferred_element_type=F32)
    first = (pl.program_id(0) % tiles_per_seq) == 0
    zh = ph[:, :C_WIDTH] * ph[:, C_WIDTH:] * jnp.where(first, 0.0, 1.0)
    z = p[:, C_WIDTH:2 * C_WIDTH] * p[:, 2 * C_WIDTH:]
    row = lax.broadcasted_iota(I32, (tm, C_WIDTH), 0)
    z1 = jnp.where(row == 0, zh[SUBLANES - 1:SUBLANES, :], pltpu.roll(z, 1, 0))
    z2 = jnp.where(row == 0, zh[SUBLANES - 2:SUBLANES - 1, :],
                   jnp.where(row == 1, zh[SUBLANES - 1:SUBLANES, :], pltpu.roll(z, 2, 0)))
    cw = cw_ref[...]
    y = cw[0:1, :] * z2 + cw[1:2, :] * z1 + cw[2:3, :] * z
    yc_ref[...] = (p[:, :C_WIDTH] * y).astype(yc_ref.dtype)


def _proj_c(x2, g, w, cw, seq):
    t = x2.shape[0]
    tm = min(TM_PROJ, seq)
    per8 = tm // SUBLANES
    return pl.pallas_call(
        functools.partial(_proj_c_kernel, tm=tm, tiles_per_seq=seq // tm),
        grid=(t // tm,),
        in_specs=[pl.BlockSpec((tm, D_MODEL), lambda i: (i, 0)),
                  pl.BlockSpec((SUBLANES, D_MODEL), lambda i: (jnp.maximum(i * per8 - 1, 0), 0)),
                  _full((1, D_MODEL)), _full((D_MODEL, 3 * C_WIDTH)), _full((CONV_WIDTH, C_WIDTH))],
        out_specs=pl.BlockSpec((tm, C_WIDTH), lambda i: (i, 0)),
        out_shape=jax.ShapeDtypeStruct((t, C_WIDTH), BF16),
        compiler_params=_params(),
        name="proj_c",
    )(x2, x2, g, w, cw)


def _merge_kernel(x_ref, g_ref, wg_ref, ya_ref, yb_ref, yc_ref, wa_ref, wb_ref, wc_ref, wo_ref, o_ref):
    x = x_ref[...]
    h = _rms(x, g_ref[...]).astype(BF16)
    merged = None
    for j, (y_ref, w_ref) in enumerate(((ya_ref, wa_ref), (yb_ref, wb_ref), (yc_ref, wc_ref))):
        gate = jnp.dot(h, wg_ref[:, j * D_MODEL:(j + 1) * D_MODEL], preferred_element_type=F32)
        term = jax.nn.sigmoid(gate) * jnp.dot(y_ref[...], w_ref[...], preferred_element_type=F32)
        merged = term if merged is None else merged + term
    o_ref[...] = x + jnp.dot(merged.astype(BF16), wo_ref[...], preferred_element_type=F32)


def _merge(x2, g, wg, ya, yb, yc, wa, wb, wc, wo):
    t = x2.shape[0]
    tm = min(TM_MERGE, t)
    tok = lambda n: pl.BlockSpec((tm, n), lambda i: (i, 0))
    wproj = _full((A_WIDTH, D_MODEL))
    return pl.pallas_call(
        _merge_kernel,
        grid=(t // tm,),
        in_specs=[tok(D_MODEL), _full((1, D_MODEL)), _full((D_MODEL, N_BRANCH * D_MODEL)),
                  tok(A_WIDTH), tok(B_WIDTH), tok(C_WIDTH), wproj, wproj, wproj, _full((D_MODEL, D_MODEL))],
        out_specs=tok(D_MODEL),
        out_shape=jax.ShapeDtypeStruct((t, D_MODEL), F32),
        compiler_params=_params(),
        name="merge",
    )(x2, g, wg, ya, yb, yc, wa, wb, wc, wo)


def _router_kernel(x_ref, g_ref, w_ref, b_ref, h_ref, eid_ref, wt_ref):
    h = _rms(x_ref[...], g_ref[...])
    h_ref[...] = h
    logits = jnp.dot(h, w_ref[...], preferred_element_type=F32, precision=lax.Precision.HIGHEST) + b_ref[...]
    lane = lax.broadcasted_iota(I32, logits.shape, 1)
    lane_f = lane.astype(F32)
    ninf = -jnp.inf

    def first_max(vals):
        vmax = jnp.max(vals, axis=1, keepdims=True)
        idx = jnp.min(jnp.where(vals == vmax, lane_f, float(LANES)), axis=1, keepdims=True)
        return vmax, idx

    gl = jnp.where(lane < N_GROUPS, logits, ninf)
    gmax, grp = first_max(gl)
    gsum = jnp.sum(jnp.where(lane < N_GROUPS, jnp.exp(logits - gmax), 0.0), axis=1, keepdims=True)
    gw = 1.0 / gsum
    e_lane = lane - N_GROUPS
    in_grp = (e_lane >= 0) & (e_lane < N_EXPERTS) & ((e_lane >> 3).astype(F32) == grp)
    el = jnp.where(in_grp, logits, ninf)
    v1, i1 = first_max(el)
    v2, i2 = first_max(jnp.where(lane_f == i1, ninf, el))
    e2 = jnp.exp(v2 - v1)
    den = 1.0 + e2
    w1 = (1.0 / den) * gw
    w2 = (e2 / den) * gw
    eid1 = (i1 - N_GROUPS).astype(I32)
    eid2 = (i2 - N_GROUPS).astype(I32)
    eid_ref[...] = jnp.where(lane == 0, eid1, jnp.where(lane == 1, eid2, 0))
    wt_ref[...] = jnp.where(lane == 0, w1, jnp.where(lane == 1, w2, 0.0))


def _router(x2, g, w, b):
    t = x2.shape[0]
    tm = min(TM_ROUTE, t)
    tok = lambda n: pl.BlockSpec((tm, n), lambda i: (i, 0))
    return pl.pallas_call(
        _router_kernel,
        grid=(t // tm,),
        in_specs=[tok(D_MODEL), _full((1, D_MODEL)), _full((D_MODEL, LANES)), _full((1, LANES))],
        out_specs=[tok(D_MODEL), tok(LANES), tok(LANES)],
        out_shape=[jax.ShapeDtypeStruct((t, D_MODEL), F32), jax.ShapeDtypeStruct((t, LANES), I32),
                   jax.ShapeDtypeStruct((t, LANES), F32)],
        compiler_params=_params(),
        name="router",
    )(x2, g, w, b)


def _rank_kernel(eid_ref, rank_ref, cnt_ref, *, tm):
    @pl.when(pl.program_id(0) == 0)
    def _():
        cnt_ref[...] = jnp.zeros_like(cnt_ref)

    eid = eid_ref[...]
    lane = lax.broadcasted_iota(I32, (tm, LANES), 1)
    o1 = lane == eid[:, 0:1]
    o2 = lane == eid[:, 1:2]
    both = jnp.where(o1 | o2, 1.0, 0.0)
    lower = lax.broadcasted_iota(I32, (tm, tm), 1) < lax.broadcasted_iota(I32, (tm, tm), 0)
    before = jnp.dot(jnp.where(lower, 1.0, 0.0).astype(BF16), both.astype(BF16),
                     preferred_element_type=F32) + cnt_ref[...]
    r1 = jnp.sum(jnp.where(o1, before, 0.0), axis=1, keepdims=True).astype(I32)
    r2 = jnp.sum(jnp.where(o2, before, 0.0), axis=1, keepdims=True).astype(I32)
    rank_ref[...] = jnp.where(lane == 0, r1, jnp.where(lane == 1, r2, 0))
    cnt_ref[...] = cnt_ref[...] + jnp.sum(both, axis=0, keepdims=True)


def _rank(eid):
    t = eid.shape[0]
    tm = min(TM_RANK, t)
    return pl.pallas_call(
        functools.partial(_rank_kernel, tm=tm),
        grid=(t // tm,),
        in_specs=[pl.BlockSpec((tm, LANES), lambda i: (i, 0))],
        out_specs=[pl.BlockSpec((tm, LANES), lambda i: (i, 0)), _full((1, LANES))],
        out_shape=[jax.ShapeDtypeStruct((t, LANES), I32), jax.ShapeDtypeStruct((1, LANES), F32)],
        compiler_params=_params(),
        name="rank",
    )(eid)


def _dest_kernel(eid_ref, rank_ref, start_ref, dest_ref):
    eid = eid_ref[...]
    lane = lax.broadcasted_iota(I32, eid.shape, 1)
    start = start_ref[...]
    s1 = jnp.sum(jnp.where(lane == eid[:, 0:1], start, 0.0), axis=1, keepdims=True).astype(I32)
    s2 = jnp.sum(jnp.where(lane == eid[:, 1:2], start, 0.0), axis=1, keepdims=True).astype(I32)
    rank = rank_ref[...]
    dest_ref[...] = jnp.where(lane == 0, s1, jnp.where(lane == 1, s2, 0)) + rank


def _dest(eid, rank, start_row):
    t = eid.shape[0]
    tm = min(TM_ROUTE, t)
    tok = pl.BlockSpec((tm, LANES), lambda i: (i, 0))
    return pl.pallas_call(
        _dest_kernel,
        grid=(t // tm,),
        in_specs=[tok, tok, _full((1, LANES))],
        out_specs=tok,
        out_shape=jax.ShapeDtypeStruct((t, LANES), I32),
        compiler_params=_params(),
        name="dest",
    )(eid, rank, start_row)


def _row_copy(src_ref, src_row, dst_ref, dst_row, sem):
    return pltpu.make_async_copy(src_ref.at[pl.ds(src_row, 1)], dst_ref.at[pl.ds(dst_row, 1)], sem)


def _scatter_kernel(dest_ref, h_ref, xs_in_ref, xs_ref, sem, *, tm):
    del xs_in_ref

    def issue(t, _):
        for j in range(2):
            _row_copy(h_ref, t, xs_ref, dest_ref[0, 0, 2 * t + j], sem).start()
        return 0

    lax.fori_loop(0, tm, issue, 0)

    def drain(t, _):
        _row_copy(h_ref, 0, xs_ref, 0, sem).wait()
        return 0

    lax.fori_loop(0, 2 * tm, drain, 0)


def _scatter_rows(dest3, h, n_rows):
    t = h.shape[0]
    tm = dest3.shape[2] // 2
    xs0 = jnp.zeros((n_rows, D_MODEL), F32)
    return pl.pallas_call(
        functools.partial(_scatter_kernel, tm=tm),
        grid=(t // tm,),
        in_specs=[pl.BlockSpec((1, 1, 2 * tm), lambda i: (i, 0, 0), memory_space=pltpu.SMEM),
                  pl.BlockSpec((tm, D_MODEL), lambda i: (i, 0)),
                  pl.BlockSpec(memory_space=pl.ANY)],
        out_specs=pl.BlockSpec(memory_space=pl.ANY),
        out_shape=jax.ShapeDtypeStruct((n_rows, D_MODEL), F32),
        scratch_shapes=[pltpu.SemaphoreType.DMA(())],
        input_output_aliases={2: 0},
        compiler_params=_params(),
        name="scatter_rows",
    )(dest3, h, xs0)


def _expert_kernel(be_ref, nused_ref, xs_ref, wg_ref, wu_ref, wd_ref, y_ref):
    @pl.when(pl.program_id(0) < nused_ref[0])
    def _():
        xb = xs_ref[...].astype(BF16)
        gate = jnp.dot(xb, wg_ref[...], preferred_element_type=F32)
        up = jnp.dot(xb, wu_ref[...], preferred_element_type=F32)
        hid = (jax.nn.silu(gate) * up).astype(BF16)
        y_ref[...] = jnp.dot(hid, wd_ref[...], preferred_element_type=F32)

    @pl.when(pl.program_id(0) >= nused_ref[0])
    def _():
        y_ref[...] = jnp.zeros_like(y_ref)


def _experts(block_exp, n_used, xs, wg, wu, wd):
    n_rows = xs.shape[0]
    grid_spec = pltpu.PrefetchScalarGridSpec(
        num_scalar_prefetch=2,
        grid=(n_rows // EXP_BLOCK,),
        in_specs=[pl.BlockSpec((EXP_BLOCK, D_MODEL), lambda b, be, nu: (b, 0)),
                  pl.BlockSpec((None, D_MODEL, D_FF_EXPERT), lambda b, be, nu: (be[b], 0, 0)),
                  pl.BlockSpec((None, D_MODEL, D_FF_EXPERT), lambda b, be, nu: (be[b], 0, 0)),
                  pl.BlockSpec((None, D_FF_EXPERT, D_MODEL), lambda b, be, nu: (be[b], 0, 0))],
        out_specs=pl.BlockSpec((EXP_BLOCK, D_MODEL), lambda b, be, nu: (b, 0)),
    )
    return pl.pallas_call(
        _expert_kernel,
        grid_spec=grid_spec,
        out_shape=jax.ShapeDtypeStruct((n_rows, D_MODEL), F32),
        compiler_params=_params(),
        name="experts",
    )(block_exp, n_used, xs, wg, wu, wd)


def _combine_kernel(dest_ref, x_ref, wt_ref, g_ref, y_ref, o_ref, buf_ref, sem, *, tm, final_norm):
    def issue(t, _):
        for j in range(2):
            _row_copy(y_ref, dest_ref[0, 0, 2 * t + j], buf_ref.at[j], t, sem).start()
        return 0

    lax.fori_loop(0, tm, issue, 0)

    def drain(t, _):
        _row_copy(y_ref, 0, buf_ref.at[0], 0, sem).wait()
        return 0

    lax.fori_loop(0, 2 * tm, drain, 0)
    wt = wt_ref[...]
    out = x_ref[...] + (wt[:, 0:1] * buf_ref[0] + wt[:, 1:2] * buf_ref[1])
    if final_norm:
        out = _rms(out, g_ref[...])
    o_ref[...] = out


def _combine(dest3, x2, wt, g_final, y, final_norm):
    t = x2.shape[0]
    tm = dest3.shape[2] // 2
    tok = lambda n: pl.BlockSpec((tm, n), lambda i: (i, 0))
    return pl.pallas_call(
        functools.partial(_combine_kernel, tm=tm, final_norm=final_norm),
        grid=(t // tm,),
        in_specs=[pl.BlockSpec((1, 1, 2 * tm), lambda i: (i, 0, 0), memory_space=pltpu.SMEM),
                  tok(D_MODEL), tok(LANES), _full((1, D_MODEL)), pl.BlockSpec(memory_space=pl.ANY)],
        out_specs=tok(D_MODEL),
        out_shape=jax.ShapeDtypeStruct((t, D_MODEL), F32),
        scratch_shapes=[pltpu.VMEM((2, tm, D_MODEL), F32), pltpu.SemaphoreType.DMA(())],
        compiler_params=_params(),
        name="combine",
    )(dest3, x2, wt, g_final, y)


def _split_w_in(w_in_l):
    sizes = [A_WIDTH, A_WIDTH, B_WIDTH, B_WIDTH, B_WIDTH, IDX_HEADS * IDX_DIM, IDX_DIM, IDX_HEADS,
             C_WIDTH, C_WIDTH, C_WIDTH, N_BRANCH * D_MODEL]
    cuts = np.cumsum([0] + sizes)
    u, va, q, k, vb, qi, ki, wi, cb, cc, cx, gates = [w_in_l[:, a:b] for a, b in zip(cuts[:-1], cuts[1:])]
    pad = lambda w: jnp.pad(w, ((0, 0), (0, LANES - w.shape[1])))
    w_a = jnp.concatenate([u, va], axis=1).astype(BF16)
    w_b = jnp.concatenate([q, k, vb, qi, pad(ki), pad(wi)], axis=1).astype(BF16)
    w_c = jnp.concatenate([cb, cc, cx], axis=1).astype(BF16)
    return w_a, w_b, w_c, gates.astype(BF16)


def _token_mixers(x2, g, w_in_l, ws, bs, conv_w, wpa, wpb, wpc, w_out, tabs_b, tabs_i, bsz, seq):
    w_a, w_b, w_c, w_g = _split_w_in(w_in_l)
    n_sel = min(TOPK_MAX, seq // 4)
    q, k, v, qi, ki, wi = _proj_b(x2, g, w_b, tabs_b, tabs_i)
    to_t = lambda a, n: a.reshape(bsz, seq, -1)[:, :, :n].swapaxes(1, 2)
    ybt = _dsa(to_t(qi, IDX_HEADS * IDX_DIM), to_t(wi, IDX_HEADS),
               ki.reshape(bsz, seq, LANES)[:, :, :IDX_DIM], to_t(q, B_WIDTH),
               k.reshape(bsz, seq, B_WIDTH), to_t(v, B_WIDTH), n_sel)
    yb = ybt.swapaxes(1, 2).reshape(bsz * seq, B_WIDTH)
    ya = _proj_a(x2, g, w_a, ws.astype(BF16), bs.T)
    yc = _proj_c(x2, g, w_c, conv_w, seq)
    bf = lambda w: w.astype(BF16)
    return _merge(x2, g, w_g, ya, yb, yc, bf(wpa), bf(wpb), bf(wpc), bf(w_out))


def _hier_moe(x2, g, rg_w, rg_b, re_w, re_b, w_gate, w_up, w_down, g_final, final_norm):
    t = x2.shape[0]
    pad_c = LANES - N_GROUPS - N_EXPERTS
    w_r = jnp.pad(jnp.concatenate([rg_w, re_w], axis=1), ((0, 0), (0, pad_c)))
    b_r = jnp.pad(jnp.concatenate([rg_b, re_b]), (0, pad_c))[None, :]
    h, eid, wt = _router(x2, g, w_r, b_r)
    rank, counts = _rank(eid)
    cnt = counts[0, :N_EXPERTS].astype(I32)
    padded = ((cnt + EXP_BLOCK - 1) // EXP_BLOCK) * EXP_BLOCK
    pend = jnp.cumsum(padded)
    pstart = pend - padded
    n_blocks = (2 * t) // EXP_BLOCK + N_EXPERTS
    n_rows = n_blocks * EXP_BLOCK
    block_exp = jnp.minimum(
        jnp.searchsorted(pend, jnp.arange(n_blocks, dtype=I32) * EXP_BLOCK, side='right'),
        N_EXPERTS - 1).astype(I32)
    n_used = (pend[-1:] // EXP_BLOCK).astype(I32)
    start_row = jnp.pad(pstart.astype(F32), (0, LANES - N_EXPERTS))[None, :]
    dest = _dest(eid, rank, start_row)
    tm = min(TM_MOVE, t)
    dest3 = dest[:, :2].reshape(t // tm, 1, 2 * tm)
    xs = _scatter_rows(dest3, h, n_rows)
    bf = lambda w: w.astype(BF16)
    y = _experts(block_exp, n_used, xs, bf(w_gate), bf(w_up), bf(w_down))
    return _combine(dest3, x2, wt, g_final, y, final_norm)


def kernel(x, positions, norm_mix_g, norm_ffn_g, norm_final_g, w_in, gmlp_ws, gmlp_b, conv_w, w_proj_a,
           w_proj_b, w_proj_c, w_out, router_group_w, router_group_b, router_expert_w, router_expert_b,
           expert_w_gate, expert_w_up, expert_w_down):
    bsz, seq, d = x.shape
    depth = w_in.shape[0]
    x2 = x.reshape(bsz * seq, d)
    pos_col = positions.astype(F32).reshape(bsz * seq, 1)
    tabs_b = _rope_tables(pos_col, B_HEAD_DIM, B_ROT)
    tabs_i = _rope_tables(pos_col, IDX_DIM, IDX_ROT)
    g_final = norm_final_g[None, :]
    for l in range(depth):
        x2 = _token_mixers(x2, norm_mix_g[l][None, :], w_in[l], gmlp_ws[l], gmlp_b[l], conv_w[l],
                           w_proj_a[l], w_proj_b[l], w_proj_c[l], w_out[l], tabs_b, tabs_i, bsz, seq)
        x2 = _hier_moe(x2, norm_ffn_g[l][None, :], router_group_w[l], router_group_b[l],
                       router_expert_w[l], router_expert_b[l], expert_w_gate[l], expert_w_up[l],
                       expert_w_down[l], g_final, l == depth - 1)
    return x2.reshape(bsz, seq, d)
```

```python
import functools

import numpy as np
import jax
import jax.numpy as jnp
from jax import lax
from jax.experimental import pallas as pl
from jax.experimental.pallas import tpu as pltpu

D_MODEL = 1024
CHUNK = 128
A_GROUPS = 4
A_GROUP_DIM = 128
A_WIDTH = A_GROUPS * A_GROUP_DIM
B_HEADS = 4
B_HEAD_DIM = 128
B_WIDTH = B_HEADS * B_HEAD_DIM
B_ROT = B_HEAD_DIM // 4
IDX_HEADS = 4
IDX_DIM = 64
IDX_ROT = IDX_DIM // 4
TOPK_MAX = 256
C_WIDTH = 512
CONV_WIDTH = 3
N_BRANCH = 3
N_GROUPS = 4
EXP_PER_GROUP = 8
N_EXPERTS = N_GROUPS * EXP_PER_GROUP
D_FF_EXPERT = 512
ROPE_THETA = 500000.0
EPS = 1e-6
NEG = -1e30

LANES = 128
SUBLANES = 8
VMEM_LIMIT = 56 * 1024 * 1024

TM_PROJ = 512
TM_MERGE = 256
TM_ROUTE = 512
TM_RANK = 256
TM_MOVE = 256
EXP_BLOCK = 256
DSA_TQ = 128
DSA_KC = 256
DSA_SKC = 512
DSA_SUB = 256
DSA_RB = 512
V_ROWS = B_HEAD_DIM + 16
LOG2E = 1.4426950408889634

F32 = jnp.float32
BF16 = jnp.bfloat16
I32 = jnp.int32
I16 = jnp.int16
INT_MIN = -2 ** 31


def _monotone_key_of(value):
    bits = int(np.array(value, np.float32).view(np.int32))
    return bits ^ ((bits >> 31) & 0x7FFFFFFF)


NEG_KEY = _monotone_key_of(NEG)
NEG_HI = NEG_KEY >> 16
NEG_LO = (NEG_KEY & 0xFFFF) - 32768


def _params(n_axes=1, semantics=None):
    return pltpu.CompilerParams(
        dimension_semantics=semantics or ("arbitrary",) * n_axes,
        vmem_limit_bytes=VMEM_LIMIT)


def _rms(x, g):
    ms = jnp.mean(x * x, axis=-1, keepdims=True)
    return x * lax.rsqrt(ms + EPS) * g


def _full(shape):
    nd = len(shape)
    return pl.BlockSpec(shape, lambda *_: (0,) * nd)


def _rope_table_kernel(pos_ref, invf_ref, mrot_ref, ma_ref, mb_ref, c_ref, sa_ref, sb_ref):
    ang = pos_ref[...] * invf_ref[...]
    c = jnp.cos(ang)
    s = jnp.sin(ang)
    c_ref[...] = jnp.where(mrot_ref[...] > 0, c, 1.0)
    sa_ref[...] = jnp.where(ma_ref[...] > 0, -s, 0.0)
    sb_ref[...] = jnp.where(mb_ref[...] > 0, s, 0.0)


def _rope_tables(pos_col, head_dim, rot):
    t = pos_col.shape[0]
    half = rot // 2
    inv_freq = jnp.float32(ROPE_THETA) ** (-jnp.arange(half, dtype=F32) * 2.0 / rot)
    lane = np.arange(LANES) % head_dim
    mrot = lane < rot
    ma = lane < half
    mb = (lane >= half) & (lane < rot)
    invf = jnp.where(jnp.asarray(mrot), inv_freq[np.where(mrot, lane % half, 0)], 0.0)[None, :]
    row = lambda m: jnp.asarray(m.astype(np.float32))[None, :]
    tm = min(1024, t)
    spec_row = _full((1, LANES))
    out_spec = pl.BlockSpec((tm, LANES), lambda i: (i, 0))
    return pl.pallas_call(
        _rope_table_kernel,
        grid=(t // tm,),
        in_specs=[pl.BlockSpec((tm, 1), lambda i: (i, 0)), spec_row, spec_row, spec_row, spec_row],
        out_specs=[out_spec] * 3,
        out_shape=[jax.ShapeDtypeStruct((t, LANES), F32)] * 3,
        compiler_params=_params(),
        name="rope_tables",
    )(pos_col, invf, row(mrot), row(ma), row(mb))


def _rope(x, c, sa, sb, half):
    parts = []
    for j in range(x.shape[1] // LANES):
        xj = x[:, j * LANES:(j + 1) * LANES]
        parts.append(xj * c + pltpu.roll(xj, LANES - half, 1) * sa + pltpu.roll(xj, half, 1) * sb)
    return parts


def _proj_b_kernel(x_ref, g_ref, w_ref, cb_ref, sab_ref, sbb_ref, ci_ref, sai_ref, sbi_ref,
                   q_ref, k_ref, v_ref, qi_ref, ki_ref, wi_ref):
    h = _rms(x_ref[...], g_ref[...]).astype(BF16)
    p = jnp.dot(h, w_ref[...], preferred_element_type=F32)
    tb = (cb_ref[...], sab_ref[...], sbb_ref[...])
    ti = (ci_ref[...], sai_ref[...], sbi_ref[...])
    o = 0
    qs = _rope(p[:, o:o + B_WIDTH], *tb, B_ROT // 2)
    for j, qj in enumerate(qs):
        q_ref[:, j * LANES:(j + 1) * LANES] = (qj * (B_HEAD_DIM ** -0.5 * LOG2E)).astype(BF16)
    o += B_WIDTH
    for j, kj in enumerate(_rope(p[:, o:o + B_WIDTH], *tb, B_ROT // 2)):
        k_ref[:, j * LANES:(j + 1) * LANES] = kj.astype(BF16)
    o += B_WIDTH
    v_ref[...] = p[:, o:o + B_WIDTH].astype(BF16)
    o += B_WIDTH
    for j, qj in enumerate(_rope(p[:, o:o + IDX_HEADS * IDX_DIM], *ti, IDX_ROT // 2)):
        qi_ref[:, j * LANES:(j + 1) * LANES] = qj.astype(BF16)
    o += IDX_HEADS * IDX_DIM
    ki_ref[...] = _rope(p[:, o:o + LANES], *ti, IDX_ROT // 2)[0].astype(BF16)
    o += LANES
    wi_ref[...] = p[:, o:o + LANES] * ((IDX_HEADS ** -0.5) * (IDX_DIM ** -0.5))


def _proj_b(x2, g, w, tabs_b, tabs_i):
    t = x2.shape[0]
    tm = min(TM_PROJ, t)
    tok = lambda n: pl.BlockSpec((tm, n), lambda i: (i, 0))
    n_out = w.shape[1]
    outs = [(B_WIDTH, BF16), (B_WIDTH, BF16), (B_WIDTH, BF16), (IDX_HEADS * IDX_DIM, BF16),
            (LANES, BF16), (LANES, F32)]
    return pl.pallas_call(
        _proj_b_kernel,
        grid=(t // tm,),
        in_specs=[tok(D_MODEL), _full((1, D_MODEL)), _full((D_MODEL, n_out))] + [tok(LANES)] * 6,
        out_specs=[tok(n) for n, _ in outs],
        out_shape=[jax.ShapeDtypeStruct((t, n), d) for n, d in outs],
        compiler_params=_params(),
        name="proj_b",
    )(x2, g, w, *tabs_b, *tabs_i)


def _dsa_kernel(qit_ref, wit_ref, ki_ref, qt_ref, k_ref, vt_ref, o_ref, keys_ref, k16_ref, acc_ref,
                bias0_ref, bias1_ref, lg0_ref, lg1_ref,
                *, seq, n_sel, tq, kc, skc, sub, rb):
    i = pl.program_id(1)
    q0 = i * tq
    n_full = q0 // skc
    n_rows = (n_full + 1) * skc
    n_pad = (seq - 1 - q0) - lax.broadcasted_iota(I32, (1, tq), 1)
    n_chunk = (q0 + tq + kc - 1) // kc
    wit = wit_ref[...]

    def score_chunk(c, masked):
        for s in range(skc // sub):
            r0 = pl.multiple_of(c * skc + s * sub, sub)
            kic = ki_ref[pl.ds(r0, sub), :]
            sc = None
            for h in range(IDX_HEADS):
                a = jnp.dot(kic, qit_ref[h * IDX_DIM:(h + 1) * IDX_DIM, :], preferred_element_type=F32)
                term = jnp.maximum(a, 0.0) * wit[h:h + 1, :]
                sc = term if sc is None else sc + term
            bits = lax.bitcast_convert_type(sc, I32)
            key = bits ^ ((bits >> 31) & 0x7FFFFFFF)
            key = jnp.where(key == -1, 0, key)
            if masked:
                qpos = q0 + lax.broadcasted_iota(I32, (sub, tq), 1)
                krow = r0 + lax.broadcasted_iota(I32, (sub, tq), 0)
                key = jnp.where(krow <= qpos, key, INT_MIN)
            keys_ref[pl.ds(r0, sub), :] = key
            k16_ref[pl.ds(r0, sub), :] = (key >> 16).astype(I16)

    def score_body(c, _):
        score_chunk(c, False)
        return 0

    lax.fori_loop(0, n_full, score_body, 0)
    score_chunk(n_full, True)

    acc_rows = 64

    def count16(cand, strict):
        c16 = jnp.broadcast_to(cand, (acc_rows, tq)).astype(I16)

        def body(r, acc):
            r0 = pl.multiple_of(r * rb, rb)
            blk = k16_ref[pl.ds(r0, rb), :]
            for j in range(rb // acc_rows):
                part = blk[j * acc_rows:(j + 1) * acc_rows, :]
                hit = (part > c16) if strict else (part >= c16)
                acc = acc + jnp.where(hit, jnp.int16(1), jnp.int16(0))
            return acc

        acc = lax.fori_loop(0, n_rows // rb, body, jnp.zeros((acc_rows, tq), I16))
        return jnp.sum(acc.astype(I32).astype(F32), axis=0, keepdims=True).astype(I32)

    def search16(extra):
        def bit_body(b, carry):
            ans_u, = carry
            cand_u = ans_u | jnp.left_shift(jnp.int32(1), 15 - b)
            cand = cand_u - 32768
            ok = (count16(cand, False) + extra(cand, False)) >= n_sel
            return (jnp.where(ok, cand_u, ans_u),)

        ans_u, = lax.fori_loop(0, 16, bit_body, (jnp.zeros((1, tq), I32),))
        return ans_u - 32768

    pad_hi = lambda v, strict: jnp.where((NEG_HI > v) if strict else (NEG_HI >= v), n_pad, 0)
    thi = search16(pad_hi)
    c_above = count16(thi, True) + pad_hi(thi, True)

    def prep_body(r, _):
        r0 = pl.multiple_of(r * rb, rb)
        key = keys_ref[pl.ds(r0, rb), :]
        lo = (key & 0xFFFF) - 32768
        k16_ref[pl.ds(r0, rb), :] = jnp.where((key >> 16) == thi, lo, -32768).astype(I16)
        return 0

    lax.fori_loop(0, n_rows // rb, prep_body, 0)
    pad_in = thi == NEG_HI

    def pad_lo(v, strict):
        hit = (NEG_LO > v) if strict else (NEG_LO >= v)
        return c_above + jnp.where(pad_in & hit, n_pad, 0)

    tlo = search16(pad_lo)
    thr = thi * 65536 + (tlo + 32768)
    need = (n_sel - (count16(tlo, True) + pad_lo(tlo, True))).astype(F32)

    acc_ref[...] = jnp.zeros_like(acc_ref)
    bias_refs = (bias0_ref, bias1_ref)
    lg_refs = (lg0_ref, lg1_ref)
    lower = (lax.broadcasted_iota(I32, (kc, kc), 1) < lax.broadcasted_iota(I32, (kc, kc), 0))
    lower = jnp.where(lower, 1.0, 0.0).astype(BF16)

    def produce(c, slot, tie_base):
        r0 = pl.multiple_of(c * kc, kc)
        key = keys_ref[pl.ds(r0, kc), :]
        tie = key == thr
        tie_f = jnp.where(tie, 1.0, 0.0)
        before = jnp.dot(lower, tie_f.astype(BF16), preferred_element_type=F32)
        sel = (key > thr) | (tie & (before < need - tie_base))
        bias_refs[slot][...] = jnp.where(sel, 0.0, NEG)
        for h in range(B_HEADS):
            hs = slice(h * B_HEAD_DIM, (h + 1) * B_HEAD_DIM)
            lg_refs[slot][h] = jnp.dot(k_ref[pl.ds(r0, kc), hs], qt_ref[hs, :], preferred_element_type=F32)
        return tie_base + before[kc - 1:kc, :] + tie_f[kc - 1:kc, :]

    def consume(c, slot, ms):
        r0 = pl.multiple_of(c * kc, kc)
        bias = bias_refs[slot][...]
        new_ms = []
        for h in range(B_HEADS):
            vs = slice(h * V_ROWS, (h + 1) * V_ROWS)
            lg = lg_refs[slot][h] + bias
            m_new = jnp.maximum(ms[h], jnp.max(lg, axis=0, keepdims=True))
            alpha = jnp.exp2(ms[h] - m_new)
            p = jnp.exp2(lg - m_new).astype(BF16)
            pv = jnp.dot(vt_ref[vs, pl.ds(r0, kc)], p, preferred_element_type=F32)
            acc_ref[vs, :] = acc_ref[vs, :] * alpha + pv
            new_ms.append(m_new)
        return tuple(new_ms)

    def pair_body(j, carry):
        ms, tie_base = carry
        c = 2 * j
        tie_base = produce(c + 1, 1, tie_base)
        ms = consume(c, 0, ms)
        tie_base = produce(c + 2, 0, tie_base)
        ms = consume(c + 1, 1, ms)
        return ms, tie_base

    row = lambda v: jnp.full((1, tq), v, F32)
    ms0 = tuple(row(NEG) for _ in range(B_HEADS))
    n_pairs = (n_chunk - 1) // 2
    ms, tie_base = lax.fori_loop(0, n_pairs, pair_body, (ms0, produce(0, 0, row(0.0))))
    c0 = 2 * n_pairs

    @pl.when(c0 == n_chunk - 1)
    def _():
        consume(c0, 0, ms)

    @pl.when(c0 != n_chunk - 1)
    def _():
        produce(c0 + 1, 1, tie_base)
        consume(c0 + 1, 1, consume(c0, 0, ms))

    for h in range(B_HEADS):
        a = acc_ref[h * V_ROWS:(h + 1) * V_ROWS, :]
        o_ref[h * B_HEAD_DIM:(h + 1) * B_HEAD_DIM, :] = (
            a[:B_HEAD_DIM, :] / a[B_HEAD_DIM:B_HEAD_DIM + 1, :]).astype(o_ref.dtype)


def _dsa(qit, wit, ki, qt, k, vt_aug, n_sel):
    bsz, _, seq = qt.shape
    tq = min(DSA_TQ, seq)
    kc = min(DSA_KC, seq)
    skc = min(DSA_SKC, seq)
    once = pl.Buffered(1)
    kern = functools.partial(_dsa_kernel, seq=seq, n_sel=n_sel, tq=tq, kc=kc, skc=skc,
                             sub=min(DSA_SUB, skc), rb=min(DSA_RB, skc))
    return pl.pallas_call(
        kern,
        grid=(bsz, seq // tq),
        in_specs=[
            pl.BlockSpec((None, IDX_HEADS * IDX_DIM, tq), lambda b, i: (b, 0, i)),
            pl.BlockSpec((None, IDX_HEADS, tq), lambda b, i: (b, 0, i)),
            pl.BlockSpec((None, seq, IDX_DIM), lambda b, i: (b, 0, 0), pipeline_mode=once),
            pl.BlockSpec((None, B_WIDTH, tq), lambda b, i: (b, 0, i)),
            pl.BlockSpec((None, seq, B_WIDTH), lambda b, i: (b, 0, 0), pipeline_mode=once),
            pl.BlockSpec((None, B_HEADS * V_ROWS, seq), lambda b, i: (b, 0, 0), pipeline_mode=once),
        ],
        out_specs=pl.BlockSpec((None, B_WIDTH, tq), lambda b, i: (b, 0, i)),
        out_shape=jax.ShapeDtypeStruct((bsz, B_WIDTH, seq), BF16),
        scratch_shapes=[pltpu.VMEM((seq, tq), I32), pltpu.VMEM((seq, tq), I16),
                        pltpu.VMEM((B_HEADS * V_ROWS, tq), F32), pltpu.VMEM((kc, tq), F32),
                        pltpu.VMEM((kc, tq), F32), pltpu.VMEM((B_HEADS, kc, tq), F32),
                        pltpu.VMEM((B_HEADS, kc, tq), F32)],
        compiler_params=_params(2),
        name="dsa",
    )(qit, wit, ki, qt, k, vt_aug)


def _proj_a_kernel(x_ref, g_ref, w_ref, ws_ref, bs_ref, ya_ref, *, tm):
    h = _rms(x_ref[...], g_ref[...]).astype(BF16)
    p = jnp.dot(h, w_ref[...], preferred_element_type=F32)
    u = jax.nn.gelu(p[:, :A_WIDTH])
    v = jax.nn.gelu(p[:, A_WIDTH:])
    mu = jnp.mean(v, axis=-1, keepdims=True)
    var = jnp.mean(jnp.square(v - mu), axis=-1, keepdims=True)
    v = ((v - mu) * lax.rsqrt(var + EPS)).astype(BF16)
    causal = (lax.broadcasted_iota(I32, (CHUNK, CHUNK), 1) <= lax.broadcasted_iota(I32, (CHUNK, CHUNK), 0))
    bs = bs_ref[...]
    for gidx in range(A_GROUPS):
        wsm = jnp.where(causal, ws_ref[gidx], 0.0).astype(BF16)
        cs = slice(gidx * A_GROUP_DIM, (gidx + 1) * A_GROUP_DIM)
        for n in range(tm // CHUNK):
            rs = slice(n * CHUNK, (n + 1) * CHUNK)
            vm = jnp.dot(wsm, v[rs, cs], preferred_element_type=F32) + bs[:, gidx:gidx + 1]
            ya_ref[rs, cs] = (u[rs, cs] * vm).astype(ya_ref.dtype)


def _proj_a(x2, g, w, ws, bs_t):
    t = x2.shape[0]
    tm = min(TM_PROJ, t)
    return pl.pallas_call(
        functools.partial(_proj_a_kernel, tm=tm),
        grid=(t // tm,),
        in_specs=[pl.BlockSpec((tm, D_MODEL), lambda i: (i, 0)), _full((1, D_MODEL)),
                  _full((D_MODEL, 2 * A_WIDTH)), _full((A_GROUPS, CHUNK, CHUNK)), _full((CHUNK, A_GROUPS))],
        out_specs=pl.BlockSpec((tm, A_WIDTH), lambda i: (i, 0)),
        out_shape=jax.ShapeDtypeStruct((t, A_WIDTH), BF16),
        compiler_params=_params(),
        name="proj_a",
    )(x2, g, w, ws, bs_t)


def _proj_c_kernel(x_ref, xh_ref, g_ref, w_ref, cw_ref, yc_ref, *, tm, tiles_per_seq):
    g = g_ref[...]
    w = w_ref[...]
    p = jnp.dot(_rms(x_ref[...], g).astype(BF16), w, preferred_element_type=F32)
    ph = jnp.dot(_rms(xh_ref[...], g).astype(BF16), w[:, C_WIDTH:], preferred_element_type=F32)
    first = (pl.program_id(0) % tiles_per_seq) == 0
    zh = ph[:, :C_WIDTH] * ph[:, C_WIDTH:] * jnp.where(first, 0.0, 1.0)
    z = p[:, C_WIDTH:2 * C_WIDTH] * p[:, 2 * C_WIDTH:]
    row = lax.broadcasted_iota(I32, (tm, C_WIDTH), 0)
    z1 = jnp.where(row == 0, zh[SUBLANES - 1:SUBLANES, :], pltpu.roll(z, 1, 0))
    z2 = jnp.where(row == 0, zh[SUBLANES - 2:SUBLANES - 1, :],
                   jnp.where(row == 1, zh[SUBLANES - 1:SUBLANES, :], pltpu.roll(z, 2, 0)))
    cw = cw_ref[...]
    y = cw[0:1, :] * z2 + cw[1:2, :] * z1 + cw[2:3, :] * z
    yc_ref[...] = (p[:, :C_WIDTH] * y).astype(yc_ref.dtype)


def _proj_c(x2, g, w, cw, seq):
    t = x2.shape[0]
    tm = min(TM_PROJ, seq)
    per8 = tm // SUBLANES
    return pl.pallas_call(
        functools.partial(_proj_c_kernel, tm=tm, tiles_per_seq=seq // tm),
        grid=(t // tm,),
        in_specs=[pl.BlockSpec((tm, D_MODEL), lambda i: (i, 0)),
                  pl.BlockSpec((SUBLANES, D_MODEL), lambda i: (jnp.maximum(i * per8 - 1, 0), 0)),
                  _full((1, D_MODEL)), _full((D_MODEL, 3 * C_WIDTH)), _full((CONV_WIDTH, C_WIDTH))],
        out_specs=pl.BlockSpec((tm, C_WIDTH), lambda i: (i, 0)),
        out_shape=jax.ShapeDtypeStruct((t, C_WIDTH), BF16),
        compiler_params=_params(),
        name="proj_c",
    )(x2, x2, g, w, cw)


def _merge_kernel(x_ref, g_ref, wg_ref, ya_ref, yb_ref, yc_ref, wa_ref, wb_ref, wc_ref, wo_ref, o_ref):
    x = x_ref[...]
    h = _rms(x, g_ref[...]).astype(BF16)
    merged = None
    for j, (y_ref, w_ref) in enumerate(((ya_ref, wa_ref), (yb_ref, wb_ref), (yc_ref, wc_ref))):
        gate = jnp.dot(h, wg_ref[:, j * D_MODEL:(j + 1) * D_MODEL], preferred_element_type=F32)
        term = jax.nn.sigmoid(gate) * jnp.dot(y_ref[...], w_ref[...], preferred_element_type=F32)
        merged = term if merged is None else merged + term
    o_ref[...] = x + jnp.dot(merged.astype(BF16), wo_ref[...], preferred_element_type=F32)


def _merge(x2, g, wg, ya, yb, yc, wa, wb, wc, wo):
    t = x2.shape[0]
    tm = min(TM_MERGE, t)
    tok = lambda n: pl.BlockSpec((tm, n), lambda i: (i, 0))
    wproj = _full((A_WIDTH, D_MODEL))
    return pl.pallas_call(
        _merge_kernel,
        grid=(t // tm,),
        in_specs=[tok(D_MODEL), _full((1, D_MODEL)), _full((D_MODEL, N_BRANCH * D_MODEL)),
                  tok(A_WIDTH), tok(B_WIDTH), tok(C_WIDTH), wproj, wproj, wproj, _full((D_MODEL, D_MODEL))],
        out_specs=tok(D_MODEL),
        out_shape=jax.ShapeDtypeStruct((t, D_MODEL), F32),
        compiler_params=_params(),
        name="merge",
    )(x2, g, wg, ya, yb, yc, wa, wb, wc, wo)


def _router_kernel(x_ref, g_ref, w_ref, b_ref, h_ref, eid_ref, wt_ref):
    h = _rms(x_ref[...], g_ref[...])
    h_ref[...] = h
    logits = jnp.dot(h, w_ref[...], preferred_element_type=F32, precision=lax.Precision.HIGHEST) + b_ref[...]
    lane = lax.broadcasted_iota(I32, logits.shape, 1)
    lane_f = lane.astype(F32)
    ninf = -jnp.inf

    def first_max(vals):
        vmax = jnp.max(vals, axis=1, keepdims=True)
        idx = jnp.min(jnp.where(vals == vmax, lane_f, float(LANES)), axis=1, keepdims=True)
        return vmax, idx

    gl = jnp.where(lane < N_GROUPS, logits, ninf)
    gmax, grp = first_max(gl)
    gsum = jnp.sum(jnp.where(lane < N_GROUPS, jnp.exp(logits - gmax), 0.0), axis=1, keepdims=True)
    gw = 1.0 / gsum
    e_lane = lane - N_GROUPS
    in_grp = (e_lane >= 0) & (e_lane < N_EXPERTS) & ((e_lane >> 3).astype(F32) == grp)
    el = jnp.where(in_grp, logits, ninf)
    v1, i1 = first_max(el)
    v2, i2 = first_max(jnp.where(lane_f == i1, ninf, el))
    e2 = jnp.exp(v2 - v1)
    den = 1.0 + e2
    w1 = (1.0 / den) * gw
    w2 = (e2 / den) * gw
    eid1 = (i1 - N_GROUPS).astype(I32)
    eid2 = (i2 - N_GROUPS).astype(I32)
    eid_ref[...] = jnp.where(lane == 0, eid1, jnp.where(lane == 1, eid2, 0))
    wt_ref[...] = jnp.where(lane == 0, w1, jnp.where(lane == 1, w2, 0.0))


def _router(x2, g, w, b):
    t = x2.shape[0]
    tm = min(TM_ROUTE, t)
    tok = lambda n: pl.BlockSpec((tm, n), lambda i: (i, 0))
    return pl.pallas_call(
        _router_kernel,
        grid=(t // tm,),
        in_specs=[tok(D_MODEL), _full((1, D_MODEL)), _full((D_MODEL, LANES)), _full((1, LANES))],
        out_specs=[tok(D_MODEL), tok(LANES), tok(LANES)],
        out_shape=[jax.ShapeDtypeStruct((t, D_MODEL), F32), jax.ShapeDtypeStruct((t, LANES), I32),
                   jax.ShapeDtypeStruct((t, LANES), F32)],
        compiler_params=_params(),
        name="router",
    )(x2, g, w, b)


def _rank_kernel(eid_ref, rank_ref, cnt_ref, *, tm):
    @pl.when(pl.program_id(0) == 0)
    def _():
        cnt_ref[...] = jnp.zeros_like(cnt_ref)

    eid = eid_ref[...]
    lane = lax.broadcasted_iota(I32, (tm, LANES), 1)
    o1 = lane == eid[:, 0:1]
    o2 = lane == eid[:, 1:2]
    both = jnp.where(o1 | o2, 1.0, 0.0)
    lower = lax.broadcasted_iota(I32, (tm, tm), 1) < lax.broadcasted_iota(I32, (tm, tm), 0)
    before = jnp.dot(jnp.where(lower, 1.0, 0.0).astype(BF16), both.astype(BF16),
                     preferred_element_type=F32) + cnt_ref[...]
    r1 = jnp.sum(jnp.where(o1, before, 0.0), axis=1, keepdims=True).astype(I32)
    r2 = jnp.sum(jnp.where(o2, before, 0.0), axis=1, keepdims=True).astype(I32)
    rank_ref[...] = jnp.where(lane == 0, r1, jnp.where(lane == 1, r2, 0))
    cnt_ref[...] = cnt_ref[...] + jnp.sum(both, axis=0, keepdims=True)


def _rank(eid):
    t = eid.shape[0]
    tm = min(TM_RANK, t)
    return pl.pallas_call(
        functools.partial(_rank_kernel, tm=tm),
        grid=(t // tm,),
        in_specs=[pl.BlockSpec((tm, LANES), lambda i: (i, 0))],
        out_specs=[pl.BlockSpec((tm, LANES), lambda i: (i, 0)), _full((1, LANES))],
        out_shape=[jax.ShapeDtypeStruct((t, LANES), I32), jax.ShapeDtypeStruct((1, LANES), F32)],
        compiler_params=_params(),
        name="rank",
    )(eid)


def _dest_kernel(eid_ref, rank_ref, start_ref, dest_ref):
    eid = eid_ref[...]
    lane = lax.broadcasted_iota(I32, eid.shape, 1)
    start = start_ref[...]
    s1 = jnp.sum(jnp.where(lane == eid[:, 0:1], start, 0.0), axis=1, keepdims=True).astype(I32)
    s2 = jnp.sum(jnp.where(lane == eid[:, 1:2], start, 0.0), axis=1, keepdims=True).astype(I32)
    rank = rank_ref[...]
    dest_ref[...] = jnp.where(lane == 0, s1, jnp.where(lane == 1, s2, 0)) + rank


def _dest(eid, rank, start_row):
    t = eid.shape[0]
    tm = min(TM_ROUTE, t)
    tok = pl.BlockSpec((tm, LANES), lambda i: (i, 0))
    return pl.pallas_call(
        _dest_kernel,
        grid=(t // tm,),
        in_specs=[tok, tok, _full((1, LANES))],
        out_specs=tok,
        out_shape=jax.ShapeDtypeStruct((t, LANES), I32),
        compiler_params=_params(),
        name="dest",
    )(eid, rank, start_row)


def _row_copy(src_ref, src_row, dst_ref, dst_row, sem):
    return pltpu.make_async_copy(src_ref.at[pl.ds(src_row, 1)], dst_ref.at[pl.ds(dst_row, 1)], sem)


def _scatter_kernel(dest_ref, h_ref, xs_in_ref, xs_ref, sem, *, tm):
    del xs_in_ref

    def issue(t, _):
        for j in range(2):
            _row_copy(h_ref, t, xs_ref, dest_ref[0, 0, 2 * t + j], sem).start()
        return 0

    lax.fori_loop(0, tm, issue, 0)

    def drain(t, _):
        _row_copy(h_ref, 0, xs_ref, 0, sem).wait()
        return 0

    lax.fori_loop(0, 2 * tm, drain, 0)


def _scatter_rows(dest3, h, n_rows):
    t = h.shape[0]
    tm = dest3.shape[2] // 2
    xs0 = jnp.zeros((n_rows, D_MODEL), F32)
    return pl.pallas_call(
        functools.partial(_scatter_kernel, tm=tm),
        grid=(t // tm,),
        in_specs=[pl.BlockSpec((1, 1, 2 * tm), lambda i: (i, 0, 0), memory_space=pltpu.SMEM),
                  pl.BlockSpec((tm, D_MODEL), lambda i: (i, 0)),
                  pl.BlockSpec(memory_space=pl.ANY)],
        out_specs=pl.BlockSpec(memory_space=pl.ANY),
        out_shape=jax.ShapeDtypeStruct((n_rows, D_MODEL), F32),
        scratch_shapes=[pltpu.SemaphoreType.DMA(())],
        input_output_aliases={2: 0},
        compiler_params=_params(),
        name="scatter_rows",
    )(dest3, h, xs0)


def _expert_kernel(be_ref, nused_ref, xs_ref, wg_ref, wu_ref, wd_ref, y_ref):
    @pl.when(pl.program_id(0) < nused_ref[0])
    def _():
        xb = xs_ref[...].astype(BF16)
        gate = jnp.dot(xb, wg_ref[...], preferred_element_type=F32)
        up = jnp.dot(xb, wu_ref[...], preferred_element_type=F32)
        hid = (jax.nn.silu(gate) * up).astype(BF16)
        y_ref[...] = jnp.dot(hid, wd_ref[...], preferred_element_type=F32)

    @pl.when(pl.program_id(0) >= nused_ref[0])
    def _():
        y_ref[...] = jnp.zeros_like(y_ref)


def _experts(block_exp, n_used, xs, wg, wu, wd):
    n_rows = xs.shape[0]
    grid_spec = pltpu.PrefetchScalarGridSpec(
        num_scalar_prefetch=2,
        grid=(n_rows // EXP_BLOCK,),
        in_specs=[pl.BlockSpec((EXP_BLOCK, D_MODEL), lambda b, be, nu: (b, 0)),
                  pl.BlockSpec((None, D_MODEL, D_FF_EXPERT), lambda b, be, nu: (be[b], 0, 0)),
                  pl.BlockSpec((None, D_MODEL, D_FF_EXPERT), lambda b, be, nu: (be[b], 0, 0)),
                  pl.BlockSpec((None, D_FF_EXPERT, D_MODEL), lambda b, be, nu: (be[b], 0, 0))],
        out_specs=pl.BlockSpec((EXP_BLOCK, D_MODEL), lambda b, be, nu: (b, 0)),
    )
    return pl.pallas_call(
        _expert_kernel,
        grid_spec=grid_spec,
        out_shape=jax.ShapeDtypeStruct((n_rows, D_MODEL), F32),
        compiler_params=_params(),
        name="experts",
    )(block_exp, n_used, xs, wg, wu, wd)


def _combine_kernel(dest_ref, x_ref, wt_ref, g_ref, y_ref, o_ref, buf_ref, sem, *, tm, final_norm):
    def issue(t, _):
        for j in range(2):
            _row_copy(y_ref, dest_ref[0, 0, 2 * t + j], buf_ref.at[j], t, sem).start()
        return 0

    lax.fori_loop(0, tm, issue, 0)

    def drain(t, _):
        _row_copy(y_ref, 0, buf_ref.at[0], 0, sem).wait()
        return 0

    lax.fori_loop(0, 2 * tm, drain, 0)
    wt = wt_ref[...]
    out = x_ref[...] + (wt[:, 0:1] * buf_ref[0] + wt[:, 1:2] * buf_ref[1])
    if final_norm:
        out = _rms(out, g_ref[...])
    o_ref[...] = out


def _combine(dest3, x2, wt, g_final, y, final_norm):
    t = x2.shape[0]
    tm = dest3.shape[2] // 2
    tok = lambda n: pl.BlockSpec((tm, n), lambda i: (i, 0))
    return pl.pallas_call(
        functools.partial(_combine_kernel, tm=tm, final_norm=final_norm),
        grid=(t // tm,),
        in_specs=[pl.BlockSpec((1, 1, 2 * tm), lambda i: (i, 0, 0), memory_space=pltpu.SMEM),
                  tok(D_MODEL), tok(LANES), _full((1, D_MODEL)), pl.BlockSpec(memory_space=pl.ANY)],
        out_specs=tok(D_MODEL),
        out_shape=jax.ShapeDtypeStruct((t, D_MODEL), F32),
        scratch_shapes=[pltpu.VMEM((2, tm, D_MODEL), F32), pltpu.SemaphoreType.DMA(())],
        compiler_params=_params(),
        name="combine",
    )(dest3, x2, wt, g_final, y)


def _split_w_in(w_in_l):
    sizes = [A_WIDTH, A_WIDTH, B_WIDTH, B_WIDTH, B_WIDTH, IDX_HEADS * IDX_DIM, IDX_DIM, IDX_HEADS,
             C_WIDTH, C_WIDTH, C_WIDTH, N_BRANCH * D_MODEL]
    cuts = np.cumsum([0] + sizes)
    u, va, q, k, vb, qi, ki, wi, cb, cc, cx, gates = [w_in_l[:, a:b] for a, b in zip(cuts[:-1], cuts[1:])]
    pad = lambda w: jnp.pad(w, ((0, 0), (0, LANES - w.shape[1])))
    w_a = jnp.concatenate([u, va], axis=1).astype(BF16)
    w_b = jnp.concatenate([q, k, vb, qi, pad(ki), pad(wi)], axis=1).astype(BF16)
    w_c = jnp.concatenate([cb, cc, cx], axis=1).astype(BF16)
    return w_a, w_b, w_c, gates.astype(BF16)


def _token_mixers(x2, g, w_in_l, ws, bs, conv_w, wpa, wpb, wpc, w_out, tabs_b, tabs_i, bsz, seq):
    w_a, w_b, w_c, w_g = _split_w_in(w_in_l)
    n_sel = min(TOPK_MAX, seq // 4)
    q, k, v, qi, ki, wi = _proj_b(x2, g, w_b, tabs_b, tabs_i)
    to_t = lambda a, n: a.reshape(bsz, seq, -1)[:, :, :n].swapaxes(1, 2)
    vt = to_t(v, B_WIDTH).reshape(bsz, B_HEADS, B_HEAD_DIM, seq)
    ones = jnp.ones((bsz, B_HEADS, V_ROWS - B_HEAD_DIM, seq), BF16)
    vt_aug = jnp.concatenate([vt, ones], axis=2).reshape(bsz, B_HEADS * V_ROWS, seq)
    ybt = _dsa(to_t(qi, IDX_HEADS * IDX_DIM), to_t(wi, IDX_HEADS),
               ki.reshape(bsz, seq, LANES)[:, :, :IDX_DIM], to_t(q, B_WIDTH),
               k.reshape(bsz, seq, B_WIDTH), vt_aug, n_sel)
    yb = ybt.swapaxes(1, 2).reshape(bsz * seq, B_WIDTH)
    ya = _proj_a(x2, g, w_a, ws.astype(BF16), bs.T)
    yc = _proj_c(x2, g, w_c, conv_w, seq)
    bf = lambda w: w.astype(BF16)
    return _merge(x2, g, w_g, ya, yb, yc, bf(wpa), bf(wpb), bf(wpc), bf(w_out))


def _hier_moe(x2, g, rg_w, rg_b, re_w, re_b, w_gate, w_up, w_down, g_final, final_norm):
    t = x2.shape[0]
    pad_c = LANES - N_GROUPS - N_EXPERTS
    w_r = jnp.pad(jnp.concatenate([rg_w, re_w], axis=1), ((0, 0), (0, pad_c)))
    b_r = jnp.pad(jnp.concatenate([rg_b, re_b]), (0, pad_c))[None, :]
    h, eid, wt = _router(x2, g, w_r, b_r)
    rank, counts = _rank(eid)
    cnt = counts[0, :N_EXPERTS].astype(I32)
    padded = ((cnt + EXP_BLOCK - 1) // EXP_BLOCK) * EXP_BLOCK
    pend = jnp.cumsum(padded)
    pstart = pend - padded
    n_blocks = (2 * t) // EXP_BLOCK + N_EXPERTS
    n_rows = n_blocks * EXP_BLOCK
    block_exp = jnp.minimum(
        jnp.searchsorted(pend, jnp.arange(n_blocks, dtype=I32) * EXP_BLOCK, side='right'),
        N_EXPERTS - 1).astype(I32)
    n_used = (pend[-1:] // EXP_BLOCK).astype(I32)
    start_row = jnp.pad(pstart.astype(F32), (0, LANES - N_EXPERTS))[None, :]
    dest = _dest(eid, rank, start_row)
    tm = min(TM_MOVE, t)
    dest3 = dest[:, :2].reshape(t // tm, 1, 2 * tm)
    xs = _scatter_rows(dest3, h, n_rows)
    bf = lambda w: w.astype(BF16)
    y = _experts(block_exp, n_used, xs, bf(w_gate), bf(w_up), bf(w_down))
    return _combine(dest3, x2, wt, g_final, y, final_norm)


def kernel(x, positions, norm_mix_g, norm_ffn_g, norm_final_g, w_in, gmlp_ws, gmlp_b, conv_w, w_proj_a,
           w_proj_b, w_proj_c, w_out, router_group_w, router_group_b, router_expert_w, router_expert_b,
           expert_w_gate, expert_w_up, expert_w_down):
    bsz, seq, d = x.shape
    depth = w_in.shape[0]
    x2 = x.reshape(bsz * seq, d)
    pos_col = positions.astype(F32).reshape(bsz * seq, 1)
    tabs_b = _rope_tables(pos_col, B_HEAD_DIM, B_ROT)
    tabs_i = _rope_tables(pos_col, IDX_DIM, IDX_ROT)
    g_final = norm_final_g[None, :]
    for l in range(depth):
        x2 = _token_mixers(x2, norm_mix_g[l][None, :], w_in[l], gmlp_ws[l], gmlp_b[l], conv_w[l],
                           w_proj_a[l], w_proj_b[l], w_proj_c[l], w_out[l], tabs_b, tabs_i, bsz, seq)
        x2 = _hier_moe(x2, norm_ffn_g[l][None, :], router_group_w[l], router_group_b[l],
                       router_expert_w[l], router_expert_b[l], expert_w_gate[l], expert_w_up[l],
                       expert_w_down[l], g_final, l == depth - 1)
    return x2.reshape(bsz, seq, d)
```

```python
import functools

import numpy as np
import jax
import jax.numpy as jnp
from jax import lax
from jax.experimental import pallas as pl
from jax.experimental.pallas import tpu as pltpu

D_MODEL = 1024
CHUNK = 128
A_GROUPS = 4
A_GROUP_DIM = 128
A_WIDTH = A_GROUPS * A_GROUP_DIM
B_HEADS = 4
B_HEAD_DIM = 128
B_WIDTH = B_HEADS * B_HEAD_DIM
B_ROT = B_HEAD_DIM // 4
IDX_HEADS = 4
IDX_DIM = 64
IDX_ROT = IDX_DIM // 4
TOPK_MAX = 256
C_WIDTH = 512
CONV_WIDTH = 3
N_BRANCH = 3
N_GROUPS = 4
EXP_PER_GROUP = 8
N_EXPERTS = N_GROUPS * EXP_PER_GROUP
D_FF_EXPERT = 512
ROPE_THETA = 500000.0
EPS = 1e-6
NEG = -1e30

LANES = 128
SUBLANES = 8
VMEM_LIMIT = 56 * 1024 * 1024

TM_PROJ = 512
TM_MERGE = 256
TM_ROUTE = 512
TM_RANK = 256
TM_MOVE = 256
EXP_BLOCK = 256
DSA_TQ = 128
DSA_KC = 256
DSA_SKC = 512
DSA_SUB = 256
DSA_RB = 512
V_ROWS = B_HEAD_DIM + 16
LOG2E = 1.4426950408889634

F32 = jnp.float32
BF16 = jnp.bfloat16
I32 = jnp.int32
INT_MIN = -2 ** 31


def _monotone_key_of(value):
    bits = int(np.array(value, np.float32).view(np.int32))
    return bits ^ ((bits >> 31) & 0x7FFFFFFF)


NEG_KEY = _monotone_key_of(NEG)


def _params(n_axes=1, semantics=None):
    return pltpu.CompilerParams(
        dimension_semantics=semantics or ("arbitrary",) * n_axes,
        vmem_limit_bytes=VMEM_LIMIT)


def _rms(x, g):
    ms = jnp.mean(x * x, axis=-1, keepdims=True)
    return x * lax.rsqrt(ms + EPS) * g


def _full(shape):
    nd = len(shape)
    return pl.BlockSpec(shape, lambda *_: (0,) * nd)


def _rope_table_kernel(pos_ref, invf_ref, mrot_ref, ma_ref, mb_ref, c_ref, sa_ref, sb_ref):
    ang = pos_ref[...] * invf_ref[...]
    c = jnp.cos(ang)
    s = jnp.sin(ang)
    c_ref[...] = jnp.where(mrot_ref[...] > 0, c, 1.0)
    sa_ref[...] = jnp.where(ma_ref[...] > 0, -s, 0.0)
    sb_ref[...] = jnp.where(mb_ref[...] > 0, s, 0.0)


def _rope_tables(pos_col, head_dim, rot):
    t = pos_col.shape[0]
    half = rot // 2
    inv_freq = jnp.float32(ROPE_THETA) ** (-jnp.arange(half, dtype=F32) * 2.0 / rot)
    lane = np.arange(LANES) % head_dim
    mrot = lane < rot
    ma = lane < half
    mb = (lane >= half) & (lane < rot)
    invf = jnp.where(jnp.asarray(mrot), inv_freq[np.where(mrot, lane % half, 0)], 0.0)[None, :]
    row = lambda m: jnp.asarray(m.astype(np.float32))[None, :]
    tm = min(1024, t)
    spec_row = _full((1, LANES))
    out_spec = pl.BlockSpec((tm, LANES), lambda i: (i, 0))
    return pl.pallas_call(
        _rope_table_kernel,
        grid=(t // tm,),
        in_specs=[pl.BlockSpec((tm, 1), lambda i: (i, 0)), spec_row, spec_row, spec_row, spec_row],
        out_specs=[out_spec] * 3,
        out_shape=[jax.ShapeDtypeStruct((t, LANES), F32)] * 3,
        compiler_params=_params(),
        name="rope_tables",
    )(pos_col, invf, row(mrot), row(ma), row(mb))


def _rope(x, c, sa, sb, half):
    parts = []
    for j in range(x.shape[1] // LANES):
        xj = x[:, j * LANES:(j + 1) * LANES]
        parts.append(xj * c + pltpu.roll(xj, LANES - half, 1) * sa + pltpu.roll(xj, half, 1) * sb)
    return parts


def _proj_b_kernel(x_ref, g_ref, w_ref, cb_ref, sab_ref, sbb_ref, ci_ref, sai_ref, sbi_ref,
                   q_ref, k_ref, v_ref, qi_ref, ki_ref, wi_ref):
    h = _rms(x_ref[...], g_ref[...]).astype(BF16)
    p = jnp.dot(h, w_ref[...], preferred_element_type=F32)
    tb = (cb_ref[...], sab_ref[...], sbb_ref[...])
    ti = (ci_ref[...], sai_ref[...], sbi_ref[...])
    o = 0
    qs = _rope(p[:, o:o + B_WIDTH], *tb, B_ROT // 2)
    for j, qj in enumerate(qs):
        q_ref[:, j * LANES:(j + 1) * LANES] = (qj * (B_HEAD_DIM ** -0.5 * LOG2E)).astype(BF16)
    o += B_WIDTH
    for j, kj in enumerate(_rope(p[:, o:o + B_WIDTH], *tb, B_ROT // 2)):
        k_ref[:, j * LANES:(j + 1) * LANES] = kj.astype(BF16)
    o += B_WIDTH
    v_ref[...] = p[:, o:o + B_WIDTH].astype(BF16)
    o += B_WIDTH
    for j, qj in enumerate(_rope(p[:, o:o + IDX_HEADS * IDX_DIM], *ti, IDX_ROT // 2)):
        qi_ref[:, j * LANES:(j + 1) * LANES] = qj.astype(BF16)
    o += IDX_HEADS * IDX_DIM
    ki_ref[...] = _rope(p[:, o:o + LANES], *ti, IDX_ROT // 2)[0].astype(BF16)
    o += LANES
    wi_ref[...] = p[:, o:o + LANES] * ((IDX_HEADS ** -0.5) * (IDX_DIM ** -0.5))


def _proj_b(x2, g, w, tabs_b, tabs_i):
    t = x2.shape[0]
    tm = min(TM_PROJ, t)
    tok = lambda n: pl.BlockSpec((tm, n), lambda i: (i, 0))
    n_out = w.shape[1]
    outs = [(B_WIDTH, BF16), (B_WIDTH, BF16), (B_WIDTH, BF16), (IDX_HEADS * IDX_DIM, BF16),
            (LANES, BF16), (LANES, F32)]
    return pl.pallas_call(
        _proj_b_kernel,
        grid=(t // tm,),
        in_specs=[tok(D_MODEL), _full((1, D_MODEL)), _full((D_MODEL, n_out))] + [tok(LANES)] * 6,
        out_specs=[tok(n) for n, _ in outs],
        out_shape=[jax.ShapeDtypeStruct((t, n), d) for n, d in outs],
        compiler_params=_params(),
        name="proj_b",
    )(x2, g, w, *tabs_b, *tabs_i)


def _ordered_key(score):
    bits = lax.bitcast_convert_type(score, I32)
    key = bits ^ ((bits >> 31) & 0x7FFFFFFF)
    return jnp.where(key == -1, 0, key)


def _ce_desc(vals, i, j):
    a, b = vals[i], vals[j]
    vals[i] = jnp.maximum(a, b)
    vals[j] = jnp.minimum(a, b)


def _bitonic_sort_desc(vals):
    n = len(vals)
    k = 2
    while k <= n:
        j = k // 2
        while j >= 1:
            for i in range(n):
                l = i ^ j
                if l > i:
                    if (i & k) == 0:
                        _ce_desc(vals, i, l)
                    else:
                        _ce_desc(vals, l, i)
            j //= 2
        k *= 2


def _bitonic_merge_desc(vals):
    n = len(vals)
    j = n // 2
    while j >= 1:
        for i in range(n):
            l = i ^ j
            if l > i:
                _ce_desc(vals, i, l)
        j //= 2


def _dsa_kernel(qit_ref, wit_ref, ki_ref, qt_ref, k_ref, vt_ref, o_ref, keys_ref, top_ref, topk_ref, acc_ref,
                bias0_ref, bias1_ref, lg0_ref, lg1_ref,
                *, seq, n_sel, tq, kc, skc, sub, rb):
    i = pl.program_id(1)
    q0 = i * tq
    n_full = q0 // skc
    n_rows = (n_full + 1) * skc
    n_pad = (seq - 1 - q0) - lax.broadcasted_iota(I32, (1, tq), 1)
    n_chunk = (q0 + tq + kc - 1) // kc
    wit = wit_ref[...]

    n_top = skc // SUBLANES
    top_ref[...] = jnp.full((skc, tq), -jnp.inf, F32)

    def score_chunk(c, masked):
        vals = []
        for s in range(skc // sub):
            r0 = pl.multiple_of(c * skc + s * sub, sub)
            kic = ki_ref[pl.ds(r0, sub), :]
            sc = None
            for h in range(IDX_HEADS):
                a = jnp.dot(kic, qit_ref[h * IDX_DIM:(h + 1) * IDX_DIM, :], preferred_element_type=F32)
                term = jnp.maximum(a, 0.0) * wit[h:h + 1, :]
                sc = term if sc is None else sc + term
            if masked:
                qpos = q0 + lax.broadcasted_iota(I32, (sub, tq), 1)
                krow = r0 + lax.broadcasted_iota(I32, (sub, tq), 0)
                sc = jnp.where(krow <= qpos, sc, -jnp.inf)
            keys_ref[pl.ds(r0, sub), :] = _ordered_key(sc)
            vals += [sc[SUBLANES * i:SUBLANES * (i + 1), :] for i in range(sub // SUBLANES)]
        _bitonic_sort_desc(vals)
        merged = [jnp.maximum(top_ref[SUBLANES * i:SUBLANES * (i + 1), :], vals[n_top - 1 - i])
                  for i in range(n_top)]
        _bitonic_merge_desc(merged)
        for i in range(n_top):
            top_ref[SUBLANES * i:SUBLANES * (i + 1), :] = merged[i]

    def score_body(c, _):
        score_chunk(c, False)
        return 0

    lax.fori_loop(0, n_full, score_body, 0)
    score_chunk(n_full, True)

    def search(count):
        def bit_body(b, carry):
            ans_u, = carry
            cand_u = ans_u | jnp.left_shift(jnp.int32(1), 31 - b)
            ok = count(cand_u ^ INT_MIN, False) >= n_sel
            return (jnp.where(ok, cand_u, ans_u),)

        ans_u, = lax.fori_loop(0, 32, bit_body, (jnp.zeros((1, tq), I32),))
        thr = ans_u ^ INT_MIN
        return thr, (n_sel - count(thr, True)).astype(F32)

    def pads(cand, strict):
        return jnp.where((NEG_KEY > cand) if strict else (NEG_KEY >= cand), n_pad, 0)

    topk_ref[...] = _ordered_key(top_ref[...])

    def count_top(cand, strict):
        t = topk_ref[...]
        hit = (t > cand) if strict else (t >= cand)
        return jnp.sum(jnp.where(hit, 1.0, 0.0), axis=0, keepdims=True).astype(I32) + pads(cand, strict)

    def count_all(cand, strict):
        acc_rows = 4 * SUBLANES

        def body(r, acc):
            r0 = pl.multiple_of(r * rb, rb)
            blk = keys_ref[pl.ds(r0, rb), :]
            for j in range(rb // acc_rows):
                part = blk[j * acc_rows:(j + 1) * acc_rows, :]
                hit = (part > cand) if strict else (part >= cand)
                acc = acc + jnp.where(hit, 1, 0)
            return acc

        acc = lax.fori_loop(0, n_rows // rb, body, jnp.zeros((acc_rows, tq), I32))
        return jnp.sum(acc.astype(F32), axis=0, keepdims=True).astype(I32) + pads(cand, strict)

    thr, need = search(count_top)
    last = topk_ref[skc - SUBLANES:skc, :]
    overflow = jnp.max(jnp.where(last > thr, 1.0, 0.0))
    thr, need = lax.cond(overflow > 0.0, lambda: search(count_all), lambda: (thr, need))

    acc_ref[...] = jnp.zeros_like(acc_ref)
    bias_refs = (bias0_ref, bias1_ref)
    lg_refs = (lg0_ref, lg1_ref)
    lower = (lax.broadcasted_iota(I32, (kc, kc), 1) < lax.broadcasted_iota(I32, (kc, kc), 0))
    lower = jnp.where(lower, 1.0, 0.0).astype(BF16)

    def produce(c, slot, tie_base):
        r0 = pl.multiple_of(c * kc, kc)
        key = keys_ref[pl.ds(r0, kc), :]
        tie = key == thr
        tie_f = jnp.where(tie, 1.0, 0.0)
        before = jnp.dot(lower, tie_f.astype(BF16), preferred_element_type=F32)
        sel = (key > thr) | (tie & (before < need - tie_base))
        bias_refs[slot][...] = jnp.where(sel, 0.0, NEG)
        for h in range(B_HEADS):
            hs = slice(h * B_HEAD_DIM, (h + 1) * B_HEAD_DIM)
            lg_refs[slot][h] = jnp.dot(k_ref[pl.ds(r0, kc), hs], qt_ref[hs, :], preferred_element_type=F32)
        return tie_base + before[kc - 1:kc, :] + tie_f[kc - 1:kc, :]

    def consume(c, slot, ms):
        r0 = pl.multiple_of(c * kc, kc)
        bias = bias_refs[slot][...]
        new_ms = []
        for h in range(B_HEADS):
            vs = slice(h * V_ROWS, (h + 1) * V_ROWS)
            lg = lg_refs[slot][h] + bias
            m_new = jnp.maximum(ms[h], jnp.max(lg, axis=0, keepdims=True))
            alpha = jnp.exp2(ms[h] - m_new)
            p = jnp.exp2(lg - m_new).astype(BF16)
            pv = jnp.dot(vt_ref[vs, pl.ds(r0, kc)], p, preferred_element_type=F32)
            acc_ref[vs, :] = acc_ref[vs, :] * alpha + pv
            new_ms.append(m_new)
        return tuple(new_ms)

    def pair_body(j, carry):
        ms, tie_base = carry
        c = 2 * j
        tie_base = produce(c + 1, 1, tie_base)
        ms = consume(c, 0, ms)
        tie_base = produce(c + 2, 0, tie_base)
        ms = consume(c + 1, 1, ms)
        return ms, tie_base

    row = lambda v: jnp.full((1, tq), v, F32)
    ms0 = tuple(row(NEG) for _ in range(B_HEADS))
    n_pairs = (n_chunk - 1) // 2
    ms, tie_base = lax.fori_loop(0, n_pairs, pair_body, (ms0, produce(0, 0, row(0.0))))
    c0 = 2 * n_pairs

    @pl.when(c0 == n_chunk - 1)
    def _():
        consume(c0, 0, ms)

    @pl.when(c0 != n_chunk - 1)
    def _():
        produce(c0 + 1, 1, tie_base)
        consume(c0 + 1, 1, consume(c0, 0, ms))

    for h in range(B_HEADS):
        a = acc_ref[h * V_ROWS:(h + 1) * V_ROWS, :]
        o_ref[h * B_HEAD_DIM:(h + 1) * B_HEAD_DIM, :] = (
            a[:B_HEAD_DIM, :] / a[B_HEAD_DIM:B_HEAD_DIM + 1, :]).astype(o_ref.dtype)


def _dsa(qit, wit, ki, qt, k, vt_aug, n_sel):
    bsz, _, seq = qt.shape
    tq = min(DSA_TQ, seq)
    kc = min(DSA_KC, seq)
    skc = min(DSA_SKC, seq)
    once = pl.Buffered(1)
    kern = functools.partial(_dsa_kernel, seq=seq, n_sel=n_sel, tq=tq, kc=kc, skc=skc,
                             sub=min(DSA_SUB, skc), rb=min(DSA_RB, skc))
    return pl.pallas_call(
        kern,
        grid=(bsz, seq // tq),
        in_specs=[
            pl.BlockSpec((None, IDX_HEADS * IDX_DIM, tq), lambda b, i: (b, 0, i)),
            pl.BlockSpec((None, IDX_HEADS, tq), lambda b, i: (b, 0, i)),
            pl.BlockSpec((None, seq, IDX_DIM), lambda b, i: (b, 0, 0), pipeline_mode=once),
            pl.BlockSpec((None, B_WIDTH, tq), lambda b, i: (b, 0, i)),
            pl.BlockSpec((None, seq, B_WIDTH), lambda b, i: (b, 0, 0), pipeline_mode=once),
            pl.BlockSpec((None, B_HEADS * V_ROWS, seq), lambda b, i: (b, 0, 0), pipeline_mode=once),
        ],
        out_specs=pl.BlockSpec((None, B_WIDTH, tq), lambda b, i: (b, 0, i)),
        out_shape=jax.ShapeDtypeStruct((bsz, B_WIDTH, seq), BF16),
        scratch_shapes=[pltpu.VMEM((seq, tq), I32), pltpu.VMEM((skc, tq), F32), pltpu.VMEM((skc, tq), I32),
                        pltpu.VMEM((B_HEADS * V_ROWS, tq), F32), pltpu.VMEM((kc, tq), F32),
                        pltpu.VMEM((kc, tq), F32), pltpu.VMEM((B_HEADS, kc, tq), F32),
                        pltpu.VMEM((B_HEADS, kc, tq), F32)],
        compiler_params=_params(2),
        name="dsa",
    )(qit, wit, ki, qt, k, vt_aug)


def _proj_a_kernel(x_ref, g_ref, w_ref, ws_ref, bs_ref, ya_ref, *, tm):
    h = _rms(x_ref[...], g_ref[...]).astype(BF16)
    p = jnp.dot(h, w_ref[...], preferred_element_type=F32)
    u = jax.nn.gelu(p[:, :A_WIDTH])
    v = jax.nn.gelu(p[:, A_WIDTH:])
    mu = jnp.mean(v, axis=-1, keepdims=True)
    var = jnp.mean(jnp.square(v - mu), axis=-1, keepdims=True)
    v = ((v - mu) * lax.rsqrt(var + EPS)).astype(BF16)
    causal = (lax.broadcasted_iota(I32, (CHUNK, CHUNK), 1) <= lax.broadcasted_iota(I32, (CHUNK, CHUNK), 0))
    bs = bs_ref[...]
    for gidx in range(A_GROUPS):
        wsm = jnp.where(causal, ws_ref[gidx], 0.0).astype(BF16)
        cs = slice(gidx * A_GROUP_DIM, (gidx + 1) * A_GROUP_DIM)
        for n in range(tm // CHUNK):
            rs = slice(n * CHUNK, (n + 1) * CHUNK)
            vm = jnp.dot(wsm, v[rs, cs], preferred_element_type=F32) + bs[:, gidx:gidx + 1]
            ya_ref[rs, cs] = (u[rs, cs] * vm).astype(ya_ref.dtype)


def _proj_a(x2, g, w, ws, bs_t):
    t = x2.shape[0]
    tm = min(TM_PROJ, t)
    return pl.pallas_call(
        functools.partial(_proj_a_kernel, tm=tm),
        grid=(t // tm,),
        in_specs=[pl.BlockSpec((tm, D_MODEL), lambda i: (i, 0)), _full((1, D_MODEL)),
                  _full((D_MODEL, 2 * A_WIDTH)), _full((A_GROUPS, CHUNK, CHUNK)), _full((CHUNK, A_GROUPS))],
        out_specs=pl.BlockSpec((tm, A_WIDTH), lambda i: (i, 0)),
        out_shape=jax.ShapeDtypeStruct((t, A_WIDTH), BF16),
        compiler_params=_params(),
        name="proj_a",
    )(x2, g, w, ws, bs_t)


def _proj_c_kernel(x_ref, xh_ref, g_ref, w_ref, cw_ref, yc_ref, *, tm, tiles_per_seq):
    g = g_ref[...]
    w = w_ref[...]
    p = jnp.dot(_rms(x_ref[...], g).astype(BF16), w, preferred_element_type=F32)
    ph = jnp.dot(_rms(xh_ref[...], g).astype(BF16), w[:, C_WIDTH:], preferred_element_type=F32)
    first = (pl.program_id(0) % tiles_per_seq) == 0
    zh = ph[:, :C_WIDTH] * ph[:, C_WIDTH:] * jnp.where(first, 0.0, 1.0)
    z = p[:, C_WIDTH:2 * C_WIDTH] * p[:, 2 * C_WIDTH:]
    row = lax.broadcasted_iota(I32, (tm, C_WIDTH), 0)
    z1 = jnp.where(row == 0, zh[SUBLANES - 1:SUBLANES, :], pltpu.roll(z, 1, 0))
    z2 = jnp.where(row == 0, zh[SUBLANES - 2:SUBLANES - 1, :],
                   jnp.where(row == 1, zh[SUBLANES - 1:SUBLANES, :], pltpu.roll(z, 2, 0)))
    cw = cw_ref[...]
    y = cw[0:1, :] * z2 + cw[1:2, :] * z1 + cw[2:3, :] * z
    yc_ref[...] = (p[:, :C_WIDTH] * y).astype(yc_ref.dtype)


def _proj_c(x2, g, w, cw, seq):
    t = x2.shape[0]
    tm = min(TM_PROJ, seq)
    per8 = tm // SUBLANES
    return pl.pallas_call(
        functools.partial(_proj_c_kernel, tm=tm, tiles_per_seq=seq // tm),
        grid=(t // tm,),
        in_specs=[pl.BlockSpec((tm, D_MODEL), lambda i: (i, 0)),
                  pl.BlockSpec((SUBLANES, D_MODEL), lambda i: (jnp.maximum(i * per8 - 1, 0), 0)),
                  _full((1, D_MODEL)), _full((D_MODEL, 3 * C_WIDTH)), _full((CONV_WIDTH, C_WIDTH))],
        out_specs=pl.BlockSpec((tm, C_WIDTH), lambda i: (i, 0)),
        out_shape=jax.ShapeDtypeStruct((t, C_WIDTH), BF16),
        compiler_params=_params(),
        name="proj_c",
    )(x2, x2, g, w, cw)


def _merge_kernel(x_ref, g_ref, wg_ref, ya_ref, yb_ref, yc_ref, wa_ref, wb_ref, wc_ref, wo_ref, o_ref):
    x = x_ref[...]
    h = _rms(x, g_ref[...]).astype(BF16)
    merged = None
    for j, (y_ref, w_ref) in enumerate(((ya_ref, wa_ref), (yb_ref, wb_ref), (yc_ref, wc_ref))):
        gate = jnp.dot(h, wg_ref[:, j * D_MODEL:(j + 1) * D_MODEL], preferred_element_type=F32)
        term = jax.nn.sigmoid(gate) * jnp.dot(y_ref[...], w_ref[...], preferred_element_type=F32)
        merged = term if merged is None else merged + term
    o_ref[...] = x + jnp.dot(merged.astype(BF16), wo_ref[...], preferred_element_type=F32)


def _merge(x2, g, wg, ya, yb, yc, wa, wb, wc, wo):
    t = x2.shape[0]
    tm = min(TM_MERGE, t)
    tok = lambda n: pl.BlockSpec((tm, n), lambda i: (i, 0))
    wproj = _full((A_WIDTH, D_MODEL))
    return pl.pallas_call(
        _merge_kernel,
        grid=(t // tm,),
        in_specs=[tok(D_MODEL), _full((1, D_MODEL)), _full((D_MODEL, N_BRANCH * D_MODEL)),
                  tok(A_WIDTH), tok(B_WIDTH), tok(C_WIDTH), wproj, wproj, wproj, _full((D_MODEL, D_MODEL))],
        out_specs=tok(D_MODEL),
        out_shape=jax.ShapeDtypeStruct((t, D_MODEL), F32),
        compiler_params=_params(),
        name="merge",
    )(x2, g, wg, ya, yb, yc, wa, wb, wc, wo)


def _router_kernel(x_ref, g_ref, w_ref, b_ref, h_ref, eid_ref, wt_ref):
    h = _rms(x_ref[...], g_ref[...])
    h_ref[...] = h
    logits = jnp.dot(h, w_ref[...], preferred_element_type=F32, precision=lax.Precision.HIGHEST) + b_ref[...]
    lane = lax.broadcasted_iota(I32, logits.shape, 1)
    lane_f = lane.astype(F32)
    ninf = -jnp.inf

    def first_max(vals):
        vmax = jnp.max(vals, axis=1, keepdims=True)
        idx = jnp.min(jnp.where(vals == vmax, lane_f, float(LANES)), axis=1, keepdims=True)
        return vmax, idx

    gl = jnp.where(lane < N_GROUPS, logits, ninf)
    gmax, grp = first_max(gl)
    gsum = jnp.sum(jnp.where(lane < N_GROUPS, jnp.exp(logits - gmax), 0.0), axis=1, keepdims=True)
    gw = 1.0 / gsum
    e_lane = lane - N_GROUPS
    in_grp = (e_lane >= 0) & (e_lane < N_EXPERTS) & ((e_lane >> 3).astype(F32) == grp)
    el = jnp.where(in_grp, logits, ninf)
    v1, i1 = first_max(el)
    v2, i2 = first_max(jnp.where(lane_f == i1, ninf, el))
    e2 = jnp.exp(v2 - v1)
    den = 1.0 + e2
    w1 = (1.0 / den) * gw
    w2 = (e2 / den) * gw
    eid1 = (i1 - N_GROUPS).astype(I32)
    eid2 = (i2 - N_GROUPS).astype(I32)
    eid_ref[...] = jnp.where(lane == 0, eid1, jnp.where(lane == 1, eid2, 0))
    wt_ref[...] = jnp.where(lane == 0, w1, jnp.where(lane == 1, w2, 0.0))


def _router(x2, g, w, b):
    t = x2.shape[0]
    tm = min(TM_ROUTE, t)
    tok = lambda n: pl.BlockSpec((tm, n), lambda i: (i, 0))
    return pl.pallas_call(
        _router_kernel,
        grid=(t // tm,),
        in_specs=[tok(D_MODEL), _full((1, D_MODEL)), _full((D_MODEL, LANES)), _full((1, LANES))],
        out_specs=[tok(D_MODEL), tok(LANES), tok(LANES)],
        out_shape=[jax.ShapeDtypeStruct((t, D_MODEL), F32), jax.ShapeDtypeStruct((t, LANES), I32),
                   jax.ShapeDtypeStruct((t, LANES), F32)],
        compiler_params=_params(),
        name="router",
    )(x2, g, w, b)


def _rank_kernel(eid_ref, rank_ref, cnt_ref, *, tm):
    @pl.when(pl.program_id(0) == 0)
    def _():
        cnt_ref[...] = jnp.zeros_like(cnt_ref)

    eid = eid_ref[...]
    lane = lax.broadcasted_iota(I32, (tm, LANES), 1)
    o1 = lane == eid[:, 0:1]
    o2 = lane == eid[:, 1:2]
    both = jnp.where(o1 | o2, 1.0, 0.0)
    lower = lax.broadcasted_iota(I32, (tm, tm), 1) < lax.broadcasted_iota(I32, (tm, tm), 0)
    before = jnp.dot(jnp.where(lower, 1.0, 0.0).astype(BF16), both.astype(BF16),
                     preferred_element_type=F32) + cnt_ref[...]
    r1 = jnp.sum(jnp.where(o1, before, 0.0), axis=1, keepdims=True).astype(I32)
    r2 = jnp.sum(jnp.where(o2, before, 0.0), axis=1, keepdims=True).astype(I32)
    rank_ref[...] = jnp.where(lane == 0, r1, jnp.where(lane == 1, r2, 0))
    cnt_ref[...] = cnt_ref[...] + jnp.sum(both, axis=0, keepdims=True)


def _rank(eid):
    t = eid.shape[0]
    tm = min(TM_RANK, t)
    return pl.pallas_call(
        functools.partial(_rank_kernel, tm=tm),
        grid=(t // tm,),
        in_specs=[pl.BlockSpec((tm, LANES), lambda i: (i, 0))],
        out_specs=[pl.BlockSpec((tm, LANES), lambda i: (i, 0)), _full((1, LANES))],
        out_shape=[jax.ShapeDtypeStruct((t, LANES), I32), jax.ShapeDtypeStruct((1, LANES), F32)],
        compiler_params=_params(),
        name="rank",
    )(eid)


def _dest_kernel(eid_ref, rank_ref, start_ref, dest_ref):
    eid = eid_ref[...]
    lane = lax.broadcasted_iota(I32, eid.shape, 1)
    start = start_ref[...]
    s1 = jnp.sum(jnp.where(lane == eid[:, 0:1], start, 0.0), axis=1, keepdims=True).astype(I32)
    s2 = jnp.sum(jnp.where(lane == eid[:, 1:2], start, 0.0), axis=1, keepdims=True).astype(I32)
    rank = rank_ref[...]
    dest_ref[...] = jnp.where(lane == 0, s1, jnp.where(lane == 1, s2, 0)) + rank


def _dest(eid, rank, start_row):
    t = eid.shape[0]
    tm = min(TM_ROUTE, t)
    tok = pl.BlockSpec((tm, LANES), lambda i: (i, 0))
    return pl.pallas_call(
        _dest_kernel,
        grid=(t // tm,),
        in_specs=[tok, tok, _full((1, LANES))],
        out_specs=tok,
        out_shape=jax.ShapeDtypeStruct((t, LANES), I32),
        compiler_params=_params(),
        name="dest",
    )(eid, rank, start_row)


def _row_copy(src_ref, src_row, dst_ref, dst_row, sem):
    return pltpu.make_async_copy(src_ref.at[pl.ds(src_row, 1)], dst_ref.at[pl.ds(dst_row, 1)], sem)


def _scatter_kernel(dest_ref, h_ref, xs_in_ref, xs_ref, sem, *, tm):
    del xs_in_ref

    def issue(t, _):
        for j in range(2):
            _row_copy(h_ref, t, xs_ref, dest_ref[0, 0, 2 * t + j], sem).start()
        return 0

    lax.fori_loop(0, tm, issue, 0)

    def drain(t, _):
        _row_copy(h_ref, 0, xs_ref, 0, sem).wait()
        return 0

    lax.fori_loop(0, 2 * tm, drain, 0)


def _scatter_rows(dest3, h, n_rows):
    t = h.shape[0]
    tm = dest3.shape[2] // 2
    xs0 = jnp.zeros((n_rows, D_MODEL), F32)
    return pl.pallas_call(
        functools.partial(_scatter_kernel, tm=tm),
        grid=(t // tm,),
        in_specs=[pl.BlockSpec((1, 1, 2 * tm), lambda i: (i, 0, 0), memory_space=pltpu.SMEM),
                  pl.BlockSpec((tm, D_MODEL), lambda i: (i, 0)),
                  pl.BlockSpec(memory_space=pl.ANY)],
        out_specs=pl.BlockSpec(memory_space=pl.ANY),
        out_shape=jax.ShapeDtypeStruct((n_rows, D_MODEL), F32),
        scratch_shapes=[pltpu.SemaphoreType.DMA(())],
        input_output_aliases={2: 0},
        compiler_params=_params(),
        name="scatter_rows",
    )(dest3, h, xs0)


def _expert_kernel(be_ref, nused_ref, xs_ref, wg_ref, wu_ref, wd_ref, y_ref):
    @pl.when(pl.program_id(0) < nused_ref[0])
    def _():
        xb = xs_ref[...].astype(BF16)
        gate = jnp.dot(xb, wg_ref[...], preferred_element_type=F32)
        up = jnp.dot(xb, wu_ref[...], preferred_element_type=F32)
        hid = (jax.nn.silu(gate) * up).astype(BF16)
        y_ref[...] = jnp.dot(hid, wd_ref[...], preferred_element_type=F32)

    @pl.when(pl.program_id(0) >= nused_ref[0])
    def _():
        y_ref[...] = jnp.zeros_like(y_ref)


def _experts(block_exp, n_used, xs, wg, wu, wd):
    n_rows = xs.shape[0]
    grid_spec = pltpu.PrefetchScalarGridSpec(
        num_scalar_prefetch=2,
        grid=(n_rows // EXP_BLOCK,),
        in_specs=[pl.BlockSpec((EXP_BLOCK, D_MODEL), lambda b, be, nu: (b, 0)),
                  pl.BlockSpec((None, D_MODEL, D_FF_EXPERT), lambda b, be, nu: (be[b], 0, 0)),
                  pl.BlockSpec((None, D_MODEL, D_FF_EXPERT), lambda b, be, nu: (be[b], 0, 0)),
                  pl.BlockSpec((None, D_FF_EXPERT, D_MODEL), lambda b, be, nu: (be[b], 0, 0))],
        out_specs=pl.BlockSpec((EXP_BLOCK, D_MODEL), lambda b, be, nu: (b, 0)),
    )
    return pl.pallas_call(
        _expert_kernel,
        grid_spec=grid_spec,
        out_shape=jax.ShapeDtypeStruct((n_rows, D_MODEL), F32),
        compiler_params=_params(),
        name="experts",
    )(block_exp, n_used, xs, wg, wu, wd)


def _combine_kernel(dest_ref, x_ref, wt_ref, g_ref, y_ref, o_ref, buf_ref, sem, *, tm, final_norm):
    def issue(t, _):
        for j in range(2):
            _row_copy(y_ref, dest_ref[0, 0, 2 * t + j], buf_ref.at[j], t, sem).start()
        return 0

    lax.fori_loop(0, tm, issue, 0)

    def drain(t, _):
        _row_copy(y_ref, 0, buf_ref.at[0], 0, sem).wait()
        return 0

    lax.fori_loop(0, 2 * tm, drain, 0)
    wt = wt_ref[...]
    out = x_ref[...] + (wt[:, 0:1] * buf_ref[0] + wt[:, 1:2] * buf_ref[1])
    if final_norm:
        out = _rms(out, g_ref[...])
    o_ref[...] = out


def _combine(dest3, x2, wt, g_final, y, final_norm):
    t = x2.shape[0]
    tm = dest3.shape[2] // 2
    tok = lambda n: pl.BlockSpec((tm, n), lambda i: (i, 0))
    return pl.pallas_call(
        functools.partial(_combine_kernel, tm=tm, final_norm=final_norm),
        grid=(t // tm,),
        in_specs=[pl.BlockSpec((1, 1, 2 * tm), lambda i: (i, 0, 0), memory_space=pltpu.SMEM),
                  tok(D_MODEL), tok(LANES), _full((1, D_MODEL)), pl.BlockSpec(memory_space=pl.ANY)],
        out_specs=tok(D_MODEL),
        out_shape=jax.ShapeDtypeStruct((t, D_MODEL), F32),
        scratch_shapes=[pltpu.VMEM((2, tm, D_MODEL), F32), pltpu.SemaphoreType.DMA(())],
        compiler_params=_params(),
        name="combine",
    )(dest3, x2, wt, g_final, y)


def _split_w_in(w_in_l):
    sizes = [A_WIDTH, A_WIDTH, B_WIDTH, B_WIDTH, B_WIDTH, IDX_HEADS * IDX_DIM, IDX_DIM, IDX_HEADS,
             C_WIDTH, C_WIDTH, C_WIDTH, N_BRANCH * D_MODEL]
    cuts = np.cumsum([0] + sizes)
    u, va, q, k, vb, qi, ki, wi, cb, cc, cx, gates = [w_in_l[:, a:b] for a, b in zip(cuts[:-1], cuts[1:])]
    pad = lambda w: jnp.pad(w, ((0, 0), (0, LANES - w.shape[1])))
    w_a = jnp.concatenate([u, va], axis=1).astype(BF16)
    w_b = jnp.concatenate([q, k, vb, qi, pad(ki), pad(wi)], axis=1).astype(BF16)
    w_c = jnp.concatenate([cb, cc, cx], axis=1).astype(BF16)
    return w_a, w_b, w_c, gates.astype(BF16)


def _token_mixers(x2, g, w_in_l, ws, bs, conv_w, wpa, wpb, wpc, w_out, tabs_b, tabs_i, bsz, seq):
    w_a, w_b, w_c, w_g = _split_w_in(w_in_l)
    n_sel = min(TOPK_MAX, seq // 4)
    q, k, v, qi, ki, wi = _proj_b(x2, g, w_b, tabs_b, tabs_i)
    to_t = lambda a, n: a.reshape(bsz, seq, -1)[:, :, :n].swapaxes(1, 2)
    vt = to_t(v, B_WIDTH).reshape(bsz, B_HEADS, B_HEAD_DIM, seq)
    ones = jnp.ones((bsz, B_HEADS, V_ROWS - B_HEAD_DIM, seq), BF16)
    vt_aug = jnp.concatenate([vt, ones], axis=2).reshape(bsz, B_HEADS * V_ROWS, seq)
    ybt = _dsa(to_t(qi, IDX_HEADS * IDX_DIM), to_t(wi, IDX_HEADS),
               ki.reshape(bsz, seq, LANES)[:, :, :IDX_DIM], to_t(q, B_WIDTH),
               k.reshape(bsz, seq, B_WIDTH), vt_aug, n_sel)
    yb = ybt.swapaxes(1, 2).reshape(bsz * seq, B_WIDTH)
    ya = _proj_a(x2, g, w_a, ws.astype(BF16), bs.T)
    yc = _proj_c(x2, g, w_c, conv_w, seq)
    bf = lambda w: w.astype(BF16)
    return _merge(x2, g, w_g, ya, yb, yc, bf(wpa), bf(wpb), bf(wpc), bf(w_out))


def _hier_moe(x2, g, rg_w, rg_b, re_w, re_b, w_gate, w_up, w_down, g_final, final_norm):
    t = x2.shape[0]
    pad_c = LANES - N_GROUPS - N_EXPERTS
    w_r = jnp.pad(jnp.concatenate([rg_w, re_w], axis=1), ((0, 0), (0, pad_c)))
    b_r = jnp.pad(jnp.concatenate([rg_b, re_b]), (0, pad_c))[None, :]
    h, eid, wt = _router(x2, g, w_r, b_r)
    rank, counts = _rank(eid)
    cnt = counts[0, :N_EXPERTS].astype(I32)
    padded = ((cnt + EXP_BLOCK - 1) // EXP_BLOCK) * EXP_BLOCK
    pend = jnp.cumsum(padded)
    pstart = pend - padded
    n_blocks = (2 * t) // EXP_BLOCK + N_EXPERTS
    n_rows = n_blocks * EXP_BLOCK
    block_row = jnp.arange(n_blocks, dtype=I32) * EXP_BLOCK
    block_exp = jnp.minimum(jnp.sum((pend[None, :] <= block_row[:, None]).astype(I32), axis=1),
                            N_EXPERTS - 1)
    n_used = (pend[-1:] // EXP_BLOCK).astype(I32)
    start_row = jnp.pad(pstart.astype(F32), (0, LANES - N_EXPERTS))[None, :]
    dest = _dest(eid, rank, start_row)
    tm = min(TM_MOVE, t)
    dest3 = dest[:, :2].reshape(t // tm, 1, 2 * tm)
    xs = _scatter_rows(dest3, h, n_rows)
    bf = lambda w: w.astype(BF16)
    y = _experts(block_exp, n_used, xs, bf(w_gate), bf(w_up), bf(w_down))
    return _combine(dest3, x2, wt, g_final, y, final_norm)


def kernel(x, positions, norm_mix_g, norm_ffn_g, norm_final_g, w_in, gmlp_ws, gmlp_b, conv_w, w_proj_a,
           w_proj_b, w_proj_c, w_out, router_group_w, router_group_b, router_expert_w, router_expert_b,
           expert_w_gate, expert_w_up, expert_w_down):
    bsz, seq, d = x.shape
    depth = w_in.shape[0]
    x2 = x.reshape(bsz * seq, d)
    pos_col = positions.astype(F32).reshape(bsz * seq, 1)
    tabs_b = _rope_tables(pos_col, B_HEAD_DIM, B_ROT)
    tabs_i = _rope_tables(pos_col, IDX_DIM, IDX_ROT)
    g_final = norm_final_g[None, :]
    for l in range(depth):
        x2 = _token_mixers(x2, norm_mix_g[l][None, :], w_in[l], gmlp_ws[l], gmlp_b[l], conv_w[l],
                           w_proj_a[l], w_proj_b[l], w_proj_c[l], w_out[l], tabs_b, tabs_i, bsz, seq)
        x2 = _hier_moe(x2, norm_ffn_g[l][None, :], router_group_w[l], router_group_b[l],
                       router_expert_w[l], router_expert_b[l], expert_w_gate[l], expert_w_up[l],
                       expert_w_down[l], g_final, l == depth - 1)
    return x2.reshape(bsz, seq, d)
```

```python
import functools

import numpy as np
import jax
import jax.numpy as jnp
from jax import lax
from jax.experimental import pallas as pl
from jax.experimental.pallas import tpu as pltpu

D_MODEL = 1024
CHUNK = 128
A_GROUPS = 4
A_GROUP_DIM = 128
A_WIDTH = A_GROUPS * A_GROUP_DIM
B_HEADS = 4
B_HEAD_DIM = 128
B_WIDTH = B_HEADS * B_HEAD_DIM
B_ROT = B_HEAD_DIM // 4
IDX_HEADS = 4
IDX_DIM = 64
IDX_ROT = IDX_DIM // 4
TOPK_MAX = 256
C_WIDTH = 512
CONV_WIDTH = 3
N_BRANCH = 3
N_GROUPS = 4
EXP_PER_GROUP = 8
N_EXPERTS = N_GROUPS * EXP_PER_GROUP
D_FF_EXPERT = 512
ROPE_THETA = 500000.0
EPS = 1e-6
NEG = -1e30

LANES = 128
SUBLANES = 8
VMEM_LIMIT = 56 * 1024 * 1024

TM_PROJ = 512
TM_MERGE = 256
TM_ROUTE = 512
TM_RANK = 256
TM_MOVE = 256
EXP_BLOCK = 256
DSA_TQ = 128
DSA_KC = 256
DSA_SKC = 512
DSA_SUB = 256
DSA_RB = 512
V_ROWS = B_HEAD_DIM + 16
LOG2E = 1.4426950408889634

F32 = jnp.float32
BF16 = jnp.bfloat16
I32 = jnp.int32
INT_MIN = -2 ** 31


def _monotone_key_of(value):
    bits = int(np.array(value, np.float32).view(np.int32))
    return bits ^ ((bits >> 31) & 0x7FFFFFFF)


NEG_KEY = _monotone_key_of(NEG)


def _params(n_axes=1, semantics=None):
    return pltpu.CompilerParams(
        dimension_semantics=semantics or ("arbitrary",) * n_axes,
        vmem_limit_bytes=VMEM_LIMIT)


def _rms(x, g):
    ms = jnp.mean(x * x, axis=-1, keepdims=True)
    return x * lax.rsqrt(ms + EPS) * g


def _full(shape):
    nd = len(shape)
    return pl.BlockSpec(shape, lambda *_: (0,) * nd)


def _rope_table_kernel(pos_ref, invf_ref, mrot_ref, ma_ref, mb_ref, c_ref, sa_ref, sb_ref):
    ang = pos_ref[...] * invf_ref[...]
    c = jnp.cos(ang)
    s = jnp.sin(ang)
    c_ref[...] = jnp.where(mrot_ref[...] > 0, c, 1.0)
    sa_ref[...] = jnp.where(ma_ref[...] > 0, -s, 0.0)
    sb_ref[...] = jnp.where(mb_ref[...] > 0, s, 0.0)


def _rope_tables(pos_col, head_dim, rot):
    t = pos_col.shape[0]
    half = rot // 2
    inv_freq = jnp.float32(ROPE_THETA) ** (-jnp.arange(half, dtype=F32) * 2.0 / rot)
    lane = np.arange(LANES) % head_dim
    mrot = lane < rot
    ma = lane < half
    mb = (lane >= half) & (lane < rot)
    invf = jnp.where(jnp.asarray(mrot), inv_freq[np.where(mrot, lane % half, 0)], 0.0)[None, :]
    row = lambda m: jnp.asarray(m.astype(np.float32))[None, :]
    tm = min(1024, t)
    spec_row = _full((1, LANES))
    out_spec = pl.BlockSpec((tm, LANES), lambda i: (i, 0))
    return pl.pallas_call(
        _rope_table_kernel,
        grid=(t // tm,),
        in_specs=[pl.BlockSpec((tm, 1), lambda i: (i, 0)), spec_row, spec_row, spec_row, spec_row],
        out_specs=[out_spec] * 3,
        out_shape=[jax.ShapeDtypeStruct((t, LANES), F32)] * 3,
        compiler_params=_params(),
        name="rope_tables",
    )(pos_col, invf, row(mrot), row(ma), row(mb))


def _rope(x, c, sa, sb, half):
    parts = []
    for j in range(x.shape[1] // LANES):
        xj = x[:, j * LANES:(j + 1) * LANES]
        parts.append(xj * c + pltpu.roll(xj, LANES - half, 1) * sa + pltpu.roll(xj, half, 1) * sb)
    return parts


def _proj_b_kernel(x_ref, g_ref, w_ref, cb_ref, sab_ref, sbb_ref, ci_ref, sai_ref, sbi_ref,
                   q_ref, k_ref, v_ref, qi_ref, ki_ref, wi_ref):
    h = _rms(x_ref[...], g_ref[...]).astype(BF16)
    p = jnp.dot(h, w_ref[...], preferred_element_type=F32)
    tb = (cb_ref[...], sab_ref[...], sbb_ref[...])
    ti = (ci_ref[...], sai_ref[...], sbi_ref[...])
    o = 0
    qs = _rope(p[:, o:o + B_WIDTH], *tb, B_ROT // 2)
    for j, qj in enumerate(qs):
        q_ref[:, j * LANES:(j + 1) * LANES] = (qj * (B_HEAD_DIM ** -0.5 * LOG2E)).astype(BF16)
    o += B_WIDTH
    for j, kj in enumerate(_rope(p[:, o:o + B_WIDTH], *tb, B_ROT // 2)):
        k_ref[:, j * LANES:(j + 1) * LANES] = kj.astype(BF16)
    o += B_WIDTH
    v_ref[...] = p[:, o:o + B_WIDTH].astype(BF16)
    o += B_WIDTH
    for j, qj in enumerate(_rope(p[:, o:o + IDX_HEADS * IDX_DIM], *ti, IDX_ROT // 2)):
        qi_ref[:, j * LANES:(j + 1) * LANES] = qj.astype(BF16)
    o += IDX_HEADS * IDX_DIM
    ki_ref[...] = _rope(p[:, o:o + LANES], *ti, IDX_ROT // 2)[0].astype(BF16)
    o += LANES
    wi_ref[...] = p[:, o:o + LANES] * ((IDX_HEADS ** -0.5) * (IDX_DIM ** -0.5))


def _proj_b(x2, g, w, tabs_b, tabs_i):
    t = x2.shape[0]
    tm = min(TM_PROJ, t)
    tok = lambda n: pl.BlockSpec((tm, n), lambda i: (i, 0))
    n_out = w.shape[1]
    outs = [(B_WIDTH, BF16), (B_WIDTH, BF16), (B_WIDTH, BF16), (IDX_HEADS * IDX_DIM, BF16),
            (LANES, BF16), (LANES, F32)]
    return pl.pallas_call(
        _proj_b_kernel,
        grid=(t // tm,),
        in_specs=[tok(D_MODEL), _full((1, D_MODEL)), _full((D_MODEL, n_out))] + [tok(LANES)] * 6,
        out_specs=[tok(n) for n, _ in outs],
        out_shape=[jax.ShapeDtypeStruct((t, n), d) for n, d in outs],
        compiler_params=_params(),
        name="proj_b",
    )(x2, g, w, *tabs_b, *tabs_i)


def _ordered_key(score):
    bits = lax.bitcast_convert_type(score, I32)
    key = bits ^ ((bits >> 31) & 0x7FFFFFFF)
    return jnp.where(key == -1, 0, key)


def _score_of_key(key):
    return lax.bitcast_convert_type(key ^ ((key >> 31) & 0x7FFFFFFF), F32)


def _ce_desc(vals, i, j):
    a, b = vals[i], vals[j]
    vals[i] = jnp.maximum(a, b)
    vals[j] = jnp.minimum(a, b)


def _bitonic_sort(vals, desc=True):
    n = len(vals)
    k = 2
    while k <= n:
        j = k // 2
        while j >= 1:
            for i in range(n):
                l = i ^ j
                if l > i:
                    if ((i & k) == 0) == desc:
                        _ce_desc(vals, i, l)
                    else:
                        _ce_desc(vals, l, i)
            j //= 2
        k *= 2


def _bitonic_merge_desc(vals):
    n = len(vals)
    j = n // 2
    while j >= 1:
        for i in range(n):
            l = i ^ j
            if l > i:
                _ce_desc(vals, i, l)
        j //= 2


def _dsa_kernel(qit_ref, wit_ref, ki_ref, qt_ref, k_ref, vt_ref, o_ref, sc_ref, top_ref, topk_ref, acc_ref,
                bias0_ref, bias1_ref, lg0_ref, lg1_ref,
                *, seq, n_sel, tq, kc, skc, sub, rb):
    i = pl.program_id(1)
    q0 = i * tq
    n_full = q0 // skc
    n_rows = (n_full + 1) * skc
    n_pad = (seq - 1 - q0) - lax.broadcasted_iota(I32, (1, tq), 1)
    n_chunk = (q0 + tq + kc - 1) // kc
    wit = wit_ref[...]

    n_top = skc // SUBLANES
    top_ref[...] = jnp.full((skc, tq), -jnp.inf, F32)

    def score_batch(c, masked):
        halves = []
        for s in range(skc // sub):
            r0 = pl.multiple_of(c * skc + s * sub, sub)
            kic = ki_ref[pl.ds(r0, sub), :]
            sc = None
            for h in range(IDX_HEADS):
                a = jnp.dot(kic, qit_ref[h * IDX_DIM:(h + 1) * IDX_DIM, :], preferred_element_type=F32)
                term = jnp.maximum(a, 0.0) * wit[h:h + 1, :]
                sc = term if sc is None else sc + term
            if masked:
                qpos = q0 + lax.broadcasted_iota(I32, (sub, tq), 1)
                krow = r0 + lax.broadcasted_iota(I32, (sub, tq), 0)
                sc = jnp.where(krow <= qpos, sc, -jnp.inf)
            sc_ref[pl.ds(r0, sub), :] = sc
            half = [sc[SUBLANES * i:SUBLANES * (i + 1), :] for i in range(sub // SUBLANES)]
            _bitonic_sort(half, desc=(s % 2 == 0))
            halves.append(half)
        vals = halves[0] + halves[1]
        _bitonic_merge_desc(vals)
        return vals

    def keep_top(batches):
        top = [top_ref[SUBLANES * i:SUBLANES * (i + 1), :] for i in range(n_top)]
        for vals in batches:
            top = [jnp.maximum(top[i], vals[n_top - 1 - i]) for i in range(n_top)]
            _bitonic_merge_desc(top)
        for i in range(n_top):
            top_ref[SUBLANES * i:SUBLANES * (i + 1), :] = top[i]

    def score_pair(j, _):
        keep_top([score_batch(2 * j, False), score_batch(2 * j + 1, False)])
        return 0

    lax.fori_loop(0, n_full // 2, score_pair, 0)

    @pl.when(n_full % 2 == 1)
    def _():
        keep_top([score_batch(n_full - 1, False)])

    keep_top([score_batch(n_full, True)])

    def search(count):
        def bit_body(b, carry):
            ans_u, = carry
            cand_u = ans_u | jnp.left_shift(jnp.int32(1), 31 - b)
            ok = count(cand_u ^ INT_MIN, False) >= n_sel
            return (jnp.where(ok, cand_u, ans_u),)

        ans_u, = lax.fori_loop(0, 32, bit_body, (jnp.zeros((1, tq), I32),))
        thr = ans_u ^ INT_MIN
        return thr, (n_sel - count(thr, True)).astype(F32)

    def pads(cand, strict):
        return jnp.where((NEG_KEY > cand) if strict else (NEG_KEY >= cand), n_pad, 0)

    topk_ref[...] = _ordered_key(top_ref[...])

    def count_top(cand, strict):
        t = topk_ref[...]
        hit = (t > cand) if strict else (t >= cand)
        return jnp.sum(jnp.where(hit, 1.0, 0.0), axis=0, keepdims=True).astype(I32) + pads(cand, strict)

    def count_all(cand, strict):
        acc_rows = 4 * SUBLANES
        cand_f = _score_of_key(cand)

        def body(r, acc):
            r0 = pl.multiple_of(r * rb, rb)
            blk = sc_ref[pl.ds(r0, rb), :]
            for j in range(rb // acc_rows):
                part = blk[j * acc_rows:(j + 1) * acc_rows, :]
                hit = (part > cand_f) if strict else (part >= cand_f)
                acc = acc + jnp.where(hit, 1, 0)
            return acc

        acc = lax.fori_loop(0, n_rows // rb, body, jnp.zeros((acc_rows, tq), I32))
        return jnp.sum(acc.astype(F32), axis=0, keepdims=True).astype(I32) + pads(cand, strict)

    thr, need = search(count_top)
    last = topk_ref[skc - SUBLANES:skc, :]
    overflow = jnp.max(jnp.where(last > thr, 1.0, 0.0))
    thr, need = lax.cond(overflow > 0.0, lambda: search(count_all), lambda: (thr, need))
    thr_f = _score_of_key(thr)

    acc_ref[...] = jnp.zeros_like(acc_ref)
    bias_refs = (bias0_ref, bias1_ref)
    lg_refs = (lg0_ref, lg1_ref)
    lower = (lax.broadcasted_iota(I32, (kc, kc), 1) < lax.broadcasted_iota(I32, (kc, kc), 0))
    lower = jnp.where(lower, 1.0, 0.0).astype(BF16)

    def produce(c, slot, tie_base):
        r0 = pl.multiple_of(c * kc, kc)
        sc = sc_ref[pl.ds(r0, kc), :]
        tie = sc == thr_f
        tie_f = jnp.where(tie, 1.0, 0.0)
        before = jnp.dot(lower, tie_f.astype(BF16), preferred_element_type=F32)
        sel = (sc > thr_f) | (tie & (before < need - tie_base))
        bias_refs[slot][...] = jnp.where(sel, 0.0, NEG)
        for h in range(B_HEADS):
            hs = slice(h * B_HEAD_DIM, (h + 1) * B_HEAD_DIM)
            lg_refs[slot][h] = jnp.dot(k_ref[pl.ds(r0, kc), hs], qt_ref[hs, :], preferred_element_type=F32)
        return tie_base + before[kc - 1:kc, :] + tie_f[kc - 1:kc, :]

    def consume(c, slot, ms):
        r0 = pl.multiple_of(c * kc, kc)
        bias = bias_refs[slot][...]
        new_ms = []
        for h in range(B_HEADS):
            vs = slice(h * V_ROWS, (h + 1) * V_ROWS)
            lg = lg_refs[slot][h] + bias
            m_new = jnp.maximum(ms[h], jnp.max(lg, axis=0, keepdims=True))
            alpha = jnp.exp2(ms[h] - m_new)
            p = jnp.exp2(lg - m_new).astype(BF16)
            pv = jnp.dot(vt_ref[vs, pl.ds(r0, kc)], p, preferred_element_type=F32)
            acc_ref[vs, :] = acc_ref[vs, :] * alpha + pv
            new_ms.append(m_new)
        return tuple(new_ms)

    def pair_body(j, carry):
        ms, tie_base = carry
        c = 2 * j
        tie_base = produce(c + 1, 1, tie_base)
        ms = consume(c, 0, ms)
        tie_base = produce(c + 2, 0, tie_base)
        ms = consume(c + 1, 1, ms)
        return ms, tie_base

    row = lambda v: jnp.full((1, tq), v, F32)
    ms0 = tuple(row(NEG) for _ in range(B_HEADS))
    n_pairs = (n_chunk - 1) // 2
    ms, tie_base = lax.fori_loop(0, n_pairs, pair_body, (ms0, produce(0, 0, row(0.0))))
    c0 = 2 * n_pairs

    @pl.when(c0 == n_chunk - 1)
    def _():
        consume(c0, 0, ms)

    @pl.when(c0 != n_chunk - 1)
    def _():
        produce(c0 + 1, 1, tie_base)
        consume(c0 + 1, 1, consume(c0, 0, ms))

    for h in range(B_HEADS):
        a = acc_ref[h * V_ROWS:(h + 1) * V_ROWS, :]
        o_ref[h * B_HEAD_DIM:(h + 1) * B_HEAD_DIM, :] = (
            a[:B_HEAD_DIM, :] / a[B_HEAD_DIM:B_HEAD_DIM + 1, :]).astype(o_ref.dtype)


def _dsa(qit, wit, ki, qt, k, vt_aug, n_sel):
    bsz, _, seq = qt.shape
    tq = min(DSA_TQ, seq)
    kc = min(DSA_KC, seq)
    skc = min(DSA_SKC, seq)
    once = pl.Buffered(1)
    kern = functools.partial(_dsa_kernel, seq=seq, n_sel=n_sel, tq=tq, kc=kc, skc=skc,
                             sub=min(DSA_SUB, skc), rb=min(DSA_RB, skc))
    return pl.pallas_call(
        kern,
        grid=(bsz, seq // tq),
        in_specs=[
            pl.BlockSpec((None, IDX_HEADS * IDX_DIM, tq), lambda b, i: (b, 0, i)),
            pl.BlockSpec((None, IDX_HEADS, tq), lambda b, i: (b, 0, i)),
            pl.BlockSpec((None, seq, IDX_DIM), lambda b, i: (b, 0, 0), pipeline_mode=once),
            pl.BlockSpec((None, B_WIDTH, tq), lambda b, i: (b, 0, i)),
            pl.BlockSpec((None, seq, B_WIDTH), lambda b, i: (b, 0, 0), pipeline_mode=once),
            pl.BlockSpec((None, B_HEADS * V_ROWS, seq), lambda b, i: (b, 0, 0), pipeline_mode=once),
        ],
        out_specs=pl.BlockSpec((None, B_WIDTH, tq), lambda b, i: (b, 0, i)),
        out_shape=jax.ShapeDtypeStruct((bsz, B_WIDTH, seq), BF16),
        scratch_shapes=[pltpu.VMEM((seq, tq), F32), pltpu.VMEM((skc, tq), F32), pltpu.VMEM((skc, tq), I32),
                        pltpu.VMEM((B_HEADS * V_ROWS, tq), F32), pltpu.VMEM((kc, tq), F32),
                        pltpu.VMEM((kc, tq), F32), pltpu.VMEM((B_HEADS, kc, tq), F32),
                        pltpu.VMEM((B_HEADS, kc, tq), F32)],
        compiler_params=_params(2),
        name="dsa",
    )(qit, wit, ki, qt, k, vt_aug)


def _proj_a_kernel(x_ref, g_ref, w_ref, ws_ref, bs_ref, ya_ref, *, tm):
    h = _rms(x_ref[...], g_ref[...]).astype(BF16)
    p = jnp.dot(h, w_ref[...], preferred_element_type=F32)
    u = jax.nn.gelu(p[:, :A_WIDTH])
    v = jax.nn.gelu(p[:, A_WIDTH:])
    mu = jnp.mean(v, axis=-1, keepdims=True)
    var = jnp.mean(jnp.square(v - mu), axis=-1, keepdims=True)
    v = ((v - mu) * lax.rsqrt(var + EPS)).astype(BF16)
    causal = (lax.broadcasted_iota(I32, (CHUNK, CHUNK), 1) <= lax.broadcasted_iota(I32, (CHUNK, CHUNK), 0))
    bs = bs_ref[...]
    for gidx in range(A_GROUPS):
        wsm = jnp.where(causal, ws_ref[gidx], 0.0).astype(BF16)
        cs = slice(gidx * A_GROUP_DIM, (gidx + 1) * A_GROUP_DIM)
        for n in range(tm // CHUNK):
            rs = slice(n * CHUNK, (n + 1) * CHUNK)
            vm = jnp.dot(wsm, v[rs, cs], preferred_element_type=F32) + bs[:, gidx:gidx + 1]
            ya_ref[rs, cs] = (u[rs, cs] * vm).astype(ya_ref.dtype)


def _proj_a(x2, g, w, ws, bs_t):
    t = x2.shape[0]
    tm = min(TM_PROJ, t)
    return pl.pallas_call(
        functools.partial(_proj_a_kernel, tm=tm),
        grid=(t // tm,),
        in_specs=[pl.BlockSpec((tm, D_MODEL), lambda i: (i, 0)), _full((1, D_MODEL)),
                  _full((D_MODEL, 2 * A_WIDTH)), _full((A_GROUPS, CHUNK, CHUNK)), _full((CHUNK, A_GROUPS))],
        out_specs=pl.BlockSpec((tm, A_WIDTH), lambda i: (i, 0)),
        out_shape=jax.ShapeDtypeStruct((t, A_WIDTH), BF16),
        compiler_params=_params(),
        name="proj_a",
    )(x2, g, w, ws, bs_t)


def _proj_c_kernel(x_ref, xh_ref, g_ref, w_ref, cw_ref, yc_ref, *, tm, tiles_per_seq):
    g = g_ref[...]
    w = w_ref[...]
    p = jnp.dot(_rms(x_ref[...], g).astype(BF16), w, preferred_element_type=F32)
    ph = jnp.dot(_rms(xh_ref[...], g).astype(BF16), w[:, C_WIDTH:], preferred_element_type=F32)
    first = (pl.program_id(0) % tiles_per_seq) == 0
    zh = ph[:, :C_WIDTH] * ph[:, C_WIDTH:] * jnp.where(first, 0.0, 1.0)
    z = p[:, C_WIDTH:2 * C_WIDTH] * p[:, 2 * C_WIDTH:]
    row = lax.broadcasted_iota(I32, (tm, C_WIDTH), 0)
    z1 = jnp.where(row == 0, zh[SUBLANES - 1:SUBLANES, :], pltpu.roll(z, 1, 0))
    z2 = jnp.where(row == 0, zh[SUBLANES - 2:SUBLANES - 1, :],
                   jnp.where(row == 1, zh[SUBLANES - 1:SUBLANES, :], pltpu.roll(z, 2, 0)))
    cw = cw_ref[...]
    y = cw[0:1, :] * z2 + cw[1:2, :] * z1 + cw[2:3, :] * z
    yc_ref[...] = (p[:, :C_WIDTH] * y).astype(yc_ref.dtype)


def _proj_c(x2, g, w, cw, seq):
    t = x2.shape[0]
    tm = min(TM_PROJ, seq)
    per8 = tm // SUBLANES
    return pl.pallas_call(
        functools.partial(_proj_c_kernel, tm=tm, tiles_per_seq=seq // tm),
        grid=(t // tm,),
        in_specs=[pl.BlockSpec((tm, D_MODEL), lambda i: (i, 0)),
                  pl.BlockSpec((SUBLANES, D_MODEL), lambda i: (jnp.maximum(i * per8 - 1, 0), 0)),
                  _full((1, D_MODEL)), _full((D_MODEL, 3 * C_WIDTH)), _full((CONV_WIDTH, C_WIDTH))],
        out_specs=pl.BlockSpec((tm, C_WIDTH), lambda i: (i, 0)),
        out_shape=jax.ShapeDtypeStruct((t, C_WIDTH), BF16),
        compiler_params=_params(),
        name="proj_c",
    )(x2, x2, g, w, cw)


def _merge_kernel(x_ref, g_ref, wg_ref, ya_ref, yb_ref, yc_ref, wa_ref, wb_ref, wc_ref, wo_ref, o_ref):
    x = x_ref[...]
    h = _rms(x, g_ref[...]).astype(BF16)
    merged = None
    for j, (y_ref, w_ref) in enumerate(((ya_ref, wa_ref), (yb_ref, wb_ref), (yc_ref, wc_ref))):
        gate = jnp.dot(h, wg_ref[:, j * D_MODEL:(j + 1) * D_MODEL], preferred_element_type=F32)
        term = jax.nn.sigmoid(gate) * jnp.dot(y_ref[...], w_ref[...], preferred_element_type=F32)
        merged = term if merged is None else merged + term
    o_ref[...] = x + jnp.dot(merged.astype(BF16), wo_ref[...], preferred_element_type=F32)


def _merge(x2, g, wg, ya, yb, yc, wa, wb, wc, wo):
    t = x2.shape[0]
    tm = min(TM_MERGE, t)
    tok = lambda n: pl.BlockSpec((tm, n), lambda i: (i, 0))
    wproj = _full((A_WIDTH, D_MODEL))
    return pl.pallas_call(
        _merge_kernel,
        grid=(t // tm,),
        in_specs=[tok(D_MODEL), _full((1, D_MODEL)), _full((D_MODEL, N_BRANCH * D_MODEL)),
                  tok(A_WIDTH), tok(B_WIDTH), tok(C_WIDTH), wproj, wproj, wproj, _full((D_MODEL, D_MODEL))],
        out_specs=tok(D_MODEL),
        out_shape=jax.ShapeDtypeStruct((t, D_MODEL), F32),
        compiler_params=_params(),
        name="merge",
    )(x2, g, wg, ya, yb, yc, wa, wb, wc, wo)


def _router_kernel(x_ref, g_ref, w_ref, b_ref, h_ref, eid_ref, wt_ref):
    h = _rms(x_ref[...], g_ref[...])
    h_ref[...] = h
    logits = jnp.dot(h, w_ref[...], preferred_element_type=F32, precision=lax.Precision.HIGHEST) + b_ref[...]
    lane = lax.broadcasted_iota(I32, logits.shape, 1)
    lane_f = lane.astype(F32)
    ninf = -jnp.inf

    def first_max(vals):
        vmax = jnp.max(vals, axis=1, keepdims=True)
        idx = jnp.min(jnp.where(vals == vmax, lane_f, float(LANES)), axis=1, keepdims=True)
        return vmax, idx

    gl = jnp.where(lane < N_GROUPS, logits, ninf)
    gmax, grp = first_max(gl)
    gsum = jnp.sum(jnp.where(lane < N_GROUPS, jnp.exp(logits - gmax), 0.0), axis=1, keepdims=True)
    gw = 1.0 / gsum
    e_lane = lane - N_GROUPS
    in_grp = (e_lane >= 0) & (e_lane < N_EXPERTS) & ((e_lane >> 3).astype(F32) == grp)
    el = jnp.where(in_grp, logits, ninf)
    v1, i1 = first_max(el)
    v2, i2 = first_max(jnp.where(lane_f == i1, ninf, el))
    e2 = jnp.exp(v2 - v1)
    den = 1.0 + e2
    w1 = (1.0 / den) * gw
    w2 = (e2 / den) * gw
    eid1 = (i1 - N_GROUPS).astype(I32)
    eid2 = (i2 - N_GROUPS).astype(I32)
    eid_ref[...] = jnp.where(lane == 0, eid1, jnp.where(lane == 1, eid2, 0))
    wt_ref[...] = jnp.where(lane == 0, w1, jnp.where(lane == 1, w2, 0.0))


def _router(x2, g, w, b):
    t = x2.shape[0]
    tm = min(TM_ROUTE, t)
    tok = lambda n: pl.BlockSpec((tm, n), lambda i: (i, 0))
    return pl.pallas_call(
        _router_kernel,
        grid=(t // tm,),
        in_specs=[tok(D_MODEL), _full((1, D_MODEL)), _full((D_MODEL, LANES)), _full((1, LANES))],
        out_specs=[tok(D_MODEL), tok(LANES), tok(LANES)],
        out_shape=[jax.ShapeDtypeStruct((t, D_MODEL), F32), jax.ShapeDtypeStruct((t, LANES), I32),
                   jax.ShapeDtypeStruct((t, LANES), F32)],
        compiler_params=_params(),
        name="router",
    )(x2, g, w, b)


def _rank_kernel(eid_ref, rank_ref, cnt_ref, *, tm):
    @pl.when(pl.program_id(0) == 0)
    def _():
        cnt_ref[...] = jnp.zeros_like(cnt_ref)

    eid = eid_ref[...]
    lane = lax.broadcasted_iota(I32, (tm, LANES), 1)
    o1 = lane == eid[:, 0:1]
    o2 = lane == eid[:, 1:2]
    both = jnp.where(o1 | o2, 1.0, 0.0)
    lower = lax.broadcasted_iota(I32, (tm, tm), 1) < lax.broadcasted_iota(I32, (tm, tm), 0)
    before = jnp.dot(jnp.where(lower, 1.0, 0.0).astype(BF16), both.astype(BF16),
                     preferred_element_type=F32) + cnt_ref[...]
    r1 = jnp.sum(jnp.where(o1, before, 0.0), axis=1, keepdims=True).astype(I32)
    r2 = jnp.sum(jnp.where(o2, before, 0.0), axis=1, keepdims=True).astype(I32)
    rank_ref[...] = jnp.where(lane == 0, r1, jnp.where(lane == 1, r2, 0))
    cnt_ref[...] = cnt_ref[...] + jnp.sum(both, axis=0, keepdims=True)


def _rank(eid):
    t = eid.shape[0]
    tm = min(TM_RANK, t)
    return pl.pallas_call(
        functools.partial(_rank_kernel, tm=tm),
        grid=(t // tm,),
        in_specs=[pl.BlockSpec((tm, LANES), lambda i: (i, 0))],
        out_specs=[pl.BlockSpec((tm, LANES), lambda i: (i, 0)), _full((1, LANES))],
        out_shape=[jax.ShapeDtypeStruct((t, LANES), I32), jax.ShapeDtypeStruct((1, LANES), F32)],
        compiler_params=_params(),
        name="rank",
    )(eid)


def _dest_kernel(eid_ref, rank_ref, start_ref, dest_ref):
    eid = eid_ref[...]
    lane = lax.broadcasted_iota(I32, eid.shape, 1)
    start = start_ref[...]
    s1 = jnp.sum(jnp.where(lane == eid[:, 0:1], start, 0.0), axis=1, keepdims=True).astype(I32)
    s2 = jnp.sum(jnp.where(lane == eid[:, 1:2], start, 0.0), axis=1, keepdims=True).astype(I32)
    rank = rank_ref[...]
    dest_ref[...] = jnp.where(lane == 0, s1, jnp.where(lane == 1, s2, 0)) + rank


def _dest(eid, rank, start_row):
    t = eid.shape[0]
    tm = min(TM_ROUTE, t)
    tok = pl.BlockSpec((tm, LANES), lambda i: (i, 0))
    return pl.pallas_call(
        _dest_kernel,
        grid=(t // tm,),
        in_specs=[tok, tok, _full((1, LANES))],
        out_specs=tok,
        out_shape=jax.ShapeDtypeStruct((t, LANES), I32),
        compiler_params=_params(),
        name="dest",
    )(eid, rank, start_row)


def _row_copy(src_ref, src_row, dst_ref, dst_row, sem):
    return pltpu.make_async_copy(src_ref.at[pl.ds(src_row, 1)], dst_ref.at[pl.ds(dst_row, 1)], sem)


def _rows_copy(src_ref, dst_ref, n, sem):
    return pltpu.make_async_copy(src_ref.at[pl.ds(0, n)], dst_ref.at[pl.ds(0, n)], sem)


def _scatter_kernel(dest_ref, h_ref, xs_in_ref, xs_ref, sem, *, tm, steps):
    del xs_in_ref
    i = pl.program_id(0)

    def issue(t, _):
        for j in range(2):
            _row_copy(h_ref, i * tm + t, xs_ref, dest_ref[0, 0, 2 * t + j], sem).start()
        return 0

    lax.fori_loop(0, tm, issue, 0)

    @pl.when(i > 0)
    def _():
        _rows_copy(h_ref, xs_ref, 2 * tm, sem).wait()

    @pl.when(i == steps - 1)
    def _():
        _rows_copy(h_ref, xs_ref, 2 * tm, sem).wait()


def _scatter_rows(dest3, h, n_rows):
    t = h.shape[0]
    tm = dest3.shape[2] // 2
    xs0 = jnp.zeros((n_rows, D_MODEL), F32)
    return pl.pallas_call(
        functools.partial(_scatter_kernel, tm=tm, steps=t // tm),
        grid=(t // tm,),
        in_specs=[pl.BlockSpec((1, 1, 2 * tm), lambda i: (i, 0, 0), memory_space=pltpu.SMEM),
                  pl.BlockSpec(memory_space=pl.ANY),
                  pl.BlockSpec(memory_space=pl.ANY)],
        out_specs=pl.BlockSpec(memory_space=pl.ANY),
        out_shape=jax.ShapeDtypeStruct((n_rows, D_MODEL), F32),
        scratch_shapes=[pltpu.SemaphoreType.DMA(())],
        input_output_aliases={2: 0},
        compiler_params=_params(),
        name="scatter_rows",
    )(dest3, h, xs0)


def _expert_kernel(be_ref, nused_ref, xs_ref, wg_ref, wu_ref, wd_ref, y_ref):
    @pl.when(pl.program_id(0) < nused_ref[0])
    def _():
        xb = xs_ref[...].astype(BF16)
        gate = jnp.dot(xb, wg_ref[...], preferred_element_type=F32)
        up = jnp.dot(xb, wu_ref[...], preferred_element_type=F32)
        hid = (jax.nn.silu(gate) * up).astype(BF16)
        y_ref[...] = jnp.dot(hid, wd_ref[...], preferred_element_type=F32)

    @pl.when(pl.program_id(0) >= nused_ref[0])
    def _():
        y_ref[...] = jnp.zeros_like(y_ref)


def _experts(block_exp, n_used, xs, wg, wu, wd):
    n_rows = xs.shape[0]
    grid_spec = pltpu.PrefetchScalarGridSpec(
        num_scalar_prefetch=2,
        grid=(n_rows // EXP_BLOCK,),
        in_specs=[pl.BlockSpec((EXP_BLOCK, D_MODEL), lambda b, be, nu: (b, 0)),
                  pl.BlockSpec((None, D_MODEL, D_FF_EXPERT), lambda b, be, nu: (be[b], 0, 0)),
                  pl.BlockSpec((None, D_MODEL, D_FF_EXPERT), lambda b, be, nu: (be[b], 0, 0)),
                  pl.BlockSpec((None, D_FF_EXPERT, D_MODEL), lambda b, be, nu: (be[b], 0, 0))],
        out_specs=pl.BlockSpec((EXP_BLOCK, D_MODEL), lambda b, be, nu: (b, 0)),
    )
    return pl.pallas_call(
        _expert_kernel,
        grid_spec=grid_spec,
        out_shape=jax.ShapeDtypeStruct((n_rows, D_MODEL), F32),
        compiler_params=_params(),
        name="experts",
    )(block_exp, n_used, xs, wg, wu, wd)


def _combine_kernel(dest_ref, dest_next_ref, x_ref, wt_ref, g_ref, y_ref, o_ref, buf_ref, sems,
                    *, tm, steps, final_norm):
    i = pl.program_id(0)

    def gather(idx_ref, slot):
        def issue(t, _):
            for j in range(2):
                _row_copy(y_ref, idx_ref[0, 0, 2 * t + j], buf_ref.at[slot], j * tm + t, sems.at[slot]).start()
            return 0

        lax.fori_loop(0, tm, issue, 0)

    @pl.when(i == 0)
    def _():
        gather(dest_ref, 0)

    @pl.when(i + 1 < steps)
    def _():
        gather(dest_next_ref, (i + 1) % 2)

    slot = i % 2
    _rows_copy(y_ref, buf_ref.at[slot], 2 * tm, sems.at[slot]).wait()
    wt = wt_ref[...]
    rows = buf_ref[slot]
    out = x_ref[...] + (wt[:, 0:1] * rows[:tm, :] + wt[:, 1:2] * rows[tm:, :])
    if final_norm:
        out = _rms(out, g_ref[...])
    o_ref[...] = out


def _combine(dest3, x2, wt, g_final, y, final_norm):
    t = x2.shape[0]
    tm = dest3.shape[2] // 2
    steps = t // tm
    tok = lambda n: pl.BlockSpec((tm, n), lambda i: (i, 0))
    idx = lambda f: pl.BlockSpec((1, 1, 2 * tm), f, memory_space=pltpu.SMEM)
    return pl.pallas_call(
        functools.partial(_combine_kernel, tm=tm, steps=steps, final_norm=final_norm),
        grid=(steps,),
        in_specs=[idx(lambda i: (i, 0, 0)), idx(lambda i: (jnp.minimum(i + 1, steps - 1), 0, 0)),
                  tok(D_MODEL), tok(LANES), _full((1, D_MODEL)), pl.BlockSpec(memory_space=pl.ANY)],
        out_specs=tok(D_MODEL),
        out_shape=jax.ShapeDtypeStruct((t, D_MODEL), F32),
        scratch_shapes=[pltpu.VMEM((2, 2 * tm, D_MODEL), F32), pltpu.SemaphoreType.DMA((2,))],
        compiler_params=_params(),
        name="combine",
    )(dest3, dest3, x2, wt, g_final, y)


def _split_w_in(w_in_l):
    sizes = [A_WIDTH, A_WIDTH, B_WIDTH, B_WIDTH, B_WIDTH, IDX_HEADS * IDX_DIM, IDX_DIM, IDX_HEADS,
             C_WIDTH, C_WIDTH, C_WIDTH, N_BRANCH * D_MODEL]
    cuts = np.cumsum([0] + sizes)
    u, va, q, k, vb, qi, ki, wi, cb, cc, cx, gates = [w_in_l[:, a:b] for a, b in zip(cuts[:-1], cuts[1:])]
    pad = lambda w: jnp.pad(w, ((0, 0), (0, LANES - w.shape[1])))
    w_a = jnp.concatenate([u, va], axis=1).astype(BF16)
    w_b = jnp.concatenate([q, k, vb, qi, pad(ki), pad(wi)], axis=1).astype(BF16)
    w_c = jnp.concatenate([cb, cc, cx], axis=1).astype(BF16)
    return w_a, w_b, w_c, gates.astype(BF16)


def _token_mixers(x2, g, w_in_l, ws, bs, conv_w, wpa, wpb, wpc, w_out, tabs_b, tabs_i, bsz, seq):
    w_a, w_b, w_c, w_g = _split_w_in(w_in_l)
    n_sel = min(TOPK_MAX, seq // 4)
    q, k, v, qi, ki, wi = _proj_b(x2, g, w_b, tabs_b, tabs_i)
    to_t = lambda a, n: a.reshape(bsz, seq, -1)[:, :, :n].swapaxes(1, 2)
    vt = to_t(v, B_WIDTH).reshape(bsz, B_HEADS, B_HEAD_DIM, seq)
    ones = jnp.ones((bsz, B_HEADS, V_ROWS - B_HEAD_DIM, seq), BF16)
    vt_aug = jnp.concatenate([vt, ones], axis=2).reshape(bsz, B_HEADS * V_ROWS, seq)
    ybt = _dsa(to_t(qi, IDX_HEADS * IDX_DIM), to_t(wi, IDX_HEADS),
               ki.reshape(bsz, seq, LANES)[:, :, :IDX_DIM], to_t(q, B_WIDTH),
               k.reshape(bsz, seq, B_WIDTH), vt_aug, n_sel)
    yb = ybt.swapaxes(1, 2).reshape(bsz * seq, B_WIDTH)
    ya = _proj_a(x2, g, w_a, ws.astype(BF16), bs.T)
    yc = _proj_c(x2, g, w_c, conv_w, seq)
    bf = lambda w: w.astype(BF16)
    return _merge(x2, g, w_g, ya, yb, yc, bf(wpa), bf(wpb), bf(wpc), bf(w_out))


def _hier_moe(x2, g, rg_w, rg_b, re_w, re_b, w_gate, w_up, w_down, g_final, final_norm):
    t = x2.shape[0]
    pad_c = LANES - N_GROUPS - N_EXPERTS
    w_r = jnp.pad(jnp.concatenate([rg_w, re_w], axis=1), ((0, 0), (0, pad_c)))
    b_r = jnp.pad(jnp.concatenate([rg_b, re_b]), (0, pad_c))[None, :]
    h, eid, wt = _router(x2, g, w_r, b_r)
    rank, counts = _rank(eid)
    cnt = counts[0, :N_EXPERTS].astype(I32)
    padded = ((cnt + EXP_BLOCK - 1) // EXP_BLOCK) * EXP_BLOCK
    pend = jnp.cumsum(padded)
    pstart = pend - padded
    n_blocks = (2 * t) // EXP_BLOCK + N_EXPERTS
    n_rows = n_blocks * EXP_BLOCK
    block_row = jnp.arange(n_blocks, dtype=I32) * EXP_BLOCK
    block_exp = jnp.minimum(jnp.sum((pend[None, :] <= block_row[:, None]).astype(I32), axis=1),
                            N_EXPERTS - 1)
    n_used = (pend[-1:] // EXP_BLOCK).astype(I32)
    start_row = jnp.pad(pstart.astype(F32), (0, LANES - N_EXPERTS))[None, :]
    dest = _dest(eid, rank, start_row)
    tm = min(TM_MOVE, t)
    dest3 = dest[:, :2].reshape(t // tm, 1, 2 * tm)
    xs = _scatter_rows(dest3, h, n_rows)
    bf = lambda w: w.astype(BF16)
    y = _experts(block_exp, n_used, xs, bf(w_gate), bf(w_up), bf(w_down))
    return _combine(dest3, x2, wt, g_final, y, final_norm)


def kernel(x, positions, norm_mix_g, norm_ffn_g, norm_final_g, w_in, gmlp_ws, gmlp_b, conv_w, w_proj_a,
           w_proj_b, w_proj_c, w_out, router_group_w, router_group_b, router_expert_w, router_expert_b,
           expert_w_gate, expert_w_up, expert_w_down):
    bsz, seq, d = x.shape
    depth = w_in.shape[0]
    x2 = x.reshape(bsz * seq, d)
    pos_col = positions.astype(F32).reshape(bsz * seq, 1)
    tabs_b = _rope_tables(pos_col, B_HEAD_DIM, B_ROT)
    tabs_i = _rope_tables(pos_col, IDX_DIM, IDX_ROT)
    g_final = norm_final_g[None, :]
    for l in range(depth):
        x2 = _token_mixers(x2, norm_mix_g[l][None, :], w_in[l], gmlp_ws[l], gmlp_b[l], conv_w[l],
                           w_proj_a[l], w_proj_b[l], w_proj_c[l], w_out[l], tabs_b, tabs_i, bsz, seq)
        x2 = _hier_moe(x2, norm_ffn_g[l][None, :], router_group_w[l], router_group_b[l],
                       router_expert_w[l], router_expert_b[l], expert_w_gate[l], expert_w_up[l],
                       expert_w_down[l], g_final, l == depth - 1)
    return x2.reshape(bsz, seq, d)
```

```python
import functools

import numpy as np
import jax
import jax.numpy as jnp
from jax import lax
from jax.experimental import pallas as pl
from jax.experimental.pallas import tpu as pltpu

D_MODEL = 1024
CHUNK = 128
A_GROUPS = 4
A_GROUP_DIM = 128
A_WIDTH = A_GROUPS * A_GROUP_DIM
B_HEADS = 4
B_HEAD_DIM = 128
B_WIDTH = B_HEADS * B_HEAD_DIM
B_ROT = B_HEAD_DIM // 4
IDX_HEADS = 4
IDX_DIM = 64
IDX_ROT = IDX_DIM // 4
TOPK_MAX = 256
C_WIDTH = 512
CONV_WIDTH = 3
N_BRANCH = 3
N_GROUPS = 4
EXP_PER_GROUP = 8
N_EXPERTS = N_GROUPS * EXP_PER_GROUP
D_FF_EXPERT = 512
ROPE_THETA = 500000.0
EPS = 1e-6
NEG = -1e30

LANES = 128
SUBLANES = 8
VMEM_LIMIT = 56 * 1024 * 1024

TM_PROJ = 512
TM_MERGE = 256
TM_ROUTE = 512
TM_RANK = 256
TM_MOVE = 256
EXP_BLOCK = 256
DSA_TQ = 128
DSA_KC = 256
DSA_SKC = 512
DSA_SUB = 256
DSA_RB = 512
V_ROWS = B_HEAD_DIM + 16
LOG2E = 1.4426950408889634

F32 = jnp.float32
BF16 = jnp.bfloat16
I32 = jnp.int32
INT_MIN = -2 ** 31


def _monotone_key_of(value):
    bits = int(np.array(value, np.float32).view(np.int32))
    return bits ^ ((bits >> 31) & 0x7FFFFFFF)


NEG_KEY = _monotone_key_of(NEG)


def _params(n_axes=1, semantics=None):
    return pltpu.CompilerParams(
        dimension_semantics=semantics or ("arbitrary",) * n_axes,
        vmem_limit_bytes=VMEM_LIMIT)


def _rms(x, g):
    ms = jnp.mean(x * x, axis=-1, keepdims=True)
    return x * lax.rsqrt(ms + EPS) * g


def _full(shape):
    nd = len(shape)
    return pl.BlockSpec(shape, lambda *_: (0,) * nd)


def _rope_table_kernel(pos_ref, invf_ref, mrot_ref, ma_ref, mb_ref, c_ref, sa_ref, sb_ref):
    ang = pos_ref[...] * invf_ref[...]
    c = jnp.cos(ang)
    s = jnp.sin(ang)
    c_ref[...] = jnp.where(mrot_ref[...] > 0, c, 1.0)
    sa_ref[...] = jnp.where(ma_ref[...] > 0, -s, 0.0)
    sb_ref[...] = jnp.where(mb_ref[...] > 0, s, 0.0)


def _rope_tables(pos_col, head_dim, rot):
    t = pos_col.shape[0]
    half = rot // 2
    inv_freq = jnp.float32(ROPE_THETA) ** (-jnp.arange(half, dtype=F32) * 2.0 / rot)
    lane = np.arange(LANES) % head_dim
    mrot = lane < rot
    ma = lane < half
    mb = (lane >= half) & (lane < rot)
    invf = jnp.where(jnp.asarray(mrot), inv_freq[np.where(mrot, lane % half, 0)], 0.0)[None, :]
    row = lambda m: jnp.asarray(m.astype(np.float32))[None, :]
    tm = min(1024, t)
    spec_row = _full((1, LANES))
    out_spec = pl.BlockSpec((tm, LANES), lambda i: (i, 0))
    return pl.pallas_call(
        _rope_table_kernel,
        grid=(t // tm,),
        in_specs=[pl.BlockSpec((tm, 1), lambda i: (i, 0)), spec_row, spec_row, spec_row, spec_row],
        out_specs=[out_spec] * 3,
        out_shape=[jax.ShapeDtypeStruct((t, LANES), F32)] * 3,
        compiler_params=_params(),
        name="rope_tables",
    )(pos_col, invf, row(mrot), row(ma), row(mb))


def _rope(x, c, sa, sb, half):
    parts = []
    for j in range(x.shape[1] // LANES):
        xj = x[:, j * LANES:(j + 1) * LANES]
        parts.append(xj * c + pltpu.roll(xj, LANES - half, 1) * sa + pltpu.roll(xj, half, 1) * sb)
    return parts


def _proj_b_kernel(x_ref, g_ref, w_ref, cb_ref, sab_ref, sbb_ref, ci_ref, sai_ref, sbi_ref,
                   q_ref, k_ref, v_ref, qi_ref, ki_ref, wi_ref):
    h = _rms(x_ref[...], g_ref[...]).astype(BF16)
    p = jnp.dot(h, w_ref[...], preferred_element_type=F32)
    tb = (cb_ref[...], sab_ref[...], sbb_ref[...])
    ti = (ci_ref[...], sai_ref[...], sbi_ref[...])
    o = 0
    qs = _rope(p[:, o:o + B_WIDTH], *tb, B_ROT // 2)
    for j, qj in enumerate(qs):
        q_ref[:, j * LANES:(j + 1) * LANES] = (qj * (B_HEAD_DIM ** -0.5 * LOG2E)).astype(BF16)
    o += B_WIDTH
    for j, kj in enumerate(_rope(p[:, o:o + B_WIDTH], *tb, B_ROT // 2)):
        k_ref[:, j * LANES:(j + 1) * LANES] = kj.astype(BF16)
    o += B_WIDTH
    v_ref[...] = p[:, o:o + B_WIDTH].astype(BF16)
    o += B_WIDTH
    for j, qj in enumerate(_rope(p[:, o:o + IDX_HEADS * IDX_DIM], *ti, IDX_ROT // 2)):
        qi_ref[:, j * LANES:(j + 1) * LANES] = qj.astype(BF16)
    o += IDX_HEADS * IDX_DIM
    ki_ref[...] = _rope(p[:, o:o + LANES], *ti, IDX_ROT // 2)[0].astype(BF16)
    o += LANES
    wi_ref[...] = p[:, o:o + LANES] * ((IDX_HEADS ** -0.5) * (IDX_DIM ** -0.5))


def _proj_b(x2, g, w, tabs_b, tabs_i):
    t = x2.shape[0]
    tm = min(TM_PROJ, t)
    tok = lambda n: pl.BlockSpec((tm, n), lambda i: (i, 0))
    n_out = w.shape[1]
    outs = [(B_WIDTH, BF16), (B_WIDTH, BF16), (B_WIDTH, BF16), (IDX_HEADS * IDX_DIM, BF16),
            (LANES, BF16), (LANES, F32)]
    return pl.pallas_call(
        _proj_b_kernel,
        grid=(t // tm,),
        in_specs=[tok(D_MODEL), _full((1, D_MODEL)), _full((D_MODEL, n_out))] + [tok(LANES)] * 6,
        out_specs=[tok(n) for n, _ in outs],
        out_shape=[jax.ShapeDtypeStruct((t, n), d) for n, d in outs],
        compiler_params=_params(),
        name="proj_b",
    )(x2, g, w, *tabs_b, *tabs_i)


def _ordered_key(score):
    bits = lax.bitcast_convert_type(score, I32)
    key = bits ^ ((bits >> 31) & 0x7FFFFFFF)
    return jnp.where(key == -1, 0, key)


def _score_of_key(key):
    return lax.bitcast_convert_type(key ^ ((key >> 31) & 0x7FFFFFFF), F32)


def _ce_desc(vals, i, j):
    a, b = vals[i], vals[j]
    vals[i] = jnp.maximum(a, b)
    vals[j] = jnp.minimum(a, b)


def _bitonic_sort(vals, desc=True):
    n = len(vals)
    k = 2
    while k <= n:
        j = k // 2
        while j >= 1:
            for i in range(n):
                l = i ^ j
                if l > i:
                    if ((i & k) == 0) == desc:
                        _ce_desc(vals, i, l)
                    else:
                        _ce_desc(vals, l, i)
            j //= 2
        k *= 2


def _bitonic_merge_desc(vals):
    n = len(vals)
    j = n // 2
    while j >= 1:
        for i in range(n):
            l = i ^ j
            if l > i:
                _ce_desc(vals, i, l)
        j //= 2


def _dsa_kernel(qit_ref, wit_ref, ki_ref, qt_ref, k_ref, vt_ref, o_ref, sc_ref, top_ref, topk_ref, acc_ref,
                bias0_ref, bias1_ref, lg0_ref, lg1_ref,
                *, seq, n_sel, tq, kc, skc, sub, rb):
    i = pl.program_id(1)
    q0 = i * tq
    n_full = q0 // skc
    n_rows = (n_full + 1) * skc
    n_pad = (seq - 1 - q0) - lax.broadcasted_iota(I32, (1, tq), 1)
    n_chunk = (q0 + tq + kc - 1) // kc
    wit = wit_ref[...]

    n_top = skc // SUBLANES
    top_ref[...] = jnp.full((skc, tq), -jnp.inf, F32)

    def score_batch(c, masked):
        halves = []
        for s in range(skc // sub):
            r0 = pl.multiple_of(c * skc + s * sub, sub)
            kic = ki_ref[pl.ds(r0, sub), :]
            sc = None
            for h in range(IDX_HEADS):
                a = jnp.dot(kic, qit_ref[h * IDX_DIM:(h + 1) * IDX_DIM, :], preferred_element_type=F32)
                term = jnp.maximum(a, 0.0) * wit[h:h + 1, :]
                sc = term if sc is None else sc + term
            if masked:
                qpos = q0 + lax.broadcasted_iota(I32, (sub, tq), 1)
                krow = r0 + lax.broadcasted_iota(I32, (sub, tq), 0)
                sc = jnp.where(krow <= qpos, sc, -jnp.inf)
            sc_ref[pl.ds(r0, sub), :] = sc
            half = [sc[SUBLANES * i:SUBLANES * (i + 1), :] for i in range(sub // SUBLANES)]
            _bitonic_sort(half, desc=(s % 2 == 0))
            halves.append(half)
        vals = halves[0] + halves[1]
        _bitonic_merge_desc(vals)
        return vals

    def keep_top(batches):
        top = [top_ref[SUBLANES * i:SUBLANES * (i + 1), :] for i in range(n_top)]
        for vals in batches:
            top = [jnp.maximum(top[i], vals[n_top - 1 - i]) for i in range(n_top)]
            _bitonic_merge_desc(top)
        for i in range(n_top):
            top_ref[SUBLANES * i:SUBLANES * (i + 1), :] = top[i]

    def score_pair(j, _):
        keep_top([score_batch(2 * j, False), score_batch(2 * j + 1, False)])
        return 0

    lax.fori_loop(0, n_full // 2, score_pair, 0)

    @pl.when(n_full % 2 == 1)
    def _():
        keep_top([score_batch(n_full - 1, False)])

    keep_top([score_batch(n_full, True)])

    def search(count):
        def bit_body(b, carry):
            ans_u, = carry
            cand_u = ans_u | jnp.left_shift(jnp.int32(1), 31 - b)
            ok = count(cand_u ^ INT_MIN, False) >= n_sel
            return (jnp.where(ok, cand_u, ans_u),)

        ans_u, = lax.fori_loop(0, 32, bit_body, (jnp.zeros((1, tq), I32),))
        thr = ans_u ^ INT_MIN
        return thr, (n_sel - count(thr, True)).astype(F32)

    def pads(cand, strict):
        return jnp.where((NEG_KEY > cand) if strict else (NEG_KEY >= cand), n_pad, 0)

    topk_ref[...] = _ordered_key(top_ref[...])

    def count_top(cand, strict):
        t = topk_ref[...]
        hit = (t > cand) if strict else (t >= cand)
        return jnp.sum(jnp.where(hit, 1.0, 0.0), axis=0, keepdims=True).astype(I32) + pads(cand, strict)

    def search_top():
        def digit_body(b, carry):
            ans_u, = carry
            shift = 30 - 2 * b
            for d in (1, 2, 3):
                cand_u = carry[0] | jnp.left_shift(jnp.int32(d), shift)
                ans_u = jnp.where(count_top(cand_u ^ INT_MIN, False) >= n_sel, cand_u, ans_u)
            return (ans_u,)

        ans_u, = lax.fori_loop(0, 16, digit_body, (jnp.zeros((1, tq), I32),))
        thr = ans_u ^ INT_MIN
        return thr, (n_sel - count_top(thr, True)).astype(F32)

    def count_all(cand, strict):
        acc_rows = 4 * SUBLANES
        cand_f = _score_of_key(cand)

        def body(r, acc):
            r0 = pl.multiple_of(r * rb, rb)
            blk = sc_ref[pl.ds(r0, rb), :]
            for j in range(rb // acc_rows):
                part = blk[j * acc_rows:(j + 1) * acc_rows, :]
                hit = (part > cand_f) if strict else (part >= cand_f)
                acc = acc + jnp.where(hit, 1, 0)
            return acc

        acc = lax.fori_loop(0, n_rows // rb, body, jnp.zeros((acc_rows, tq), I32))
        return jnp.sum(acc.astype(F32), axis=0, keepdims=True).astype(I32) + pads(cand, strict)

    thr, need = search_top()
    last = topk_ref[skc - SUBLANES:skc, :]
    overflow = jnp.max(jnp.where(last > thr, 1.0, 0.0))
    thr, need = lax.cond(overflow > 0.0, lambda: search(count_all), lambda: (thr, need))
    thr_f = _score_of_key(thr)

    acc_ref[...] = jnp.zeros_like(acc_ref)
    bias_refs = (bias0_ref, bias1_ref)
    lg_refs = (lg0_ref, lg1_ref)
    hk = kc // 2
    lower = (lax.broadcasted_iota(I32, (hk, hk), 1) < lax.broadcasted_iota(I32, (hk, hk), 0))
    lower = jnp.where(lower, 1.0, 0.0).astype(BF16)

    def produce(c, slot, tie_base):
        r0 = pl.multiple_of(c * kc, kc)
        sc = sc_ref[pl.ds(r0, kc), :]
        tie = sc == thr_f
        tie_f = jnp.where(tie, 1.0, 0.0)
        tie_b = tie_f.astype(BF16)
        b0 = jnp.dot(lower, tie_b[:hk, :], preferred_element_type=F32)
        n0 = b0[hk - 1:hk, :] + tie_f[hk - 1:hk, :]
        b1 = jnp.dot(lower, tie_b[hk:, :], preferred_element_type=F32) + n0
        before = jnp.concatenate([b0, b1], axis=0)
        sel = (sc > thr_f) | (tie & (before < need - tie_base))
        bias_refs[slot][...] = jnp.where(sel, 0.0, NEG)
        for h in range(B_HEADS):
            hs = slice(h * B_HEAD_DIM, (h + 1) * B_HEAD_DIM)
            lg_refs[slot][h] = jnp.dot(k_ref[pl.ds(r0, kc), hs], qt_ref[hs, :], preferred_element_type=F32)
        return tie_base + b1[hk - 1:hk, :] + tie_f[kc - 1:kc, :]

    def consume(c, slot, ms):
        r0 = pl.multiple_of(c * kc, kc)
        bias = bias_refs[slot][...]
        new_ms = []
        for h in range(B_HEADS):
            vs = slice(h * V_ROWS, (h + 1) * V_ROWS)
            lg = lg_refs[slot][h] + bias
            m_new = jnp.maximum(ms[h], jnp.max(lg, axis=0, keepdims=True))
            alpha = jnp.exp2(ms[h] - m_new)
            p = jnp.exp2(lg - m_new).astype(BF16)
            pv = jnp.dot(vt_ref[vs, pl.ds(r0, kc)], p, preferred_element_type=F32)
            acc_ref[vs, :] = acc_ref[vs, :] * alpha + pv
            new_ms.append(m_new)
        return tuple(new_ms)

    def pair_body(j, carry):
        ms, tie_base = carry
        c = 2 * j
        tie_base = produce(c + 1, 1, tie_base)
        ms = consume(c, 0, ms)
        tie_base = produce(c + 2, 0, tie_base)
        ms = consume(c + 1, 1, ms)
        return ms, tie_base

    row = lambda v: jnp.full((1, tq), v, F32)
    ms0 = tuple(row(NEG) for _ in range(B_HEADS))
    n_pairs = (n_chunk - 1) // 2
    ms, tie_base = lax.fori_loop(0, n_pairs, pair_body, (ms0, produce(0, 0, row(0.0))))
    c0 = 2 * n_pairs

    @pl.when(c0 == n_chunk - 1)
    def _():
        consume(c0, 0, ms)

    @pl.when(c0 != n_chunk - 1)
    def _():
        produce(c0 + 1, 1, tie_base)
        consume(c0 + 1, 1, consume(c0, 0, ms))

    for h in range(B_HEADS):
        a = acc_ref[h * V_ROWS:(h + 1) * V_ROWS, :]
        o_ref[h * B_HEAD_DIM:(h + 1) * B_HEAD_DIM, :] = (
            a[:B_HEAD_DIM, :] / a[B_HEAD_DIM:B_HEAD_DIM + 1, :]).astype(o_ref.dtype)


def _dsa(qit, wit, ki, qt, k, vt_aug, n_sel):
    bsz, _, seq = qt.shape
    tq = min(DSA_TQ, seq)
    kc = min(DSA_KC, seq)
    skc = min(DSA_SKC, seq)
    once = pl.Buffered(1)
    kern = functools.partial(_dsa_kernel, seq=seq, n_sel=n_sel, tq=tq, kc=kc, skc=skc,
                             sub=min(DSA_SUB, skc), rb=min(DSA_RB, skc))
    return pl.pallas_call(
        kern,
        grid=(bsz, seq // tq),
        in_specs=[
            pl.BlockSpec((None, IDX_HEADS * IDX_DIM, tq), lambda b, i: (b, 0, i)),
            pl.BlockSpec((None, IDX_HEADS, tq), lambda b, i: (b, 0, i)),
            pl.BlockSpec((None, seq, IDX_DIM), lambda b, i: (b, 0, 0), pipeline_mode=once),
            pl.BlockSpec((None, B_WIDTH, tq), lambda b, i: (b, 0, i)),
            pl.BlockSpec((None, seq, B_WIDTH), lambda b, i: (b, 0, 0), pipeline_mode=once),
            pl.BlockSpec((None, B_HEADS * V_ROWS, seq), lambda b, i: (b, 0, 0), pipeline_mode=once),
        ],
        out_specs=pl.BlockSpec((None, B_WIDTH, tq), lambda b, i: (b, 0, i)),
        out_shape=jax.ShapeDtypeStruct((bsz, B_WIDTH, seq), BF16),
        scratch_shapes=[pltpu.VMEM((seq, tq), F32), pltpu.VMEM((skc, tq), F32), pltpu.VMEM((skc, tq), I32),
                        pltpu.VMEM((B_HEADS * V_ROWS, tq), F32), pltpu.VMEM((kc, tq), F32),
                        pltpu.VMEM((kc, tq), F32), pltpu.VMEM((B_HEADS, kc, tq), F32),
                        pltpu.VMEM((B_HEADS, kc, tq), F32)],
        compiler_params=_params(2),
        name="dsa",
    )(qit, wit, ki, qt, k, vt_aug)


def _proj_a_kernel(x_ref, g_ref, w_ref, ws_ref, bs_ref, ya_ref, *, tm):
    h = _rms(x_ref[...], g_ref[...]).astype(BF16)
    p = jnp.dot(h, w_ref[...], preferred_element_type=F32)
    u = jax.nn.gelu(p[:, :A_WIDTH])
    v = jax.nn.gelu(p[:, A_WIDTH:])
    mu = jnp.mean(v, axis=-1, keepdims=True)
    var = jnp.mean(jnp.square(v - mu), axis=-1, keepdims=True)
    v = ((v - mu) * lax.rsqrt(var + EPS)).astype(BF16)
    causal = (lax.broadcasted_iota(I32, (CHUNK, CHUNK), 1) <= lax.broadcasted_iota(I32, (CHUNK, CHUNK), 0))
    bs = bs_ref[...]
    for gidx in range(A_GROUPS):
        wsm = jnp.where(causal, ws_ref[gidx], 0.0).astype(BF16)
        cs = slice(gidx * A_GROUP_DIM, (gidx + 1) * A_GROUP_DIM)
        for n in range(tm // CHUNK):
            rs = slice(n * CHUNK, (n + 1) * CHUNK)
            vm = jnp.dot(wsm, v[rs, cs], preferred_element_type=F32) + bs[:, gidx:gidx + 1]
            ya_ref[rs, cs] = (u[rs, cs] * vm).astype(ya_ref.dtype)


def _proj_a(x2, g, w, ws, bs_t):
    t = x2.shape[0]
    tm = min(TM_PROJ, t)
    return pl.pallas_call(
        functools.partial(_proj_a_kernel, tm=tm),
        grid=(t // tm,),
        in_specs=[pl.BlockSpec((tm, D_MODEL), lambda i: (i, 0)), _full((1, D_MODEL)),
                  _full((D_MODEL, 2 * A_WIDTH)), _full((A_GROUPS, CHUNK, CHUNK)), _full((CHUNK, A_GROUPS))],
        out_specs=pl.BlockSpec((tm, A_WIDTH), lambda i: (i, 0)),
        out_shape=jax.ShapeDtypeStruct((t, A_WIDTH), BF16),
        compiler_params=_params(),
        name="proj_a",
    )(x2, g, w, ws, bs_t)


def _proj_c_kernel(x_ref, xh_ref, g_ref, w_ref, cw_ref, yc_ref, *, tm, tiles_per_seq):
    g = g_ref[...]
    w = w_ref[...]
    p = jnp.dot(_rms(x_ref[...], g).astype(BF16), w, preferred_element_type=F32)
    ph = jnp.dot(_rms(xh_ref[...], g).astype(BF16), w[:, C_WIDTH:], preferred_element_type=F32)
    first = (pl.program_id(0) % tiles_per_seq) == 0
    zh = ph[:, :C_WIDTH] * ph[:, C_WIDTH:] * jnp.where(first, 0.0, 1.0)
    z = p[:, C_WIDTH:2 * C_WIDTH] * p[:, 2 * C_WIDTH:]
    row = lax.broadcasted_iota(I32, (tm, C_WIDTH), 0)
    z1 = jnp.where(row == 0, zh[SUBLANES - 1:SUBLANES, :], pltpu.roll(z, 1, 0))
    z2 = jnp.where(row == 0, zh[SUBLANES - 2:SUBLANES - 1, :],
                   jnp.where(row == 1, zh[SUBLANES - 1:SUBLANES, :], pltpu.roll(z, 2, 0)))
    cw = cw_ref[...]
    y = cw[0:1, :] * z2 + cw[1:2, :] * z1 + cw[2:3, :] * z
    yc_ref[...] = (p[:, :C_WIDTH] * y).astype(yc_ref.dtype)


def _proj_c(x2, g, w, cw, seq):
    t = x2.shape[0]
    tm = min(TM_PROJ, seq)
    per8 = tm // SUBLANES
    return pl.pallas_call(
        functools.partial(_proj_c_kernel, tm=tm, tiles_per_seq=seq // tm),
        grid=(t // tm,),
        in_specs=[pl.BlockSpec((tm, D_MODEL), lambda i: (i, 0)),
                  pl.BlockSpec((SUBLANES, D_MODEL), lambda i: (jnp.maximum(i * per8 - 1, 0), 0)),
                  _full((1, D_MODEL)), _full((D_MODEL, 3 * C_WIDTH)), _full((CONV_WIDTH, C_WIDTH))],
        out_specs=pl.BlockSpec((tm, C_WIDTH), lambda i: (i, 0)),
        out_shape=jax.ShapeDtypeStruct((t, C_WIDTH), BF16),
        compiler_params=_params(),
        name="proj_c",
    )(x2, x2, g, w, cw)


def _merge_kernel(x_ref, g_ref, wg_ref, ya_ref, yb_ref, yc_ref, wa_ref, wb_ref, wc_ref, wo_ref, o_ref):
    x = x_ref[...]
    h = _rms(x, g_ref[...]).astype(BF16)
    merged = None
    for j, (y_ref, w_ref) in enumerate(((ya_ref, wa_ref), (yb_ref, wb_ref), (yc_ref, wc_ref))):
        gate = jnp.dot(h, wg_ref[:, j * D_MODEL:(j + 1) * D_MODEL], preferred_element_type=F32)
        term = jax.nn.sigmoid(gate) * jnp.dot(y_ref[...], w_ref[...], preferred_element_type=F32)
        merged = term if merged is None else merged + term
    o_ref[...] = x + jnp.dot(merged.astype(BF16), wo_ref[...], preferred_element_type=F32)


def _merge(x2, g, wg, ya, yb, yc, wa, wb, wc, wo):
    t = x2.shape[0]
    tm = min(TM_MERGE, t)
    tok = lambda n: pl.BlockSpec((tm, n), lambda i: (i, 0))
    wproj = _full((A_WIDTH, D_MODEL))
    return pl.pallas_call(
        _merge_kernel,
        grid=(t // tm,),
        in_specs=[tok(D_MODEL), _full((1, D_MODEL)), _full((D_MODEL, N_BRANCH * D_MODEL)),
                  tok(A_WIDTH), tok(B_WIDTH), tok(C_WIDTH), wproj, wproj, wproj, _full((D_MODEL, D_MODEL))],
        out_specs=tok(D_MODEL),
        out_shape=jax.ShapeDtypeStruct((t, D_MODEL), F32),
        compiler_params=_params(),
        name="merge",
    )(x2, g, wg, ya, yb, yc, wa, wb, wc, wo)


def _router_kernel(x_ref, g_ref, w_ref, b_ref, h_ref, eid_ref, wt_ref):
    h = _rms(x_ref[...], g_ref[...])
    h_ref[...] = h
    logits = jnp.dot(h, w_ref[...], preferred_element_type=F32, precision=lax.Precision.HIGHEST) + b_ref[...]
    lane = lax.broadcasted_iota(I32, logits.shape, 1)
    lane_f = lane.astype(F32)
    ninf = -jnp.inf

    def first_max(vals):
        vmax = jnp.max(vals, axis=1, keepdims=True)
        idx = jnp.min(jnp.where(vals == vmax, lane_f, float(LANES)), axis=1, keepdims=True)
        return vmax, idx

    gl = jnp.where(lane < N_GROUPS, logits, ninf)
    gmax, grp = first_max(gl)
    gsum = jnp.sum(jnp.where(lane < N_GROUPS, jnp.exp(logits - gmax), 0.0), axis=1, keepdims=True)
    gw = 1.0 / gsum
    e_lane = lane - N_GROUPS
    in_grp = (e_lane >= 0) & (e_lane < N_EXPERTS) & ((e_lane >> 3).astype(F32) == grp)
    el = jnp.where(in_grp, logits, ninf)
    v1, i1 = first_max(el)
    v2, i2 = first_max(jnp.where(lane_f == i1, ninf, el))
    e2 = jnp.exp(v2 - v1)
    den = 1.0 + e2
    w1 = (1.0 / den) * gw
    w2 = (e2 / den) * gw
    eid1 = (i1 - N_GROUPS).astype(I32)
    eid2 = (i2 - N_GROUPS).astype(I32)
    eid_ref[...] = jnp.where(lane == 0, eid1, jnp.where(lane == 1, eid2, 0))
    wt_ref[...] = jnp.where(lane == 0, w1, jnp.where(lane == 1, w2, 0.0))


def _router(x2, g, w, b):
    t = x2.shape[0]
    tm = min(TM_ROUTE, t)
    tok = lambda n: pl.BlockSpec((tm, n), lambda i: (i, 0))
    return pl.pallas_call(
        _router_kernel,
        grid=(t // tm,),
        in_specs=[tok(D_MODEL), _full((1, D_MODEL)), _full((D_MODEL, LANES)), _full((1, LANES))],
        out_specs=[tok(D_MODEL), tok(LANES), tok(LANES)],
        out_shape=[jax.ShapeDtypeStruct((t, D_MODEL), F32), jax.ShapeDtypeStruct((t, LANES), I32),
                   jax.ShapeDtypeStruct((t, LANES), F32)],
        compiler_params=_params(),
        name="router",
    )(x2, g, w, b)


def _rank_kernel(eid_ref, rank_ref, cnt_ref, *, tm):
    @pl.when(pl.program_id(0) == 0)
    def _():
        cnt_ref[...] = jnp.zeros_like(cnt_ref)

    eid = eid_ref[...]
    lane = lax.broadcasted_iota(I32, (tm, LANES), 1)
    o1 = lane == eid[:, 0:1]
    o2 = lane == eid[:, 1:2]
    both = jnp.where(o1 | o2, 1.0, 0.0)
    lower = lax.broadcasted_iota(I32, (tm, tm), 1) < lax.broadcasted_iota(I32, (tm, tm), 0)
    before = jnp.dot(jnp.where(lower, 1.0, 0.0).astype(BF16), both.astype(BF16),
                     preferred_element_type=F32) + cnt_ref[...]
    r1 = jnp.sum(jnp.where(o1, before, 0.0), axis=1, keepdims=True).astype(I32)
    r2 = jnp.sum(jnp.where(o2, before, 0.0), axis=1, keepdims=True).astype(I32)
    rank_ref[...] = jnp.where(lane == 0, r1, jnp.where(lane == 1, r2, 0))
    cnt_ref[...] = cnt_ref[...] + jnp.sum(both, axis=0, keepdims=True)


def _rank(eid):
    t = eid.shape[0]
    tm = min(TM_RANK, t)
    return pl.pallas_call(
        functools.partial(_rank_kernel, tm=tm),
        grid=(t // tm,),
        in_specs=[pl.BlockSpec((tm, LANES), lambda i: (i, 0))],
        out_specs=[pl.BlockSpec((tm, LANES), lambda i: (i, 0)), _full((1, LANES))],
        out_shape=[jax.ShapeDtypeStruct((t, LANES), I32), jax.ShapeDtypeStruct((1, LANES), F32)],
        compiler_params=_params(),
        name="rank",
    )(eid)


def _dest_kernel(eid_ref, rank_ref, start_ref, dest_ref):
    eid = eid_ref[...]
    lane = lax.broadcasted_iota(I32, eid.shape, 1)
    start = start_ref[...]
    s1 = jnp.sum(jnp.where(lane == eid[:, 0:1], start, 0.0), axis=1, keepdims=True).astype(I32)
    s2 = jnp.sum(jnp.where(lane == eid[:, 1:2], start, 0.0), axis=1, keepdims=True).astype(I32)
    rank = rank_ref[...]
    dest_ref[...] = jnp.where(lane == 0, s1, jnp.where(lane == 1, s2, 0)) + rank


def _dest(eid, rank, start_row):
    t = eid.shape[0]
    tm = min(TM_ROUTE, t)
    tok = pl.BlockSpec((tm, LANES), lambda i: (i, 0))
    return pl.pallas_call(
        _dest_kernel,
        grid=(t // tm,),
        in_specs=[tok, tok, _full((1, LANES))],
        out_specs=tok,
        out_shape=jax.ShapeDtypeStruct((t, LANES), I32),
        compiler_params=_params(),
        name="dest",
    )(eid, rank, start_row)


def _row_copy(src_ref, src_row, dst_ref, dst_row, sem):
    return pltpu.make_async_copy(src_ref.at[pl.ds(src_row, 1)], dst_ref.at[pl.ds(dst_row, 1)], sem)


def _rows_copy(src_ref, dst_ref, n, sem):
    return pltpu.make_async_copy(src_ref.at[pl.ds(0, n)], dst_ref.at[pl.ds(0, n)], sem)


def _scatter_kernel(dest_ref, h_ref, xs_in_ref, xs_ref, sem, *, tm):
    del xs_in_ref

    def issue(t, _):
        for j in range(2):
            _row_copy(h_ref, t, xs_ref, dest_ref[0, 0, 2 * t + j], sem).start()
        return 0

    lax.fori_loop(0, tm, issue, 0)
    for _ in range(2):
        _rows_copy(h_ref, xs_ref, tm, sem).wait()


def _scatter_rows(dest3, h, n_rows):
    t = h.shape[0]
    tm = dest3.shape[2] // 2
    xs0 = jnp.zeros((n_rows, D_MODEL), F32)
    return pl.pallas_call(
        functools.partial(_scatter_kernel, tm=tm),
        grid=(t // tm,),
        in_specs=[pl.BlockSpec((1, 1, 2 * tm), lambda i: (i, 0, 0), memory_space=pltpu.SMEM),
                  pl.BlockSpec((tm, D_MODEL), lambda i: (i, 0)),
                  pl.BlockSpec(memory_space=pl.ANY)],
        out_specs=pl.BlockSpec(memory_space=pl.ANY),
        out_shape=jax.ShapeDtypeStruct((n_rows, D_MODEL), F32),
        scratch_shapes=[pltpu.SemaphoreType.DMA(())],
        input_output_aliases={2: 0},
        compiler_params=_params(),
        name="scatter_rows",
    )(dest3, h, xs0)


def _expert_kernel(be_ref, nused_ref, xs_ref, wg_ref, wu_ref, wd_ref, y_ref):
    @pl.when(pl.program_id(0) < nused_ref[0])
    def _():
        xb = xs_ref[...].astype(BF16)
        gate = jnp.dot(xb, wg_ref[...], preferred_element_type=F32)
        up = jnp.dot(xb, wu_ref[...], preferred_element_type=F32)
        hid = (jax.nn.silu(gate) * up).astype(BF16)
        y_ref[...] = jnp.dot(hid, wd_ref[...], preferred_element_type=F32)

    @pl.when(pl.program_id(0) >= nused_ref[0])
    def _():
        y_ref[...] = jnp.zeros_like(y_ref)


def _experts(block_exp, n_used, xs, wg, wu, wd):
    n_rows = xs.shape[0]
    grid_spec = pltpu.PrefetchScalarGridSpec(
        num_scalar_prefetch=2,
        grid=(n_rows // EXP_BLOCK,),
        in_specs=[pl.BlockSpec((EXP_BLOCK, D_MODEL), lambda b, be, nu: (b, 0)),
                  pl.BlockSpec((None, D_MODEL, D_FF_EXPERT), lambda b, be, nu: (be[b], 0, 0)),
                  pl.BlockSpec((None, D_MODEL, D_FF_EXPERT), lambda b, be, nu: (be[b], 0, 0)),
                  pl.BlockSpec((None, D_FF_EXPERT, D_MODEL), lambda b, be, nu: (be[b], 0, 0))],
        out_specs=pl.BlockSpec((EXP_BLOCK, D_MODEL), lambda b, be, nu: (b, 0)),
    )
    return pl.pallas_call(
        _expert_kernel,
        grid_spec=grid_spec,
        out_shape=jax.ShapeDtypeStruct((n_rows, D_MODEL), F32),
        compiler_params=_params(),
        name="experts",
    )(block_exp, n_used, xs, wg, wu, wd)


def _combine_kernel(dest_ref, dest_next_ref, x_ref, wt_ref, g_ref, y_ref, o_ref, buf_ref, sems,
                    *, tm, steps, final_norm):
    i = pl.program_id(0)

    def gather(idx_ref, slot):
        def issue(t, _):
            for j in range(2):
                _row_copy(y_ref, idx_ref[0, 0, 2 * t + j], buf_ref.at[slot], j * tm + t, sems.at[slot]).start()
            return 0

        lax.fori_loop(0, tm, issue, 0)

    @pl.when(i == 0)
    def _():
        gather(dest_ref, 0)

    @pl.when(i + 1 < steps)
    def _():
        gather(dest_next_ref, (i + 1) % 2)

    slot = i % 2
    _rows_copy(y_ref, buf_ref.at[slot], 2 * tm, sems.at[slot]).wait()
    wt = wt_ref[...]
    rows = buf_ref[slot]
    out = x_ref[...] + (wt[:, 0:1] * rows[:tm, :] + wt[:, 1:2] * rows[tm:, :])
    if final_norm:
        out = _rms(out, g_ref[...])
    o_ref[...] = out


def _combine(dest3, x2, wt, g_final, y, final_norm):
    t = x2.shape[0]
    tm = dest3.shape[2] // 2
    steps = t // tm
    tok = lambda n: pl.BlockSpec((tm, n), lambda i: (i, 0))
    idx = lambda f: pl.BlockSpec((1, 1, 2 * tm), f, memory_space=pltpu.SMEM)
    return pl.pallas_call(
        functools.partial(_combine_kernel, tm=tm, steps=steps, final_norm=final_norm),
        grid=(steps,),
        in_specs=[idx(lambda i: (i, 0, 0)), idx(lambda i: (jnp.minimum(i + 1, steps - 1), 0, 0)),
                  tok(D_MODEL), tok(LANES), _full((1, D_MODEL)), pl.BlockSpec(memory_space=pl.ANY)],
        out_specs=tok(D_MODEL),
        out_shape=jax.ShapeDtypeStruct((t, D_MODEL), F32),
        scratch_shapes=[pltpu.VMEM((2, 2 * tm, D_MODEL), F32), pltpu.SemaphoreType.DMA((2,))],
        compiler_params=_params(),
        name="combine",
    )(dest3, dest3, x2, wt, g_final, y)


def _split_w_in(w_in_l):
    sizes = [A_WIDTH, A_WIDTH, B_WIDTH, B_WIDTH, B_WIDTH, IDX_HEADS * IDX_DIM, IDX_DIM, IDX_HEADS,
             C_WIDTH, C_WIDTH, C_WIDTH, N_BRANCH * D_MODEL]
    cuts = np.cumsum([0] + sizes)
    u, va, q, k, vb, qi, ki, wi, cb, cc, cx, gates = [w_in_l[:, a:b] for a, b in zip(cuts[:-1], cuts[1:])]
    pad = lambda w: jnp.pad(w, ((0, 0), (0, LANES - w.shape[1])))
    w_a = jnp.concatenate([u, va], axis=1).astype(BF16)
    w_b = jnp.concatenate([q, k, vb, qi, pad(ki), pad(wi)], axis=1).astype(BF16)
    w_c = jnp.concatenate([cb, cc, cx], axis=1).astype(BF16)
    return w_a, w_b, w_c, gates.astype(BF16)


def _token_mixers(x2, g, w_in_l, ws, bs, conv_w, wpa, wpb, wpc, w_out, tabs_b, tabs_i, bsz, seq):
    w_a, w_b, w_c, w_g = _split_w_in(w_in_l)
    n_sel = min(TOPK_MAX, seq // 4)
    q, k, v, qi, ki, wi = _proj_b(x2, g, w_b, tabs_b, tabs_i)
    to_t = lambda a, n: a.reshape(bsz, seq, -1)[:, :, :n].swapaxes(1, 2)
    vt = to_t(v, B_WIDTH).reshape(bsz, B_HEADS, B_HEAD_DIM, seq)
    ones = jnp.ones((bsz, B_HEADS, V_ROWS - B_HEAD_DIM, seq), BF16)
    vt_aug = jnp.concatenate([vt, ones], axis=2).reshape(bsz, B_HEADS * V_ROWS, seq)
    ybt = _dsa(to_t(qi, IDX_HEADS * IDX_DIM), to_t(wi, IDX_HEADS),
               ki.reshape(bsz, seq, LANES)[:, :, :IDX_DIM], to_t(q, B_WIDTH),
               k.reshape(bsz, seq, B_WIDTH), vt_aug, n_sel)
    yb = ybt.swapaxes(1, 2).reshape(bsz * seq, B_WIDTH)
    ya = _proj_a(x2, g, w_a, ws.astype(BF16), bs.T)
    yc = _proj_c(x2, g, w_c, conv_w, seq)
    bf = lambda w: w.astype(BF16)
    return _merge(x2, g, w_g, ya, yb, yc, bf(wpa), bf(wpb), bf(wpc), bf(w_out))


def _hier_moe(x2, g, rg_w, rg_b, re_w, re_b, w_gate, w_up, w_down, g_final, final_norm):
    t = x2.shape[0]
    pad_c = LANES - N_GROUPS - N_EXPERTS
    w_r = jnp.pad(jnp.concatenate([rg_w, re_w], axis=1), ((0, 0), (0, pad_c)))
    b_r = jnp.pad(jnp.concatenate([rg_b, re_b]), (0, pad_c))[None, :]
    h, eid, wt = _router(x2, g, w_r, b_r)
    rank, counts = _rank(eid)
    cnt = counts[0, :N_EXPERTS].astype(I32)
    padded = ((cnt + EXP_BLOCK - 1) // EXP_BLOCK) * EXP_BLOCK
    pend = jnp.cumsum(padded)
    pstart = pend - padded
    n_blocks = (2 * t) // EXP_BLOCK + N_EXPERTS
    n_rows = n_blocks * EXP_BLOCK
    block_row = jnp.arange(n_blocks, dtype=I32) * EXP_BLOCK
    block_exp = jnp.minimum(jnp.sum((pend[None, :] <= block_row[:, None]).astype(I32), axis=1),
                            N_EXPERTS - 1)
    n_used = (pend[-1:] // EXP_BLOCK).astype(I32)
    start_row = jnp.pad(pstart.astype(F32), (0, LANES - N_EXPERTS))[None, :]
    dest = _dest(eid, rank, start_row)
    tm = min(TM_MOVE, t)
    dest3 = dest[:, :2].reshape(t // tm, 1, 2 * tm)
    xs = _scatter_rows(dest3, h, n_rows)
    bf = lambda w: w.astype(BF16)
    y = _experts(block_exp, n_used, xs, bf(w_gate), bf(w_up), bf(w_down))
    return _combine(dest3, x2, wt, g_final, y, final_norm)


def kernel(x, positions, norm_mix_g, norm_ffn_g, norm_final_g, w_in, gmlp_ws, gmlp_b, conv_w, w_proj_a,
           w_proj_b, w_proj_c, w_out, router_group_w, router_group_b, router_expert_w, router_expert_b,
           expert_w_gate, expert_w_up, expert_w_down):
    bsz, seq, d = x.shape
    depth = w_in.shape[0]
    x2 = x.reshape(bsz * seq, d)
    pos_col = positions.astype(F32).reshape(bsz * seq, 1)
    tabs_b = _rope_tables(pos_col, B_HEAD_DIM, B_ROT)
    tabs_i = _rope_tables(pos_col, IDX_DIM, IDX_ROT)
    g_final = norm_final_g[None, :]
    for l in range(depth):
        x2 = _token_mixers(x2, norm_mix_g[l][None, :], w_in[l], gmlp_ws[l], gmlp_b[l], conv_w[l],
                           w_proj_a[l], w_proj_b[l], w_proj_c[l], w_out[l], tabs_b, tabs_i, bsz, seq)
        x2 = _hier_moe(x2, norm_ffn_g[l][None, :], router_group_w[l], router_group_b[l],
                       router_expert_w[l], router_expert_b[l], expert_w_gate[l], expert_w_up[l],
                       expert_w_down[l], g_final, l == depth - 1)
    return x2.reshape(bsz, seq, d)
```

```python
import functools

import numpy as np
import jax
import jax.numpy as jnp
from jax import lax
from jax.experimental import pallas as pl
from jax.experimental.pallas import tpu as pltpu

D_MODEL = 1024
CHUNK = 128
A_GROUPS = 4
A_GROUP_DIM = 128
A_WIDTH = A_GROUPS * A_GROUP_DIM
B_HEADS = 4
B_HEAD_DIM = 128
B_WIDTH = B_HEADS * B_HEAD_DIM
B_ROT = B_HEAD_DIM // 4
IDX_HEADS = 4
IDX_DIM = 64
IDX_ROT = IDX_DIM // 4
TOPK_MAX = 256
C_WIDTH = 512
CONV_WIDTH = 3
N_BRANCH = 3
N_GROUPS = 4
EXP_PER_GROUP = 8
N_EXPERTS = N_GROUPS * EXP_PER_GROUP
D_FF_EXPERT = 512
ROPE_THETA = 500000.0
EPS = 1e-6
NEG = -1e30

LANES = 128
SUBLANES = 8
VMEM_LIMIT = 56 * 1024 * 1024

TM_PROJ = 512
TM_MERGE = 256
TM_ROUTE = 512
TM_RANK = 256
TM_MOVE = 256
EXP_BLOCK = 256
DSA_TQ = 128
DSA_KC = 256
DSA_SKC = 512
DSA_SUB = 256
DSA_RB = 512
V_ROWS = B_HEAD_DIM + 16
LOG2E = 1.4426950408889634

F32 = jnp.float32
BF16 = jnp.bfloat16
I32 = jnp.int32
INT_MIN = -2 ** 31


def _monotone_key_of(value):
    bits = int(np.array(value, np.float32).view(np.int32))
    return bits ^ ((bits >> 31) & 0x7FFFFFFF)


NEG_KEY = _monotone_key_of(NEG)


def _params(n_axes=1, semantics=None):
    return pltpu.CompilerParams(
        dimension_semantics=semantics or ("arbitrary",) * n_axes,
        vmem_limit_bytes=VMEM_LIMIT)


def _rms(x, g):
    ms = jnp.mean(x * x, axis=-1, keepdims=True)
    return x * lax.rsqrt(ms + EPS) * g


def _full(shape):
    nd = len(shape)
    return pl.BlockSpec(shape, lambda *_: (0,) * nd)


def _rope_table_kernel(pos_ref, invf_ref, mrot_ref, ma_ref, mb_ref, c_ref, sa_ref, sb_ref):
    ang = pos_ref[...] * invf_ref[...]
    c = jnp.cos(ang)
    s = jnp.sin(ang)
    c_ref[...] = jnp.where(mrot_ref[...] > 0, c, 1.0)
    sa_ref[...] = jnp.where(ma_ref[...] > 0, -s, 0.0)
    sb_ref[...] = jnp.where(mb_ref[...] > 0, s, 0.0)


def _rope_tables(pos_col, head_dim, rot):
    t = pos_col.shape[0]
    half = rot // 2
    inv_freq = jnp.float32(ROPE_THETA) ** (-jnp.arange(half, dtype=F32) * 2.0 / rot)
    lane = np.arange(LANES) % head_dim
    mrot = lane < rot
    ma = lane < half
    mb = (lane >= half) & (lane < rot)
    invf = jnp.where(jnp.asarray(mrot), inv_freq[np.where(mrot, lane % half, 0)], 0.0)[None, :]
    row = lambda m: jnp.asarray(m.astype(np.float32))[None, :]
    tm = min(1024, t)
    spec_row = _full((1, LANES))
    out_spec = pl.BlockSpec((tm, LANES), lambda i: (i, 0))
    return pl.pallas_call(
        _rope_table_kernel,
        grid=(t // tm,),
        in_specs=[pl.BlockSpec((tm, 1), lambda i: (i, 0)), spec_row, spec_row, spec_row, spec_row],
        out_specs=[out_spec] * 3,
        out_shape=[jax.ShapeDtypeStruct((t, LANES), F32)] * 3,
        compiler_params=_params(),
        name="rope_tables",
    )(pos_col, invf, row(mrot), row(ma), row(mb))


def _rope_table_t_kernel(pos_ref, invf_ref, c_ref, s_ref):
    ang = invf_ref[...] * pos_ref[...]
    c_ref[...] = jnp.cos(ang)
    s_ref[...] = jnp.sin(ang)


def _rope_tables_t(pos_row, rot):
    t = pos_row.shape[1]
    half = rot // 2
    inv_freq = jnp.float32(ROPE_THETA) ** (-jnp.arange(half, dtype=F32) * 2.0 / rot)
    tm = min(2048, t)
    out_spec = pl.BlockSpec((half, tm), lambda i: (0, i))
    return pl.pallas_call(
        _rope_table_t_kernel,
        grid=(t // tm,),
        in_specs=[pl.BlockSpec((1, tm), lambda i: (0, i)), _full((half, 1))],
        out_specs=[out_spec] * 2,
        out_shape=[jax.ShapeDtypeStruct((half, t), F32)] * 2,
        compiler_params=_params(),
        name="rope_tables_t",
    )(pos_row, inv_freq[:, None])


def _rope_t(x, c, s):
    half = c.shape[0]
    x1, x2 = x[:half, :], x[half:2 * half, :]
    return jnp.concatenate([x1 * c - x2 * s, x1 * s + x2 * c, x[2 * half:, :]], axis=0)


def _rope(x, c, sa, sb, half):
    parts = []
    for j in range(x.shape[1] // LANES):
        xj = x[:, j * LANES:(j + 1) * LANES]
        parts.append(xj * c + pltpu.roll(xj, LANES - half, 1) * sa + pltpu.roll(xj, half, 1) * sb)
    return parts


WI_ROWS = 16


def _proj_b_kernel(x_ref, g_ref, w_ref, wt_ref, cb_ref, sab_ref, sbb_ref, ci_ref, sai_ref, sbi_ref,
                   cbt_ref, sbt_ref, cit_ref, sit_ref, k_ref, ki_ref, qt_ref, vt_ref, qit_ref, wit_ref):
    h = _rms(x_ref[...], g_ref[...]).astype(BF16)
    tm = h.shape[0]
    p = jnp.dot(h, w_ref[...], preferred_element_type=F32)
    pt = lax.dot_general(wt_ref[...], h, (((1,), (1,)), ((), ())), preferred_element_type=F32)
    tb = (cb_ref[...], sab_ref[...], sbb_ref[...])
    ti = (ci_ref[...], sai_ref[...], sbi_ref[...])
    for j, kj in enumerate(_rope(p[:, :B_WIDTH], *tb, B_ROT // 2)):
        k_ref[:, j * LANES:(j + 1) * LANES] = kj.astype(BF16)
    ki_ref[...] = _rope(p[:, B_WIDTH:B_WIDTH + LANES], *ti, IDX_ROT // 2)[0].astype(BF16)
    cbt, sbt = cbt_ref[...], sbt_ref[...]
    for hh in range(B_HEADS):
        hs = slice(hh * B_HEAD_DIM, (hh + 1) * B_HEAD_DIM)
        qt_ref[hs, :] = (_rope_t(pt[hs, :], cbt, sbt) * (B_HEAD_DIM ** -0.5 * LOG2E)).astype(BF16)
    o = B_WIDTH
    for hh in range(B_HEADS):
        vt_ref[hh * V_ROWS:hh * V_ROWS + B_HEAD_DIM, :] = (
            pt[o + hh * B_HEAD_DIM:o + (hh + 1) * B_HEAD_DIM, :].astype(BF16))
        vt_ref[hh * V_ROWS + B_HEAD_DIM:(hh + 1) * V_ROWS, :] = jnp.ones((V_ROWS - B_HEAD_DIM, tm), BF16)
    o += B_WIDTH
    cit, sit = cit_ref[...], sit_ref[...]
    for hh in range(IDX_HEADS):
        hs = slice(hh * IDX_DIM, (hh + 1) * IDX_DIM)
        qit_ref[hs, :] = _rope_t(pt[o + hh * IDX_DIM:o + (hh + 1) * IDX_DIM, :], cit, sit).astype(BF16)
    o += IDX_HEADS * IDX_DIM
    wit_ref[...] = pt[o:o + SUBLANES, :] * ((IDX_HEADS ** -0.5) * (IDX_DIM ** -0.5))


def _proj_b(x2, g, w, wt, tabs_b, tabs_i, tabs_bt, tabs_it):
    t = x2.shape[0]
    tm = min(TM_PROJ, t)
    tok = lambda n: pl.BlockSpec((tm, n), lambda i: (i, 0))
    feat = lambda n: pl.BlockSpec((n, tm), lambda i: (0, i))
    outs_tok = [(B_WIDTH, BF16), (LANES, BF16)]
    outs_feat = [(B_WIDTH, BF16), (B_HEADS * V_ROWS, BF16), (IDX_HEADS * IDX_DIM, BF16), (SUBLANES, F32)]
    return pl.pallas_call(
        _proj_b_kernel,
        grid=(t // tm,),
        in_specs=[tok(D_MODEL), _full((1, D_MODEL)), _full(w.shape), _full(wt.shape)] + [tok(LANES)] * 6
                 + [feat(B_ROT // 2)] * 2 + [feat(IDX_ROT // 2)] * 2,
        out_specs=[tok(n) for n, _ in outs_tok] + [feat(n) for n, _ in outs_feat],
        out_shape=[jax.ShapeDtypeStruct((t, n), d) for n, d in outs_tok]
                  + [jax.ShapeDtypeStruct((n, t), d) for n, d in outs_feat],
        compiler_params=_params(),
        name="proj_b",
    )(x2, g, w, wt, *tabs_b, *tabs_i, *tabs_bt, *tabs_it)


def _ordered_key(score):
    bits = lax.bitcast_convert_type(score, I32)
    key = bits ^ ((bits >> 31) & 0x7FFFFFFF)
    return jnp.where(key == -1, 0, key)


def _score_of_key(key):
    return lax.bitcast_convert_type(key ^ ((key >> 31) & 0x7FFFFFFF), F32)


def _ce_desc(vals, i, j):
    a, b = vals[i], vals[j]
    vals[i] = jnp.maximum(a, b)
    vals[j] = jnp.minimum(a, b)


def _bitonic_sort(vals, desc=True):
    n = len(vals)
    k = 2
    while k <= n:
        j = k // 2
        while j >= 1:
            for i in range(n):
                l = i ^ j
                if l > i:
                    if ((i & k) == 0) == desc:
                        _ce_desc(vals, i, l)
                    else:
                        _ce_desc(vals, l, i)
            j //= 2
        k *= 2


def _bitonic_merge_desc(vals):
    n = len(vals)
    j = n // 2
    while j >= 1:
        for i in range(n):
            l = i ^ j
            if l > i:
                _ce_desc(vals, i, l)
        j //= 2


def _dsa_kernel(qit_ref, wit_ref, ki_ref, qt_ref, k_ref, vt_ref, o_ref, sc_ref, top_ref, topk_ref, acc_ref,
                bias0_ref, bias1_ref, lg0_ref, lg1_ref,
                *, seq, n_sel, tq, kc, skc, sub, rb):
    i = pl.program_id(1)
    q0 = i * tq
    n_full = q0 // skc
    n_rows = (n_full + 1) * skc
    n_pad = (seq - 1 - q0) - lax.broadcasted_iota(I32, (1, tq), 1)
    n_chunk = (q0 + tq + kc - 1) // kc
    wit = wit_ref[...]

    n_top = skc // SUBLANES
    top_ref[...] = jnp.full((skc, tq), -jnp.inf, F32)

    def score_batch(c, masked):
        halves = []
        for s in range(skc // sub):
            r0 = pl.multiple_of(c * skc + s * sub, sub)
            kic = ki_ref[pl.ds(r0, sub), :IDX_DIM]
            sc = None
            for h in range(IDX_HEADS):
                a = jnp.dot(kic, qit_ref[h * IDX_DIM:(h + 1) * IDX_DIM, :], preferred_element_type=F32)
                term = jnp.maximum(a, 0.0) * wit[h:h + 1, :]
                sc = term if sc is None else sc + term
            if masked:
                qpos = q0 + lax.broadcasted_iota(I32, (sub, tq), 1)
                krow = r0 + lax.broadcasted_iota(I32, (sub, tq), 0)
                sc = jnp.where(krow <= qpos, sc, -jnp.inf)
            sc_ref[pl.ds(r0, sub), :] = sc
            half = [sc[SUBLANES * i:SUBLANES * (i + 1), :] for i in range(sub // SUBLANES)]
            _bitonic_sort(half, desc=(s % 2 == 0))
            halves.append(half)
        vals = halves[0] + halves[1]
        _bitonic_merge_desc(vals)
        return vals

    def keep_top(batches):
        top = [top_ref[SUBLANES * i:SUBLANES * (i + 1), :] for i in range(n_top)]
        for vals in batches:
            top = [jnp.maximum(top[i], vals[n_top - 1 - i]) for i in range(n_top)]
            _bitonic_merge_desc(top)
        for i in range(n_top):
            top_ref[SUBLANES * i:SUBLANES * (i + 1), :] = top[i]

    def score_pair(j, _):
        keep_top([score_batch(2 * j, False), score_batch(2 * j + 1, False)])
        return 0

    lax.fori_loop(0, n_full // 2, score_pair, 0)

    @pl.when(n_full % 2 == 1)
    def _():
        keep_top([score_batch(n_full - 1, False)])

    keep_top([score_batch(n_full, True)])

    def search(count):
        def bit_body(b, carry):
            ans_u, = carry
            cand_u = ans_u | jnp.left_shift(jnp.int32(1), 31 - b)
            ok = count(cand_u ^ INT_MIN, False) >= n_sel
            return (jnp.where(ok, cand_u, ans_u),)

        ans_u, = lax.fori_loop(0, 32, bit_body, (jnp.zeros((1, tq), I32),))
        thr = ans_u ^ INT_MIN
        return thr, (n_sel - count(thr, True)).astype(F32)

    def pads(cand, strict):
        return jnp.where((NEG_KEY > cand) if strict else (NEG_KEY >= cand), n_pad, 0)

    topk_ref[...] = _ordered_key(top_ref[...])

    def count_top(cand, strict):
        t = topk_ref[...]
        hit = (t > cand) if strict else (t >= cand)
        return jnp.sum(jnp.where(hit, 1.0, 0.0), axis=0, keepdims=True).astype(I32) + pads(cand, strict)

    def search_top():
        def digit_body(b, carry):
            ans_u, = carry
            shift = 30 - 2 * b
            for d in (1, 2, 3):
                cand_u = carry[0] | jnp.left_shift(jnp.int32(d), shift)
                ans_u = jnp.where(count_top(cand_u ^ INT_MIN, False) >= n_sel, cand_u, ans_u)
            return (ans_u,)

        ans_u, = lax.fori_loop(0, 16, digit_body, (jnp.zeros((1, tq), I32),))
        thr = ans_u ^ INT_MIN
        return thr, (n_sel - count_top(thr, True)).astype(F32)

    def count_all(cand, strict):
        acc_rows = 4 * SUBLANES
        cand_f = _score_of_key(cand)

        def body(r, acc):
            r0 = pl.multiple_of(r * rb, rb)
            blk = sc_ref[pl.ds(r0, rb), :]
            for j in range(rb // acc_rows):
                part = blk[j * acc_rows:(j + 1) * acc_rows, :]
                hit = (part > cand_f) if strict else (part >= cand_f)
                acc = acc + jnp.where(hit, 1, 0)
            return acc

        acc = lax.fori_loop(0, n_rows // rb, body, jnp.zeros((acc_rows, tq), I32))
        return jnp.sum(acc.astype(F32), axis=0, keepdims=True).astype(I32) + pads(cand, strict)

    thr, need = search_top()
    last = topk_ref[skc - SUBLANES:skc, :]
    overflow = jnp.max(jnp.where(last > thr, 1.0, 0.0))
    thr, need = lax.cond(overflow > 0.0, lambda: search(count_all), lambda: (thr, need))
    thr_f = _score_of_key(thr)

    acc_ref[...] = jnp.zeros_like(acc_ref)
    bias_refs = (bias0_ref, bias1_ref)
    lg_refs = (lg0_ref, lg1_ref)
    hk = kc // 2
    lower = (lax.broadcasted_iota(I32, (hk, hk), 1) < lax.broadcasted_iota(I32, (hk, hk), 0))
    lower = jnp.where(lower, 1.0, 0.0).astype(BF16)

    def produce(c, slot, tie_base):
        r0 = pl.multiple_of(c * kc, kc)
        sc = sc_ref[pl.ds(r0, kc), :]
        tie = sc == thr_f
        tie_f = jnp.where(tie, 1.0, 0.0)
        tie_b = tie_f.astype(BF16)
        b0 = jnp.dot(lower, tie_b[:hk, :], preferred_element_type=F32)
        n0 = b0[hk - 1:hk, :] + tie_f[hk - 1:hk, :]
        b1 = jnp.dot(lower, tie_b[hk:, :], preferred_element_type=F32) + n0
        before = jnp.concatenate([b0, b1], axis=0)
        sel = (sc > thr_f) | (tie & (before < need - tie_base))
        bias_refs[slot][...] = jnp.where(sel, 0.0, NEG)
        for h in range(B_HEADS):
            hs = slice(h * B_HEAD_DIM, (h + 1) * B_HEAD_DIM)
            lg_refs[slot][h] = jnp.dot(k_ref[pl.ds(r0, kc), hs], qt_ref[hs, :], preferred_element_type=F32)
        return tie_base + b1[hk - 1:hk, :] + tie_f[kc - 1:kc, :]

    def consume(c, slot, ms):
        r0 = pl.multiple_of(c * kc, kc)
        bias = bias_refs[slot][...]
        new_ms = []
        for h in range(B_HEADS):
            vs = slice(h * V_ROWS, (h + 1) * V_ROWS)
            lg = lg_refs[slot][h] + bias
            m_new = jnp.maximum(ms[h], jnp.max(lg, axis=0, keepdims=True))
            alpha = jnp.exp2(ms[h] - m_new)
            p = jnp.exp2(lg - m_new).astype(BF16)
            pv = jnp.dot(vt_ref[vs, pl.ds(r0, kc)], p, preferred_element_type=F32)
            acc_ref[vs, :] = acc_ref[vs, :] * alpha + pv
            new_ms.append(m_new)
        return tuple(new_ms)

    def pair_body(j, carry):
        ms, tie_base = carry
        c = 2 * j
        tie_base = produce(c + 1, 1, tie_base)
        ms = consume(c, 0, ms)
        tie_base = produce(c + 2, 0, tie_base)
        ms = consume(c + 1, 1, ms)
        return ms, tie_base

    row = lambda v: jnp.full((1, tq), v, F32)
    ms0 = tuple(row(NEG) for _ in range(B_HEADS))
    n_pairs = (n_chunk - 1) // 2
    ms, tie_base = lax.fori_loop(0, n_pairs, pair_body, (ms0, produce(0, 0, row(0.0))))
    c0 = 2 * n_pairs

    @pl.when(c0 == n_chunk - 1)
    def _():
        consume(c0, 0, ms)

    @pl.when(c0 != n_chunk - 1)
    def _():
        produce(c0 + 1, 1, tie_base)
        consume(c0 + 1, 1, consume(c0, 0, ms))

    for h in range(B_HEADS):
        a = acc_ref[h * V_ROWS:(h + 1) * V_ROWS, :]
        o_ref[h * B_HEAD_DIM:(h + 1) * B_HEAD_DIM, :] = (
            a[:B_HEAD_DIM, :] / a[B_HEAD_DIM:B_HEAD_DIM + 1, :]).astype(o_ref.dtype)


def _dsa(qit, wit, ki, qt, k, vt_aug, bsz, seq, n_sel):
    tq = min(DSA_TQ, seq)
    kc = min(DSA_KC, seq)
    skc = min(DSA_SKC, seq)
    nq = seq // tq
    once = pl.Buffered(1)
    kern = functools.partial(_dsa_kernel, seq=seq, n_sel=n_sel, tq=tq, kc=kc, skc=skc,
                             sub=min(DSA_SUB, skc), rb=min(DSA_RB, skc))
    q_tile = lambda rows: pl.BlockSpec((rows, tq), lambda b, i: (0, b * nq + i))
    return pl.pallas_call(
        kern,
        grid=(bsz, nq),
        in_specs=[
            q_tile(IDX_HEADS * IDX_DIM),
            q_tile(SUBLANES),
            pl.BlockSpec((seq, LANES), lambda b, i: (b, 0), pipeline_mode=once),
            q_tile(B_WIDTH),
            pl.BlockSpec((seq, B_WIDTH), lambda b, i: (b, 0), pipeline_mode=once),
            pl.BlockSpec((B_HEADS * V_ROWS, seq), lambda b, i: (0, b), pipeline_mode=once),
        ],
        out_specs=q_tile(B_WIDTH),
        out_shape=jax.ShapeDtypeStruct((B_WIDTH, bsz * seq), BF16),
        scratch_shapes=[pltpu.VMEM((seq, tq), F32), pltpu.VMEM((skc, tq), F32), pltpu.VMEM((skc, tq), I32),
                        pltpu.VMEM((B_HEADS * V_ROWS, tq), F32), pltpu.VMEM((kc, tq), F32),
                        pltpu.VMEM((kc, tq), F32), pltpu.VMEM((B_HEADS, kc, tq), F32),
                        pltpu.VMEM((B_HEADS, kc, tq), F32)],
        compiler_params=_params(2),
        name="dsa",
    )(qit, wit, ki, qt, k, vt_aug)


def _proj_a_kernel(x_ref, g_ref, w_ref, ws_ref, bs_ref, ya_ref, *, tm):
    h = _rms(x_ref[...], g_ref[...]).astype(BF16)
    p = jnp.dot(h, w_ref[...], preferred_element_type=F32)
    u = jax.nn.gelu(p[:, :A_WIDTH])
    v = jax.nn.gelu(p[:, A_WIDTH:])
    mu = jnp.mean(v, axis=-1, keepdims=True)
    var = jnp.mean(jnp.square(v - mu), axis=-1, keepdims=True)
    v = ((v - mu) * lax.rsqrt(var + EPS)).astype(BF16)
    causal = (lax.broadcasted_iota(I32, (CHUNK, CHUNK), 1) <= lax.broadcasted_iota(I32, (CHUNK, CHUNK), 0))
    bs = bs_ref[...]
    for gidx in range(A_GROUPS):
        wsm = jnp.where(causal, ws_ref[gidx], 0.0).astype(BF16)
        cs = slice(gidx * A_GROUP_DIM, (gidx + 1) * A_GROUP_DIM)
        for n in range(tm // CHUNK):
            rs = slice(n * CHUNK, (n + 1) * CHUNK)
            vm = jnp.dot(wsm, v[rs, cs], preferred_element_type=F32) + bs[:, gidx:gidx + 1]
            ya_ref[rs, cs] = (u[rs, cs] * vm).astype(ya_ref.dtype)


def _proj_a(x2, g, w, ws, bs_t):
    t = x2.shape[0]
    tm = min(TM_PROJ, t)
    return pl.pallas_call(
        functools.partial(_proj_a_kernel, tm=tm),
        grid=(t // tm,),
        in_specs=[pl.BlockSpec((tm, D_MODEL), lambda i: (i, 0)), _full((1, D_MODEL)),
                  _full((D_MODEL, 2 * A_WIDTH)), _full((A_GROUPS, CHUNK, CHUNK)), _full((CHUNK, A_GROUPS))],
        out_specs=pl.BlockSpec((tm, A_WIDTH), lambda i: (i, 0)),
        out_shape=jax.ShapeDtypeStruct((t, A_WIDTH), BF16),
        compiler_params=_params(),
        name="proj_a",
    )(x2, g, w, ws, bs_t)


def _proj_c_kernel(x_ref, xh_ref, g_ref, w_ref, cw_ref, yc_ref, *, tm, tiles_per_seq):
    g = g_ref[...]
    w = w_ref[...]
    p = jnp.dot(_rms(x_ref[...], g).astype(BF16), w, preferred_element_type=F32)
    ph = jnp.dot(_rms(xh_ref[...], g).astype(BF16), w[:, C_WIDTH:], preferred_element_type=F32)
    first = (pl.program_id(0) % tiles_per_seq) == 0
    zh = ph[:, :C_WIDTH] * ph[:, C_WIDTH:] * jnp.where(first, 0.0, 1.0)
    z = p[:, C_WIDTH:2 * C_WIDTH] * p[:, 2 * C_WIDTH:]
    row = lax.broadcasted_iota(I32, (tm, C_WIDTH), 0)
    z1 = jnp.where(row == 0, zh[SUBLANES - 1:SUBLANES, :], pltpu.roll(z, 1, 0))
    z2 = jnp.where(row == 0, zh[SUBLANES - 2:SUBLANES - 1, :],
                   jnp.where(row == 1, zh[SUBLANES - 1:SUBLANES, :], pltpu.roll(z, 2, 0)))
    cw = cw_ref[...]
    y = cw[0:1, :] * z2 + cw[1:2, :] * z1 + cw[2:3, :] * z
    yc_ref[...] = (p[:, :C_WIDTH] * y).astype(yc_ref.dtype)


def _proj_c(x2, g, w, cw, seq):
    t = x2.shape[0]
    tm = min(TM_PROJ, seq)
    per8 = tm // SUBLANES
    return pl.pallas_call(
        functools.partial(_proj_c_kernel, tm=tm, tiles_per_seq=seq // tm),
        grid=(t // tm,),
        in_specs=[pl.BlockSpec((tm, D_MODEL), lambda i: (i, 0)),
                  pl.BlockSpec((SUBLANES, D_MODEL), lambda i: (jnp.maximum(i * per8 - 1, 0), 0)),
                  _full((1, D_MODEL)), _full((D_MODEL, 3 * C_WIDTH)), _full((CONV_WIDTH, C_WIDTH))],
        out_specs=pl.BlockSpec((tm, C_WIDTH), lambda i: (i, 0)),
        out_shape=jax.ShapeDtypeStruct((t, C_WIDTH), BF16),
        compiler_params=_params(),
        name="proj_c",
    )(x2, x2, g, w, cw)


def _merge_kernel(x_ref, g_ref, wg_ref, ya_ref, ybt_ref, yc_ref, wa_ref, wb_ref, wc_ref, wo_ref, o_ref):
    x = x_ref[...]
    h = _rms(x, g_ref[...]).astype(BF16)
    projected = (
        jnp.dot(ya_ref[...], wa_ref[...], preferred_element_type=F32),
        lax.dot_general(ybt_ref[...], wb_ref[...], (((0,), (0,)), ((), ())), preferred_element_type=F32),
        jnp.dot(yc_ref[...], wc_ref[...], preferred_element_type=F32),
    )
    merged = None
    for j, proj in enumerate(projected):
        gate = jnp.dot(h, wg_ref[:, j * D_MODEL:(j + 1) * D_MODEL], preferred_element_type=F32)
        term = jax.nn.sigmoid(gate) * proj
        merged = term if merged is None else merged + term
    o_ref[...] = x + jnp.dot(merged.astype(BF16), wo_ref[...], preferred_element_type=F32)


def _merge(x2, g, wg, ya, ybt, yc, wa, wb, wc, wo):
    t = x2.shape[0]
    tm = min(TM_MERGE, t)
    tok = lambda n: pl.BlockSpec((tm, n), lambda i: (i, 0))
    wproj = _full((A_WIDTH, D_MODEL))
    return pl.pallas_call(
        _merge_kernel,
        grid=(t // tm,),
        in_specs=[tok(D_MODEL), _full((1, D_MODEL)), _full((D_MODEL, N_BRANCH * D_MODEL)),
                  tok(A_WIDTH), pl.BlockSpec((B_WIDTH, tm), lambda i: (0, i)), tok(C_WIDTH),
                  wproj, wproj, wproj, _full((D_MODEL, D_MODEL))],
        out_specs=tok(D_MODEL),
        out_shape=jax.ShapeDtypeStruct((t, D_MODEL), F32),
        compiler_params=_params(),
        name="merge",
    )(x2, g, wg, ya, ybt, yc, wa, wb, wc, wo)


def _router_kernel(x_ref, g_ref, w_ref, b_ref, h_ref, eid_ref, wt_ref):
    h = _rms(x_ref[...], g_ref[...])
    h_ref[...] = h
    logits = jnp.dot(h, w_ref[...], preferred_element_type=F32, precision=lax.Precision.HIGHEST) + b_ref[...]
    lane = lax.broadcasted_iota(I32, logits.shape, 1)
    lane_f = lane.astype(F32)
    ninf = -jnp.inf

    def first_max(vals):
        vmax = jnp.max(vals, axis=1, keepdims=True)
        idx = jnp.min(jnp.where(vals == vmax, lane_f, float(LANES)), axis=1, keepdims=True)
        return vmax, idx

    gl = jnp.where(lane < N_GROUPS, logits, ninf)
    gmax, grp = first_max(gl)
    gsum = jnp.sum(jnp.where(lane < N_GROUPS, jnp.exp(logits - gmax), 0.0), axis=1, keepdims=True)
    gw = 1.0 / gsum
    e_lane = lane - N_GROUPS
    in_grp = (e_lane >= 0) & (e_lane < N_EXPERTS) & ((e_lane >> 3).astype(F32) == grp)
    el = jnp.where(in_grp, logits, ninf)
    v1, i1 = first_max(el)
    v2, i2 = first_max(jnp.where(lane_f == i1, ninf, el))
    e2 = jnp.exp(v2 - v1)
    den = 1.0 + e2
    w1 = (1.0 / den) * gw
    w2 = (e2 / den) * gw
    eid1 = (i1 - N_GROUPS).astype(I32)
    eid2 = (i2 - N_GROUPS).astype(I32)
    eid_ref[...] = jnp.where(lane == 0, eid1, jnp.where(lane == 1, eid2, 0))
    wt_ref[...] = jnp.where(lane == 0, w1, jnp.where(lane == 1, w2, 0.0))


def _router(x2, g, w, b):
    t = x2.shape[0]
    tm = min(TM_ROUTE, t)
    tok = lambda n: pl.BlockSpec((tm, n), lambda i: (i, 0))
    return pl.pallas_call(
        _router_kernel,
        grid=(t // tm,),
        in_specs=[tok(D_MODEL), _full((1, D_MODEL)), _full((D_MODEL, LANES)), _full((1, LANES))],
        out_specs=[tok(D_MODEL), tok(LANES), tok(LANES)],
        out_shape=[jax.ShapeDtypeStruct((t, D_MODEL), F32), jax.ShapeDtypeStruct((t, LANES), I32),
                   jax.ShapeDtypeStruct((t, LANES), F32)],
        compiler_params=_params(),
        name="router",
    )(x2, g, w, b)


def _rank_kernel(eid_ref, rank_ref, cnt_ref, *, tm):
    @pl.when(pl.program_id(0) == 0)
    def _():
        cnt_ref[...] = jnp.zeros_like(cnt_ref)

    eid = eid_ref[...]
    lane = lax.broadcasted_iota(I32, (tm, LANES), 1)
    o1 = lane == eid[:, 0:1]
    o2 = lane == eid[:, 1:2]
    both = jnp.where(o1 | o2, 1.0, 0.0)
    lower = lax.broadcasted_iota(I32, (tm, tm), 1) < lax.broadcasted_iota(I32, (tm, tm), 0)
    before = jnp.dot(jnp.where(lower, 1.0, 0.0).astype(BF16), both.astype(BF16),
                     preferred_element_type=F32) + cnt_ref[...]
    r1 = jnp.sum(jnp.where(o1, before, 0.0), axis=1, keepdims=True).astype(I32)
    r2 = jnp.sum(jnp.where(o2, before, 0.0), axis=1, keepdims=True).astype(I32)
    rank_ref[...] = jnp.where(lane == 0, r1, jnp.where(lane == 1, r2, 0))
    cnt_ref[...] = cnt_ref[...] + jnp.sum(both, axis=0, keepdims=True)


def _rank(eid):
    t = eid.shape[0]
    tm = min(TM_RANK, t)
    return pl.pallas_call(
        functools.partial(_rank_kernel, tm=tm),
        grid=(t // tm,),
        in_specs=[pl.BlockSpec((tm, LANES), lambda i: (i, 0))],
        out_specs=[pl.BlockSpec((tm, LANES), lambda i: (i, 0)), _full((1, LANES))],
        out_shape=[jax.ShapeDtypeStruct((t, LANES), I32), jax.ShapeDtypeStruct((1, LANES), F32)],
        compiler_params=_params(),
        name="rank",
    )(eid)


def _dest_kernel(eid_ref, rank_ref, start_ref, dest_ref):
    eid = eid_ref[...]
    lane = lax.broadcasted_iota(I32, eid.shape, 1)
    start = start_ref[...]
    s1 = jnp.sum(jnp.where(lane == eid[:, 0:1], start, 0.0), axis=1, keepdims=True).astype(I32)
    s2 = jnp.sum(jnp.where(lane == eid[:, 1:2], start, 0.0), axis=1, keepdims=True).astype(I32)
    rank = rank_ref[...]
    dest_ref[...] = jnp.where(lane == 0, s1, jnp.where(lane == 1, s2, 0)) + rank


def _dest(eid, rank, start_row):
    t = eid.shape[0]
    tm = min(TM_ROUTE, t)
    tok = pl.BlockSpec((tm, LANES), lambda i: (i, 0))
    return pl.pallas_call(
        _dest_kernel,
        grid=(t // tm,),
        in_specs=[tok, tok, _full((1, LANES))],
        out_specs=tok,
        out_shape=jax.ShapeDtypeStruct((t, LANES), I32),
        compiler_params=_params(),
        name="dest",
    )(eid, rank, start_row)


def _row_copy(src_ref, src_row, dst_ref, dst_row, sem):
    return pltpu.make_async_copy(src_ref.at[pl.ds(src_row, 1)], dst_ref.at[pl.ds(dst_row, 1)], sem)


def _rows_copy(src_ref, dst_ref, n, sem):
    return pltpu.make_async_copy(src_ref.at[pl.ds(0, n)], dst_ref.at[pl.ds(0, n)], sem)


def _scatter_kernel(dest_ref, h_ref, xs_in_ref, xs_ref, sem, *, tm):
    del xs_in_ref

    def issue(t, _):
        for j in range(2):
            _row_copy(h_ref, t, xs_ref, dest_ref[0, 0, 2 * t + j], sem).start()
        return 0

    lax.fori_loop(0, tm, issue, 0)
    for _ in range(2):
        _rows_copy(h_ref, xs_ref, tm, sem).wait()


def _scatter_rows(dest3, h, n_rows):
    t = h.shape[0]
    tm = dest3.shape[2] // 2
    xs0 = jnp.zeros((n_rows, D_MODEL), F32)
    return pl.pallas_call(
        functools.partial(_scatter_kernel, tm=tm),
        grid=(t // tm,),
        in_specs=[pl.BlockSpec((1, 1, 2 * tm), lambda i: (i, 0, 0), memory_space=pltpu.SMEM),
                  pl.BlockSpec((tm, D_MODEL), lambda i: (i, 0)),
                  pl.BlockSpec(memory_space=pl.ANY)],
        out_specs=pl.BlockSpec(memory_space=pl.ANY),
        out_shape=jax.ShapeDtypeStruct((n_rows, D_MODEL), F32),
        scratch_shapes=[pltpu.SemaphoreType.DMA(())],
        input_output_aliases={2: 0},
        compiler_params=_params(),
        name="scatter_rows",
    )(dest3, h, xs0)


def _expert_kernel(be_ref, nused_ref, xs_ref, wg_ref, wu_ref, wd_ref, y_ref):
    @pl.when(pl.program_id(0) < nused_ref[0])
    def _():
        xb = xs_ref[...].astype(BF16)
        gate = jnp.dot(xb, wg_ref[...], preferred_element_type=F32)
        up = jnp.dot(xb, wu_ref[...], preferred_element_type=F32)
        hid = (jax.nn.silu(gate) * up).astype(BF16)
        y_ref[...] = jnp.dot(hid, wd_ref[...], preferred_element_type=F32)

    @pl.when(pl.program_id(0) >= nused_ref[0])
    def _():
        y_ref[...] = jnp.zeros_like(y_ref)


def _experts(block_exp, n_used, xs, wg, wu, wd):
    n_rows = xs.shape[0]
    grid_spec = pltpu.PrefetchScalarGridSpec(
        num_scalar_prefetch=2,
        grid=(n_rows // EXP_BLOCK,),
        in_specs=[pl.BlockSpec((EXP_BLOCK, D_MODEL), lambda b, be, nu: (b, 0)),
                  pl.BlockSpec((None, D_MODEL, D_FF_EXPERT), lambda b, be, nu: (be[b], 0, 0)),
                  pl.BlockSpec((None, D_MODEL, D_FF_EXPERT), lambda b, be, nu: (be[b], 0, 0)),
                  pl.BlockSpec((None, D_FF_EXPERT, D_MODEL), lambda b, be, nu: (be[b], 0, 0))],
        out_specs=pl.BlockSpec((EXP_BLOCK, D_MODEL), lambda b, be, nu: (b, 0)),
    )
    return pl.pallas_call(
        _expert_kernel,
        grid_spec=grid_spec,
        out_shape=jax.ShapeDtypeStruct((n_rows, D_MODEL), F32),
        compiler_params=_params(),
        name="experts",
    )(block_exp, n_used, xs, wg, wu, wd)


def _combine_kernel(dest_ref, dest_next_ref, x_ref, wt_ref, g_ref, y_ref, o_ref, buf_ref, sems,
                    *, tm, steps, final_norm):
    i = pl.program_id(0)

    def gather(idx_ref, slot):
        def issue(t, _):
            for j in range(2):
                _row_copy(y_ref, idx_ref[0, 0, 2 * t + j], buf_ref.at[slot], j * tm + t, sems.at[slot]).start()
            return 0

        lax.fori_loop(0, tm, issue, 0)

    @pl.when(i == 0)
    def _():
        gather(dest_ref, 0)

    @pl.when(i + 1 < steps)
    def _():
        gather(dest_next_ref, (i + 1) % 2)

    slot = i % 2
    _rows_copy(y_ref, buf_ref.at[slot], 2 * tm, sems.at[slot]).wait()
    wt = wt_ref[...]
    rows = buf_ref[slot]
    out = x_ref[...] + (wt[:, 0:1] * rows[:tm, :] + wt[:, 1:2] * rows[tm:, :])
    if final_norm:
        out = _rms(out, g_ref[...])
    o_ref[...] = out


def _combine(dest3, x2, wt, g_final, y, final_norm):
    t = x2.shape[0]
    tm = dest3.shape[2] // 2
    steps = t // tm
    tok = lambda n: pl.BlockSpec((tm, n), lambda i: (i, 0))
    idx = lambda f: pl.BlockSpec((1, 1, 2 * tm), f, memory_space=pltpu.SMEM)
    return pl.pallas_call(
        functools.partial(_combine_kernel, tm=tm, steps=steps, final_norm=final_norm),
        grid=(steps,),
        in_specs=[idx(lambda i: (i, 0, 0)), idx(lambda i: (jnp.minimum(i + 1, steps - 1), 0, 0)),
                  tok(D_MODEL), tok(LANES), _full((1, D_MODEL)), pl.BlockSpec(memory_space=pl.ANY)],
        out_specs=tok(D_MODEL),
        out_shape=jax.ShapeDtypeStruct((t, D_MODEL), F32),
        scratch_shapes=[pltpu.VMEM((2, 2 * tm, D_MODEL), F32), pltpu.SemaphoreType.DMA((2,))],
        compiler_params=_params(),
        name="combine",
    )(dest3, dest3, x2, wt, g_final, y)


def _split_w_in(w_in_l):
    sizes = [A_WIDTH, A_WIDTH, B_WIDTH, B_WIDTH, B_WIDTH, IDX_HEADS * IDX_DIM, IDX_DIM, IDX_HEADS,
             C_WIDTH, C_WIDTH, C_WIDTH, N_BRANCH * D_MODEL]
    cuts = np.cumsum([0] + sizes)
    u, va, q, k, vb, qi, ki, wi, cb, cc, cx, gates = [w_in_l[:, a:b] for a, b in zip(cuts[:-1], cuts[1:])]
    pad = lambda w, n: jnp.pad(w, ((0, 0), (0, n - w.shape[1])))
    w_a = jnp.concatenate([u, va], axis=1).astype(BF16)
    w_b = jnp.concatenate([k, pad(ki, LANES)], axis=1).astype(BF16)
    w_bt = jnp.concatenate([q, vb, qi, pad(wi, WI_ROWS)], axis=1).T.astype(BF16)
    w_c = jnp.concatenate([cb, cc, cx], axis=1).astype(BF16)
    return w_a, w_b, w_bt, w_c, gates.astype(BF16)


def _token_mixers(x2, g, w_in_l, ws, bs, conv_w, wpa, wpb, wpc, w_out, tabs, bsz, seq):
    w_a, w_b, w_bt, w_c, w_g = _split_w_in(w_in_l)
    n_sel = min(TOPK_MAX, seq // 4)
    k, ki, qt, vt_aug, qit, wit = _proj_b(x2, g, w_b, w_bt, *tabs)
    ybt = _dsa(qit, wit, ki, qt, k, vt_aug, bsz, seq, n_sel)
    ya = _proj_a(x2, g, w_a, ws.astype(BF16), bs.T)
    yc = _proj_c(x2, g, w_c, conv_w, seq)
    bf = lambda w: w.astype(BF16)
    return _merge(x2, g, w_g, ya, ybt, yc, bf(wpa), bf(wpb), bf(wpc), bf(w_out))


def _hier_moe(x2, g, rg_w, rg_b, re_w, re_b, w_gate, w_up, w_down, g_final, final_norm):
    t = x2.shape[0]
    pad_c = LANES - N_GROUPS - N_EXPERTS
    w_r = jnp.pad(jnp.concatenate([rg_w, re_w], axis=1), ((0, 0), (0, pad_c)))
    b_r = jnp.pad(jnp.concatenate([rg_b, re_b]), (0, pad_c))[None, :]
    h, eid, wt = _router(x2, g, w_r, b_r)
    rank, counts = _rank(eid)
    cnt = counts[0, :N_EXPERTS].astype(I32)
    padded = ((cnt + EXP_BLOCK - 1) // EXP_BLOCK) * EXP_BLOCK
    pend = jnp.cumsum(padded)
    pstart = pend - padded
    n_blocks = (2 * t) // EXP_BLOCK + N_EXPERTS
    n_rows = n_blocks * EXP_BLOCK
    block_row = jnp.arange(n_blocks, dtype=I32) * EXP_BLOCK
    block_exp = jnp.minimum(jnp.sum((pend[None, :] <= block_row[:, None]).astype(I32), axis=1),
                            N_EXPERTS - 1)
    n_used = (pend[-1:] // EXP_BLOCK).astype(I32)
    start_row = jnp.pad(pstart.astype(F32), (0, LANES - N_EXPERTS))[None, :]
    dest = _dest(eid, rank, start_row)
    tm = min(TM_MOVE, t)
    dest3 = dest[:, :2].reshape(t // tm, 1, 2 * tm)
    xs = _scatter_rows(dest3, h, n_rows)
    bf = lambda w: w.astype(BF16)
    y = _experts(block_exp, n_used, xs, bf(w_gate), bf(w_up), bf(w_down))
    return _combine(dest3, x2, wt, g_final, y, final_norm)


def kernel(x, positions, norm_mix_g, norm_ffn_g, norm_final_g, w_in, gmlp_ws, gmlp_b, conv_w, w_proj_a,
           w_proj_b, w_proj_c, w_out, router_group_w, router_group_b, router_expert_w, router_expert_b,
           expert_w_gate, expert_w_up, expert_w_down):
    bsz, seq, d = x.shape
    depth = w_in.shape[0]
    x2 = x.reshape(bsz * seq, d)
    pos = positions.astype(F32).reshape(bsz * seq)
    tabs = (_rope_tables(pos[:, None], B_HEAD_DIM, B_ROT), _rope_tables(pos[:, None], IDX_DIM, IDX_ROT),
            _rope_tables_t(pos[None, :], B_ROT), _rope_tables_t(pos[None, :], IDX_ROT))
    g_final = norm_final_g[None, :]
    for l in range(depth):
        x2 = _token_mixers(x2, norm_mix_g[l][None, :], w_in[l], gmlp_ws[l], gmlp_b[l], conv_w[l],
                           w_proj_a[l], w_proj_b[l], w_proj_c[l], w_out[l], tabs, bsz, seq)
        x2 = _hier_moe(x2, norm_ffn_g[l][None, :], router_group_w[l], router_group_b[l],
                       router_expert_w[l], router_expert_b[l], expert_w_gate[l], expert_w_up[l],
                       expert_w_down[l], g_final, l == depth - 1)
    return x2.reshape(bsz, seq, d)
```

```python
import functools

import numpy as np
import jax
import jax.numpy as jnp
from jax import lax
from jax.experimental import pallas as pl
from jax.experimental.pallas import tpu as pltpu

D_MODEL = 1024
CHUNK = 128
A_GROUPS = 4
A_GROUP_DIM = 128
A_WIDTH = A_GROUPS * A_GROUP_DIM
B_HEADS = 4
B_HEAD_DIM = 128
B_WIDTH = B_HEADS * B_HEAD_DIM
B_ROT = B_HEAD_DIM // 4
IDX_HEADS = 4
IDX_DIM = 64
IDX_ROT = IDX_DIM // 4
TOPK_MAX = 256
C_WIDTH = 512
CONV_WIDTH = 3
N_BRANCH = 3
N_GROUPS = 4
EXP_PER_GROUP = 8
N_EXPERTS = N_GROUPS * EXP_PER_GROUP
D_FF_EXPERT = 512
ROPE_THETA = 500000.0
EPS = 1e-6
NEG = -1e30

LANES = 128
SUBLANES = 8
VMEM_LIMIT = 56 * 1024 * 1024

TM_PROJ = 512
TM_MERGE = 256
TM_ROUTE = 512
TM_RANK = 256
TM_MOVE = 256
EXP_BLOCK = 256
DSA_TQ = 128
DSA_KC = 256
DSA_SKC = 512
DSA_SUB = 256
DSA_RB = 512
V_ROWS = B_HEAD_DIM + 16
LOG2E = 1.4426950408889634

F32 = jnp.float32
BF16 = jnp.bfloat16
I32 = jnp.int32
INT_MIN = -2 ** 31


def _monotone_key_of(value):
    bits = int(np.array(value, np.float32).view(np.int32))
    return bits ^ ((bits >> 31) & 0x7FFFFFFF)


NEG_KEY = _monotone_key_of(NEG)


def _params(n_axes=1, semantics=None):
    return pltpu.CompilerParams(
        dimension_semantics=semantics or ("arbitrary",) * n_axes,
        vmem_limit_bytes=VMEM_LIMIT)


def _rms(x, g):
    ms = jnp.mean(x * x, axis=-1, keepdims=True)
    return x * lax.rsqrt(ms + EPS) * g


def _full(shape):
    nd = len(shape)
    return pl.BlockSpec(shape, lambda *_: (0,) * nd)


def _rope_table_kernel(pos_ref, invf_ref, mrot_ref, ma_ref, mb_ref, c_ref, sa_ref, sb_ref):
    ang = pos_ref[...] * invf_ref[...]
    c = jnp.cos(ang)
    s = jnp.sin(ang)
    c_ref[...] = jnp.where(mrot_ref[...] > 0, c, 1.0)
    sa_ref[...] = jnp.where(ma_ref[...] > 0, -s, 0.0)
    sb_ref[...] = jnp.where(mb_ref[...] > 0, s, 0.0)


def _rope_tables(pos_col, head_dim, rot):
    t = pos_col.shape[0]
    half = rot // 2
    inv_freq = jnp.float32(ROPE_THETA) ** (-jnp.arange(half, dtype=F32) * 2.0 / rot)
    lane = np.arange(LANES) % head_dim
    mrot = lane < rot
    ma = lane < half
    mb = (lane >= half) & (lane < rot)
    invf = jnp.where(jnp.asarray(mrot), inv_freq[np.where(mrot, lane % half, 0)], 0.0)[None, :]
    row = lambda m: jnp.asarray(m.astype(np.float32))[None, :]
    tm = min(1024, t)
    spec_row = _full((1, LANES))
    out_spec = pl.BlockSpec((tm, LANES), lambda i: (i, 0))
    return pl.pallas_call(
        _rope_table_kernel,
        grid=(t // tm,),
        in_specs=[pl.BlockSpec((tm, 1), lambda i: (i, 0)), spec_row, spec_row, spec_row, spec_row],
        out_specs=[out_spec] * 3,
        out_shape=[jax.ShapeDtypeStruct((t, LANES), F32)] * 3,
        compiler_params=_params(),
        name="rope_tables",
    )(pos_col, invf, row(mrot), row(ma), row(mb))


def _rope_table_t_kernel(pos_ref, invf_ref, c_ref, s_ref):
    ang = invf_ref[...] * pos_ref[...]
    c_ref[...] = jnp.cos(ang)
    s_ref[...] = jnp.sin(ang)


def _rope_tables_t(pos_row, rot):
    t = pos_row.shape[1]
    half = rot // 2
    inv_freq = jnp.float32(ROPE_THETA) ** (-jnp.arange(half, dtype=F32) * 2.0 / rot)
    tm = min(2048, t)
    out_spec = pl.BlockSpec((half, tm), lambda i: (0, i))
    return pl.pallas_call(
        _rope_table_t_kernel,
        grid=(t // tm,),
        in_specs=[pl.BlockSpec((1, tm), lambda i: (0, i)), _full((half, 1))],
        out_specs=[out_spec] * 2,
        out_shape=[jax.ShapeDtypeStruct((half, t), F32)] * 2,
        compiler_params=_params(),
        name="rope_tables_t",
    )(pos_row, inv_freq[:, None])


def _rope_t(x, c, s):
    half = c.shape[0]
    x1, x2 = x[:half, :], x[half:2 * half, :]
    return jnp.concatenate([x1 * c - x2 * s, x1 * s + x2 * c, x[2 * half:, :]], axis=0)


def _rope(x, c, sa, sb, half):
    parts = []
    for j in range(x.shape[1] // LANES):
        xj = x[:, j * LANES:(j + 1) * LANES]
        parts.append(xj * c + pltpu.roll(xj, LANES - half, 1) * sa + pltpu.roll(xj, half, 1) * sb)
    return parts


WI_ROWS = 16


def _proj_b_kernel(x_ref, g_ref, w_ref, wt_ref, cb_ref, sab_ref, sbb_ref, ci_ref, sai_ref, sbi_ref,
                   cbt_ref, sbt_ref, cit_ref, sit_ref, k_ref, ki_ref, qt_ref, vt_ref, qit_ref, wit_ref):
    h = _rms(x_ref[...], g_ref[...]).astype(BF16)
    tm = h.shape[0]
    p = jnp.dot(h, w_ref[...], preferred_element_type=F32)
    pt = lax.dot_general(wt_ref[...], h, (((1,), (1,)), ((), ())), preferred_element_type=F32)
    tb = (cb_ref[...], sab_ref[...], sbb_ref[...])
    ti = (ci_ref[...], sai_ref[...], sbi_ref[...])
    for j, kj in enumerate(_rope(p[:, :B_WIDTH], *tb, B_ROT // 2)):
        k_ref[:, j * LANES:(j + 1) * LANES] = kj.astype(BF16)
    ki_ref[...] = _rope(p[:, B_WIDTH:B_WIDTH + LANES], *ti, IDX_ROT // 2)[0].astype(BF16)
    cbt, sbt = cbt_ref[...], sbt_ref[...]
    for hh in range(B_HEADS):
        hs = slice(hh * B_HEAD_DIM, (hh + 1) * B_HEAD_DIM)
        qt_ref[hs, :] = (_rope_t(pt[hs, :], cbt, sbt) * (B_HEAD_DIM ** -0.5 * LOG2E)).astype(BF16)
    o = B_WIDTH
    for hh in range(B_HEADS):
        vt_ref[hh * V_ROWS:hh * V_ROWS + B_HEAD_DIM, :] = (
            pt[o + hh * B_HEAD_DIM:o + (hh + 1) * B_HEAD_DIM, :].astype(BF16))
        vt_ref[hh * V_ROWS + B_HEAD_DIM:(hh + 1) * V_ROWS, :] = jnp.ones((V_ROWS - B_HEAD_DIM, tm), BF16)
    o += B_WIDTH
    cit, sit = cit_ref[...], sit_ref[...]
    for hh in range(IDX_HEADS):
        hs = slice(hh * IDX_DIM, (hh + 1) * IDX_DIM)
        qit_ref[hs, :] = _rope_t(pt[o + hh * IDX_DIM:o + (hh + 1) * IDX_DIM, :], cit, sit).astype(BF16)
    o += IDX_HEADS * IDX_DIM
    wit_ref[...] = pt[o:o + SUBLANES, :] * ((IDX_HEADS ** -0.5) * (IDX_DIM ** -0.5))


def _proj_b(x2, g, w, wt, tabs_b, tabs_i, tabs_bt, tabs_it):
    t = x2.shape[0]
    tm = min(TM_PROJ, t)
    tok = lambda n: pl.BlockSpec((tm, n), lambda i: (i, 0))
    feat = lambda n: pl.BlockSpec((n, tm), lambda i: (0, i))
    outs_tok = [(B_WIDTH, BF16), (LANES, BF16)]
    outs_feat = [(B_WIDTH, BF16), (B_HEADS * V_ROWS, BF16), (IDX_HEADS * IDX_DIM, BF16), (SUBLANES, F32)]
    return pl.pallas_call(
        _proj_b_kernel,
        grid=(t // tm,),
        in_specs=[tok(D_MODEL), _full((1, D_MODEL)), _full(w.shape), _full(wt.shape)] + [tok(LANES)] * 6
                 + [feat(B_ROT // 2)] * 2 + [feat(IDX_ROT // 2)] * 2,
        out_specs=[tok(n) for n, _ in outs_tok] + [feat(n) for n, _ in outs_feat],
        out_shape=[jax.ShapeDtypeStruct((t, n), d) for n, d in outs_tok]
                  + [jax.ShapeDtypeStruct((n, t), d) for n, d in outs_feat],
        compiler_params=_params(),
        name="proj_b",
    )(x2, g, w, wt, *tabs_b, *tabs_i, *tabs_bt, *tabs_it)


def _ordered_key(score):
    bits = lax.bitcast_convert_type(score, I32)
    key = bits ^ ((bits >> 31) & 0x7FFFFFFF)
    return jnp.where(key == -1, 0, key)


def _score_of_key(key):
    return lax.bitcast_convert_type(key ^ ((key >> 31) & 0x7FFFFFFF), F32)


def _ce_desc(vals, i, j):
    a, b = vals[i], vals[j]
    vals[i] = jnp.maximum(a, b)
    vals[j] = jnp.minimum(a, b)


def _bitonic_sort(vals, desc=True):
    n = len(vals)
    k = 2
    while k <= n:
        j = k // 2
        while j >= 1:
            for i in range(n):
                l = i ^ j
                if l > i:
                    if ((i & k) == 0) == desc:
                        _ce_desc(vals, i, l)
                    else:
                        _ce_desc(vals, l, i)
            j //= 2
        k *= 2


def _bitonic_merge_desc(vals):
    n = len(vals)
    j = n // 2
    while j >= 1:
        for i in range(n):
            l = i ^ j
            if l > i:
                _ce_desc(vals, i, l)
        j //= 2


def _dsa_kernel(qit_ref, wit_ref, ki_ref, qt_ref, k_ref, vt_ref, o_ref, sc_ref, top_ref, topk_ref, acc_ref,
                bias0_ref, bias1_ref, lg0_ref, lg1_ref,
                *, seq, n_sel, tq, kc, skc, sub, rb):
    i = pl.program_id(1)
    q0 = i * tq
    n_full = q0 // skc
    n_rows = (n_full + 1) * skc
    n_pad = (seq - 1 - q0) - lax.broadcasted_iota(I32, (1, tq), 1)
    n_chunk = (q0 + tq + kc - 1) // kc
    wit = wit_ref[...]

    n_top = skc // SUBLANES
    top_ref[...] = jnp.full((skc, tq), -jnp.inf, F32)

    def score_batch(c, masked):
        halves = []
        for s in range(skc // sub):
            r0 = pl.multiple_of(c * skc + s * sub, sub)
            kic = ki_ref[pl.ds(r0, sub), :IDX_DIM]
            sc = None
            for h in range(IDX_HEADS):
                a = jnp.dot(kic, qit_ref[h * IDX_DIM:(h + 1) * IDX_DIM, :], preferred_element_type=F32)
                term = jnp.maximum(a, 0.0) * wit[h:h + 1, :]
                sc = term if sc is None else sc + term
            if masked:
                qpos = q0 + lax.broadcasted_iota(I32, (sub, tq), 1)
                krow = r0 + lax.broadcasted_iota(I32, (sub, tq), 0)
                sc = jnp.where(krow <= qpos, sc, -jnp.inf)
            sc_ref[pl.ds(r0, sub), :] = sc
            half = [sc[SUBLANES * i:SUBLANES * (i + 1), :] for i in range(sub // SUBLANES)]
            _bitonic_sort(half, desc=(s % 2 == 0))
            halves.append(half)
        vals = halves[0] + halves[1]
        _bitonic_merge_desc(vals)
        return vals

    def keep_top(batches):
        top = [top_ref[SUBLANES * i:SUBLANES * (i + 1), :] for i in range(n_top)]
        for vals in batches:
            top = [jnp.maximum(top[i], vals[n_top - 1 - i]) for i in range(n_top)]
            _bitonic_merge_desc(top)
        for i in range(n_top):
            top_ref[SUBLANES * i:SUBLANES * (i + 1), :] = top[i]

    def score_pair(j, _):
        keep_top([score_batch(2 * j, False), score_batch(2 * j + 1, False)])
        return 0

    lax.fori_loop(0, n_full // 2, score_pair, 0)

    @pl.when(n_full % 2 == 1)
    def _():
        keep_top([score_batch(n_full - 1, False)])

    keep_top([score_batch(n_full, True)])

    def search(count):
        def bit_body(b, carry):
            ans_u, = carry
            cand_u = ans_u | jnp.left_shift(jnp.int32(1), 31 - b)
            ok = count(cand_u ^ INT_MIN, False) >= n_sel
            return (jnp.where(ok, cand_u, ans_u),)

        ans_u, = lax.fori_loop(0, 32, bit_body, (jnp.zeros((1, tq), I32),))
        thr = ans_u ^ INT_MIN
        return thr, (n_sel - count(thr, True)).astype(F32)

    def pads(cand, strict):
        return jnp.where((NEG_KEY > cand) if strict else (NEG_KEY >= cand), n_pad, 0)

    topk_ref[...] = _ordered_key(top_ref[...])

    def count_top(cand, strict):
        t = topk_ref[...]
        hit = (t > cand) if strict else (t >= cand)
        return jnp.sum(jnp.where(hit, 1.0, 0.0), axis=0, keepdims=True).astype(I32) + pads(cand, strict)

    def search_top():
        def digit_body(b, carry):
            ans_u, = carry
            shift = 30 - 2 * b
            for d in (1, 2, 3):
                cand_u = carry[0] | jnp.left_shift(jnp.int32(d), shift)
                ans_u = jnp.where(count_top(cand_u ^ INT_MIN, False) >= n_sel, cand_u, ans_u)
            return (ans_u,)

        ans_u, = lax.fori_loop(0, 16, digit_body, (jnp.zeros((1, tq), I32),))
        thr = ans_u ^ INT_MIN
        return thr, (n_sel - count_top(thr, True)).astype(F32)

    def count_all(cand, strict):
        acc_rows = 4 * SUBLANES
        cand_f = _score_of_key(cand)

        def body(r, acc):
            r0 = pl.multiple_of(r * rb, rb)
            blk = sc_ref[pl.ds(r0, rb), :]
            for j in range(rb // acc_rows):
                part = blk[j * acc_rows:(j + 1) * acc_rows, :]
                hit = (part > cand_f) if strict else (part >= cand_f)
                acc = acc + jnp.where(hit, 1, 0)
            return acc

        acc = lax.fori_loop(0, n_rows // rb, body, jnp.zeros((acc_rows, tq), I32))
        return jnp.sum(acc.astype(F32), axis=0, keepdims=True).astype(I32) + pads(cand, strict)

    thr, need = search_top()
    last = topk_ref[skc - SUBLANES:skc, :]
    overflow = jnp.max(jnp.where(last > thr, 1.0, 0.0))
    thr, need = lax.cond(overflow > 0.0, lambda: search(count_all), lambda: (thr, need))
    thr_f = _score_of_key(thr)

    acc_ref[...] = jnp.zeros_like(acc_ref)
    bias_refs = (bias0_ref, bias1_ref)
    lg_refs = (lg0_ref, lg1_ref)
    hk = kc // 2
    lower = (lax.broadcasted_iota(I32, (hk, hk), 1) < lax.broadcasted_iota(I32, (hk, hk), 0))
    lower = jnp.where(lower, 1.0, 0.0).astype(BF16)

    def produce(c, slot, tie_base):
        r0 = pl.multiple_of(c * kc, kc)
        sc = sc_ref[pl.ds(r0, kc), :]
        tie = sc == thr_f
        tie_f = jnp.where(tie, 1.0, 0.0)
        tie_b = tie_f.astype(BF16)
        b0 = jnp.dot(lower, tie_b[:hk, :], preferred_element_type=F32)
        n0 = b0[hk - 1:hk, :] + tie_f[hk - 1:hk, :]
        b1 = jnp.dot(lower, tie_b[hk:, :], preferred_element_type=F32) + n0
        before = jnp.concatenate([b0, b1], axis=0)
        sel = (sc > thr_f) | (tie & (before < need - tie_base))
        bias_refs[slot][...] = jnp.where(sel, 0.0, NEG)
        for h in range(B_HEADS):
            hs = slice(h * B_HEAD_DIM, (h + 1) * B_HEAD_DIM)
            lg_refs[slot][h] = jnp.dot(k_ref[pl.ds(r0, kc), hs], qt_ref[hs, :], preferred_element_type=F32)
        return tie_base + b1[hk - 1:hk, :] + tie_f[kc - 1:kc, :]

    def consume(c, slot, ms):
        r0 = pl.multiple_of(c * kc, kc)
        bias = bias_refs[slot][...]
        new_ms = []
        for h in range(B_HEADS):
            vs = slice(h * V_ROWS, (h + 1) * V_ROWS)
            lg = lg_refs[slot][h] + bias
            m_new = jnp.maximum(ms[h], jnp.max(lg, axis=0, keepdims=True))
            alpha = jnp.exp2(ms[h] - m_new)
            p = jnp.exp2(lg - m_new).astype(BF16)
            pv = jnp.dot(vt_ref[vs, pl.ds(r0, kc)], p, preferred_element_type=F32)
            acc_ref[vs, :] = acc_ref[vs, :] * alpha + pv
            new_ms.append(m_new)
        return tuple(new_ms)

    def pair_body(j, carry):
        ms, tie_base = carry
        c = 2 * j
        tie_base = produce(c + 1, 1, tie_base)
        ms = consume(c, 0, ms)
        tie_base = produce(c + 2, 0, tie_base)
        ms = consume(c + 1, 1, ms)
        return ms, tie_base

    row = lambda v: jnp.full((1, tq), v, F32)
    ms0 = tuple(row(NEG) for _ in range(B_HEADS))
    n_pairs = (n_chunk - 1) // 2
    ms, tie_base = lax.fori_loop(0, n_pairs, pair_body, (ms0, produce(0, 0, row(0.0))))
    c0 = 2 * n_pairs

    @pl.when(c0 == n_chunk - 1)
    def _():
        consume(c0, 0, ms)

    @pl.when(c0 != n_chunk - 1)
    def _():
        produce(c0 + 1, 1, tie_base)
        consume(c0 + 1, 1, consume(c0, 0, ms))

    for h in range(B_HEADS):
        a = acc_ref[h * V_ROWS:(h + 1) * V_ROWS, :]
        o_ref[h * B_HEAD_DIM:(h + 1) * B_HEAD_DIM, :] = (
            a[:B_HEAD_DIM, :] / a[B_HEAD_DIM:B_HEAD_DIM + 1, :]).astype(o_ref.dtype)


def _dsa(qit, wit, ki, qt, k, vt_aug, bsz, seq, n_sel):
    tq = min(DSA_TQ, seq)
    kc = min(DSA_KC, seq)
    skc = min(DSA_SKC, seq)
    nq = seq // tq
    once = pl.Buffered(1)
    kern = functools.partial(_dsa_kernel, seq=seq, n_sel=n_sel, tq=tq, kc=kc, skc=skc,
                             sub=min(DSA_SUB, skc), rb=min(DSA_RB, skc))
    q_tile = lambda rows: pl.BlockSpec((rows, tq), lambda b, i: (0, b * nq + i))
    return pl.pallas_call(
        kern,
        grid=(bsz, nq),
        in_specs=[
            q_tile(IDX_HEADS * IDX_DIM),
            q_tile(SUBLANES),
            pl.BlockSpec((seq, LANES), lambda b, i: (b, 0), pipeline_mode=once),
            q_tile(B_WIDTH),
            pl.BlockSpec((seq, B_WIDTH), lambda b, i: (b, 0), pipeline_mode=once),
            pl.BlockSpec((B_HEADS * V_ROWS, seq), lambda b, i: (0, b), pipeline_mode=once),
        ],
        out_specs=q_tile(B_WIDTH),
        out_shape=jax.ShapeDtypeStruct((B_WIDTH, bsz * seq), BF16),
        scratch_shapes=[pltpu.VMEM((seq, tq), F32), pltpu.VMEM((skc, tq), F32), pltpu.VMEM((skc, tq), I32),
                        pltpu.VMEM((B_HEADS * V_ROWS, tq), F32), pltpu.VMEM((kc, tq), F32),
                        pltpu.VMEM((kc, tq), F32), pltpu.VMEM((B_HEADS, kc, tq), F32),
                        pltpu.VMEM((B_HEADS, kc, tq), F32)],
        compiler_params=_params(2),
        name="dsa",
    )(qit, wit, ki, qt, k, vt_aug)


def _proj_a_kernel(x_ref, g_ref, w_ref, ws_ref, bs_ref, ya_ref, *, tm):
    h = _rms(x_ref[...], g_ref[...]).astype(BF16)
    p = jnp.dot(h, w_ref[...], preferred_element_type=F32)
    u = jax.nn.gelu(p[:, :A_WIDTH])
    v = jax.nn.gelu(p[:, A_WIDTH:])
    mu = jnp.mean(v, axis=-1, keepdims=True)
    var = jnp.mean(jnp.square(v - mu), axis=-1, keepdims=True)
    v = ((v - mu) * lax.rsqrt(var + EPS)).astype(BF16)
    causal = (lax.broadcasted_iota(I32, (CHUNK, CHUNK), 1) <= lax.broadcasted_iota(I32, (CHUNK, CHUNK), 0))
    bs = bs_ref[...]
    for gidx in range(A_GROUPS):
        wsm = jnp.where(causal, ws_ref[gidx], 0.0).astype(BF16)
        cs = slice(gidx * A_GROUP_DIM, (gidx + 1) * A_GROUP_DIM)
        for n in range(tm // CHUNK):
            rs = slice(n * CHUNK, (n + 1) * CHUNK)
            vm = jnp.dot(wsm, v[rs, cs], preferred_element_type=F32) + bs[:, gidx:gidx + 1]
            ya_ref[rs, cs] = (u[rs, cs] * vm).astype(ya_ref.dtype)


def _proj_a(x2, g, w, ws, bs_t):
    t = x2.shape[0]
    tm = min(TM_PROJ, t)
    return pl.pallas_call(
        functools.partial(_proj_a_kernel, tm=tm),
        grid=(t // tm,),
        in_specs=[pl.BlockSpec((tm, D_MODEL), lambda i: (i, 0)), _full((1, D_MODEL)),
                  _full((D_MODEL, 2 * A_WIDTH)), _full((A_GROUPS, CHUNK, CHUNK)), _full((CHUNK, A_GROUPS))],
        out_specs=pl.BlockSpec((tm, A_WIDTH), lambda i: (i, 0)),
        out_shape=jax.ShapeDtypeStruct((t, A_WIDTH), BF16),
        compiler_params=_params(),
        name="proj_a",
    )(x2, g, w, ws, bs_t)


def _proj_c_kernel(x_ref, xh_ref, g_ref, w_ref, cw_ref, yc_ref, *, tm, tiles_per_seq):
    g = g_ref[...]
    w = w_ref[...]
    p = jnp.dot(_rms(x_ref[...], g).astype(BF16), w, preferred_element_type=F32)
    ph = jnp.dot(_rms(xh_ref[...], g).astype(BF16), w[:, C_WIDTH:], preferred_element_type=F32)
    first = (pl.program_id(0) % tiles_per_seq) == 0
    zh = ph[:, :C_WIDTH] * ph[:, C_WIDTH:] * jnp.where(first, 0.0, 1.0)
    z = p[:, C_WIDTH:2 * C_WIDTH] * p[:, 2 * C_WIDTH:]
    row = lax.broadcasted_iota(I32, (tm, C_WIDTH), 0)
    z1 = jnp.where(row == 0, zh[SUBLANES - 1:SUBLANES, :], pltpu.roll(z, 1, 0))
    z2 = jnp.where(row == 0, zh[SUBLANES - 2:SUBLANES - 1, :],
                   jnp.where(row == 1, zh[SUBLANES - 1:SUBLANES, :], pltpu.roll(z, 2, 0)))
    cw = cw_ref[...]
    y = cw[0:1, :] * z2 + cw[1:2, :] * z1 + cw[2:3, :] * z
    yc_ref[...] = (p[:, :C_WIDTH] * y).astype(yc_ref.dtype)


def _proj_c(x2, g, w, cw, seq):
    t = x2.shape[0]
    tm = min(TM_PROJ, seq)
    per8 = tm // SUBLANES
    return pl.pallas_call(
        functools.partial(_proj_c_kernel, tm=tm, tiles_per_seq=seq // tm),
        grid=(t // tm,),
        in_specs=[pl.BlockSpec((tm, D_MODEL), lambda i: (i, 0)),
                  pl.BlockSpec((SUBLANES, D_MODEL), lambda i: (jnp.maximum(i * per8 - 1, 0), 0)),
                  _full((1, D_MODEL)), _full((D_MODEL, 3 * C_WIDTH)), _full((CONV_WIDTH, C_WIDTH))],
        out_specs=pl.BlockSpec((tm, C_WIDTH), lambda i: (i, 0)),
        out_shape=jax.ShapeDtypeStruct((t, C_WIDTH), BF16),
        compiler_params=_params(),
        name="proj_c",
    )(x2, x2, g, w, cw)


def _merge_kernel(x_ref, g_ref, wg_ref, ya_ref, ybt_ref, yc_ref, wa_ref, wb_ref, wc_ref, wo_ref, o_ref):
    x = x_ref[...]
    h = _rms(x, g_ref[...]).astype(BF16)
    projected = (
        jnp.dot(ya_ref[...], wa_ref[...], preferred_element_type=F32),
        lax.dot_general(ybt_ref[...], wb_ref[...], (((0,), (0,)), ((), ())), preferred_element_type=F32),
        jnp.dot(yc_ref[...], wc_ref[...], preferred_element_type=F32),
    )
    merged = None
    for j, proj in enumerate(projected):
        gate = jnp.dot(h, wg_ref[:, j * D_MODEL:(j + 1) * D_MODEL], preferred_element_type=F32)
        term = jax.nn.sigmoid(gate) * proj
        merged = term if merged is None else merged + term
    o_ref[...] = x + jnp.dot(merged.astype(BF16), wo_ref[...], preferred_element_type=F32)


def _merge(x2, g, wg, ya, ybt, yc, wa, wb, wc, wo):
    t = x2.shape[0]
    tm = min(TM_MERGE, t)
    tok = lambda n: pl.BlockSpec((tm, n), lambda i: (i, 0))
    wproj = _full((A_WIDTH, D_MODEL))
    return pl.pallas_call(
        _merge_kernel,
        grid=(t // tm,),
        in_specs=[tok(D_MODEL), _full((1, D_MODEL)), _full((D_MODEL, N_BRANCH * D_MODEL)),
                  tok(A_WIDTH), pl.BlockSpec((B_WIDTH, tm), lambda i: (0, i)), tok(C_WIDTH),
                  wproj, wproj, wproj, _full((D_MODEL, D_MODEL))],
        out_specs=tok(D_MODEL),
        out_shape=jax.ShapeDtypeStruct((t, D_MODEL), F32),
        compiler_params=_params(),
        name="merge",
    )(x2, g, wg, ya, ybt, yc, wa, wb, wc, wo)


def _router_kernel(x_ref, g_ref, w_ref, b_ref, h_ref, eid_ref, wt_ref):
    h = _rms(x_ref[...], g_ref[...])
    h_ref[...] = h
    logits = jnp.dot(h, w_ref[...], preferred_element_type=F32, precision=lax.Precision.HIGHEST) + b_ref[...]
    lane = lax.broadcasted_iota(I32, logits.shape, 1)
    lane_f = lane.astype(F32)
    ninf = -jnp.inf

    def first_max(vals):
        vmax = jnp.max(vals, axis=1, keepdims=True)
        idx = jnp.min(jnp.where(vals == vmax, lane_f, float(LANES)), axis=1, keepdims=True)
        return vmax, idx

    gl = jnp.where(lane < N_GROUPS, logits, ninf)
    gmax, grp = first_max(gl)
    gsum = jnp.sum(jnp.where(lane < N_GROUPS, jnp.exp(logits - gmax), 0.0), axis=1, keepdims=True)
    gw = 1.0 / gsum
    e_lane = lane - N_GROUPS
    in_grp = (e_lane >= 0) & (e_lane < N_EXPERTS) & ((e_lane >> 3).astype(F32) == grp)
    el = jnp.where(in_grp, logits, ninf)
    v1, i1 = first_max(el)
    v2, i2 = first_max(jnp.where(lane_f == i1, ninf, el))
    e2 = jnp.exp(v2 - v1)
    den = 1.0 + e2
    w1 = (1.0 / den) * gw
    w2 = (e2 / den) * gw
    eid1 = (i1 - N_GROUPS).astype(I32)
    eid2 = (i2 - N_GROUPS).astype(I32)
    eid_ref[...] = jnp.where(lane == 0, eid1, jnp.where(lane == 1, eid2, 0))
    wt_ref[...] = jnp.where(lane == 0, w1, jnp.where(lane == 1, w2, 0.0))


def _router(x2, g, w, b):
    t = x2.shape[0]
    tm = min(TM_ROUTE, t)
    tok = lambda n: pl.BlockSpec((tm, n), lambda i: (i, 0))
    return pl.pallas_call(
        _router_kernel,
        grid=(t // tm,),
        in_specs=[tok(D_MODEL), _full((1, D_MODEL)), _full((D_MODEL, LANES)), _full((1, LANES))],
        out_specs=[tok(D_MODEL), tok(LANES), tok(LANES)],
        out_shape=[jax.ShapeDtypeStruct((t, D_MODEL), F32), jax.ShapeDtypeStruct((t, LANES), I32),
                   jax.ShapeDtypeStruct((t, LANES), F32)],
        compiler_params=_params(),
        name="router",
    )(x2, g, w, b)


def _rank_kernel(eid_ref, rank_ref, cnt_ref, *, tm):
    @pl.when(pl.program_id(0) == 0)
    def _():
        cnt_ref[...] = jnp.zeros_like(cnt_ref)

    eid = eid_ref[...]
    lane = lax.broadcasted_iota(I32, (tm, LANES), 1)
    o1 = lane == eid[:, 0:1]
    o2 = lane == eid[:, 1:2]
    both = jnp.where(o1 | o2, 1.0, 0.0)
    lower = lax.broadcasted_iota(I32, (tm, tm), 1) < lax.broadcasted_iota(I32, (tm, tm), 0)
    before = jnp.dot(jnp.where(lower, 1.0, 0.0).astype(BF16), both.astype(BF16),
                     preferred_element_type=F32) + cnt_ref[...]
    r1 = jnp.sum(jnp.where(o1, before, 0.0), axis=1, keepdims=True).astype(I32)
    r2 = jnp.sum(jnp.where(o2, before, 0.0), axis=1, keepdims=True).astype(I32)
    rank_ref[...] = jnp.where(lane == 0, r1, jnp.where(lane == 1, r2, 0))
    cnt_ref[...] = cnt_ref[...] + jnp.sum(both, axis=0, keepdims=True)


def _rank(eid):
    t = eid.shape[0]
    tm = min(TM_RANK, t)
    return pl.pallas_call(
        functools.partial(_rank_kernel, tm=tm),
        grid=(t // tm,),
        in_specs=[pl.BlockSpec((tm, LANES), lambda i: (i, 0))],
        out_specs=[pl.BlockSpec((tm, LANES), lambda i: (i, 0)), _full((1, LANES))],
        out_shape=[jax.ShapeDtypeStruct((t, LANES), I32), jax.ShapeDtypeStruct((1, LANES), F32)],
        compiler_params=_params(),
        name="rank",
    )(eid)


def _dest_kernel(eid_ref, rank_ref, start_ref, dest_ref):
    eid = eid_ref[...]
    lane = lax.broadcasted_iota(I32, eid.shape, 1)
    start = start_ref[...]
    s1 = jnp.sum(jnp.where(lane == eid[:, 0:1], start, 0.0), axis=1, keepdims=True).astype(I32)
    s2 = jnp.sum(jnp.where(lane == eid[:, 1:2], start, 0.0), axis=1, keepdims=True).astype(I32)
    dest = jnp.where(lane == 0, s1, jnp.where(lane == 1, s2, 0)) + rank_ref[...]
    dest_ref[...] = jnp.transpose(dest.astype(F32))[:SUBLANES, :].astype(I32)


def _dest(eid, rank, start_row):
    t = eid.shape[0]
    tm = min(TM_ROUTE, t)
    tok = pl.BlockSpec((tm, LANES), lambda i: (i, 0))
    return pl.pallas_call(
        _dest_kernel,
        grid=(t // tm,),
        in_specs=[tok, tok, _full((1, LANES))],
        out_specs=pl.BlockSpec((SUBLANES, tm), lambda i: (0, i)),
        out_shape=jax.ShapeDtypeStruct((SUBLANES, t), I32),
        compiler_params=_params(),
        name="dest",
    )(eid, rank, start_row)


def _row_copy(src_ref, src_row, dst_ref, dst_row, sem):
    return pltpu.make_async_copy(src_ref.at[pl.ds(src_row, 1)], dst_ref.at[pl.ds(dst_row, 1)], sem)


def _rows_copy(src_ref, dst_ref, n, sem):
    return pltpu.make_async_copy(src_ref.at[pl.ds(0, n)], dst_ref.at[pl.ds(0, n)], sem)


def _scatter_kernel(dest_ref, h_ref, xs_in_ref, xs_ref, sem, *, tm):
    del xs_in_ref

    def issue(t, _):
        for j in range(2):
            _row_copy(h_ref, t, xs_ref, dest_ref[j, t], sem).start()
        return 0

    lax.fori_loop(0, tm, issue, 0)
    for _ in range(2):
        _rows_copy(h_ref, xs_ref, tm, sem).wait()


def _scatter_rows(dest_t, h, n_rows):
    t = h.shape[0]
    tm = min(TM_MOVE, t)
    xs0 = jnp.zeros((n_rows, D_MODEL), F32)
    return pl.pallas_call(
        functools.partial(_scatter_kernel, tm=tm),
        grid=(t // tm,),
        in_specs=[pl.BlockSpec((SUBLANES, tm), lambda i: (0, i), memory_space=pltpu.SMEM),
                  pl.BlockSpec((tm, D_MODEL), lambda i: (i, 0)),
                  pl.BlockSpec(memory_space=pl.ANY)],
        out_specs=pl.BlockSpec(memory_space=pl.ANY),
        out_shape=jax.ShapeDtypeStruct((n_rows, D_MODEL), F32),
        scratch_shapes=[pltpu.SemaphoreType.DMA(())],
        input_output_aliases={2: 0},
        compiler_params=_params(),
        name="scatter_rows",
    )(dest_t, h, xs0)


def _expert_kernel(be_ref, nused_ref, xs_ref, wg_ref, wu_ref, wd_ref, y_ref, wg_bf, wu_bf, wd_bf):
    b = pl.program_id(0)

    @pl.when((b == 0) | (be_ref[b] != be_ref[jnp.maximum(b - 1, 0)]))
    def _():
        wg_bf[...] = wg_ref[...].astype(BF16)
        wu_bf[...] = wu_ref[...].astype(BF16)
        wd_bf[...] = wd_ref[...].astype(BF16)

    @pl.when(b < nused_ref[0])
    def _():
        xb = xs_ref[...].astype(BF16)
        gate = jnp.dot(xb, wg_bf[...], preferred_element_type=F32)
        up = jnp.dot(xb, wu_bf[...], preferred_element_type=F32)
        hid = (jax.nn.silu(gate) * up).astype(BF16)
        y_ref[...] = jnp.dot(hid, wd_bf[...], preferred_element_type=F32)

    @pl.when(b >= nused_ref[0])
    def _():
        y_ref[...] = jnp.zeros_like(y_ref)


def _experts(block_exp, n_used, xs, wg, wu, wd):
    n_rows = xs.shape[0]
    grid_spec = pltpu.PrefetchScalarGridSpec(
        num_scalar_prefetch=2,
        grid=(n_rows // EXP_BLOCK,),
        in_specs=[pl.BlockSpec((EXP_BLOCK, D_MODEL), lambda b, be, nu: (b, 0)),
                  pl.BlockSpec((None, D_MODEL, D_FF_EXPERT), lambda b, be, nu: (be[b], 0, 0)),
                  pl.BlockSpec((None, D_MODEL, D_FF_EXPERT), lambda b, be, nu: (be[b], 0, 0)),
                  pl.BlockSpec((None, D_FF_EXPERT, D_MODEL), lambda b, be, nu: (be[b], 0, 0))],
        out_specs=pl.BlockSpec((EXP_BLOCK, D_MODEL), lambda b, be, nu: (b, 0)),
        scratch_shapes=[pltpu.VMEM((D_MODEL, D_FF_EXPERT), BF16), pltpu.VMEM((D_MODEL, D_FF_EXPERT), BF16),
                        pltpu.VMEM((D_FF_EXPERT, D_MODEL), BF16)],
    )
    return pl.pallas_call(
        _expert_kernel,
        grid_spec=grid_spec,
        out_shape=jax.ShapeDtypeStruct((n_rows, D_MODEL), F32),
        compiler_params=_params(),
        name="experts",
    )(block_exp, n_used, xs, wg, wu, wd)


def _combine_kernel(dest_ref, dest_next_ref, x_ref, wt_ref, g_ref, y_ref, o_ref, buf_ref, sems,
                    *, tm, steps, final_norm):
    i = pl.program_id(0)

    def gather(idx_ref, slot):
        def issue(t, _):
            for j in range(2):
                _row_copy(y_ref, idx_ref[j, t], buf_ref.at[slot], j * tm + t, sems.at[slot]).start()
            return 0

        lax.fori_loop(0, tm, issue, 0)

    @pl.when(i == 0)
    def _():
        gather(dest_ref, 0)

    @pl.when(i + 1 < steps)
    def _():
        gather(dest_next_ref, (i + 1) % 2)

    slot = i % 2
    _rows_copy(y_ref, buf_ref.at[slot], 2 * tm, sems.at[slot]).wait()
    wt = wt_ref[...]
    rows = buf_ref[slot]
    out = x_ref[...] + (wt[:, 0:1] * rows[:tm, :] + wt[:, 1:2] * rows[tm:, :])
    if final_norm:
        out = _rms(out, g_ref[...])
    o_ref[...] = out


def _combine(dest_t, x2, wt, g_final, y, final_norm):
    t = x2.shape[0]
    tm = min(TM_MOVE, t)
    steps = t // tm
    tok = lambda n: pl.BlockSpec((tm, n), lambda i: (i, 0))
    idx = lambda f: pl.BlockSpec((SUBLANES, tm), f, memory_space=pltpu.SMEM)
    return pl.pallas_call(
        functools.partial(_combine_kernel, tm=tm, steps=steps, final_norm=final_norm),
        grid=(steps,),
        in_specs=[idx(lambda i: (0, i)), idx(lambda i: (0, jnp.minimum(i + 1, steps - 1))),
                  tok(D_MODEL), tok(LANES), _full((1, D_MODEL)), pl.BlockSpec(memory_space=pl.ANY)],
        out_specs=tok(D_MODEL),
        out_shape=jax.ShapeDtypeStruct((t, D_MODEL), F32),
        scratch_shapes=[pltpu.VMEM((2, 2 * tm, D_MODEL), F32), pltpu.SemaphoreType.DMA((2,))],
        compiler_params=_params(),
        name="combine",
    )(dest_t, dest_t, x2, wt, g_final, y)


def _split_w_in(w_in_l):
    sizes = [A_WIDTH, A_WIDTH, B_WIDTH, B_WIDTH, B_WIDTH, IDX_HEADS * IDX_DIM, IDX_DIM, IDX_HEADS,
             C_WIDTH, C_WIDTH, C_WIDTH, N_BRANCH * D_MODEL]
    cuts = np.cumsum([0] + sizes)
    u, va, q, k, vb, qi, ki, wi, cb, cc, cx, gates = [w_in_l[:, a:b] for a, b in zip(cuts[:-1], cuts[1:])]
    pad = lambda w, n: jnp.pad(w, ((0, 0), (0, n - w.shape[1])))
    w_a = jnp.concatenate([u, va], axis=1).astype(BF16)
    w_b = jnp.concatenate([k, pad(ki, LANES)], axis=1).astype(BF16)
    w_bt = jnp.concatenate([q, vb, qi, pad(wi, WI_ROWS)], axis=1).T.astype(BF16)
    w_c = jnp.concatenate([cb, cc, cx], axis=1).astype(BF16)
    return w_a, w_b, w_bt, w_c, gates.astype(BF16)


def _token_mixers(x2, g, w_in_l, ws, bs, conv_w, wpa, wpb, wpc, w_out, tabs, bsz, seq):
    w_a, w_b, w_bt, w_c, w_g = _split_w_in(w_in_l)
    n_sel = min(TOPK_MAX, seq // 4)
    k, ki, qt, vt_aug, qit, wit = _proj_b(x2, g, w_b, w_bt, *tabs)
    ybt = _dsa(qit, wit, ki, qt, k, vt_aug, bsz, seq, n_sel)
    ya = _proj_a(x2, g, w_a, ws.astype(BF16), bs.T)
    yc = _proj_c(x2, g, w_c, conv_w, seq)
    bf = lambda w: w.astype(BF16)
    return _merge(x2, g, w_g, ya, ybt, yc, bf(wpa), bf(wpb), bf(wpc), bf(w_out))


def _hier_moe(x2, g, rg_w, rg_b, re_w, re_b, w_gate, w_up, w_down, g_final, final_norm):
    t = x2.shape[0]
    pad_c = LANES - N_GROUPS - N_EXPERTS
    w_r = jnp.pad(jnp.concatenate([rg_w, re_w], axis=1), ((0, 0), (0, pad_c)))
    b_r = jnp.pad(jnp.concatenate([rg_b, re_b]), (0, pad_c))[None, :]
    h, eid, wt = _router(x2, g, w_r, b_r)
    rank, counts = _rank(eid)
    cnt = counts[0, :N_EXPERTS].astype(I32)
    padded = ((cnt + EXP_BLOCK - 1) // EXP_BLOCK) * EXP_BLOCK
    pend = jnp.cumsum(padded)
    pstart = pend - padded
    n_blocks = (2 * t) // EXP_BLOCK + N_EXPERTS
    n_rows = n_blocks * EXP_BLOCK
    block_row = jnp.arange(n_blocks, dtype=I32) * EXP_BLOCK
    block_exp = jnp.minimum(jnp.sum((pend[None, :] <= block_row[:, None]).astype(I32), axis=1),
                            N_EXPERTS - 1)
    n_used = (pend[-1:] // EXP_BLOCK).astype(I32)
    start_row = jnp.pad(pstart.astype(F32), (0, LANES - N_EXPERTS))[None, :]
    dest_t = _dest(eid, rank, start_row)
    xs = _scatter_rows(dest_t, h, n_rows)
    y = _experts(block_exp, n_used, xs, w_gate, w_up, w_down)
    return _combine(dest_t, x2, wt, g_final, y, final_norm)


def kernel(x, positions, norm_mix_g, norm_ffn_g, norm_final_g, w_in, gmlp_ws, gmlp_b, conv_w, w_proj_a,
           w_proj_b, w_proj_c, w_out, router_group_w, router_group_b, router_expert_w, router_expert_b,
           expert_w_gate, expert_w_up, expert_w_down):
    bsz, seq, d = x.shape
    depth = w_in.shape[0]
    x2 = x.reshape(bsz * seq, d)
    pos = positions.astype(F32).reshape(bsz * seq)
    tabs = (_rope_tables(pos[:, None], B_HEAD_DIM, B_ROT), _rope_tables(pos[:, None], IDX_DIM, IDX_ROT),
            _rope_tables_t(pos[None, :], B_ROT), _rope_tables_t(pos[None, :], IDX_ROT))
    g_final = norm_final_g[None, :]
    for l in range(depth):
        x2 = _token_mixers(x2, norm_mix_g[l][None, :], w_in[l], gmlp_ws[l], gmlp_b[l], conv_w[l],
                           w_proj_a[l], w_proj_b[l], w_proj_c[l], w_out[l], tabs, bsz, seq)
        x2 = _hier_moe(x2, norm_ffn_g[l][None, :], router_group_w[l], router_group_b[l],
                       router_expert_w[l], router_expert_b[l], expert_w_gate[l], expert_w_up[l],
                       expert_w_down[l], g_final, l == depth - 1)
    return x2.reshape(bsz, seq, d)
```

```python
import functools

import numpy as np
import jax
import jax.numpy as jnp
from jax import lax
from jax.experimental import pallas as pl
from jax.experimental.pallas import tpu as pltpu

D_MODEL = 1024
CHUNK = 128
A_GROUPS = 4
A_GROUP_DIM = 128
A_WIDTH = A_GROUPS * A_GROUP_DIM
B_HEADS = 4
B_HEAD_DIM = 128
B_WIDTH = B_HEADS * B_HEAD_DIM
B_ROT = B_HEAD_DIM // 4
IDX_HEADS = 4
IDX_DIM = 64
IDX_ROT = IDX_DIM // 4
TOPK_MAX = 256
C_WIDTH = 512
CONV_WIDTH = 3
N_BRANCH = 3
N_GROUPS = 4
EXP_PER_GROUP = 8
N_EXPERTS = N_GROUPS * EXP_PER_GROUP
D_FF_EXPERT = 512
ROPE_THETA = 500000.0
EPS = 1e-6
NEG = -1e30

LANES = 128
SUBLANES = 8
VMEM_LIMIT = 56 * 1024 * 1024

TM_PROJ = 512
TM_MERGE = 256
TM_ROUTE = 512
TM_RANK = 256
TM_MOVE = 256
EXP_BLOCK = 256
DSA_TQ = 128
DSA_KC = 256
DSA_SKC = 512
DSA_SUB = 256
DSA_RB = 512
V_ROWS = B_HEAD_DIM + 16
LOG2E = 1.4426950408889634

F32 = jnp.float32
BF16 = jnp.bfloat16
I32 = jnp.int32
INT_MIN = -2 ** 31


def _monotone_key_of(value):
    bits = int(np.array(value, np.float32).view(np.int32))
    return bits ^ ((bits >> 31) & 0x7FFFFFFF)


NEG_KEY = _monotone_key_of(NEG)


def _params(n_axes=1, semantics=None):
    return pltpu.CompilerParams(
        dimension_semantics=semantics or ("arbitrary",) * n_axes,
        vmem_limit_bytes=VMEM_LIMIT)


def _rms(x, g):
    ms = jnp.mean(x * x, axis=-1, keepdims=True)
    return x * lax.rsqrt(ms + EPS) * g


def _full(shape):
    nd = len(shape)
    return pl.BlockSpec(shape, lambda *_: (0,) * nd)


def _rope_table_kernel(pos_ref, invf_ref, mrot_ref, ma_ref, mb_ref, c_ref, sa_ref, sb_ref):
    ang = pos_ref[...] * invf_ref[...]
    c = jnp.cos(ang)
    s = jnp.sin(ang)
    c_ref[...] = jnp.where(mrot_ref[...] > 0, c, 1.0)
    sa_ref[...] = jnp.where(ma_ref[...] > 0, -s, 0.0)
    sb_ref[...] = jnp.where(mb_ref[...] > 0, s, 0.0)


def _rope_tables(pos_col, head_dim, rot):
    t = pos_col.shape[0]
    half = rot // 2
    inv_freq = jnp.float32(ROPE_THETA) ** (-jnp.arange(half, dtype=F32) * 2.0 / rot)
    lane = np.arange(LANES) % head_dim
    mrot = lane < rot
    ma = lane < half
    mb = (lane >= half) & (lane < rot)
    invf = jnp.where(jnp.asarray(mrot), inv_freq[np.where(mrot, lane % half, 0)], 0.0)[None, :]
    row = lambda m: jnp.asarray(m.astype(np.float32))[None, :]
    tm = min(1024, t)
    spec_row = _full((1, LANES))
    out_spec = pl.BlockSpec((tm, LANES), lambda i: (i, 0))
    return pl.pallas_call(
        _rope_table_kernel,
        grid=(t // tm,),
        in_specs=[pl.BlockSpec((tm, 1), lambda i: (i, 0)), spec_row, spec_row, spec_row, spec_row],
        out_specs=[out_spec] * 3,
        out_shape=[jax.ShapeDtypeStruct((t, LANES), F32)] * 3,
        compiler_params=_params(),
        name="rope_tables",
    )(pos_col, invf, row(mrot), row(ma), row(mb))


def _rope_table_t_kernel(pos_ref, invf_ref, c_ref, s_ref):
    ang = invf_ref[...] * pos_ref[...]
    c_ref[...] = jnp.cos(ang)
    s_ref[...] = jnp.sin(ang)


def _rope_tables_t(pos_row, rot):
    t = pos_row.shape[1]
    half = rot // 2
    inv_freq = jnp.float32(ROPE_THETA) ** (-jnp.arange(half, dtype=F32) * 2.0 / rot)
    tm = min(2048, t)
    out_spec = pl.BlockSpec((half, tm), lambda i: (0, i))
    return pl.pallas_call(
        _rope_table_t_kernel,
        grid=(t // tm,),
        in_specs=[pl.BlockSpec((1, tm), lambda i: (0, i)), _full((half, 1))],
        out_specs=[out_spec] * 2,
        out_shape=[jax.ShapeDtypeStruct((half, t), F32)] * 2,
        compiler_params=_params(),
        name="rope_tables_t",
    )(pos_row, inv_freq[:, None])


def _rope_t(x, c, s):
    half = c.shape[0]
    x1, x2 = x[:half, :], x[half:2 * half, :]
    return jnp.concatenate([x1 * c - x2 * s, x1 * s + x2 * c, x[2 * half:, :]], axis=0)


def _rope(x, c, sa, sb, half):
    parts = []
    for j in range(x.shape[1] // LANES):
        xj = x[:, j * LANES:(j + 1) * LANES]
        parts.append(xj * c + pltpu.roll(xj, LANES - half, 1) * sa + pltpu.roll(xj, half, 1) * sb)
    return parts


WI_ROWS = 16


def _proj_b_kernel(x_ref, g_ref, w_ref, wt_ref, cb_ref, sab_ref, sbb_ref, ci_ref, sai_ref, sbi_ref,
                   cbt_ref, sbt_ref, cit_ref, sit_ref, k_ref, ki_ref, qt_ref, vt_ref, qit_ref, wit_ref):
    h = _rms(x_ref[...], g_ref[...]).astype(BF16)
    tm = h.shape[0]
    p = jnp.dot(h, w_ref[...], preferred_element_type=F32)
    pt = lax.dot_general(wt_ref[...], h, (((1,), (1,)), ((), ())), preferred_element_type=F32)
    tb = (cb_ref[...], sab_ref[...], sbb_ref[...])
    ti = (ci_ref[...], sai_ref[...], sbi_ref[...])
    for j, kj in enumerate(_rope(p[:, :B_WIDTH], *tb, B_ROT // 2)):
        k_ref[:, j * LANES:(j + 1) * LANES] = kj.astype(BF16)
    ki_ref[...] = _rope(p[:, B_WIDTH:B_WIDTH + LANES], *ti, IDX_ROT // 2)[0].astype(BF16)
    cbt, sbt = cbt_ref[...], sbt_ref[...]
    for hh in range(B_HEADS):
        hs = slice(hh * B_HEAD_DIM, (hh + 1) * B_HEAD_DIM)
        qt_ref[hs, :] = (_rope_t(pt[hs, :], cbt, sbt) * (B_HEAD_DIM ** -0.5 * LOG2E)).astype(BF16)
    o = B_WIDTH
    for hh in range(B_HEADS):
        vt_ref[hh * V_ROWS:hh * V_ROWS + B_HEAD_DIM, :] = (
            pt[o + hh * B_HEAD_DIM:o + (hh + 1) * B_HEAD_DIM, :].astype(BF16))
        vt_ref[hh * V_ROWS + B_HEAD_DIM:(hh + 1) * V_ROWS, :] = jnp.ones((V_ROWS - B_HEAD_DIM, tm), BF16)
    o += B_WIDTH
    cit, sit = cit_ref[...], sit_ref[...]
    for hh in range(IDX_HEADS):
        hs = slice(hh * IDX_DIM, (hh + 1) * IDX_DIM)
        qit_ref[hs, :] = _rope_t(pt[o + hh * IDX_DIM:o + (hh + 1) * IDX_DIM, :], cit, sit).astype(BF16)
    o += IDX_HEADS * IDX_DIM
    wit_ref[...] = pt[o:o + SUBLANES, :] * ((IDX_HEADS ** -0.5) * (IDX_DIM ** -0.5))


def _proj_b(x2, g, w, wt, tabs_b, tabs_i, tabs_bt, tabs_it):
    t = x2.shape[0]
    tm = min(TM_PROJ, t)
    tok = lambda n: pl.BlockSpec((tm, n), lambda i: (i, 0))
    feat = lambda n: pl.BlockSpec((n, tm), lambda i: (0, i))
    outs_tok = [(B_WIDTH, BF16), (LANES, BF16)]
    outs_feat = [(B_WIDTH, BF16), (B_HEADS * V_ROWS, BF16), (IDX_HEADS * IDX_DIM, BF16), (SUBLANES, F32)]
    return pl.pallas_call(
        _proj_b_kernel,
        grid=(t // tm,),
        in_specs=[tok(D_MODEL), _full((1, D_MODEL)), _full(w.shape), _full(wt.shape)] + [tok(LANES)] * 6
                 + [feat(B_ROT // 2)] * 2 + [feat(IDX_ROT // 2)] * 2,
        out_specs=[tok(n) for n, _ in outs_tok] + [feat(n) for n, _ in outs_feat],
        out_shape=[jax.ShapeDtypeStruct((t, n), d) for n, d in outs_tok]
                  + [jax.ShapeDtypeStruct((n, t), d) for n, d in outs_feat],
        compiler_params=_params(),
        name="proj_b",
    )(x2, g, w, wt, *tabs_b, *tabs_i, *tabs_bt, *tabs_it)


def _ordered_key(score):
    bits = lax.bitcast_convert_type(score, I32)
    key = bits ^ ((bits >> 31) & 0x7FFFFFFF)
    return jnp.where(key == -1, 0, key)


def _score_of_key(key):
    return lax.bitcast_convert_type(key ^ ((key >> 31) & 0x7FFFFFFF), F32)


def _ce_desc(vals, i, j):
    a, b = vals[i], vals[j]
    vals[i] = jnp.maximum(a, b)
    vals[j] = jnp.minimum(a, b)


def _bitonic_sort(vals, desc=True):
    n = len(vals)
    k = 2
    while k <= n:
        j = k // 2
        while j >= 1:
            for i in range(n):
                l = i ^ j
                if l > i:
                    if ((i & k) == 0) == desc:
                        _ce_desc(vals, i, l)
                    else:
                        _ce_desc(vals, l, i)
            j //= 2
        k *= 2


def _bitonic_merge_desc(vals):
    n = len(vals)
    j = n // 2
    while j >= 1:
        for i in range(n):
            l = i ^ j
            if l > i:
                _ce_desc(vals, i, l)
        j //= 2


def _dsa_kernel(qit_ref, wit_ref, ki_ref, qt_ref, k_ref, vt_ref, o_ref, sc_ref, top_ref, topk_ref, acc_ref,
                bias0_ref, bias1_ref, lg0_ref, lg1_ref,
                *, seq, n_sel, tq, kc, skc, sub, rb):
    i = pl.program_id(1)
    q0 = i * tq
    n_full = q0 // skc
    n_rows = (n_full + 1) * skc
    n_pad = (seq - 1 - q0) - lax.broadcasted_iota(I32, (1, tq), 1)
    n_chunk = (q0 + tq + kc - 1) // kc
    wit = wit_ref[...]

    n_top = skc // SUBLANES
    top_ref[...] = jnp.full((skc, tq), -jnp.inf, F32)

    def score_batch(c, masked):
        halves = []
        for s in range(skc // sub):
            r0 = pl.multiple_of(c * skc + s * sub, sub)
            kic = ki_ref[pl.ds(r0, sub), :IDX_DIM]
            sc = None
            for h in range(IDX_HEADS):
                a = jnp.dot(kic, qit_ref[h * IDX_DIM:(h + 1) * IDX_DIM, :], preferred_element_type=F32)
                term = jnp.maximum(a, 0.0) * wit[h:h + 1, :]
                sc = term if sc is None else sc + term
            if masked:
                qpos = q0 + lax.broadcasted_iota(I32, (sub, tq), 1)
                krow = r0 + lax.broadcasted_iota(I32, (sub, tq), 0)
                sc = jnp.where(krow <= qpos, sc, -jnp.inf)
            sc_ref[pl.ds(r0, sub), :] = sc
            half = [sc[SUBLANES * i:SUBLANES * (i + 1), :] for i in range(sub // SUBLANES)]
            _bitonic_sort(half, desc=(s % 2 == 0))
            halves.append(half)
        vals = halves[0] + halves[1]
        _bitonic_merge_desc(vals)
        return vals

    def keep_top(batches):
        top = [top_ref[SUBLANES * i:SUBLANES * (i + 1), :] for i in range(n_top)]
        for vals in batches:
            top = [jnp.maximum(top[i], vals[n_top - 1 - i]) for i in range(n_top)]
            _bitonic_merge_desc(top)
        for i in range(n_top):
            top_ref[SUBLANES * i:SUBLANES * (i + 1), :] = top[i]

    def score_pair(j, _):
        keep_top([score_batch(2 * j, False), score_batch(2 * j + 1, False)])
        return 0

    lax.fori_loop(0, n_full // 2, score_pair, 0)

    @pl.when(n_full % 2 == 1)
    def _():
        keep_top([score_batch(n_full - 1, False)])

    keep_top([score_batch(n_full, True)])

    def search(count):
        def bit_body(b, carry):
            ans_u, = carry
            cand_u = ans_u | jnp.left_shift(jnp.int32(1), 31 - b)
            ok = count(cand_u ^ INT_MIN, False) >= n_sel
            return (jnp.where(ok, cand_u, ans_u),)

        ans_u, = lax.fori_loop(0, 32, bit_body, (jnp.zeros((1, tq), I32),))
        thr = ans_u ^ INT_MIN
        return thr, (n_sel - count(thr, True)).astype(F32)

    def pads(cand, strict):
        return jnp.where((NEG_KEY > cand) if strict else (NEG_KEY >= cand), n_pad, 0)

    topk_ref[...] = _ordered_key(top_ref[...])

    def count_top(cand, strict):
        t = topk_ref[...]
        hit = (t > cand) if strict else (t >= cand)
        return jnp.sum(jnp.where(hit, 1.0, 0.0), axis=0, keepdims=True).astype(I32) + pads(cand, strict)

    def search_top():
        def digit_body(b, carry):
            ans_u, = carry
            shift = 30 - 2 * b
            for d in (1, 2, 3):
                cand_u = carry[0] | jnp.left_shift(jnp.int32(d), shift)
                ans_u = jnp.where(count_top(cand_u ^ INT_MIN, False) >= n_sel, cand_u, ans_u)
            return (ans_u,)

        ans_u, = lax.fori_loop(0, 16, digit_body, (jnp.zeros((1, tq), I32),))
        thr = ans_u ^ INT_MIN
        return thr, (n_sel - count_top(thr, True)).astype(F32)

    def count_all(cand, strict):
        acc_rows = 4 * SUBLANES
        cand_f = _score_of_key(cand)

        def body(r, acc):
            r0 = pl.multiple_of(r * rb, rb)
            blk = sc_ref[pl.ds(r0, rb), :]
            for j in range(rb // acc_rows):
                part = blk[j * acc_rows:(j + 1) * acc_rows, :]
                hit = (part > cand_f) if strict else (part >= cand_f)
                acc = acc + jnp.where(hit, 1, 0)
            return acc

        acc = lax.fori_loop(0, n_rows // rb, body, jnp.zeros((acc_rows, tq), I32))
        return jnp.sum(acc.astype(F32), axis=0, keepdims=True).astype(I32) + pads(cand, strict)

    thr, need = search_top()
    last = topk_ref[skc - SUBLANES:skc, :]
    overflow = jnp.max(jnp.where(last > thr, 1.0, 0.0))
    thr, need = lax.cond(overflow > 0.0, lambda: search(count_all), lambda: (thr, need))
    thr_f = _score_of_key(thr)

    acc_ref[...] = jnp.zeros_like(acc_ref)
    bias_refs = (bias0_ref, bias1_ref)
    lg_refs = (lg0_ref, lg1_ref)
    hk = kc // 2
    lower = (lax.broadcasted_iota(I32, (hk, hk), 1) < lax.broadcasted_iota(I32, (hk, hk), 0))
    lower = jnp.where(lower, 1.0, 0.0).astype(BF16)

    def produce(c, slot, tie_base):
        r0 = pl.multiple_of(c * kc, kc)
        sc = sc_ref[pl.ds(r0, kc), :]
        tie = sc == thr_f
        tie_f = jnp.where(tie, 1.0, 0.0)
        tie_b = tie_f.astype(BF16)
        b0 = jnp.dot(lower, tie_b[:hk, :], preferred_element_type=F32)
        n0 = b0[hk - 1:hk, :] + tie_f[hk - 1:hk, :]
        b1 = jnp.dot(lower, tie_b[hk:, :], preferred_element_type=F32) + n0
        before = jnp.concatenate([b0, b1], axis=0)
        sel = (sc > thr_f) | (tie & (before < need - tie_base))
        bias_refs[slot][...] = jnp.where(sel, 0.0, NEG)
        for h in range(B_HEADS):
            hs = slice(h * B_HEAD_DIM, (h + 1) * B_HEAD_DIM)
            lg_refs[slot][h] = jnp.dot(k_ref[pl.ds(r0, kc), hs], qt_ref[hs, :], preferred_element_type=F32)
        return tie_base + b1[hk - 1:hk, :] + tie_f[kc - 1:kc, :]

    def consume(c, slot, ms):
        r0 = pl.multiple_of(c * kc, kc)
        bias = bias_refs[slot][...]
        new_ms = []
        for h in range(B_HEADS):
            vs = slice(h * V_ROWS, (h + 1) * V_ROWS)
            lg = lg_refs[slot][h] + bias
            m_new = jnp.maximum(ms[h], jnp.max(lg, axis=0, keepdims=True))
            alpha = jnp.exp2(ms[h] - m_new)
            p = jnp.exp2(lg - m_new).astype(BF16)
            pv = jnp.dot(vt_ref[vs, pl.ds(r0, kc)], p, preferred_element_type=F32)
            acc_ref[vs, :] = acc_ref[vs, :] * alpha + pv
            new_ms.append(m_new)
        return tuple(new_ms)

    def pair_body(j, carry):
        ms, tie_base = carry
        c = 2 * j
        tie_base = produce(c + 1, 1, tie_base)
        ms = consume(c, 0, ms)
        tie_base = produce(c + 2, 0, tie_base)
        ms = consume(c + 1, 1, ms)
        return ms, tie_base

    row = lambda v: jnp.full((1, tq), v, F32)
    ms0 = tuple(row(NEG) for _ in range(B_HEADS))
    n_pairs = (n_chunk - 1) // 2
    ms, tie_base = lax.fori_loop(0, n_pairs, pair_body, (ms0, produce(0, 0, row(0.0))))
    c0 = 2 * n_pairs

    @pl.when(c0 == n_chunk - 1)
    def _():
        consume(c0, 0, ms)

    @pl.when(c0 != n_chunk - 1)
    def _():
        produce(c0 + 1, 1, tie_base)
        consume(c0 + 1, 1, consume(c0, 0, ms))

    for h in range(B_HEADS):
        a = acc_ref[h * V_ROWS:(h + 1) * V_ROWS, :]
        o_ref[h * B_HEAD_DIM:(h + 1) * B_HEAD_DIM, :] = (
            a[:B_HEAD_DIM, :] / a[B_HEAD_DIM:B_HEAD_DIM + 1, :]).astype(o_ref.dtype)


def _dsa(qit, wit, ki, qt, k, vt_aug, bsz, seq, n_sel):
    tq = min(DSA_TQ, seq)
    kc = min(DSA_KC, seq)
    skc = min(DSA_SKC, seq)
    nq = seq // tq
    once = pl.Buffered(1)
    kern = functools.partial(_dsa_kernel, seq=seq, n_sel=n_sel, tq=tq, kc=kc, skc=skc,
                             sub=min(DSA_SUB, skc), rb=min(DSA_RB, skc))
    q_tile = lambda rows: pl.BlockSpec((rows, tq), lambda b, i: (0, b * nq + i))
    return pl.pallas_call(
        kern,
        grid=(bsz, nq),
        in_specs=[
            q_tile(IDX_HEADS * IDX_DIM),
            q_tile(SUBLANES),
            pl.BlockSpec((seq, LANES), lambda b, i: (b, 0), pipeline_mode=once),
            q_tile(B_WIDTH),
            pl.BlockSpec((seq, B_WIDTH), lambda b, i: (b, 0), pipeline_mode=once),
            pl.BlockSpec((B_HEADS * V_ROWS, seq), lambda b, i: (0, b), pipeline_mode=once),
        ],
        out_specs=q_tile(B_WIDTH),
        out_shape=jax.ShapeDtypeStruct((B_WIDTH, bsz * seq), BF16),
        scratch_shapes=[pltpu.VMEM((seq, tq), F32), pltpu.VMEM((skc, tq), F32), pltpu.VMEM((skc, tq), I32),
                        pltpu.VMEM((B_HEADS * V_ROWS, tq), F32), pltpu.VMEM((kc, tq), F32),
                        pltpu.VMEM((kc, tq), F32), pltpu.VMEM((B_HEADS, kc, tq), F32),
                        pltpu.VMEM((B_HEADS, kc, tq), F32)],
        compiler_params=_params(2),
        name="dsa",
    )(qit, wit, ki, qt, k, vt_aug)


def _proj_a_kernel(x_ref, g_ref, w_ref, ws_ref, bs_ref, ya_ref, *, tm):
    h = _rms(x_ref[...], g_ref[...]).astype(BF16)
    p = jnp.dot(h, w_ref[...], preferred_element_type=F32)
    u = jax.nn.gelu(p[:, :A_WIDTH])
    v = jax.nn.gelu(p[:, A_WIDTH:])
    mu = jnp.mean(v, axis=-1, keepdims=True)
    var = jnp.mean(jnp.square(v - mu), axis=-1, keepdims=True)
    v = ((v - mu) * lax.rsqrt(var + EPS)).astype(BF16)
    causal = (lax.broadcasted_iota(I32, (CHUNK, CHUNK), 1) <= lax.broadcasted_iota(I32, (CHUNK, CHUNK), 0))
    bs = bs_ref[...]
    for gidx in range(A_GROUPS):
        wsm = jnp.where(causal, ws_ref[gidx], 0.0).astype(BF16)
        cs = slice(gidx * A_GROUP_DIM, (gidx + 1) * A_GROUP_DIM)
        for n in range(tm // CHUNK):
            rs = slice(n * CHUNK, (n + 1) * CHUNK)
            vm = jnp.dot(wsm, v[rs, cs], preferred_element_type=F32) + bs[:, gidx:gidx + 1]
            ya_ref[rs, cs] = (u[rs, cs] * vm).astype(ya_ref.dtype)


def _proj_a(x2, g, w, ws, bs_t):
    t = x2.shape[0]
    tm = min(TM_PROJ, t)
    return pl.pallas_call(
        functools.partial(_proj_a_kernel, tm=tm),
        grid=(t // tm,),
        in_specs=[pl.BlockSpec((tm, D_MODEL), lambda i: (i, 0)), _full((1, D_MODEL)),
                  _full((D_MODEL, 2 * A_WIDTH)), _full((A_GROUPS, CHUNK, CHUNK)), _full((CHUNK, A_GROUPS))],
        out_specs=pl.BlockSpec((tm, A_WIDTH), lambda i: (i, 0)),
        out_shape=jax.ShapeDtypeStruct((t, A_WIDTH), BF16),
        compiler_params=_params(),
        name="proj_a",
    )(x2, g, w, ws, bs_t)


def _proj_c_kernel(x_ref, xh_ref, g_ref, w_ref, cw_ref, yc_ref, *, tm, tiles_per_seq):
    g = g_ref[...]
    w = w_ref[...]
    p = jnp.dot(_rms(x_ref[...], g).astype(BF16), w, preferred_element_type=F32)
    ph = jnp.dot(_rms(xh_ref[...], g).astype(BF16), w[:, C_WIDTH:], preferred_element_type=F32)
    first = (pl.program_id(0) % tiles_per_seq) == 0
    zh = ph[:, :C_WIDTH] * ph[:, C_WIDTH:] * jnp.where(first, 0.0, 1.0)
    z = p[:, C_WIDTH:2 * C_WIDTH] * p[:, 2 * C_WIDTH:]
    row = lax.broadcasted_iota(I32, (tm, C_WIDTH), 0)
    z1 = jnp.where(row == 0, zh[SUBLANES - 1:SUBLANES, :], pltpu.roll(z, 1, 0))
    z2 = jnp.where(row == 0, zh[SUBLANES - 2:SUBLANES - 1, :],
                   jnp.where(row == 1, zh[SUBLANES - 1:SUBLANES, :], pltpu.roll(z, 2, 0)))
    cw = cw_ref[...]
    y = cw[0:1, :] * z2 + cw[1:2, :] * z1 + cw[2:3, :] * z
    yc_ref[...] = (p[:, :C_WIDTH] * y).astype(yc_ref.dtype)


def _proj_c(x2, g, w, cw, seq):
    t = x2.shape[0]
    tm = min(TM_PROJ, seq)
    per8 = tm // SUBLANES
    return pl.pallas_call(
        functools.partial(_proj_c_kernel, tm=tm, tiles_per_seq=seq // tm),
        grid=(t // tm,),
        in_specs=[pl.BlockSpec((tm, D_MODEL), lambda i: (i, 0)),
                  pl.BlockSpec((SUBLANES, D_MODEL), lambda i: (jnp.maximum(i * per8 - 1, 0), 0)),
                  _full((1, D_MODEL)), _full((D_MODEL, 3 * C_WIDTH)), _full((CONV_WIDTH, C_WIDTH))],
        out_specs=pl.BlockSpec((tm, C_WIDTH), lambda i: (i, 0)),
        out_shape=jax.ShapeDtypeStruct((t, C_WIDTH), BF16),
        compiler_params=_params(),
        name="proj_c",
    )(x2, x2, g, w, cw)


def _merge_kernel(x_ref, g_ref, wg_ref, ya_ref, ybt_ref, yc_ref, wa_ref, wb_ref, wc_ref, wo_ref, o_ref):
    x = x_ref[...]
    h = _rms(x, g_ref[...]).astype(BF16)
    projected = (
        jnp.dot(ya_ref[...], wa_ref[...], preferred_element_type=F32),
        lax.dot_general(ybt_ref[...], wb_ref[...], (((0,), (0,)), ((), ())), preferred_element_type=F32),
        jnp.dot(yc_ref[...], wc_ref[...], preferred_element_type=F32),
    )
    merged = None
    for j, proj in enumerate(projected):
        gate = jnp.dot(h, wg_ref[:, j * D_MODEL:(j + 1) * D_MODEL], preferred_element_type=F32)
        term = jax.nn.sigmoid(gate) * proj
        merged = term if merged is None else merged + term
    o_ref[...] = x + jnp.dot(merged.astype(BF16), wo_ref[...], preferred_element_type=F32)


def _merge(x2, g, wg, ya, ybt, yc, wa, wb, wc, wo):
    t = x2.shape[0]
    tm = min(TM_MERGE, t)
    tok = lambda n: pl.BlockSpec((tm, n), lambda i: (i, 0))
    wproj = _full((A_WIDTH, D_MODEL))
    return pl.pallas_call(
        _merge_kernel,
        grid=(t // tm,),
        in_specs=[tok(D_MODEL), _full((1, D_MODEL)), _full((D_MODEL, N_BRANCH * D_MODEL)),
                  tok(A_WIDTH), pl.BlockSpec((B_WIDTH, tm), lambda i: (0, i)), tok(C_WIDTH),
                  wproj, wproj, wproj, _full((D_MODEL, D_MODEL))],
        out_specs=tok(D_MODEL),
        out_shape=jax.ShapeDtypeStruct((t, D_MODEL), F32),
        compiler_params=_params(),
        name="merge",
    )(x2, g, wg, ya, ybt, yc, wa, wb, wc, wo)


def _router_kernel(x_ref, g_ref, w_ref, b_ref, h_ref, eid_ref, wt_ref):
    h = _rms(x_ref[...], g_ref[...])
    h_ref[...] = h
    logits = jnp.dot(h, w_ref[...], preferred_element_type=F32, precision=lax.Precision.HIGHEST) + b_ref[...]
    lane = lax.broadcasted_iota(I32, logits.shape, 1)
    lane_f = lane.astype(F32)
    ninf = -jnp.inf

    def first_max(vals):
        vmax = jnp.max(vals, axis=1, keepdims=True)
        idx = jnp.min(jnp.where(vals == vmax, lane_f, float(LANES)), axis=1, keepdims=True)
        return vmax, idx

    gl = jnp.where(lane < N_GROUPS, logits, ninf)
    gmax, grp = first_max(gl)
    gsum = jnp.sum(jnp.where(lane < N_GROUPS, jnp.exp(logits - gmax), 0.0), axis=1, keepdims=True)
    gw = 1.0 / gsum
    e_lane = lane - N_GROUPS
    in_grp = (e_lane >= 0) & (e_lane < N_EXPERTS) & ((e_lane >> 3).astype(F32) == grp)
    el = jnp.where(in_grp, logits, ninf)
    v1, i1 = first_max(el)
    v2, i2 = first_max(jnp.where(lane_f == i1, ninf, el))
    e2 = jnp.exp(v2 - v1)
    den = 1.0 + e2
    w1 = (1.0 / den) * gw
    w2 = (e2 / den) * gw
    eid1 = (i1 - N_GROUPS).astype(I32)
    eid2 = (i2 - N_GROUPS).astype(I32)
    eid_ref[...] = jnp.where(lane == 0, eid1, jnp.where(lane == 1, eid2, 0))
    wt_ref[...] = jnp.where(lane == 0, w1, jnp.where(lane == 1, w2, 0.0))


def _router(x2, g, w, b):
    t = x2.shape[0]
    tm = min(TM_ROUTE, t)
    tok = lambda n: pl.BlockSpec((tm, n), lambda i: (i, 0))
    return pl.pallas_call(
        _router_kernel,
        grid=(t // tm,),
        in_specs=[tok(D_MODEL), _full((1, D_MODEL)), _full((D_MODEL, LANES)), _full((1, LANES))],
        out_specs=[tok(D_MODEL), tok(LANES), tok(LANES)],
        out_shape=[jax.ShapeDtypeStruct((t, D_MODEL), F32), jax.ShapeDtypeStruct((t, LANES), I32),
                   jax.ShapeDtypeStruct((t, LANES), F32)],
        compiler_params=_params(),
        name="router",
    )(x2, g, w, b)


def _rank_kernel(eid_ref, rank_ref, cnt_ref, *, tm):
    @pl.when(pl.program_id(0) == 0)
    def _():
        cnt_ref[...] = jnp.zeros_like(cnt_ref)

    eid = eid_ref[...]
    lane = lax.broadcasted_iota(I32, (tm, LANES), 1)
    o1 = lane == eid[:, 0:1]
    o2 = lane == eid[:, 1:2]
    both = jnp.where(o1 | o2, 1.0, 0.0)
    lower = lax.broadcasted_iota(I32, (tm, tm), 1) < lax.broadcasted_iota(I32, (tm, tm), 0)
    before = jnp.dot(jnp.where(lower, 1.0, 0.0).astype(BF16), both.astype(BF16),
                     preferred_element_type=F32) + cnt_ref[...]
    r1 = jnp.sum(jnp.where(o1, before, 0.0), axis=1, keepdims=True).astype(I32)
    r2 = jnp.sum(jnp.where(o2, before, 0.0), axis=1, keepdims=True).astype(I32)
    rank_ref[...] = jnp.where(lane == 0, r1, jnp.where(lane == 1, r2, 0))
    cnt_ref[...] = cnt_ref[...] + jnp.sum(both, axis=0, keepdims=True)


def _rank(eid):
    t = eid.shape[0]
    tm = min(TM_RANK, t)
    return pl.pallas_call(
        functools.partial(_rank_kernel, tm=tm),
        grid=(t // tm,),
        in_specs=[pl.BlockSpec((tm, LANES), lambda i: (i, 0))],
        out_specs=[pl.BlockSpec((tm, LANES), lambda i: (i, 0)), _full((1, LANES))],
        out_shape=[jax.ShapeDtypeStruct((t, LANES), I32), jax.ShapeDtypeStruct((1, LANES), F32)],
        compiler_params=_params(),
        name="rank",
    )(eid)


def _dest_kernel(eid_ref, rank_ref, start_ref, d1_ref, d2_ref):
    eid = eid_ref[...]
    lane = lax.broadcasted_iota(I32, eid.shape, 1)
    start = start_ref[...]
    s1 = jnp.sum(jnp.where(lane == eid[:, 0:1], start, 0.0), axis=1, keepdims=True).astype(I32)
    s2 = jnp.sum(jnp.where(lane == eid[:, 1:2], start, 0.0), axis=1, keepdims=True).astype(I32)
    dest = jnp.where(lane == 0, s1, jnp.where(lane == 1, s2, 0)) + rank_ref[...]
    dest_t = jnp.transpose(dest.astype(F32))[:SUBLANES, :].astype(I32)
    d1_ref[...] = dest_t[0:1, :]
    d2_ref[...] = dest_t[1:2, :]


def _dest(eid, rank, start_row):
    t = eid.shape[0]
    tm = min(TM_ROUTE, t)
    tok = pl.BlockSpec((tm, LANES), lambda i: (i, 0))
    return pl.pallas_call(
        _dest_kernel,
        grid=(t // tm,),
        in_specs=[tok, tok, _full((1, LANES))],
        out_specs=[pl.BlockSpec((1, tm), lambda i: (0, i))] * 2,
        out_shape=[jax.ShapeDtypeStruct((1, t), I32)] * 2,
        compiler_params=_params(),
        name="dest",
    )(eid, rank, start_row)


def _row_copy(src_ref, src_row, dst_ref, dst_row, sem):
    return pltpu.make_async_copy(src_ref.at[pl.ds(src_row, 1)], dst_ref.at[pl.ds(dst_row, 1)], sem)


def _rows_copy(src_ref, dst_ref, n, sem):
    return pltpu.make_async_copy(src_ref.at[pl.ds(0, n)], dst_ref.at[pl.ds(0, n)], sem)


def _scatter_kernel(d1_ref, d2_ref, h_ref, xs_in_ref, xs_ref, sem, *, tm):
    del xs_in_ref

    def issue(t, _):
        for d_ref in (d1_ref, d2_ref):
            _row_copy(h_ref, t, xs_ref, d_ref[0, t], sem).start()
        return 0

    lax.fori_loop(0, tm, issue, 0)
    for _ in range(2):
        _rows_copy(h_ref, xs_ref, tm, sem).wait()


def _scatter_rows(dests, h, n_rows):
    t = h.shape[0]
    tm = min(TM_MOVE, t)
    xs0 = jnp.zeros((n_rows, D_MODEL), F32)
    return pl.pallas_call(
        functools.partial(_scatter_kernel, tm=tm),
        grid=(t // tm,),
        in_specs=[pl.BlockSpec((1, tm), lambda i: (0, i), memory_space=pltpu.SMEM)] * 2
                 + [pl.BlockSpec((tm, D_MODEL), lambda i: (i, 0)), pl.BlockSpec(memory_space=pl.ANY)],
        out_specs=pl.BlockSpec(memory_space=pl.ANY),
        out_shape=jax.ShapeDtypeStruct((n_rows, D_MODEL), F32),
        scratch_shapes=[pltpu.SemaphoreType.DMA(())],
        input_output_aliases={3: 0},
        compiler_params=_params(),
        name="scatter_rows",
    )(*dests, h, xs0)


def _expert_kernel(be_ref, nused_ref, xs_ref, wg_ref, wu_ref, wd_ref, y_ref, wg_bf, wu_bf, wd_bf):
    b = pl.program_id(0)

    @pl.when((b == 0) | (be_ref[b] != be_ref[jnp.maximum(b - 1, 0)]))
    def _():
        wg_bf[...] = wg_ref[...].astype(BF16)
        wu_bf[...] = wu_ref[...].astype(BF16)
        wd_bf[...] = wd_ref[...].astype(BF16)

    @pl.when(b < nused_ref[0])
    def _():
        xb = xs_ref[...].astype(BF16)
        gate = jnp.dot(xb, wg_bf[...], preferred_element_type=F32)
        up = jnp.dot(xb, wu_bf[...], preferred_element_type=F32)
        hid = (jax.nn.silu(gate) * up).astype(BF16)
        y_ref[...] = jnp.dot(hid, wd_bf[...], preferred_element_type=F32)

    @pl.when(b >= nused_ref[0])
    def _():
        y_ref[...] = jnp.zeros_like(y_ref)


def _experts(block_exp, n_used, xs, wg, wu, wd, layer):
    n_rows = xs.shape[0]
    w_in_spec = pl.BlockSpec((None, None, D_MODEL, D_FF_EXPERT), lambda b, be, nu: (layer, be[b], 0, 0))
    grid_spec = pltpu.PrefetchScalarGridSpec(
        num_scalar_prefetch=2,
        grid=(n_rows // EXP_BLOCK,),
        in_specs=[pl.BlockSpec((EXP_BLOCK, D_MODEL), lambda b, be, nu: (b, 0)), w_in_spec, w_in_spec,
                  pl.BlockSpec((None, None, D_FF_EXPERT, D_MODEL), lambda b, be, nu: (layer, be[b], 0, 0))],
        out_specs=pl.BlockSpec((EXP_BLOCK, D_MODEL), lambda b, be, nu: (b, 0)),
        scratch_shapes=[pltpu.VMEM((D_MODEL, D_FF_EXPERT), BF16), pltpu.VMEM((D_MODEL, D_FF_EXPERT), BF16),
                        pltpu.VMEM((D_FF_EXPERT, D_MODEL), BF16)],
    )
    return pl.pallas_call(
        _expert_kernel,
        grid_spec=grid_spec,
        out_shape=jax.ShapeDtypeStruct((n_rows, D_MODEL), F32),
        compiler_params=_params(),
        name="experts",
    )(block_exp, n_used, xs, wg, wu, wd)


def _combine_kernel(d1_ref, d2_ref, d1_next_ref, d2_next_ref, x_ref, wt_ref, g_ref, y_ref, o_ref, buf_ref, sems,
                    *, tm, steps, final_norm):
    i = pl.program_id(0)

    def gather(idx_refs, slot):
        def issue(t, _):
            for j, idx_ref in enumerate(idx_refs):
                _row_copy(y_ref, idx_ref[0, t], buf_ref.at[slot], j * tm + t, sems.at[slot]).start()
            return 0

        lax.fori_loop(0, tm, issue, 0)

    @pl.when(i == 0)
    def _():
        gather((d1_ref, d2_ref), 0)

    @pl.when(i + 1 < steps)
    def _():
        gather((d1_next_ref, d2_next_ref), (i + 1) % 2)

    slot = i % 2
    _rows_copy(y_ref, buf_ref.at[slot], 2 * tm, sems.at[slot]).wait()
    wt = wt_ref[...]
    rows = buf_ref[slot]
    out = x_ref[...] + (wt[:, 0:1] * rows[:tm, :] + wt[:, 1:2] * rows[tm:, :])
    if final_norm:
        out = _rms(out, g_ref[...])
    o_ref[...] = out


def _combine(dests, x2, wt, g_final, y, final_norm):
    t = x2.shape[0]
    tm = min(TM_MOVE, t)
    steps = t // tm
    tok = lambda n: pl.BlockSpec((tm, n), lambda i: (i, 0))
    idx = lambda f: pl.BlockSpec((1, tm), f, memory_space=pltpu.SMEM)
    return pl.pallas_call(
        functools.partial(_combine_kernel, tm=tm, steps=steps, final_norm=final_norm),
        grid=(steps,),
        in_specs=[idx(lambda i: (0, i))] * 2 + [idx(lambda i: (0, jnp.minimum(i + 1, steps - 1)))] * 2
                 + [tok(D_MODEL), tok(LANES), _full((1, D_MODEL)), pl.BlockSpec(memory_space=pl.ANY)],
        out_specs=tok(D_MODEL),
        out_shape=jax.ShapeDtypeStruct((t, D_MODEL), F32),
        scratch_shapes=[pltpu.VMEM((2, 2 * tm, D_MODEL), F32), pltpu.SemaphoreType.DMA((2,))],
        compiler_params=_params(),
        name="combine",
    )(*dests, *dests, x2, wt, g_final, y)


def _split_w_in(w_in_l):
    sizes = [A_WIDTH, A_WIDTH, B_WIDTH, B_WIDTH, B_WIDTH, IDX_HEADS * IDX_DIM, IDX_DIM, IDX_HEADS,
             C_WIDTH, C_WIDTH, C_WIDTH, N_BRANCH * D_MODEL]
    cuts = np.cumsum([0] + sizes)
    u, va, q, k, vb, qi, ki, wi, cb, cc, cx, gates = [w_in_l[:, a:b] for a, b in zip(cuts[:-1], cuts[1:])]
    pad = lambda w, n: jnp.pad(w, ((0, 0), (0, n - w.shape[1])))
    w_a = jnp.concatenate([u, va], axis=1).astype(BF16)
    w_b = jnp.concatenate([k, pad(ki, LANES)], axis=1).astype(BF16)
    w_bt = jnp.concatenate([q, vb, qi, pad(wi, WI_ROWS)], axis=1).T.astype(BF16)
    w_c = jnp.concatenate([cb, cc, cx], axis=1).astype(BF16)
    return w_a, w_b, w_bt, w_c, gates.astype(BF16)


def _token_mixers(x2, g, w_in_l, ws, bs, conv_w, wpa, wpb, wpc, w_out, tabs, bsz, seq):
    w_a, w_b, w_bt, w_c, w_g = _split_w_in(w_in_l)
    n_sel = min(TOPK_MAX, seq // 4)
    k, ki, qt, vt_aug, qit, wit = _proj_b(x2, g, w_b, w_bt, *tabs)
    ybt = _dsa(qit, wit, ki, qt, k, vt_aug, bsz, seq, n_sel)
    ya = _proj_a(x2, g, w_a, ws.astype(BF16), bs.T)
    yc = _proj_c(x2, g, w_c, conv_w, seq)
    bf = lambda w: w.astype(BF16)
    return _merge(x2, g, w_g, ya, ybt, yc, bf(wpa), bf(wpb), bf(wpc), bf(w_out))


def _hier_moe(x2, g, rg_w, rg_b, re_w, re_b, w_gate, w_up, w_down, g_final, layer):
    t = x2.shape[0]
    pad_c = LANES - N_GROUPS - N_EXPERTS
    w_r = jnp.pad(jnp.concatenate([rg_w, re_w], axis=1), ((0, 0), (0, pad_c)))
    b_r = jnp.pad(jnp.concatenate([rg_b, re_b]), (0, pad_c))[None, :]
    h, eid, wt = _router(x2, g, w_r, b_r)
    rank, counts = _rank(eid)
    cnt = counts[0, :N_EXPERTS].astype(I32)
    padded = ((cnt + EXP_BLOCK - 1) // EXP_BLOCK) * EXP_BLOCK
    pend = jnp.cumsum(padded)
    pstart = pend - padded
    n_blocks = (2 * t) // EXP_BLOCK + N_EXPERTS
    n_rows = n_blocks * EXP_BLOCK
    block_row = jnp.arange(n_blocks, dtype=I32) * EXP_BLOCK
    block_exp = jnp.minimum(jnp.sum((pend[None, :] <= block_row[:, None]).astype(I32), axis=1),
                            N_EXPERTS - 1)
    n_used = (pend[-1:] // EXP_BLOCK).astype(I32)
    start_row = jnp.pad(pstart.astype(F32), (0, LANES - N_EXPERTS))[None, :]
    dests = _dest(eid, rank, start_row)
    xs = _scatter_rows(dests, h, n_rows)
    y = _experts(block_exp, n_used, xs, w_gate, w_up, w_down, layer)
    return _combine(dests, x2, wt, g_final, y, layer == w_gate.shape[0] - 1)


def kernel(x, positions, norm_mix_g, norm_ffn_g, norm_final_g, w_in, gmlp_ws, gmlp_b, conv_w, w_proj_a,
           w_proj_b, w_proj_c, w_out, router_group_w, router_group_b, router_expert_w, router_expert_b,
           expert_w_gate, expert_w_up, expert_w_down):
    bsz, seq, d = x.shape
    depth = w_in.shape[0]
    x2 = x.reshape(bsz * seq, d)
    pos = positions.astype(F32).reshape(bsz * seq)
    tabs = (_rope_tables(pos[:, None], B_HEAD_DIM, B_ROT), _rope_tables(pos[:, None], IDX_DIM, IDX_ROT),
            _rope_tables_t(pos[None, :], B_ROT), _rope_tables_t(pos[None, :], IDX_ROT))
    g_final = norm_final_g[None, :]
    for l in range(depth):
        x2 = _token_mixers(x2, norm_mix_g[l][None, :], w_in[l], gmlp_ws[l], gmlp_b[l], conv_w[l],
                           w_proj_a[l], w_proj_b[l], w_proj_c[l], w_out[l], tabs, bsz, seq)
        x2 = _hier_moe(x2, norm_ffn_g[l][None, :], router_group_w[l], router_group_b[l],
                       router_expert_w[l], router_expert_b[l], expert_w_gate, expert_w_up,
                       expert_w_down, g_final, l)
    return x2.reshape(bsz, seq, d)
```

```python
import functools

import numpy as np
import jax
import jax.numpy as jnp
from jax import lax
from jax.experimental import pallas as pl
from jax.experimental.pallas import tpu as pltpu

D_MODEL = 1024
CHUNK = 128
A_GROUPS = 4
A_GROUP_DIM = 128
A_WIDTH = A_GROUPS * A_GROUP_DIM
B_HEADS = 4
B_HEAD_DIM = 128
B_WIDTH = B_HEADS * B_HEAD_DIM
B_ROT = B_HEAD_DIM // 4
IDX_HEADS = 4
IDX_DIM = 64
IDX_ROT = IDX_DIM // 4
TOPK_MAX = 256
C_WIDTH = 512
CONV_WIDTH = 3
N_BRANCH = 3
N_GROUPS = 4
EXP_PER_GROUP = 8
N_EXPERTS = N_GROUPS * EXP_PER_GROUP
D_FF_EXPERT = 512
ROPE_THETA = 500000.0
EPS = 1e-6
NEG = -1e30

LANES = 128
SUBLANES = 8
VMEM_LIMIT = 56 * 1024 * 1024

TM_PROJ = 512
TM_MERGE = 256
TM_ROUTE = 512
TM_RANK = 256
TM_MOVE = 256
EXP_BLOCK = 256
DSA_TQ = 128
DSA_KC = 256
DSA_SKC = 512
DSA_SUB = 256
DSA_RB = 512
V_ROWS = B_HEAD_DIM + 16
LOG2E = 1.4426950408889634

F32 = jnp.float32
BF16 = jnp.bfloat16
I32 = jnp.int32
INT_MIN = -2 ** 31


def _monotone_key_of(value):
    bits = int(np.array(value, np.float32).view(np.int32))
    return bits ^ ((bits >> 31) & 0x7FFFFFFF)


NEG_KEY = _monotone_key_of(NEG)


def _params(n_axes=1, semantics=None):
    return pltpu.CompilerParams(
        dimension_semantics=semantics or ("arbitrary",) * n_axes,
        vmem_limit_bytes=VMEM_LIMIT)


def _rms(x, g):
    ms = jnp.mean(x * x, axis=-1, keepdims=True)
    return x * lax.rsqrt(ms + EPS) * g


def _full(shape):
    nd = len(shape)
    return pl.BlockSpec(shape, lambda *_: (0,) * nd)


def _rope_table_kernel(pos_ref, invf_ref, mrot_ref, ma_ref, mb_ref, c_ref, sa_ref, sb_ref):
    ang = pos_ref[...] * invf_ref[...]
    c = jnp.cos(ang)
    s = jnp.sin(ang)
    c_ref[...] = jnp.where(mrot_ref[...] > 0, c, 1.0)
    sa_ref[...] = jnp.where(ma_ref[...] > 0, -s, 0.0)
    sb_ref[...] = jnp.where(mb_ref[...] > 0, s, 0.0)


def _rope_tables(pos_col, head_dim, rot):
    t = pos_col.shape[0]
    half = rot // 2
    inv_freq = jnp.float32(ROPE_THETA) ** (-jnp.arange(half, dtype=F32) * 2.0 / rot)
    lane = np.arange(LANES) % head_dim
    mrot = lane < rot
    ma = lane < half
    mb = (lane >= half) & (lane < rot)
    invf = jnp.where(jnp.asarray(mrot), inv_freq[np.where(mrot, lane % half, 0)], 0.0)[None, :]
    row = lambda m: jnp.asarray(m.astype(np.float32))[None, :]
    tm = min(1024, t)
    spec_row = _full((1, LANES))
    out_spec = pl.BlockSpec((tm, LANES), lambda i: (i, 0))
    return pl.pallas_call(
        _rope_table_kernel,
        grid=(t // tm,),
        in_specs=[pl.BlockSpec((tm, 1), lambda i: (i, 0)), spec_row, spec_row, spec_row, spec_row],
        out_specs=[out_spec] * 3,
        out_shape=[jax.ShapeDtypeStruct((t, LANES), F32)] * 3,
        compiler_params=_params(),
        name="rope_tables",
    )(pos_col, invf, row(mrot), row(ma), row(mb))


def _rope_table_t_kernel(pos_ref, invf_ref, c_ref, s_ref):
    ang = invf_ref[...] * pos_ref[...]
    c_ref[...] = jnp.cos(ang)
    s_ref[...] = jnp.sin(ang)


def _rope_tables_t(pos_row, rot):
    t = pos_row.shape[1]
    half = rot // 2
    inv_freq = jnp.float32(ROPE_THETA) ** (-jnp.arange(half, dtype=F32) * 2.0 / rot)
    tm = min(2048, t)
    out_spec = pl.BlockSpec((half, tm), lambda i: (0, i))
    return pl.pallas_call(
        _rope_table_t_kernel,
        grid=(t // tm,),
        in_specs=[pl.BlockSpec((1, tm), lambda i: (0, i)), _full((half, 1))],
        out_specs=[out_spec] * 2,
        out_shape=[jax.ShapeDtypeStruct((half, t), F32)] * 2,
        compiler_params=_params(),
        name="rope_tables_t",
    )(pos_row, inv_freq[:, None])


def _rope_t(x, c, s):
    half = c.shape[0]
    x1, x2 = x[:half, :], x[half:2 * half, :]
    return jnp.concatenate([x1 * c - x2 * s, x1 * s + x2 * c, x[2 * half:, :]], axis=0)


def _rope(x, c, sa, sb, half):
    parts = []
    for j in range(x.shape[1] // LANES):
        xj = x[:, j * LANES:(j + 1) * LANES]
        parts.append(xj * c + pltpu.roll(xj, LANES - half, 1) * sa + pltpu.roll(xj, half, 1) * sb)
    return parts


WI_ROWS = 16


def _proj_b_kernel(x_ref, g_ref, w_ref, wt_ref, cb_ref, sab_ref, sbb_ref, ci_ref, sai_ref, sbi_ref,
                   cbt_ref, sbt_ref, cit_ref, sit_ref, k_ref, ki_ref, qt_ref, vt_ref, qit_ref, wit_ref):
    h = _rms(x_ref[...], g_ref[...]).astype(BF16)
    tm = h.shape[0]
    p = jnp.dot(h, w_ref[...], preferred_element_type=F32)
    pt = lax.dot_general(wt_ref[...], h, (((1,), (1,)), ((), ())), preferred_element_type=F32)
    tb = (cb_ref[...], sab_ref[...], sbb_ref[...])
    ti = (ci_ref[...], sai_ref[...], sbi_ref[...])
    for j, kj in enumerate(_rope(p[:, :B_WIDTH], *tb, B_ROT // 2)):
        k_ref[:, j * LANES:(j + 1) * LANES] = kj.astype(BF16)
    ki_ref[...] = _rope(p[:, B_WIDTH:B_WIDTH + LANES], *ti, IDX_ROT // 2)[0].astype(BF16)
    cbt, sbt = cbt_ref[...], sbt_ref[...]
    for hh in range(B_HEADS):
        hs = slice(hh * B_HEAD_DIM, (hh + 1) * B_HEAD_DIM)
        qt_ref[hs, :] = (_rope_t(pt[hs, :], cbt, sbt) * (B_HEAD_DIM ** -0.5 * LOG2E)).astype(BF16)
    o = B_WIDTH
    for hh in range(B_HEADS):
        vt_ref[hh * V_ROWS:hh * V_ROWS + B_HEAD_DIM, :] = (
            pt[o + hh * B_HEAD_DIM:o + (hh + 1) * B_HEAD_DIM, :].astype(BF16))
        vt_ref[hh * V_ROWS + B_HEAD_DIM:(hh + 1) * V_ROWS, :] = jnp.ones((V_ROWS - B_HEAD_DIM, tm), BF16)
    o += B_WIDTH
    cit, sit = cit_ref[...], sit_ref[...]
    for hh in range(IDX_HEADS):
        hs = slice(hh * IDX_DIM, (hh + 1) * IDX_DIM)
        qit_ref[hs, :] = _rope_t(pt[o + hh * IDX_DIM:o + (hh + 1) * IDX_DIM, :], cit, sit).astype(BF16)
    o += IDX_HEADS * IDX_DIM
    wit_ref[...] = pt[o:o + SUBLANES, :] * ((IDX_HEADS ** -0.5) * (IDX_DIM ** -0.5))


def _proj_b(x2, g, w, wt, tabs_b, tabs_i, tabs_bt, tabs_it):
    t = x2.shape[0]
    tm = min(TM_PROJ, t)
    tok = lambda n: pl.BlockSpec((tm, n), lambda i: (i, 0))
    feat = lambda n: pl.BlockSpec((n, tm), lambda i: (0, i))
    outs_tok = [(B_WIDTH, BF16), (LANES, BF16)]
    outs_feat = [(B_WIDTH, BF16), (B_HEADS * V_ROWS, BF16), (IDX_HEADS * IDX_DIM, BF16), (SUBLANES, F32)]
    return pl.pallas_call(
        _proj_b_kernel,
        grid=(t // tm,),
        in_specs=[tok(D_MODEL), _full((1, D_MODEL)), _full(w.shape), _full(wt.shape)] + [tok(LANES)] * 6
                 + [feat(B_ROT // 2)] * 2 + [feat(IDX_ROT // 2)] * 2,
        out_specs=[tok(n) for n, _ in outs_tok] + [feat(n) for n, _ in outs_feat],
        out_shape=[jax.ShapeDtypeStruct((t, n), d) for n, d in outs_tok]
                  + [jax.ShapeDtypeStruct((n, t), d) for n, d in outs_feat],
        compiler_params=_params(),
        name="proj_b",
    )(x2, g, w, wt, *tabs_b, *tabs_i, *tabs_bt, *tabs_it)


def _ordered_key(score):
    bits = lax.bitcast_convert_type(score, I32)
    key = bits ^ ((bits >> 31) & 0x7FFFFFFF)
    return jnp.where(key == -1, 0, key)


def _score_of_key(key):
    return lax.bitcast_convert_type(key ^ ((key >> 31) & 0x7FFFFFFF), F32)


def _ce_desc(vals, i, j):
    a, b = vals[i], vals[j]
    vals[i] = jnp.maximum(a, b)
    vals[j] = jnp.minimum(a, b)


def _bitonic_sort(vals, desc=True):
    n = len(vals)
    k = 2
    while k <= n:
        j = k // 2
        while j >= 1:
            for i in range(n):
                l = i ^ j
                if l > i:
                    if ((i & k) == 0) == desc:
                        _ce_desc(vals, i, l)
                    else:
                        _ce_desc(vals, l, i)
            j //= 2
        k *= 2


def _bitonic_merge_desc(vals):
    n = len(vals)
    j = n // 2
    while j >= 1:
        for i in range(n):
            l = i ^ j
            if l > i:
                _ce_desc(vals, i, l)
        j //= 2


def _dsa_kernel(qit_ref, wit_ref, ki_ref, qt_ref, k_ref, vt_ref, o_ref, sc_ref, top_ref, topk_ref, acc_ref,
                bias0_ref, bias1_ref, lg0_ref, lg1_ref,
                *, seq, n_sel, tq, kc, skc, sub, rb):
    i = pl.program_id(1)
    q0 = i * tq
    n_full = q0 // skc
    n_rows = (n_full + 1) * skc
    n_pad = (seq - 1 - q0) - lax.broadcasted_iota(I32, (1, tq), 1)
    n_chunk = (q0 + tq + kc - 1) // kc
    wit = wit_ref[...]

    n_top = skc // SUBLANES
    top_ref[...] = jnp.full((skc, tq), -jnp.inf, F32)

    def score_batch(c, masked):
        halves = []
        for s in range(skc // sub):
            r0 = pl.multiple_of(c * skc + s * sub, sub)
            kic = ki_ref[pl.ds(r0, sub), :IDX_DIM]
            sc = None
            for h in range(IDX_HEADS):
                a = jnp.dot(kic, qit_ref[h * IDX_DIM:(h + 1) * IDX_DIM, :], preferred_element_type=F32)
                term = jnp.maximum(a, 0.0) * wit[h:h + 1, :]
                sc = term if sc is None else sc + term
            if masked:
                qpos = q0 + lax.broadcasted_iota(I32, (sub, tq), 1)
                krow = r0 + lax.broadcasted_iota(I32, (sub, tq), 0)
                sc = jnp.where(krow <= qpos, sc, -jnp.inf)
            sc_ref[pl.ds(r0, sub), :] = sc
            half = [sc[SUBLANES * i:SUBLANES * (i + 1), :] for i in range(sub // SUBLANES)]
            _bitonic_sort(half, desc=(s % 2 == 0))
            halves.append(half)
        vals = halves[0] + halves[1]
        _bitonic_merge_desc(vals)
        return vals

    def keep_top(batches):
        top = [top_ref[SUBLANES * i:SUBLANES * (i + 1), :] for i in range(n_top)]
        for vals in batches:
            top = [jnp.maximum(top[i], vals[n_top - 1 - i]) for i in range(n_top)]
            _bitonic_merge_desc(top)
        for i in range(n_top):
            top_ref[SUBLANES * i:SUBLANES * (i + 1), :] = top[i]

    def score_pair(j, _):
        keep_top([score_batch(2 * j, False), score_batch(2 * j + 1, False)])
        return 0

    lax.fori_loop(0, n_full // 2, score_pair, 0)

    @pl.when(n_full % 2 == 1)
    def _():
        keep_top([score_batch(n_full - 1, False)])

    keep_top([score_batch(n_full, True)])

    def search(count):
        def bit_body(b, carry):
            ans_u, = carry
            cand_u = ans_u | jnp.left_shift(jnp.int32(1), 31 - b)
            ok = count(cand_u ^ INT_MIN, False) >= n_sel
            return (jnp.where(ok, cand_u, ans_u),)

        ans_u, = lax.fori_loop(0, 32, bit_body, (jnp.zeros((1, tq), I32),))
        thr = ans_u ^ INT_MIN
        return thr, (n_sel - count(thr, True)).astype(F32)

    def pads(cand, strict):
        return jnp.where((NEG_KEY > cand) if strict else (NEG_KEY >= cand), n_pad, 0)

    topk_ref[...] = _ordered_key(top_ref[...])

    def count_top(cand, strict):
        t = topk_ref[...]
        hit = (t > cand) if strict else (t >= cand)
        return jnp.sum(jnp.where(hit, 1.0, 0.0), axis=0, keepdims=True).astype(I32) + pads(cand, strict)

    def search_top():
        def digit_body(b, carry):
            ans_u, = carry
            shift = 30 - 2 * b
            for d in (1, 2, 3):
                cand_u = carry[0] | jnp.left_shift(jnp.int32(d), shift)
                ans_u = jnp.where(count_top(cand_u ^ INT_MIN, False) >= n_sel, cand_u, ans_u)
            return (ans_u,)

        ans_u, = lax.fori_loop(0, 16, digit_body, (jnp.zeros((1, tq), I32),))
        thr = ans_u ^ INT_MIN
        return thr, (n_sel - count_top(thr, True)).astype(F32)

    def count_all(cand, strict):
        acc_rows = 4 * SUBLANES
        cand_f = _score_of_key(cand)

        def body(r, acc):
            r0 = pl.multiple_of(r * rb, rb)
            blk = sc_ref[pl.ds(r0, rb), :]
            for j in range(rb // acc_rows):
                part = blk[j * acc_rows:(j + 1) * acc_rows, :]
                hit = (part > cand_f) if strict else (part >= cand_f)
                acc = acc + jnp.where(hit, 1, 0)
            return acc

        acc = lax.fori_loop(0, n_rows // rb, body, jnp.zeros((acc_rows, tq), I32))
        return jnp.sum(acc.astype(F32), axis=0, keepdims=True).astype(I32) + pads(cand, strict)

    thr, need = search_top()
    last = topk_ref[skc - SUBLANES:skc, :]
    overflow = jnp.max(jnp.where(last > thr, 1.0, 0.0))
    thr, need = lax.cond(overflow > 0.0, lambda: search(count_all), lambda: (thr, need))
    thr_f = _score_of_key(thr)

    acc_ref[...] = jnp.zeros_like(acc_ref)
    bias_refs = (bias0_ref, bias1_ref)
    lg_refs = (lg0_ref, lg1_ref)
    hk = kc // 2
    lower = (lax.broadcasted_iota(I32, (hk, hk), 1) < lax.broadcasted_iota(I32, (hk, hk), 0))
    lower = jnp.where(lower, 1.0, 0.0).astype(BF16)

    def produce(c, slot, tie_base):
        r0 = pl.multiple_of(c * kc, kc)
        sc = sc_ref[pl.ds(r0, kc), :]
        tie = sc == thr_f
        tie_f = jnp.where(tie, 1.0, 0.0)
        tie_b = tie_f.astype(BF16)
        b0 = jnp.dot(lower, tie_b[:hk, :], preferred_element_type=F32)
        n0 = b0[hk - 1:hk, :] + tie_f[hk - 1:hk, :]
        b1 = jnp.dot(lower, tie_b[hk:, :], preferred_element_type=F32) + n0
        before = jnp.concatenate([b0, b1], axis=0)
        sel = (sc > thr_f) | (tie & (before < need - tie_base))
        bias_refs[slot][...] = jnp.where(sel, 0.0, NEG)
        for h in range(B_HEADS):
            hs = slice(h * B_HEAD_DIM, (h + 1) * B_HEAD_DIM)
            lg_refs[slot][h] = jnp.dot(k_ref[pl.ds(r0, kc), hs], qt_ref[hs, :], preferred_element_type=F32)
        return tie_base + b1[hk - 1:hk, :] + tie_f[kc - 1:kc, :]

    def consume(c, slot, ms):
        r0 = pl.multiple_of(c * kc, kc)
        bias = bias_refs[slot][...]
        new_ms = []
        for h in range(B_HEADS):
            vs = slice(h * V_ROWS, (h + 1) * V_ROWS)
            lg = lg_refs[slot][h] + bias
            m_new = jnp.maximum(ms[h], jnp.max(lg, axis=0, keepdims=True))
            alpha = jnp.exp2(ms[h] - m_new)
            p = jnp.exp2(lg - m_new).astype(BF16)
            pv = jnp.dot(vt_ref[vs, pl.ds(r0, kc)], p, preferred_element_type=F32)
            acc_ref[vs, :] = acc_ref[vs, :] * alpha + pv
            new_ms.append(m_new)
        return tuple(new_ms)

    def pair_body(j, carry):
        ms, tie_base = carry
        c = 2 * j
        tie_base = produce(c + 1, 1, tie_base)
        ms = consume(c, 0, ms)
        tie_base = produce(c + 2, 0, tie_base)
        ms = consume(c + 1, 1, ms)
        return ms, tie_base

    row = lambda v: jnp.full((1, tq), v, F32)
    ms0 = tuple(row(NEG) for _ in range(B_HEADS))
    n_pairs = (n_chunk - 1) // 2
    ms, tie_base = lax.fori_loop(0, n_pairs, pair_body, (ms0, produce(0, 0, row(0.0))))
    c0 = 2 * n_pairs

    @pl.when(c0 == n_chunk - 1)
    def _():
        consume(c0, 0, ms)

    @pl.when(c0 != n_chunk - 1)
    def _():
        produce(c0 + 1, 1, tie_base)
        consume(c0 + 1, 1, consume(c0, 0, ms))

    for h in range(B_HEADS):
        a = acc_ref[h * V_ROWS:(h + 1) * V_ROWS, :]
        o_ref[h * B_HEAD_DIM:(h + 1) * B_HEAD_DIM, :] = (
            a[:B_HEAD_DIM, :] / a[B_HEAD_DIM:B_HEAD_DIM + 1, :]).astype(o_ref.dtype)


def _dsa(qit, wit, ki, qt, k, vt_aug, bsz, seq, n_sel):
    tq = min(DSA_TQ, seq)
    kc = min(DSA_KC, seq)
    skc = min(DSA_SKC, seq)
    nq = seq // tq
    once = pl.Buffered(1)
    kern = functools.partial(_dsa_kernel, seq=seq, n_sel=n_sel, tq=tq, kc=kc, skc=skc,
                             sub=min(DSA_SUB, skc), rb=min(DSA_RB, skc))
    q_tile = lambda rows: pl.BlockSpec((rows, tq), lambda b, i: (0, b * nq + i))
    return pl.pallas_call(
        kern,
        grid=(bsz, nq),
        in_specs=[
            q_tile(IDX_HEADS * IDX_DIM),
            q_tile(SUBLANES),
            pl.BlockSpec((seq, LANES), lambda b, i: (b, 0), pipeline_mode=once),
            q_tile(B_WIDTH),
            pl.BlockSpec((seq, B_WIDTH), lambda b, i: (b, 0), pipeline_mode=once),
            pl.BlockSpec((B_HEADS * V_ROWS, seq), lambda b, i: (0, b), pipeline_mode=once),
        ],
        out_specs=q_tile(B_WIDTH),
        out_shape=jax.ShapeDtypeStruct((B_WIDTH, bsz * seq), BF16),
        scratch_shapes=[pltpu.VMEM((seq, tq), F32), pltpu.VMEM((skc, tq), F32), pltpu.VMEM((skc, tq), I32),
                        pltpu.VMEM((B_HEADS * V_ROWS, tq), F32), pltpu.VMEM((kc, tq), F32),
                        pltpu.VMEM((kc, tq), F32), pltpu.VMEM((B_HEADS, kc, tq), F32),
                        pltpu.VMEM((B_HEADS, kc, tq), F32)],
        compiler_params=_params(2),
        name="dsa",
    )(qit, wit, ki, qt, k, vt_aug)


def _proj_a_kernel(x_ref, g_ref, w_ref, ws_ref, bs_ref, ya_ref, *, tm):
    h = _rms(x_ref[...], g_ref[...]).astype(BF16)
    p = jnp.dot(h, w_ref[...], preferred_element_type=F32)
    u = jax.nn.gelu(p[:, :A_WIDTH])
    v = jax.nn.gelu(p[:, A_WIDTH:])
    mu = jnp.mean(v, axis=-1, keepdims=True)
    var = jnp.mean(jnp.square(v - mu), axis=-1, keepdims=True)
    v = ((v - mu) * lax.rsqrt(var + EPS)).astype(BF16)
    causal = (lax.broadcasted_iota(I32, (CHUNK, CHUNK), 1) <= lax.broadcasted_iota(I32, (CHUNK, CHUNK), 0))
    bs = bs_ref[...]
    for gidx in range(A_GROUPS):
        wsm = jnp.where(causal, ws_ref[gidx], 0.0).astype(BF16)
        cs = slice(gidx * A_GROUP_DIM, (gidx + 1) * A_GROUP_DIM)
        for n in range(tm // CHUNK):
            rs = slice(n * CHUNK, (n + 1) * CHUNK)
            vm = jnp.dot(wsm, v[rs, cs], preferred_element_type=F32) + bs[:, gidx:gidx + 1]
            ya_ref[rs, cs] = (u[rs, cs] * vm).astype(ya_ref.dtype)


def _proj_a(x2, g, w, ws, bs_t):
    t = x2.shape[0]
    tm = min(TM_PROJ, t)
    return pl.pallas_call(
        functools.partial(_proj_a_kernel, tm=tm),
        grid=(t // tm,),
        in_specs=[pl.BlockSpec((tm, D_MODEL), lambda i: (i, 0)), _full((1, D_MODEL)),
                  _full((D_MODEL, 2 * A_WIDTH)), _full((A_GROUPS, CHUNK, CHUNK)), _full((CHUNK, A_GROUPS))],
        out_specs=pl.BlockSpec((tm, A_WIDTH), lambda i: (i, 0)),
        out_shape=jax.ShapeDtypeStruct((t, A_WIDTH), BF16),
        compiler_params=_params(),
        name="proj_a",
    )(x2, g, w, ws, bs_t)


def _proj_c_kernel(x_ref, xh_ref, g_ref, w_ref, cw_ref, yc_ref, *, tm, tiles_per_seq):
    g = g_ref[...]
    w = w_ref[...]
    p = jnp.dot(_rms(x_ref[...], g).astype(BF16), w, preferred_element_type=F32)
    ph = jnp.dot(_rms(xh_ref[...], g).astype(BF16), w[:, C_WIDTH:], preferred_element_type=F32)
    first = (pl.program_id(0) % tiles_per_seq) == 0
    zh = ph[:, :C_WIDTH] * ph[:, C_WIDTH:] * jnp.where(first, 0.0, 1.0)
    z = p[:, C_WIDTH:2 * C_WIDTH] * p[:, 2 * C_WIDTH:]
    row = lax.broadcasted_iota(I32, (tm, C_WIDTH), 0)
    z1 = jnp.where(row == 0, zh[SUBLANES - 1:SUBLANES, :], pltpu.roll(z, 1, 0))
    z2 = jnp.where(row == 0, zh[SUBLANES - 2:SUBLANES - 1, :],
                   jnp.where(row == 1, zh[SUBLANES - 1:SUBLANES, :], pltpu.roll(z, 2, 0)))
    cw = cw_ref[...]
    y = cw[0:1, :] * z2 + cw[1:2, :] * z1 + cw[2:3, :] * z
    yc_ref[...] = (p[:, :C_WIDTH] * y).astype(yc_ref.dtype)


def _proj_c(x2, g, w, cw, seq):
    t = x2.shape[0]
    tm = min(TM_PROJ, seq)
    per8 = tm // SUBLANES
    return pl.pallas_call(
        functools.partial(_proj_c_kernel, tm=tm, tiles_per_seq=seq // tm),
        grid=(t // tm,),
        in_specs=[pl.BlockSpec((tm, D_MODEL), lambda i: (i, 0)),
                  pl.BlockSpec((SUBLANES, D_MODEL), lambda i: (jnp.maximum(i * per8 - 1, 0), 0)),
                  _full((1, D_MODEL)), _full((D_MODEL, 3 * C_WIDTH)), _full((CONV_WIDTH, C_WIDTH))],
        out_specs=pl.BlockSpec((tm, C_WIDTH), lambda i: (i, 0)),
        out_shape=jax.ShapeDtypeStruct((t, C_WIDTH), BF16),
        compiler_params=_params(),
        name="proj_c",
    )(x2, x2, g, w, cw)


def _merge_kernel(x_ref, g_ref, wg_ref, ya_ref, ybt_ref, yc_ref, wa_ref, wb_ref, wc_ref, wo_ref, o_ref):
    x = x_ref[...]
    h = _rms(x, g_ref[...]).astype(BF16)
    projected = (
        jnp.dot(ya_ref[...], wa_ref[...], preferred_element_type=F32),
        lax.dot_general(ybt_ref[...], wb_ref[...], (((0,), (0,)), ((), ())), preferred_element_type=F32),
        jnp.dot(yc_ref[...], wc_ref[...], preferred_element_type=F32),
    )
    merged = None
    for j, proj in enumerate(projected):
        gate = jnp.dot(h, wg_ref[:, j * D_MODEL:(j + 1) * D_MODEL], preferred_element_type=F32)
        term = jax.nn.sigmoid(gate) * proj
        merged = term if merged is None else merged + term
    o_ref[...] = x + jnp.dot(merged.astype(BF16), wo_ref[...], preferred_element_type=F32)


def _merge(x2, g, wg, ya, ybt, yc, wa, wb, wc, wo):
    t = x2.shape[0]
    tm = min(TM_MERGE, t)
    tok = lambda n: pl.BlockSpec((tm, n), lambda i: (i, 0))
    wproj = _full((A_WIDTH, D_MODEL))
    return pl.pallas_call(
        _merge_kernel,
        grid=(t // tm,),
        in_specs=[tok(D_MODEL), _full((1, D_MODEL)), _full((D_MODEL, N_BRANCH * D_MODEL)),
                  tok(A_WIDTH), pl.BlockSpec((B_WIDTH, tm), lambda i: (0, i)), tok(C_WIDTH),
                  wproj, wproj, wproj, _full((D_MODEL, D_MODEL))],
        out_specs=tok(D_MODEL),
        out_shape=jax.ShapeDtypeStruct((t, D_MODEL), F32),
        compiler_params=_params(),
        name="merge",
    )(x2, g, wg, ya, ybt, yc, wa, wb, wc, wo)


def _router_kernel(x_ref, g_ref, w_ref, b_ref, h_ref, eid_ref, wt_ref):
    h = _rms(x_ref[...], g_ref[...])
    h_ref[...] = h
    logits = jnp.dot(h, w_ref[...], preferred_element_type=F32, precision=lax.Precision.HIGHEST) + b_ref[...]
    lane = lax.broadcasted_iota(I32, logits.shape, 1)
    lane_f = lane.astype(F32)
    ninf = -jnp.inf

    def first_max(vals):
        vmax = jnp.max(vals, axis=1, keepdims=True)
        idx = jnp.min(jnp.where(vals == vmax, lane_f, float(LANES)), axis=1, keepdims=True)
        return vmax, idx

    gl = jnp.where(lane < N_GROUPS, logits, ninf)
    gmax, grp = first_max(gl)
    gsum = jnp.sum(jnp.where(lane < N_GROUPS, jnp.exp(logits - gmax), 0.0), axis=1, keepdims=True)
    gw = 1.0 / gsum
    e_lane = lane - N_GROUPS
    in_grp = (e_lane >= 0) & (e_lane < N_EXPERTS) & ((e_lane >> 3).astype(F32) == grp)
    el = jnp.where(in_grp, logits, ninf)
    v1, i1 = first_max(el)
    v2, i2 = first_max(jnp.where(lane_f == i1, ninf, el))
    e2 = jnp.exp(v2 - v1)
    den = 1.0 + e2
    w1 = (1.0 / den) * gw
    w2 = (e2 / den) * gw
    eid1 = (i1 - N_GROUPS).astype(I32)
    eid2 = (i2 - N_GROUPS).astype(I32)
    eid_ref[...] = jnp.where(lane == 0, eid1, jnp.where(lane == 1, eid2, 0))
    wt_ref[...] = jnp.where(lane == 0, w1, jnp.where(lane == 1, w2, 0.0))


def _router(x2, g, w, b):
    t = x2.shape[0]
    tm = min(TM_ROUTE, t)
    tok = lambda n: pl.BlockSpec((tm, n), lambda i: (i, 0))
    return pl.pallas_call(
        _router_kernel,
        grid=(t // tm,),
        in_specs=[tok(D_MODEL), _full((1, D_MODEL)), _full((D_MODEL, LANES)), _full((1, LANES))],
        out_specs=[tok(D_MODEL), tok(LANES), tok(LANES)],
        out_shape=[jax.ShapeDtypeStruct((t, D_MODEL), F32), jax.ShapeDtypeStruct((t, LANES), I32),
                   jax.ShapeDtypeStruct((t, LANES), F32)],
        compiler_params=_params(),
        name="router",
    )(x2, g, w, b)


def _rank_kernel(eid_ref, rank_ref, cnt_ref, *, tm):
    @pl.when(pl.program_id(0) == 0)
    def _():
        cnt_ref[...] = jnp.zeros_like(cnt_ref)

    eid = eid_ref[...]
    lane = lax.broadcasted_iota(I32, (tm, LANES), 1)
    o1 = lane == eid[:, 0:1]
    o2 = lane == eid[:, 1:2]
    both = jnp.where(o1 | o2, 1.0, 0.0)
    lower = lax.broadcasted_iota(I32, (tm, tm), 1) < lax.broadcasted_iota(I32, (tm, tm), 0)
    before = jnp.dot(jnp.where(lower, 1.0, 0.0).astype(BF16), both.astype(BF16),
                     preferred_element_type=F32) + cnt_ref[...]
    r1 = jnp.sum(jnp.where(o1, before, 0.0), axis=1, keepdims=True).astype(I32)
    r2 = jnp.sum(jnp.where(o2, before, 0.0), axis=1, keepdims=True).astype(I32)
    rank_ref[...] = jnp.where(lane == 0, r1, jnp.where(lane == 1, r2, 0))
    cnt_ref[...] = cnt_ref[...] + jnp.sum(both, axis=0, keepdims=True)


def _rank(eid):
    t = eid.shape[0]
    tm = min(TM_RANK, t)
    return pl.pallas_call(
        functools.partial(_rank_kernel, tm=tm),
        grid=(t // tm,),
        in_specs=[pl.BlockSpec((tm, LANES), lambda i: (i, 0))],
        out_specs=[pl.BlockSpec((tm, LANES), lambda i: (i, 0)), _full((1, LANES))],
        out_shape=[jax.ShapeDtypeStruct((t, LANES), I32), jax.ShapeDtypeStruct((1, LANES), F32)],
        compiler_params=_params(),
        name="rank",
    )(eid)


def _dest_kernel(eid_ref, rank_ref, start_ref, d1_ref, d2_ref):
    eid = eid_ref[...]
    lane = lax.broadcasted_iota(I32, eid.shape, 1)
    start = start_ref[...]
    s1 = jnp.sum(jnp.where(lane == eid[:, 0:1], start, 0.0), axis=1, keepdims=True).astype(I32)
    s2 = jnp.sum(jnp.where(lane == eid[:, 1:2], start, 0.0), axis=1, keepdims=True).astype(I32)
    dest = jnp.where(lane == 0, s1, jnp.where(lane == 1, s2, 0)) + rank_ref[...]
    dest_t = jnp.transpose(dest.astype(F32))[:SUBLANES, :].astype(I32)
    d1_ref[...] = dest_t[0:1, :]
    d2_ref[...] = dest_t[1:2, :]


def _dest(eid, rank, start_row):
    t = eid.shape[0]
    tm = min(TM_ROUTE, t)
    tok = pl.BlockSpec((tm, LANES), lambda i: (i, 0))
    return pl.pallas_call(
        _dest_kernel,
        grid=(t // tm,),
        in_specs=[tok, tok, _full((1, LANES))],
        out_specs=[pl.BlockSpec((1, tm), lambda i: (0, i))] * 2,
        out_shape=[jax.ShapeDtypeStruct((1, t), I32)] * 2,
        compiler_params=_params(),
        name="dest",
    )(eid, rank, start_row)


def _row_copy(src_ref, src_row, dst_ref, dst_row, sem):
    return pltpu.make_async_copy(src_ref.at[pl.ds(src_row, 1)], dst_ref.at[pl.ds(dst_row, 1)], sem)


def _rows_copy(src_ref, dst_ref, n, sem):
    return pltpu.make_async_copy(src_ref.at[pl.ds(0, n)], dst_ref.at[pl.ds(0, n)], sem)


def _scatter_kernel(d1_ref, d2_ref, h_ref, xs_in_ref, xs_ref, sem, *, tm):
    del xs_in_ref

    def issue(t, _):
        for d_ref in (d1_ref, d2_ref):
            _row_copy(h_ref, t, xs_ref, d_ref[0, t], sem).start()
        return 0

    lax.fori_loop(0, tm, issue, 0)
    for _ in range(2):
        _rows_copy(h_ref, xs_ref, tm, sem).wait()


def _scatter_rows(dests, h, n_rows):
    t = h.shape[0]
    tm = min(TM_MOVE, t)
    xs0 = jnp.zeros((n_rows, D_MODEL), F32)
    return pl.pallas_call(
        functools.partial(_scatter_kernel, tm=tm),
        grid=(t // tm,),
        in_specs=[pl.BlockSpec((1, tm), lambda i: (0, i), memory_space=pltpu.SMEM)] * 2
                 + [pl.BlockSpec((tm, D_MODEL), lambda i: (i, 0)), pl.BlockSpec(memory_space=pl.ANY)],
        out_specs=pl.BlockSpec(memory_space=pl.ANY),
        out_shape=jax.ShapeDtypeStruct((n_rows, D_MODEL), F32),
        scratch_shapes=[pltpu.SemaphoreType.DMA(())],
        input_output_aliases={3: 0},
        compiler_params=_params(),
        name="scatter_rows",
    )(*dests, h, xs0)


def _expert_kernel(be_ref, nused_ref, xs_ref, wg_ref, wu_ref, wd_ref, y_ref, wg_bf, wu_bf, wd_bf):
    b = pl.program_id(0)

    @pl.when((b == 0) | (be_ref[b] != be_ref[jnp.maximum(b - 1, 0)]))
    def _():
        wg_bf[...] = wg_ref[...].astype(BF16)
        wu_bf[...] = wu_ref[...].astype(BF16)
        wd_bf[...] = wd_ref[...].astype(BF16)

    @pl.when(b < nused_ref[0])
    def _():
        xb = xs_ref[...].astype(BF16)
        gate = jnp.dot(xb, wg_bf[...], preferred_element_type=F32)
        up = jnp.dot(xb, wu_bf[...], preferred_element_type=F32)
        hid = (jax.nn.silu(gate) * up).astype(BF16)
        y_ref[...] = jnp.dot(hid, wd_bf[...], preferred_element_type=F32)

    @pl.when(b >= nused_ref[0])
    def _():
        y_ref[...] = jnp.zeros_like(y_ref)


def _experts(block_exp, n_used, xs, wg, wu, wd, layer):
    n_rows = xs.shape[0]
    w_in_spec = pl.BlockSpec((None, None, D_MODEL, D_FF_EXPERT), lambda b, be, nu: (layer, be[b], 0, 0))
    grid_spec = pltpu.PrefetchScalarGridSpec(
        num_scalar_prefetch=2,
        grid=(n_rows // EXP_BLOCK,),
        in_specs=[pl.BlockSpec((EXP_BLOCK, D_MODEL), lambda b, be, nu: (b, 0)), w_in_spec, w_in_spec,
                  pl.BlockSpec((None, None, D_FF_EXPERT, D_MODEL), lambda b, be, nu: (layer, be[b], 0, 0))],
        out_specs=pl.BlockSpec((EXP_BLOCK, D_MODEL), lambda b, be, nu: (b, 0)),
        scratch_shapes=[pltpu.VMEM((D_MODEL, D_FF_EXPERT), BF16), pltpu.VMEM((D_MODEL, D_FF_EXPERT), BF16),
                        pltpu.VMEM((D_FF_EXPERT, D_MODEL), BF16)],
    )
    return pl.pallas_call(
        _expert_kernel,
        grid_spec=grid_spec,
        out_shape=jax.ShapeDtypeStruct((n_rows, D_MODEL), F32),
        compiler_params=_params(),
        name="experts",
    )(block_exp, n_used, xs, wg, wu, wd)


def _combine_kernel(d1_ref, d2_ref, d1_next_ref, d2_next_ref, x_ref, wt_ref, g_ref, y_ref, o_ref, buf_ref, sems,
                    *, tm, steps, final_norm):
    i = pl.program_id(0)

    def gather(idx_refs, slot):
        def issue(t, _):
            for j, idx_ref in enumerate(idx_refs):
                _row_copy(y_ref, idx_ref[0, t], buf_ref.at[slot], j * tm + t, sems.at[slot]).start()
            return 0

        lax.fori_loop(0, tm, issue, 0)

    @pl.when(i == 0)
    def _():
        gather((d1_ref, d2_ref), 0)

    @pl.when(i + 1 < steps)
    def _():
        gather((d1_next_ref, d2_next_ref), (i + 1) % 2)

    slot = i % 2
    _rows_copy(y_ref, buf_ref.at[slot], 2 * tm, sems.at[slot]).wait()
    wt = wt_ref[...]
    rows = buf_ref[slot]
    out = x_ref[...] + (wt[:, 0:1] * rows[:tm, :] + wt[:, 1:2] * rows[tm:, :])
    if final_norm:
        out = _rms(out, g_ref[...])
    o_ref[...] = out


def _combine(dests, x2, wt, g_final, y, final_norm):
    t = x2.shape[0]
    tm = min(TM_MOVE, t)
    steps = t // tm
    tok = lambda n: pl.BlockSpec((tm, n), lambda i: (i, 0))
    idx = lambda f: pl.BlockSpec((1, tm), f, memory_space=pltpu.SMEM)
    return pl.pallas_call(
        functools.partial(_combine_kernel, tm=tm, steps=steps, final_norm=final_norm),
        grid=(steps,),
        in_specs=[idx(lambda i: (0, i))] * 2 + [idx(lambda i: (0, jnp.minimum(i + 1, steps - 1)))] * 2
                 + [tok(D_MODEL), tok(LANES), _full((1, D_MODEL)), pl.BlockSpec(memory_space=pl.ANY)],
        out_specs=tok(D_MODEL),
        out_shape=jax.ShapeDtypeStruct((t, D_MODEL), F32),
        scratch_shapes=[pltpu.VMEM((2, 2 * tm, D_MODEL), F32), pltpu.SemaphoreType.DMA((2,))],
        compiler_params=_params(),
        name="combine",
    )(*dests, *dests, x2, wt, g_final, y)


W_IN_SIZES = (A_WIDTH, A_WIDTH, B_WIDTH, B_WIDTH, B_WIDTH, IDX_HEADS * IDX_DIM, IDX_DIM, IDX_HEADS,
              C_WIDTH, C_WIDTH, C_WIDTH, N_BRANCH * D_MODEL)
W_IN_CUTS = tuple(int(c) for c in np.cumsum((0,) + W_IN_SIZES))
W_BT_ROWS = 2 * B_WIDTH + IDX_HEADS * IDX_DIM + WI_ROWS


def _repack_kernel(w_ref, wa_ref, wb_ref, wbt_ref, wc_ref, wg_ref):
    x = w_ref[...]
    u, va, q, k, vb, qi, ki, wi, cb, cc, cx, gates = [
        x[:, a:b] for a, b in zip(W_IN_CUTS[:-1], W_IN_CUTS[1:])]
    for j, part in enumerate((u, va)):
        wa_ref[:, j * A_WIDTH:(j + 1) * A_WIDTH] = part.astype(BF16)
    wb_ref[:, :B_WIDTH] = k.astype(BF16)
    ki_pad = jnp.concatenate([ki, jnp.zeros((ki.shape[0], LANES - IDX_DIM), F32)], axis=1)
    wb_ref[:, B_WIDTH:] = ki_pad.astype(BF16)
    row = 0
    for part in (q, vb, qi):
        for j in range(part.shape[1] // LANES):
            wbt_ref[row:row + LANES, :] = jnp.transpose(part[:, j * LANES:(j + 1) * LANES]).astype(BF16)
            row += LANES
    lane = lax.broadcasted_iota(I32, (x.shape[0], LANES), 1)
    wi_wide = jnp.where(lane < IDX_HEADS, x[:, W_IN_CUTS[7]:W_IN_CUTS[7] + LANES], 0.0)
    wbt_ref[row:row + WI_ROWS, :] = jnp.transpose(wi_wide)[:WI_ROWS, :].astype(BF16)
    for j, part in enumerate((cb, cc, cx)):
        wc_ref[:, j * C_WIDTH:(j + 1) * C_WIDTH] = part.astype(BF16)
    wg_ref[...] = gates.astype(BF16)


def _repack_w_in(w_in):
    depth, d, d_in = w_in.shape
    rows = lambda n: pl.BlockSpec((None, LANES, n), lambda l, i: (l, i, 0))
    outs = [2 * A_WIDTH, B_WIDTH + LANES, None, 3 * C_WIDTH, N_BRANCH * D_MODEL]
    shapes = [jax.ShapeDtypeStruct((depth, W_BT_ROWS, d) if n is None else (depth, d, n), BF16) for n in outs]
    specs = [pl.BlockSpec((None, W_BT_ROWS, LANES), lambda l, i: (l, 0, i)) if n is None else rows(n)
             for n in outs]
    return pl.pallas_call(
        _repack_kernel,
        grid=(depth, d // LANES),
        in_specs=[rows(d_in)],
        out_specs=specs,
        out_shape=shapes,
        compiler_params=_params(2),
        name="repack_w_in",
    )(w_in)


def _token_mixers(x2, g, w_parts, ws, bs, conv_w, wpa, wpb, wpc, w_out, tabs, bsz, seq):
    w_a, w_b, w_bt, w_c, w_g = w_parts
    n_sel = min(TOPK_MAX, seq // 4)
    k, ki, qt, vt_aug, qit, wit = _proj_b(x2, g, w_b, w_bt, *tabs)
    ybt = _dsa(qit, wit, ki, qt, k, vt_aug, bsz, seq, n_sel)
    ya = _proj_a(x2, g, w_a, ws.astype(BF16), bs.T)
    yc = _proj_c(x2, g, w_c, conv_w, seq)
    bf = lambda w: w.astype(BF16)
    return _merge(x2, g, w_g, ya, ybt, yc, bf(wpa), bf(wpb), bf(wpc), bf(w_out))


def _hier_moe(x2, g, rg_w, rg_b, re_w, re_b, w_gate, w_up, w_down, g_final, layer):
    t = x2.shape[0]
    pad_c = LANES - N_GROUPS - N_EXPERTS
    w_r = jnp.pad(jnp.concatenate([rg_w, re_w], axis=1), ((0, 0), (0, pad_c)))
    b_r = jnp.pad(jnp.concatenate([rg_b, re_b]), (0, pad_c))[None, :]
    h, eid, wt = _router(x2, g, w_r, b_r)
    rank, counts = _rank(eid)
    cnt = counts[0, :N_EXPERTS].astype(I32)
    padded = ((cnt + EXP_BLOCK - 1) // EXP_BLOCK) * EXP_BLOCK
    pend = jnp.cumsum(padded)
    pstart = pend - padded
    n_blocks = (2 * t) // EXP_BLOCK + N_EXPERTS
    n_rows = n_blocks * EXP_BLOCK
    block_row = jnp.arange(n_blocks, dtype=I32) * EXP_BLOCK
    block_exp = jnp.minimum(jnp.sum((pend[None, :] <= block_row[:, None]).astype(I32), axis=1),
                            N_EXPERTS - 1)
    n_used = (pend[-1:] // EXP_BLOCK).astype(I32)
    start_row = jnp.pad(pstart.astype(F32), (0, LANES - N_EXPERTS))[None, :]
    dests = _dest(eid, rank, start_row)
    xs = _scatter_rows(dests, h, n_rows)
    y = _experts(block_exp, n_used, xs, w_gate, w_up, w_down, layer)
    return _combine(dests, x2, wt, g_final, y, layer == w_gate.shape[0] - 1)


def kernel(x, positions, norm_mix_g, norm_ffn_g, norm_final_g, w_in, gmlp_ws, gmlp_b, conv_w, w_proj_a,
           w_proj_b, w_proj_c, w_out, router_group_w, router_group_b, router_expert_w, router_expert_b,
           expert_w_gate, expert_w_up, expert_w_down):
    bsz, seq, d = x.shape
    depth = w_in.shape[0]
    x2 = x.reshape(bsz * seq, d)
    pos = positions.astype(F32).reshape(bsz * seq)
    tabs = (_rope_tables(pos[:, None], B_HEAD_DIM, B_ROT), _rope_tables(pos[:, None], IDX_DIM, IDX_ROT),
            _rope_tables_t(pos[None, :], B_ROT), _rope_tables_t(pos[None, :], IDX_ROT))
    g_final = norm_final_g[None, :]
    w_parts = _repack_w_in(w_in)
    for l in range(depth):
        x2 = _token_mixers(x2, norm_mix_g[l][None, :], [w[l] for w in w_parts], gmlp_ws[l], gmlp_b[l], conv_w[l],
                           w_proj_a[l], w_proj_b[l], w_proj_c[l], w_out[l], tabs, bsz, seq)
        x2 = _hier_moe(x2, norm_ffn_g[l][None, :], router_group_w[l], router_group_b[l],
                       router_expert_w[l], router_expert_b[l], expert_w_gate, expert_w_up,
                       expert_w_down, g_final, l)
    return x2.reshape(bsz, seq, d)
```

```python
import functools

import numpy as np
import jax
import jax.numpy as jnp
from jax import lax
from jax.experimental import pallas as pl
from jax.experimental.pallas import tpu as pltpu

D_MODEL = 1024
CHUNK = 128
A_GROUPS = 4
A_GROUP_DIM = 128
A_WIDTH = A_GROUPS * A_GROUP_DIM
B_HEADS = 4
B_HEAD_DIM = 128
B_WIDTH = B_HEADS * B_HEAD_DIM
B_ROT = B_HEAD_DIM // 4
IDX_HEADS = 4
IDX_DIM = 64
IDX_ROT = IDX_DIM // 4
TOPK_MAX = 256
C_WIDTH = 512
CONV_WIDTH = 3
N_BRANCH = 3
N_GROUPS = 4
EXP_PER_GROUP = 8
N_EXPERTS = N_GROUPS * EXP_PER_GROUP
D_FF_EXPERT = 512
ROPE_THETA = 500000.0
EPS = 1e-6
NEG = -1e30

LANES = 128
SUBLANES = 8
VMEM_LIMIT = 56 * 1024 * 1024

TM_PROJ = 512
TM_MERGE = 256
TM_ROUTE = 512
TM_RANK = 256
TM_MOVE = 256
EXP_BLOCK = 512
DSA_TQ = 128
DSA_KC = 256
DSA_SKC = 512
DSA_SUB = 256
DSA_RB = 512
V_ROWS = B_HEAD_DIM + 16
LOG2E = 1.4426950408889634

F32 = jnp.float32
BF16 = jnp.bfloat16
I32 = jnp.int32
INT_MIN = -2 ** 31


def _monotone_key_of(value):
    bits = int(np.array(value, np.float32).view(np.int32))
    return bits ^ ((bits >> 31) & 0x7FFFFFFF)


NEG_KEY = _monotone_key_of(NEG)


def _params(n_axes=1, semantics=None):
    return pltpu.CompilerParams(
        dimension_semantics=semantics or ("arbitrary",) * n_axes,
        vmem_limit_bytes=VMEM_LIMIT)


def _rms(x, g):
    ms = jnp.mean(x * x, axis=-1, keepdims=True)
    return x * lax.rsqrt(ms + EPS) * g


def _full(shape):
    nd = len(shape)
    return pl.BlockSpec(shape, lambda *_: (0,) * nd)


def _rope_table_kernel(pos_ref, invf_ref, mrot_ref, ma_ref, mb_ref, c_ref, sa_ref, sb_ref):
    ang = pos_ref[...] * invf_ref[...]
    c = jnp.cos(ang)
    s = jnp.sin(ang)
    c_ref[...] = jnp.where(mrot_ref[...] > 0, c, 1.0)
    sa_ref[...] = jnp.where(ma_ref[...] > 0, -s, 0.0)
    sb_ref[...] = jnp.where(mb_ref[...] > 0, s, 0.0)


def _rope_tables(pos_col, head_dim, rot):
    t = pos_col.shape[0]
    half = rot // 2
    inv_freq = jnp.float32(ROPE_THETA) ** (-jnp.arange(half, dtype=F32) * 2.0 / rot)
    lane = np.arange(LANES) % head_dim
    mrot = lane < rot
    ma = lane < half
    mb = (lane >= half) & (lane < rot)
    invf = jnp.where(jnp.asarray(mrot), inv_freq[np.where(mrot, lane % half, 0)], 0.0)[None, :]
    row = lambda m: jnp.asarray(m.astype(np.float32))[None, :]
    tm = min(1024, t)
    spec_row = _full((1, LANES))
    out_spec = pl.BlockSpec((tm, LANES), lambda i: (i, 0))
    return pl.pallas_call(
        _rope_table_kernel,
        grid=(t // tm,),
        in_specs=[pl.BlockSpec((tm, 1), lambda i: (i, 0)), spec_row, spec_row, spec_row, spec_row],
        out_specs=[out_spec] * 3,
        out_shape=[jax.ShapeDtypeStruct((t, LANES), F32)] * 3,
        compiler_params=_params(),
        name="rope_tables",
    )(pos_col, invf, row(mrot), row(ma), row(mb))


def _rope_table_t_kernel(pos_ref, invf_ref, c_ref, s_ref):
    ang = invf_ref[...] * pos_ref[...]
    c_ref[...] = jnp.cos(ang)
    s_ref[...] = jnp.sin(ang)


def _rope_tables_t(pos_row, rot):
    t = pos_row.shape[1]
    half = rot // 2
    inv_freq = jnp.float32(ROPE_THETA) ** (-jnp.arange(half, dtype=F32) * 2.0 / rot)
    tm = min(2048, t)
    out_spec = pl.BlockSpec((half, tm), lambda i: (0, i))
    return pl.pallas_call(
        _rope_table_t_kernel,
        grid=(t // tm,),
        in_specs=[pl.BlockSpec((1, tm), lambda i: (0, i)), _full((half, 1))],
        out_specs=[out_spec] * 2,
        out_shape=[jax.ShapeDtypeStruct((half, t), F32)] * 2,
        compiler_params=_params(),
        name="rope_tables_t",
    )(pos_row, inv_freq[:, None])


def _rope_t(x, c, s):
    half = c.shape[0]
    x1, x2 = x[:half, :], x[half:2 * half, :]
    return jnp.concatenate([x1 * c - x2 * s, x1 * s + x2 * c, x[2 * half:, :]], axis=0)


def _rope(x, c, sa, sb, half):
    parts = []
    for j in range(x.shape[1] // LANES):
        xj = x[:, j * LANES:(j + 1) * LANES]
        parts.append(xj * c + pltpu.roll(xj, LANES - half, 1) * sa + pltpu.roll(xj, half, 1) * sb)
    return parts


WI_ROWS = 16


def _proj_b_kernel(x_ref, g_ref, w_ref, wt_ref, cb_ref, sab_ref, sbb_ref, ci_ref, sai_ref, sbi_ref,
                   cbt_ref, sbt_ref, cit_ref, sit_ref, k_ref, ki_ref, qt_ref, vt_ref, qit_ref, wit_ref):
    h = _rms(x_ref[...], g_ref[...]).astype(BF16)
    tm = h.shape[0]
    p = jnp.dot(h, w_ref[...], preferred_element_type=F32)
    pt = lax.dot_general(wt_ref[...], h, (((1,), (1,)), ((), ())), preferred_element_type=F32)
    tb = (cb_ref[...], sab_ref[...], sbb_ref[...])
    ti = (ci_ref[...], sai_ref[...], sbi_ref[...])
    for j, kj in enumerate(_rope(p[:, :B_WIDTH], *tb, B_ROT // 2)):
        k_ref[:, j * LANES:(j + 1) * LANES] = kj.astype(BF16)
    ki_ref[...] = _rope(p[:, B_WIDTH:B_WIDTH + LANES], *ti, IDX_ROT // 2)[0].astype(BF16)
    cbt, sbt = cbt_ref[...], sbt_ref[...]
    for hh in range(B_HEADS):
        hs = slice(hh * B_HEAD_DIM, (hh + 1) * B_HEAD_DIM)
        qt_ref[hs, :] = (_rope_t(pt[hs, :], cbt, sbt) * (B_HEAD_DIM ** -0.5 * LOG2E)).astype(BF16)
    o = B_WIDTH
    for hh in range(B_HEADS):
        vt_ref[hh * V_ROWS:hh * V_ROWS + B_HEAD_DIM, :] = (
            pt[o + hh * B_HEAD_DIM:o + (hh + 1) * B_HEAD_DIM, :].astype(BF16))
        vt_ref[hh * V_ROWS + B_HEAD_DIM:(hh + 1) * V_ROWS, :] = jnp.ones((V_ROWS - B_HEAD_DIM, tm), BF16)
    o += B_WIDTH
    cit, sit = cit_ref[...], sit_ref[...]
    for hh in range(IDX_HEADS):
        hs = slice(hh * IDX_DIM, (hh + 1) * IDX_DIM)
        qit_ref[hs, :] = _rope_t(pt[o + hh * IDX_DIM:o + (hh + 1) * IDX_DIM, :], cit, sit).astype(BF16)
    o += IDX_HEADS * IDX_DIM
    wit_ref[...] = pt[o:o + SUBLANES, :] * ((IDX_HEADS ** -0.5) * (IDX_DIM ** -0.5))


def _proj_b(x2, g, w, wt, tabs_b, tabs_i, tabs_bt, tabs_it):
    t = x2.shape[0]
    tm = min(TM_PROJ, t)
    tok = lambda n: pl.BlockSpec((tm, n), lambda i: (i, 0))
    feat = lambda n: pl.BlockSpec((n, tm), lambda i: (0, i))
    outs_tok = [(B_WIDTH, BF16), (LANES, BF16)]
    outs_feat = [(B_WIDTH, BF16), (B_HEADS * V_ROWS, BF16), (IDX_HEADS * IDX_DIM, BF16), (SUBLANES, F32)]
    return pl.pallas_call(
        _proj_b_kernel,
        grid=(t // tm,),
        in_specs=[tok(D_MODEL), _full((1, D_MODEL)), _full(w.shape), _full(wt.shape)] + [tok(LANES)] * 6
                 + [feat(B_ROT // 2)] * 2 + [feat(IDX_ROT // 2)] * 2,
        out_specs=[tok(n) for n, _ in outs_tok] + [feat(n) for n, _ in outs_feat],
        out_shape=[jax.ShapeDtypeStruct((t, n), d) for n, d in outs_tok]
                  + [jax.ShapeDtypeStruct((n, t), d) for n, d in outs_feat],
        compiler_params=_params(),
        name="proj_b",
    )(x2, g, w, wt, *tabs_b, *tabs_i, *tabs_bt, *tabs_it)


def _ordered_key(score):
    bits = lax.bitcast_convert_type(score, I32)
    key = bits ^ ((bits >> 31) & 0x7FFFFFFF)
    return jnp.where(key == -1, 0, key)


def _score_of_key(key):
    return lax.bitcast_convert_type(key ^ ((key >> 31) & 0x7FFFFFFF), F32)


def _ce_desc(vals, i, j):
    a, b = vals[i], vals[j]
    vals[i] = jnp.maximum(a, b)
    vals[j] = jnp.minimum(a, b)


def _bitonic_sort(vals, desc=True):
    n = len(vals)
    k = 2
    while k <= n:
        j = k // 2
        while j >= 1:
            for i in range(n):
                l = i ^ j
                if l > i:
                    if ((i & k) == 0) == desc:
                        _ce_desc(vals, i, l)
                    else:
                        _ce_desc(vals, l, i)
            j //= 2
        k *= 2


def _bitonic_merge_desc(vals):
    n = len(vals)
    j = n // 2
    while j >= 1:
        for i in range(n):
            l = i ^ j
            if l > i:
                _ce_desc(vals, i, l)
        j //= 2


def _dsa_kernel(qit_ref, wit_ref, ki_ref, qt_ref, k_ref, vt_ref, o_ref, sc_ref, top_ref, topk_ref, acc_ref,
                bias0_ref, bias1_ref, lg0_ref, lg1_ref,
                *, seq, n_sel, tq, kc, skc, sub, rb):
    i = pl.program_id(1)
    q0 = i * tq
    n_full = q0 // skc
    n_rows = (n_full + 1) * skc
    n_pad = (seq - 1 - q0) - lax.broadcasted_iota(I32, (1, tq), 1)
    n_chunk = (q0 + tq + kc - 1) // kc
    wit = wit_ref[...]

    n_top = skc // SUBLANES
    top_ref[...] = jnp.full((skc, tq), -jnp.inf, F32)

    def score_batch(c, masked):
        halves = []
        for s in range(skc // sub):
            r0 = pl.multiple_of(c * skc + s * sub, sub)
            kic = ki_ref[pl.ds(r0, sub), :IDX_DIM]
            sc = None
            for h in range(IDX_HEADS):
                a = jnp.dot(kic, qit_ref[h * IDX_DIM:(h + 1) * IDX_DIM, :], preferred_element_type=F32)
                term = jnp.maximum(a, 0.0) * wit[h:h + 1, :]
                sc = term if sc is None else sc + term
            if masked:
                qpos = q0 + lax.broadcasted_iota(I32, (sub, tq), 1)
                krow = r0 + lax.broadcasted_iota(I32, (sub, tq), 0)
                sc = jnp.where(krow <= qpos, sc, -jnp.inf)
            sc_ref[pl.ds(r0, sub), :] = sc
            half = [sc[SUBLANES * i:SUBLANES * (i + 1), :] for i in range(sub // SUBLANES)]
            _bitonic_sort(half, desc=(s % 2 == 0))
            halves.append(half)
        vals = halves[0] + halves[1]
        _bitonic_merge_desc(vals)
        return vals

    def keep_top(batches):
        top = [top_ref[SUBLANES * i:SUBLANES * (i + 1), :] for i in range(n_top)]
        for vals in batches:
            top = [jnp.maximum(top[i], vals[n_top - 1 - i]) for i in range(n_top)]
            _bitonic_merge_desc(top)
        for i in range(n_top):
            top_ref[SUBLANES * i:SUBLANES * (i + 1), :] = top[i]

    def score_pair(j, _):
        keep_top([score_batch(2 * j, False), score_batch(2 * j + 1, False)])
        return 0

    lax.fori_loop(0, n_full // 2, score_pair, 0)

    @pl.when(n_full % 2 == 1)
    def _():
        keep_top([score_batch(n_full - 1, False)])

    keep_top([score_batch(n_full, True)])

    def search(count):
        def bit_body(b, carry):
            ans_u, = carry
            cand_u = ans_u | jnp.left_shift(jnp.int32(1), 31 - b)
            ok = count(cand_u ^ INT_MIN, False) >= n_sel
            return (jnp.where(ok, cand_u, ans_u),)

        ans_u, = lax.fori_loop(0, 32, bit_body, (jnp.zeros((1, tq), I32),))
        thr = ans_u ^ INT_MIN
        return thr, (n_sel - count(thr, True)).astype(F32)

    def pads(cand, strict):
        return jnp.where((NEG_KEY > cand) if strict else (NEG_KEY >= cand), n_pad, 0)

    topk_ref[...] = _ordered_key(top_ref[...])

    def count_top(cand, strict):
        t = topk_ref[...]
        hit = (t > cand) if strict else (t >= cand)
        return jnp.sum(jnp.where(hit, 1.0, 0.0), axis=0, keepdims=True).astype(I32) + pads(cand, strict)

    def search_top():
        def digit_body(b, carry):
            ans_u, = carry
            shift = 30 - 2 * b
            for d in (1, 2, 3):
                cand_u = carry[0] | jnp.left_shift(jnp.int32(d), shift)
                ans_u = jnp.where(count_top(cand_u ^ INT_MIN, False) >= n_sel, cand_u, ans_u)
            return (ans_u,)

        ans_u, = lax.fori_loop(0, 16, digit_body, (jnp.zeros((1, tq), I32),))
        thr = ans_u ^ INT_MIN
        return thr, (n_sel - count_top(thr, True)).astype(F32)

    def count_all(cand, strict):
        acc_rows = 4 * SUBLANES
        cand_f = _score_of_key(cand)

        def body(r, acc):
            r0 = pl.multiple_of(r * rb, rb)
            blk = sc_ref[pl.ds(r0, rb), :]
            for j in range(rb // acc_rows):
                part = blk[j * acc_rows:(j + 1) * acc_rows, :]
                hit = (part > cand_f) if strict else (part >= cand_f)
                acc = acc + jnp.where(hit, 1, 0)
            return acc

        acc = lax.fori_loop(0, n_rows // rb, body, jnp.zeros((acc_rows, tq), I32))
        return jnp.sum(acc.astype(F32), axis=0, keepdims=True).astype(I32) + pads(cand, strict)

    thr, need = search_top()
    last = topk_ref[skc - SUBLANES:skc, :]
    overflow = jnp.max(jnp.where(last > thr, 1.0, 0.0))
    thr, need = lax.cond(overflow > 0.0, lambda: search(count_all), lambda: (thr, need))
    thr_f = _score_of_key(thr)

    acc_ref[...] = jnp.zeros_like(acc_ref)
    bias_refs = (bias0_ref, bias1_ref)
    lg_refs = (lg0_ref, lg1_ref)
    hk = kc // 2
    lower = (lax.broadcasted_iota(I32, (hk, hk), 1) < lax.broadcasted_iota(I32, (hk, hk), 0))
    lower = jnp.where(lower, 1.0, 0.0).astype(BF16)

    def produce(c, slot, tie_base):
        r0 = pl.multiple_of(c * kc, kc)
        sc = sc_ref[pl.ds(r0, kc), :]
        tie = sc == thr_f
        tie_f = jnp.where(tie, 1.0, 0.0)
        tie_b = tie_f.astype(BF16)
        b0 = jnp.dot(lower, tie_b[:hk, :], preferred_element_type=F32)
        n0 = b0[hk - 1:hk, :] + tie_f[hk - 1:hk, :]
        b1 = jnp.dot(lower, tie_b[hk:, :], preferred_element_type=F32) + n0
        before = jnp.concatenate([b0, b1], axis=0)
        sel = (sc > thr_f) | (tie & (before < need - tie_base))
        bias_refs[slot][...] = jnp.where(sel, 0.0, NEG)
        for h in range(B_HEADS):
            hs = slice(h * B_HEAD_DIM, (h + 1) * B_HEAD_DIM)
            lg_refs[slot][h] = jnp.dot(k_ref[pl.ds(r0, kc), hs], qt_ref[hs, :], preferred_element_type=F32)
        return tie_base + b1[hk - 1:hk, :] + tie_f[kc - 1:kc, :]

    def consume(c, slot, ms):
        r0 = pl.multiple_of(c * kc, kc)
        bias = bias_refs[slot][...]
        new_ms = []
        for h in range(B_HEADS):
            vs = slice(h * V_ROWS, (h + 1) * V_ROWS)
            lg = lg_refs[slot][h] + bias
            m_new = jnp.maximum(ms[h], jnp.max(lg, axis=0, keepdims=True))
            alpha = jnp.exp2(ms[h] - m_new)
            p = jnp.exp2(lg - m_new).astype(BF16)
            pv = jnp.dot(vt_ref[vs, pl.ds(r0, kc)], p, preferred_element_type=F32)
            acc_ref[vs, :] = acc_ref[vs, :] * alpha + pv
            new_ms.append(m_new)
        return tuple(new_ms)

    def pair_body(j, carry):
        ms, tie_base = carry
        c = 2 * j
        tie_base = produce(c + 1, 1, tie_base)
        ms = consume(c, 0, ms)
        tie_base = produce(c + 2, 0, tie_base)
        ms = consume(c + 1, 1, ms)
        return ms, tie_base

    row = lambda v: jnp.full((1, tq), v, F32)
    ms0 = tuple(row(NEG) for _ in range(B_HEADS))
    n_pairs = (n_chunk - 1) // 2
    ms, tie_base = lax.fori_loop(0, n_pairs, pair_body, (ms0, produce(0, 0, row(0.0))))
    c0 = 2 * n_pairs

    @pl.when(c0 == n_chunk - 1)
    def _():
        consume(c0, 0, ms)

    @pl.when(c0 != n_chunk - 1)
    def _():
        produce(c0 + 1, 1, tie_base)
        consume(c0 + 1, 1, consume(c0, 0, ms))

    for h in range(B_HEADS):
        a = acc_ref[h * V_ROWS:(h + 1) * V_ROWS, :]
        o_ref[h * B_HEAD_DIM:(h + 1) * B_HEAD_DIM, :] = (
            a[:B_HEAD_DIM, :] / a[B_HEAD_DIM:B_HEAD_DIM + 1, :]).astype(o_ref.dtype)


def _dsa(qit, wit, ki, qt, k, vt_aug, bsz, seq, n_sel):
    tq = min(DSA_TQ, seq)
    kc = min(DSA_KC, seq)
    skc = min(DSA_SKC, seq)
    nq = seq // tq
    once = pl.Buffered(1)
    kern = functools.partial(_dsa_kernel, seq=seq, n_sel=n_sel, tq=tq, kc=kc, skc=skc,
                             sub=min(DSA_SUB, skc), rb=min(DSA_RB, skc))
    q_tile = lambda rows: pl.BlockSpec((rows, tq), lambda b, i: (0, b * nq + i))
    return pl.pallas_call(
        kern,
        grid=(bsz, nq),
        in_specs=[
            q_tile(IDX_HEADS * IDX_DIM),
            q_tile(SUBLANES),
            pl.BlockSpec((seq, LANES), lambda b, i: (b, 0), pipeline_mode=once),
            q_tile(B_WIDTH),
            pl.BlockSpec((seq, B_WIDTH), lambda b, i: (b, 0), pipeline_mode=once),
            pl.BlockSpec((B_HEADS * V_ROWS, seq), lambda b, i: (0, b), pipeline_mode=once),
        ],
        out_specs=q_tile(B_WIDTH),
        out_shape=jax.ShapeDtypeStruct((B_WIDTH, bsz * seq), BF16),
        scratch_shapes=[pltpu.VMEM((seq, tq), F32), pltpu.VMEM((skc, tq), F32), pltpu.VMEM((skc, tq), I32),
                        pltpu.VMEM((B_HEADS * V_ROWS, tq), F32), pltpu.VMEM((kc, tq), F32),
                        pltpu.VMEM((kc, tq), F32), pltpu.VMEM((B_HEADS, kc, tq), F32),
                        pltpu.VMEM((B_HEADS, kc, tq), F32)],
        compiler_params=_params(2),
        name="dsa",
    )(qit, wit, ki, qt, k, vt_aug)


def _proj_a_kernel(x_ref, g_ref, w_ref, ws_ref, bs_ref, ya_ref, *, tm):
    h = _rms(x_ref[...], g_ref[...]).astype(BF16)
    p = jnp.dot(h, w_ref[...], preferred_element_type=F32)
    u = jax.nn.gelu(p[:, :A_WIDTH])
    v = jax.nn.gelu(p[:, A_WIDTH:])
    mu = jnp.mean(v, axis=-1, keepdims=True)
    var = jnp.mean(jnp.square(v - mu), axis=-1, keepdims=True)
    v = ((v - mu) * lax.rsqrt(var + EPS)).astype(BF16)
    causal = (lax.broadcasted_iota(I32, (CHUNK, CHUNK), 1) <= lax.broadcasted_iota(I32, (CHUNK, CHUNK), 0))
    bs = bs_ref[...]
    for gidx in range(A_GROUPS):
        wsm = jnp.where(causal, ws_ref[gidx], 0.0).astype(BF16)
        cs = slice(gidx * A_GROUP_DIM, (gidx + 1) * A_GROUP_DIM)
        for n in range(tm // CHUNK):
            rs = slice(n * CHUNK, (n + 1) * CHUNK)
            vm = jnp.dot(wsm, v[rs, cs], preferred_element_type=F32) + bs[:, gidx:gidx + 1]
            ya_ref[rs, cs] = (u[rs, cs] * vm).astype(ya_ref.dtype)


def _proj_a(x2, g, w, ws, bs_t):
    t = x2.shape[0]
    tm = min(TM_PROJ, t)
    return pl.pallas_call(
        functools.partial(_proj_a_kernel, tm=tm),
        grid=(t // tm,),
        in_specs=[pl.BlockSpec((tm, D_MODEL), lambda i: (i, 0)), _full((1, D_MODEL)),
                  _full((D_MODEL, 2 * A_WIDTH)), _full((A_GROUPS, CHUNK, CHUNK)), _full((CHUNK, A_GROUPS))],
        out_specs=pl.BlockSpec((tm, A_WIDTH), lambda i: (i, 0)),
        out_shape=jax.ShapeDtypeStruct((t, A_WIDTH), BF16),
        compiler_params=_params(),
        name="proj_a",
    )(x2, g, w, ws, bs_t)


def _proj_c_kernel(x_ref, xh_ref, g_ref, w_ref, cw_ref, yc_ref, *, tm, tiles_per_seq):
    g = g_ref[...]
    w = w_ref[...]
    p = jnp.dot(_rms(x_ref[...], g).astype(BF16), w, preferred_element_type=F32)
    ph = jnp.dot(_rms(xh_ref[...], g).astype(BF16), w[:, C_WIDTH:], preferred_element_type=F32)
    first = (pl.program_id(0) % tiles_per_seq) == 0
    zh = ph[:, :C_WIDTH] * ph[:, C_WIDTH:] * jnp.where(first, 0.0, 1.0)
    z = p[:, C_WIDTH:2 * C_WIDTH] * p[:, 2 * C_WIDTH:]
    row = lax.broadcasted_iota(I32, (tm, C_WIDTH), 0)
    z1 = jnp.where(row == 0, zh[SUBLANES - 1:SUBLANES, :], pltpu.roll(z, 1, 0))
    z2 = jnp.where(row == 0, zh[SUBLANES - 2:SUBLANES - 1, :],
                   jnp.where(row == 1, zh[SUBLANES - 1:SUBLANES, :], pltpu.roll(z, 2, 0)))
    cw = cw_ref[...]
    y = cw[0:1, :] * z2 + cw[1:2, :] * z1 + cw[2:3, :] * z
    yc_ref[...] = (p[:, :C_WIDTH] * y).astype(yc_ref.dtype)


def _proj_c(x2, g, w, cw, seq):
    t = x2.shape[0]
    tm = min(TM_PROJ, seq)
    per8 = tm // SUBLANES
    return pl.pallas_call(
        functools.partial(_proj_c_kernel, tm=tm, tiles_per_seq=seq // tm),
        grid=(t // tm,),
        in_specs=[pl.BlockSpec((tm, D_MODEL), lambda i: (i, 0)),
                  pl.BlockSpec((SUBLANES, D_MODEL), lambda i: (jnp.maximum(i * per8 - 1, 0), 0)),
                  _full((1, D_MODEL)), _full((D_MODEL, 3 * C_WIDTH)), _full((CONV_WIDTH, C_WIDTH))],
        out_specs=pl.BlockSpec((tm, C_WIDTH), lambda i: (i, 0)),
        out_shape=jax.ShapeDtypeStruct((t, C_WIDTH), BF16),
        compiler_params=_params(),
        name="proj_c",
    )(x2, x2, g, w, cw)


def _merge_kernel(x_ref, g_ref, wg_ref, ya_ref, ybt_ref, yc_ref, wa_ref, wb_ref, wc_ref, wo_ref, o_ref):
    x = x_ref[...]
    h = _rms(x, g_ref[...]).astype(BF16)
    projected = (
        jnp.dot(ya_ref[...], wa_ref[...], preferred_element_type=F32),
        lax.dot_general(ybt_ref[...], wb_ref[...], (((0,), (0,)), ((), ())), preferred_element_type=F32),
        jnp.dot(yc_ref[...], wc_ref[...], preferred_element_type=F32),
    )
    merged = None
    for j, proj in enumerate(projected):
        gate = jnp.dot(h, wg_ref[:, j * D_MODEL:(j + 1) * D_MODEL], preferred_element_type=F32)
        term = jax.nn.sigmoid(gate) * proj
        merged = term if merged is None else merged + term
    o_ref[...] = x + jnp.dot(merged.astype(BF16), wo_ref[...], preferred_element_type=F32)


def _merge(x2, g, wg, ya, ybt, yc, wa, wb, wc, wo):
    t = x2.shape[0]
    tm = min(TM_MERGE, t)
    tok = lambda n: pl.BlockSpec((tm, n), lambda i: (i, 0))
    wproj = _full((A_WIDTH, D_MODEL))
    return pl.pallas_call(
        _merge_kernel,
        grid=(t // tm,),
        in_specs=[tok(D_MODEL), _full((1, D_MODEL)), _full((D_MODEL, N_BRANCH * D_MODEL)),
                  tok(A_WIDTH), pl.BlockSpec((B_WIDTH, tm), lambda i: (0, i)), tok(C_WIDTH),
                  wproj, wproj, wproj, _full((D_MODEL, D_MODEL))],
        out_specs=tok(D_MODEL),
        out_shape=jax.ShapeDtypeStruct((t, D_MODEL), F32),
        compiler_params=_params(),
        name="merge",
    )(x2, g, wg, ya, ybt, yc, wa, wb, wc, wo)


def _router_kernel(x_ref, g_ref, w_ref, b_ref, h_ref, eid_ref, wt_ref):
    h = _rms(x_ref[...], g_ref[...])
    h_ref[...] = h
    logits = jnp.dot(h, w_ref[...], preferred_element_type=F32, precision=lax.Precision.HIGHEST) + b_ref[...]
    lane = lax.broadcasted_iota(I32, logits.shape, 1)
    lane_f = lane.astype(F32)
    ninf = -jnp.inf

    def first_max(vals):
        vmax = jnp.max(vals, axis=1, keepdims=True)
        idx = jnp.min(jnp.where(vals == vmax, lane_f, float(LANES)), axis=1, keepdims=True)
        return vmax, idx

    gl = jnp.where(lane < N_GROUPS, logits, ninf)
    gmax, grp = first_max(gl)
    gsum = jnp.sum(jnp.where(lane < N_GROUPS, jnp.exp(logits - gmax), 0.0), axis=1, keepdims=True)
    gw = 1.0 / gsum
    e_lane = lane - N_GROUPS
    in_grp = (e_lane >= 0) & (e_lane < N_EXPERTS) & ((e_lane >> 3).astype(F32) == grp)
    el = jnp.where(in_grp, logits, ninf)
    v1, i1 = first_max(el)
    v2, i2 = first_max(jnp.where(lane_f == i1, ninf, el))
    e2 = jnp.exp(v2 - v1)
    den = 1.0 + e2
    w1 = (1.0 / den) * gw
    w2 = (e2 / den) * gw
    eid1 = (i1 - N_GROUPS).astype(I32)
    eid2 = (i2 - N_GROUPS).astype(I32)
    eid_ref[...] = jnp.where(lane == 0, eid1, jnp.where(lane == 1, eid2, 0))
    wt_ref[...] = jnp.where(lane == 0, w1, jnp.where(lane == 1, w2, 0.0))


def _router(x2, g, w, b):
    t = x2.shape[0]
    tm = min(TM_ROUTE, t)
    tok = lambda n: pl.BlockSpec((tm, n), lambda i: (i, 0))
    return pl.pallas_call(
        _router_kernel,
        grid=(t // tm,),
        in_specs=[tok(D_MODEL), _full((1, D_MODEL)), _full((D_MODEL, LANES)), _full((1, LANES))],
        out_specs=[tok(D_MODEL), tok(LANES), tok(LANES)],
        out_shape=[jax.ShapeDtypeStruct((t, D_MODEL), F32), jax.ShapeDtypeStruct((t, LANES), I32),
                   jax.ShapeDtypeStruct((t, LANES), F32)],
        compiler_params=_params(),
        name="router",
    )(x2, g, w, b)


def _rank_kernel(eid_ref, rank_ref, cnt_ref, *, tm):
    @pl.when(pl.program_id(0) == 0)
    def _():
        cnt_ref[...] = jnp.zeros_like(cnt_ref)

    eid = eid_ref[...]
    lane = lax.broadcasted_iota(I32, (tm, LANES), 1)
    o1 = lane == eid[:, 0:1]
    o2 = lane == eid[:, 1:2]
    both = jnp.where(o1 | o2, 1.0, 0.0)
    lower = lax.broadcasted_iota(I32, (tm, tm), 1) < lax.broadcasted_iota(I32, (tm, tm), 0)
    before = jnp.dot(jnp.where(lower, 1.0, 0.0).astype(BF16), both.astype(BF16),
                     preferred_element_type=F32) + cnt_ref[...]
    r1 = jnp.sum(jnp.where(o1, before, 0.0), axis=1, keepdims=True).astype(I32)
    r2 = jnp.sum(jnp.where(o2, before, 0.0), axis=1, keepdims=True).astype(I32)
    rank_ref[...] = jnp.where(lane == 0, r1, jnp.where(lane == 1, r2, 0))
    cnt_ref[...] = cnt_ref[...] + jnp.sum(both, axis=0, keepdims=True)


def _rank(eid):
    t = eid.shape[0]
    tm = min(TM_RANK, t)
    return pl.pallas_call(
        functools.partial(_rank_kernel, tm=tm),
        grid=(t // tm,),
        in_specs=[pl.BlockSpec((tm, LANES), lambda i: (i, 0))],
        out_specs=[pl.BlockSpec((tm, LANES), lambda i: (i, 0)), _full((1, LANES))],
        out_shape=[jax.ShapeDtypeStruct((t, LANES), I32), jax.ShapeDtypeStruct((1, LANES), F32)],
        compiler_params=_params(),
        name="rank",
    )(eid)


def _dest_kernel(eid_ref, rank_ref, start_ref, d1_ref, d2_ref):
    eid = eid_ref[...]
    lane = lax.broadcasted_iota(I32, eid.shape, 1)
    start = start_ref[...]
    s1 = jnp.sum(jnp.where(lane == eid[:, 0:1], start, 0.0), axis=1, keepdims=True).astype(I32)
    s2 = jnp.sum(jnp.where(lane == eid[:, 1:2], start, 0.0), axis=1, keepdims=True).astype(I32)
    dest = jnp.where(lane == 0, s1, jnp.where(lane == 1, s2, 0)) + rank_ref[...]
    dest_t = jnp.transpose(dest.astype(F32))[:SUBLANES, :].astype(I32)
    d1_ref[...] = dest_t[0:1, :]
    d2_ref[...] = dest_t[1:2, :]


def _dest(eid, rank, start_row):
    t = eid.shape[0]
    tm = min(TM_ROUTE, t)
    tok = pl.BlockSpec((tm, LANES), lambda i: (i, 0))
    return pl.pallas_call(
        _dest_kernel,
        grid=(t // tm,),
        in_specs=[tok, tok, _full((1, LANES))],
        out_specs=[pl.BlockSpec((1, tm), lambda i: (0, i))] * 2,
        out_shape=[jax.ShapeDtypeStruct((1, t), I32)] * 2,
        compiler_params=_params(),
        name="dest",
    )(eid, rank, start_row)


def _row_copy(src_ref, src_row, dst_ref, dst_row, sem):
    return pltpu.make_async_copy(src_ref.at[pl.ds(src_row, 1)], dst_ref.at[pl.ds(dst_row, 1)], sem)


def _rows_copy(src_ref, dst_ref, n, sem):
    return pltpu.make_async_copy(src_ref.at[pl.ds(0, n)], dst_ref.at[pl.ds(0, n)], sem)


def _scatter_kernel(d1_ref, d2_ref, h_ref, xs_in_ref, xs_ref, sem, *, tm):
    del xs_in_ref

    def issue(t, _):
        for d_ref in (d1_ref, d2_ref):
            _row_copy(h_ref, t, xs_ref, d_ref[0, t], sem).start()
        return 0

    lax.fori_loop(0, tm, issue, 0)
    for _ in range(2):
        _rows_copy(h_ref, xs_ref, tm, sem).wait()


def _scatter_rows(dests, h, n_rows):
    t = h.shape[0]
    tm = min(TM_MOVE, t)
    xs0 = jnp.zeros((n_rows, D_MODEL), F32)
    return pl.pallas_call(
        functools.partial(_scatter_kernel, tm=tm),
        grid=(t // tm,),
        in_specs=[pl.BlockSpec((1, tm), lambda i: (0, i), memory_space=pltpu.SMEM)] * 2
                 + [pl.BlockSpec((tm, D_MODEL), lambda i: (i, 0)), pl.BlockSpec(memory_space=pl.ANY)],
        out_specs=pl.BlockSpec(memory_space=pl.ANY),
        out_shape=jax.ShapeDtypeStruct((n_rows, D_MODEL), F32),
        scratch_shapes=[pltpu.SemaphoreType.DMA(())],
        input_output_aliases={3: 0},
        compiler_params=_params(),
        name="scatter_rows",
    )(*dests, h, xs0)


def _expert_kernel(be_ref, nused_ref, xs_ref, wg_ref, wu_ref, wd_ref, y_ref, wg_bf, wu_bf, wd_bf):
    b = pl.program_id(0)

    @pl.when((b == 0) | (be_ref[b] != be_ref[jnp.maximum(b - 1, 0)]))
    def _():
        wg_bf[...] = wg_ref[...].astype(BF16)
        wu_bf[...] = wu_ref[...].astype(BF16)
        wd_bf[...] = wd_ref[...].astype(BF16)

    @pl.when(b < nused_ref[0])
    def _():
        xb = xs_ref[...].astype(BF16)
        gate = jnp.dot(xb, wg_bf[...], preferred_element_type=F32)
        up = jnp.dot(xb, wu_bf[...], preferred_element_type=F32)
        hid = (jax.nn.silu(gate) * up).astype(BF16)
        y_ref[...] = jnp.dot(hid, wd_bf[...], preferred_element_type=F32)

    @pl.when(b >= nused_ref[0])
    def _():
        y_ref[...] = jnp.zeros_like(y_ref)


def _experts(block_exp, n_used, xs, wg, wu, wd, layer):
    n_rows = xs.shape[0]
    w_in_spec = pl.BlockSpec((None, None, D_MODEL, D_FF_EXPERT), lambda b, be, nu: (layer, be[b], 0, 0))
    grid_spec = pltpu.PrefetchScalarGridSpec(
        num_scalar_prefetch=2,
        grid=(n_rows // EXP_BLOCK,),
        in_specs=[pl.BlockSpec((EXP_BLOCK, D_MODEL), lambda b, be, nu: (b, 0)), w_in_spec, w_in_spec,
                  pl.BlockSpec((None, None, D_FF_EXPERT, D_MODEL), lambda b, be, nu: (layer, be[b], 0, 0))],
        out_specs=pl.BlockSpec((EXP_BLOCK, D_MODEL), lambda b, be, nu: (b, 0)),
        scratch_shapes=[pltpu.VMEM((D_MODEL, D_FF_EXPERT), BF16), pltpu.VMEM((D_MODEL, D_FF_EXPERT), BF16),
                        pltpu.VMEM((D_FF_EXPERT, D_MODEL), BF16)],
    )
    return pl.pallas_call(
        _expert_kernel,
        grid_spec=grid_spec,
        out_shape=jax.ShapeDtypeStruct((n_rows, D_MODEL), F32),
        compiler_params=_params(),
        name="experts",
    )(block_exp, n_used, xs, wg, wu, wd)


def _combine_kernel(d1_ref, d2_ref, d1_next_ref, d2_next_ref, x_ref, wt_ref, g_ref, y_ref, o_ref, buf_ref, sems,
                    *, tm, steps, final_norm):
    i = pl.program_id(0)

    def gather(idx_refs, slot):
        def issue(t, _):
            for j, idx_ref in enumerate(idx_refs):
                _row_copy(y_ref, idx_ref[0, t], buf_ref.at[slot], j * tm + t, sems.at[slot]).start()
            return 0

        lax.fori_loop(0, tm, issue, 0)

    @pl.when(i == 0)
    def _():
        gather((d1_ref, d2_ref), 0)

    @pl.when(i + 1 < steps)
    def _():
        gather((d1_next_ref, d2_next_ref), (i + 1) % 2)

    slot = i % 2
    _rows_copy(y_ref, buf_ref.at[slot], 2 * tm, sems.at[slot]).wait()
    wt = wt_ref[...]
    rows = buf_ref[slot]
    out = x_ref[...] + (wt[:, 0:1] * rows[:tm, :] + wt[:, 1:2] * rows[tm:, :])
    if final_norm:
        out = _rms(out, g_ref[...])
    o_ref[...] = out


def _combine(dests, x2, wt, g_final, y, final_norm):
    t = x2.shape[0]
    tm = min(TM_MOVE, t)
    steps = t // tm
    tok = lambda n: pl.BlockSpec((tm, n), lambda i: (i, 0))
    idx = lambda f: pl.BlockSpec((1, tm), f, memory_space=pltpu.SMEM)
    return pl.pallas_call(
        functools.partial(_combine_kernel, tm=tm, steps=steps, final_norm=final_norm),
        grid=(steps,),
        in_specs=[idx(lambda i: (0, i))] * 2 + [idx(lambda i: (0, jnp.minimum(i + 1, steps - 1)))] * 2
                 + [tok(D_MODEL), tok(LANES), _full((1, D_MODEL)), pl.BlockSpec(memory_space=pl.ANY)],
        out_specs=tok(D_MODEL),
        out_shape=jax.ShapeDtypeStruct((t, D_MODEL), F32),
        scratch_shapes=[pltpu.VMEM((2, 2 * tm, D_MODEL), F32), pltpu.SemaphoreType.DMA((2,))],
        compiler_params=_params(),
        name="combine",
    )(*dests, *dests, x2, wt, g_final, y)


W_IN_SIZES = (A_WIDTH, A_WIDTH, B_WIDTH, B_WIDTH, B_WIDTH, IDX_HEADS * IDX_DIM, IDX_DIM, IDX_HEADS,
              C_WIDTH, C_WIDTH, C_WIDTH, N_BRANCH * D_MODEL)
W_IN_CUTS = tuple(int(c) for c in np.cumsum((0,) + W_IN_SIZES))
W_BT_ROWS = 2 * B_WIDTH + IDX_HEADS * IDX_DIM + WI_ROWS


def _repack_kernel(w_ref, wa_ref, wb_ref, wbt_ref, wc_ref, wg_ref):
    x = w_ref[...]
    u, va, q, k, vb, qi, ki, wi, cb, cc, cx, gates = [
        x[:, a:b] for a, b in zip(W_IN_CUTS[:-1], W_IN_CUTS[1:])]
    for j, part in enumerate((u, va)):
        wa_ref[:, j * A_WIDTH:(j + 1) * A_WIDTH] = part.astype(BF16)
    wb_ref[:, :B_WIDTH] = k.astype(BF16)
    ki_pad = jnp.concatenate([ki, jnp.zeros((ki.shape[0], LANES - IDX_DIM), F32)], axis=1)
    wb_ref[:, B_WIDTH:] = ki_pad.astype(BF16)
    row = 0
    for part in (q, vb, qi):
        for j in range(part.shape[1] // LANES):
            wbt_ref[row:row + LANES, :] = jnp.transpose(part[:, j * LANES:(j + 1) * LANES]).astype(BF16)
            row += LANES
    lane = lax.broadcasted_iota(I32, (x.shape[0], LANES), 1)
    wi_wide = jnp.where(lane < IDX_HEADS, x[:, W_IN_CUTS[7]:W_IN_CUTS[7] + LANES], 0.0)
    wbt_ref[row:row + WI_ROWS, :] = jnp.transpose(wi_wide)[:WI_ROWS, :].astype(BF16)
    for j, part in enumerate((cb, cc, cx)):
        wc_ref[:, j * C_WIDTH:(j + 1) * C_WIDTH] = part.astype(BF16)
    wg_ref[...] = gates.astype(BF16)


def _repack_w_in(w_in):
    depth, d, d_in = w_in.shape
    rows = lambda n: pl.BlockSpec((None, LANES, n), lambda l, i: (l, i, 0))
    outs = [2 * A_WIDTH, B_WIDTH + LANES, None, 3 * C_WIDTH, N_BRANCH * D_MODEL]
    shapes = [jax.ShapeDtypeStruct((depth, W_BT_ROWS, d) if n is None else (depth, d, n), BF16) for n in outs]
    specs = [pl.BlockSpec((None, W_BT_ROWS, LANES), lambda l, i: (l, 0, i)) if n is None else rows(n)
             for n in outs]
    return pl.pallas_call(
        _repack_kernel,
        grid=(depth, d // LANES),
        in_specs=[rows(d_in)],
        out_specs=specs,
        out_shape=shapes,
        compiler_params=_params(2),
        name="repack_w_in",
    )(w_in)


def _token_mixers(x2, g, w_parts, ws, bs, conv_w, wpa, wpb, wpc, w_out, tabs, bsz, seq):
    w_a, w_b, w_bt, w_c, w_g = w_parts
    n_sel = min(TOPK_MAX, seq // 4)
    k, ki, qt, vt_aug, qit, wit = _proj_b(x2, g, w_b, w_bt, *tabs)
    ybt = _dsa(qit, wit, ki, qt, k, vt_aug, bsz, seq, n_sel)
    ya = _proj_a(x2, g, w_a, ws.astype(BF16), bs.T)
    yc = _proj_c(x2, g, w_c, conv_w, seq)
    bf = lambda w: w.astype(BF16)
    return _merge(x2, g, w_g, ya, ybt, yc, bf(wpa), bf(wpb), bf(wpc), bf(w_out))


def _hier_moe(x2, g, rg_w, rg_b, re_w, re_b, w_gate, w_up, w_down, g_final, layer):
    t = x2.shape[0]
    pad_c = LANES - N_GROUPS - N_EXPERTS
    w_r = jnp.pad(jnp.concatenate([rg_w, re_w], axis=1), ((0, 0), (0, pad_c)))
    b_r = jnp.pad(jnp.concatenate([rg_b, re_b]), (0, pad_c))[None, :]
    h, eid, wt = _router(x2, g, w_r, b_r)
    rank, counts = _rank(eid)
    cnt = counts[0, :N_EXPERTS].astype(I32)
    padded = ((cnt + EXP_BLOCK - 1) // EXP_BLOCK) * EXP_BLOCK
    pend = jnp.cumsum(padded)
    pstart = pend - padded
    n_blocks = (2 * t) // EXP_BLOCK + N_EXPERTS
    n_rows = n_blocks * EXP_BLOCK
    block_row = jnp.arange(n_blocks, dtype=I32) * EXP_BLOCK
    block_exp = jnp.minimum(jnp.sum((pend[None, :] <= block_row[:, None]).astype(I32), axis=1),
                            N_EXPERTS - 1)
    n_used = (pend[-1:] // EXP_BLOCK).astype(I32)
    start_row = jnp.pad(pstart.astype(F32), (0, LANES - N_EXPERTS))[None, :]
    dests = _dest(eid, rank, start_row)
    xs = _scatter_rows(dests, h, n_rows)
    y = _experts(block_exp, n_used, xs, w_gate, w_up, w_down, layer)
    return _combine(dests, x2, wt, g_final, y, layer == w_gate.shape[0] - 1)


def kernel(x, positions, norm_mix_g, norm_ffn_g, norm_final_g, w_in, gmlp_ws, gmlp_b, conv_w, w_proj_a,
           w_proj_b, w_proj_c, w_out, router_group_w, router_group_b, router_expert_w, router_expert_b,
           expert_w_gate, expert_w_up, expert_w_down):
    bsz, seq, d = x.shape
    depth = w_in.shape[0]
    x2 = x.reshape(bsz * seq, d)
    pos = positions.astype(F32).reshape(bsz * seq)
    tabs = (_rope_tables(pos[:, None], B_HEAD_DIM, B_ROT), _rope_tables(pos[:, None], IDX_DIM, IDX_ROT),
            _rope_tables_t(pos[None, :], B_ROT), _rope_tables_t(pos[None, :], IDX_ROT))
    g_final = norm_final_g[None, :]
    w_parts = _repack_w_in(w_in)
    for l in range(depth):
        x2 = _token_mixers(x2, norm_mix_g[l][None, :], [w[l] for w in w_parts], gmlp_ws[l], gmlp_b[l], conv_w[l],
                           w_proj_a[l], w_proj_b[l], w_proj_c[l], w_out[l], tabs, bsz, seq)
        x2 = _hier_moe(x2, norm_ffn_g[l][None, :], router_group_w[l], router_group_b[l],
                       router_expert_w[l], router_expert_b[l], expert_w_gate, expert_w_up,
                       expert_w_down, g_final, l)
    return x2.reshape(bsz, seq, d)
```

```python
import functools

import numpy as np
import jax
import jax.numpy as jnp
from jax import lax
from jax.experimental import pallas as pl
from jax.experimental.pallas import tpu as pltpu

D_MODEL = 1024
CHUNK = 128
A_GROUPS = 4
A_GROUP_DIM = 128
A_WIDTH = A_GROUPS * A_GROUP_DIM
B_HEADS = 4
B_HEAD_DIM = 128
B_WIDTH = B_HEADS * B_HEAD_DIM
B_ROT = B_HEAD_DIM // 4
IDX_HEADS = 4
IDX_DIM = 64
IDX_ROT = IDX_DIM // 4
TOPK_MAX = 256
C_WIDTH = 512
CONV_WIDTH = 3
N_BRANCH = 3
N_GROUPS = 4
EXP_PER_GROUP = 8
N_EXPERTS = N_GROUPS * EXP_PER_GROUP
D_FF_EXPERT = 512
ROPE_THETA = 500000.0
EPS = 1e-6
NEG = -1e30

LANES = 128
SUBLANES = 8
VMEM_LIMIT = 56 * 1024 * 1024

TM_PROJ = 512
TM_MERGE = 256
TM_ROUTE = 512
TM_RANK = 256
TM_MOVE = 256
EXP_BLOCK = 512
DSA_TQ = 128
DSA_KC = 256
DSA_SKC = 512
DSA_SUB = 256
DSA_RB = 512
V_ROWS = B_HEAD_DIM + 16
LOG2E = 1.4426950408889634

F32 = jnp.float32
BF16 = jnp.bfloat16
I32 = jnp.int32
INT_MIN = -2 ** 31


def _monotone_key_of(value):
    bits = int(np.array(value, np.float32).view(np.int32))
    return bits ^ ((bits >> 31) & 0x7FFFFFFF)


NEG_KEY = _monotone_key_of(NEG)


def _params(n_axes=1, semantics=None):
    return pltpu.CompilerParams(
        dimension_semantics=semantics or ("arbitrary",) * n_axes,
        vmem_limit_bytes=VMEM_LIMIT)


def _rms(x, g):
    ms = jnp.mean(x * x, axis=-1, keepdims=True)
    return x * lax.rsqrt(ms + EPS) * g


def _full(shape):
    nd = len(shape)
    return pl.BlockSpec(shape, lambda *_: (0,) * nd)


def _rope_table_kernel(pos_ref, invf_ref, mrot_ref, ma_ref, mb_ref, c_ref, sa_ref, sb_ref):
    ang = pos_ref[...] * invf_ref[...]
    c = jnp.cos(ang)
    s = jnp.sin(ang)
    c_ref[...] = jnp.where(mrot_ref[...] > 0, c, 1.0)
    sa_ref[...] = jnp.where(ma_ref[...] > 0, -s, 0.0)
    sb_ref[...] = jnp.where(mb_ref[...] > 0, s, 0.0)


def _rope_tables(pos_col, head_dim, rot):
    t = pos_col.shape[0]
    half = rot // 2
    inv_freq = jnp.float32(ROPE_THETA) ** (-jnp.arange(half, dtype=F32) * 2.0 / rot)
    lane = np.arange(LANES) % head_dim
    mrot = lane < rot
    ma = lane < half
    mb = (lane >= half) & (lane < rot)
    invf = jnp.where(jnp.asarray(mrot), inv_freq[np.where(mrot, lane % half, 0)], 0.0)[None, :]
    row = lambda m: jnp.asarray(m.astype(np.float32))[None, :]
    tm = min(1024, t)
    spec_row = _full((1, LANES))
    out_spec = pl.BlockSpec((tm, LANES), lambda i: (i, 0))
    return pl.pallas_call(
        _rope_table_kernel,
        grid=(t // tm,),
        in_specs=[pl.BlockSpec((tm, 1), lambda i: (i, 0)), spec_row, spec_row, spec_row, spec_row],
        out_specs=[out_spec] * 3,
        out_shape=[jax.ShapeDtypeStruct((t, LANES), F32)] * 3,
        compiler_params=_params(),
        name="rope_tables",
    )(pos_col, invf, row(mrot), row(ma), row(mb))


def _rope_table_t_kernel(pos_ref, invf_ref, c_ref, s_ref):
    ang = invf_ref[...] * pos_ref[...]
    c_ref[...] = jnp.cos(ang)
    s_ref[...] = jnp.sin(ang)


def _rope_tables_t(pos_row, rot):
    t = pos_row.shape[1]
    half = rot // 2
    inv_freq = jnp.float32(ROPE_THETA) ** (-jnp.arange(half, dtype=F32) * 2.0 / rot)
    tm = min(2048, t)
    out_spec = pl.BlockSpec((half, tm), lambda i: (0, i))
    return pl.pallas_call(
        _rope_table_t_kernel,
        grid=(t // tm,),
        in_specs=[pl.BlockSpec((1, tm), lambda i: (0, i)), _full((half, 1))],
        out_specs=[out_spec] * 2,
        out_shape=[jax.ShapeDtypeStruct((half, t), F32)] * 2,
        compiler_params=_params(),
        name="rope_tables_t",
    )(pos_row, inv_freq[:, None])


def _rope_t(x, c, s):
    half = c.shape[0]
    x1, x2 = x[:half, :], x[half:2 * half, :]
    return jnp.concatenate([x1 * c - x2 * s, x1 * s + x2 * c, x[2 * half:, :]], axis=0)


def _rope(x, c, sa, sb, half):
    parts = []
    for j in range(x.shape[1] // LANES):
        xj = x[:, j * LANES:(j + 1) * LANES]
        parts.append(xj * c + pltpu.roll(xj, LANES - half, 1) * sa + pltpu.roll(xj, half, 1) * sb)
    return parts


WI_ROWS = 16


def _proj_b_kernel(x_ref, g_ref, w_ref, wt_ref, cb_ref, sab_ref, sbb_ref, ci_ref, sai_ref, sbi_ref,
                   cbt_ref, sbt_ref, cit_ref, sit_ref, k_ref, ki_ref, qt_ref, vt_ref, qit_ref, wit_ref):
    h = _rms(x_ref[...], g_ref[...]).astype(BF16)
    tm = h.shape[0]
    p = jnp.dot(h, w_ref[...], preferred_element_type=F32)
    pt = lax.dot_general(wt_ref[...], h, (((1,), (1,)), ((), ())), preferred_element_type=F32)
    tb = (cb_ref[...], sab_ref[...], sbb_ref[...])
    ti = (ci_ref[...], sai_ref[...], sbi_ref[...])
    for j, kj in enumerate(_rope(p[:, :B_WIDTH], *tb, B_ROT // 2)):
        k_ref[:, j * LANES:(j + 1) * LANES] = kj.astype(BF16)
    ki_ref[...] = _rope(p[:, B_WIDTH:B_WIDTH + LANES], *ti, IDX_ROT // 2)[0].astype(BF16)
    cbt, sbt = cbt_ref[...], sbt_ref[...]
    for hh in range(B_HEADS):
        hs = slice(hh * B_HEAD_DIM, (hh + 1) * B_HEAD_DIM)
        qt_ref[hs, :] = (_rope_t(pt[hs, :], cbt, sbt) * (B_HEAD_DIM ** -0.5 * LOG2E)).astype(BF16)
    o = B_WIDTH
    for hh in range(B_HEADS):
        vt_ref[hh * V_ROWS:hh * V_ROWS + B_HEAD_DIM, :] = (
            pt[o + hh * B_HEAD_DIM:o + (hh + 1) * B_HEAD_DIM, :].astype(BF16))
        vt_ref[hh * V_ROWS + B_HEAD_DIM:(hh + 1) * V_ROWS, :] = jnp.ones((V_ROWS - B_HEAD_DIM, tm), BF16)
    o += B_WIDTH
    cit, sit = cit_ref[...], sit_ref[...]
    for hh in range(IDX_HEADS):
        hs = slice(hh * IDX_DIM, (hh + 1) * IDX_DIM)
        qit_ref[hs, :] = _rope_t(pt[o + hh * IDX_DIM:o + (hh + 1) * IDX_DIM, :], cit, sit).astype(BF16)
    o += IDX_HEADS * IDX_DIM
    wit_ref[...] = pt[o:o + SUBLANES, :] * ((IDX_HEADS ** -0.5) * (IDX_DIM ** -0.5))


def _proj_b(x2, g, w, wt, tabs_b, tabs_i, tabs_bt, tabs_it):
    t = x2.shape[0]
    tm = min(TM_PROJ, t)
    tok = lambda n: pl.BlockSpec((tm, n), lambda i: (i, 0))
    feat = lambda n: pl.BlockSpec((n, tm), lambda i: (0, i))
    outs_tok = [(B_WIDTH, BF16), (LANES, BF16)]
    outs_feat = [(B_WIDTH, BF16), (B_HEADS * V_ROWS, BF16), (IDX_HEADS * IDX_DIM, BF16), (SUBLANES, F32)]
    return pl.pallas_call(
        _proj_b_kernel,
        grid=(t // tm,),
        in_specs=[tok(D_MODEL), _full((1, D_MODEL)), _full(w.shape), _full(wt.shape)] + [tok(LANES)] * 6
                 + [feat(B_ROT // 2)] * 2 + [feat(IDX_ROT // 2)] * 2,
        out_specs=[tok(n) for n, _ in outs_tok] + [feat(n) for n, _ in outs_feat],
        out_shape=[jax.ShapeDtypeStruct((t, n), d) for n, d in outs_tok]
                  + [jax.ShapeDtypeStruct((n, t), d) for n, d in outs_feat],
        compiler_params=_params(),
        name="proj_b",
    )(x2, g, w, wt, *tabs_b, *tabs_i, *tabs_bt, *tabs_it)


def _ordered_key(score):
    bits = lax.bitcast_convert_type(score, I32)
    key = bits ^ ((bits >> 31) & 0x7FFFFFFF)
    return jnp.where(key == -1, 0, key)


def _score_of_key(key):
    return lax.bitcast_convert_type(key ^ ((key >> 31) & 0x7FFFFFFF), F32)


def _ce_desc(vals, i, j):
    a, b = vals[i], vals[j]
    vals[i] = jnp.maximum(a, b)
    vals[j] = jnp.minimum(a, b)


def _sort_desc(vals):
    n = len(vals)
    p = 1
    while p < n:
        k = p
        while k >= 1:
            for j in range(k % p, n - k, 2 * k):
                for i in range(min(k, n - j - k)):
                    if (i + j) // (2 * p) == (i + j + k) // (2 * p):
                        _ce_desc(vals, i + j, i + j + k)
            k //= 2
        p *= 2


def _bitonic_merge_desc(vals):
    n = len(vals)
    j = n // 2
    while j >= 1:
        for i in range(n):
            l = i ^ j
            if l > i:
                _ce_desc(vals, i, l)
        j //= 2


def _dsa_kernel(qit_ref, wit_ref, ki_ref, qt_ref, k_ref, vt_ref, o_ref, sc_ref, top_ref, topk_ref, acc_ref,
                bias0_ref, bias1_ref, lg0_ref, lg1_ref,
                *, seq, n_sel, tq, kc, skc, sub, rb):
    i = pl.program_id(1)
    q0 = i * tq
    n_full = q0 // skc
    n_rows = (n_full + 1) * skc
    n_pad = (seq - 1 - q0) - lax.broadcasted_iota(I32, (1, tq), 1)
    n_chunk = (q0 + tq + kc - 1) // kc
    wit = wit_ref[...]

    n_top = skc // SUBLANES
    top_ref[...] = jnp.full((skc, tq), -jnp.inf, F32)

    def score_batch(c, masked):
        vals = []
        for s in range(skc // sub):
            r0 = pl.multiple_of(c * skc + s * sub, sub)
            kic = ki_ref[pl.ds(r0, sub), :IDX_DIM]
            sc = None
            for h in range(IDX_HEADS):
                a = jnp.dot(kic, qit_ref[h * IDX_DIM:(h + 1) * IDX_DIM, :], preferred_element_type=F32)
                term = jnp.maximum(a, 0.0) * wit[h:h + 1, :]
                sc = term if sc is None else sc + term
            if masked:
                qpos = q0 + lax.broadcasted_iota(I32, (sub, tq), 1)
                krow = r0 + lax.broadcasted_iota(I32, (sub, tq), 0)
                sc = jnp.where(krow <= qpos, sc, -jnp.inf)
            sc_ref[pl.ds(r0, sub), :] = sc
            vals += [sc[SUBLANES * i:SUBLANES * (i + 1), :] for i in range(sub // SUBLANES)]
        _sort_desc(vals)
        return vals

    def keep_top(batches):
        top = [top_ref[SUBLANES * i:SUBLANES * (i + 1), :] for i in range(n_top)]
        for vals in batches:
            top = [jnp.maximum(top[i], vals[n_top - 1 - i]) for i in range(n_top)]
            _bitonic_merge_desc(top)
        for i in range(n_top):
            top_ref[SUBLANES * i:SUBLANES * (i + 1), :] = top[i]

    def score_pair(j, _):
        keep_top([score_batch(2 * j, False), score_batch(2 * j + 1, False)])
        return 0

    lax.fori_loop(0, n_full // 2, score_pair, 0)

    @pl.when(n_full % 2 == 1)
    def _():
        keep_top([score_batch(n_full - 1, False)])

    keep_top([score_batch(n_full, True)])

    def search(count):
        def bit_body(b, carry):
            ans_u, = carry
            cand_u = ans_u | jnp.left_shift(jnp.int32(1), 31 - b)
            ok = count(cand_u ^ INT_MIN, False) >= n_sel
            return (jnp.where(ok, cand_u, ans_u),)

        ans_u, = lax.fori_loop(0, 32, bit_body, (jnp.zeros((1, tq), I32),))
        thr = ans_u ^ INT_MIN
        return thr, (n_sel - count(thr, True)).astype(F32)

    def pads(cand, strict):
        return jnp.where((NEG_KEY > cand) if strict else (NEG_KEY >= cand), n_pad, 0)

    topk_ref[...] = _ordered_key(top_ref[...])

    def count_top(cand, strict):
        t = topk_ref[...]
        hit = (t > cand) if strict else (t >= cand)
        return jnp.sum(jnp.where(hit, 1.0, 0.0), axis=0, keepdims=True).astype(I32) + pads(cand, strict)

    def search_top():
        def digit_body(b, carry):
            ans_u, = carry
            shift = 30 - 2 * b
            for d in (1, 2, 3):
                cand_u = carry[0] | jnp.left_shift(jnp.int32(d), shift)
                ans_u = jnp.where(count_top(cand_u ^ INT_MIN, False) >= n_sel, cand_u, ans_u)
            return (ans_u,)

        ans_u, = lax.fori_loop(0, 16, digit_body, (jnp.zeros((1, tq), I32),))
        thr = ans_u ^ INT_MIN
        return thr, (n_sel - count_top(thr, True)).astype(F32)

    def count_all(cand, strict):
        acc_rows = 4 * SUBLANES
        cand_f = _score_of_key(cand)

        def body(r, acc):
            r0 = pl.multiple_of(r * rb, rb)
            blk = sc_ref[pl.ds(r0, rb), :]
            for j in range(rb // acc_rows):
                part = blk[j * acc_rows:(j + 1) * acc_rows, :]
                hit = (part > cand_f) if strict else (part >= cand_f)
                acc = acc + jnp.where(hit, 1, 0)
            return acc

        acc = lax.fori_loop(0, n_rows // rb, body, jnp.zeros((acc_rows, tq), I32))
        return jnp.sum(acc.astype(F32), axis=0, keepdims=True).astype(I32) + pads(cand, strict)

    thr, need = search_top()
    last = topk_ref[skc - SUBLANES:skc, :]
    overflow = jnp.max(jnp.where(last > thr, 1.0, 0.0))
    thr, need = lax.cond(overflow > 0.0, lambda: search(count_all), lambda: (thr, need))
    thr_f = _score_of_key(thr)

    acc_ref[...] = jnp.zeros_like(acc_ref)
    bias_refs = (bias0_ref, bias1_ref)
    lg_refs = (lg0_ref, lg1_ref)
    hk = kc // 2
    lower = (lax.broadcasted_iota(I32, (hk, hk), 1) < lax.broadcasted_iota(I32, (hk, hk), 0))
    lower = jnp.where(lower, 1.0, 0.0).astype(BF16)

    def produce(c, slot, tie_base):
        r0 = pl.multiple_of(c * kc, kc)
        sc = sc_ref[pl.ds(r0, kc), :]
        tie = sc == thr_f
        tie_f = jnp.where(tie, 1.0, 0.0)
        tie_b = tie_f.astype(BF16)
        b0 = jnp.dot(lower, tie_b[:hk, :], preferred_element_type=F32)
        n0 = b0[hk - 1:hk, :] + tie_f[hk - 1:hk, :]
        b1 = jnp.dot(lower, tie_b[hk:, :], preferred_element_type=F32) + n0
        before = jnp.concatenate([b0, b1], axis=0)
        sel = (sc > thr_f) | (tie & (before < need - tie_base))
        bias_refs[slot][...] = jnp.where(sel, 0.0, NEG)
        for h in range(B_HEADS):
            hs = slice(h * B_HEAD_DIM, (h + 1) * B_HEAD_DIM)
            lg_refs[slot][h] = jnp.dot(k_ref[pl.ds(r0, kc), hs], qt_ref[hs, :], preferred_element_type=F32)
        return tie_base + b1[hk - 1:hk, :] + tie_f[kc - 1:kc, :]

    def consume(c, slot, ms):
        r0 = pl.multiple_of(c * kc, kc)
        bias = bias_refs[slot][...]
        new_ms = []
        for h in range(B_HEADS):
            vs = slice(h * V_ROWS, (h + 1) * V_ROWS)
            lg = lg_refs[slot][h] + bias
            m_new = jnp.maximum(ms[h], jnp.max(lg, axis=0, keepdims=True))
            alpha = jnp.exp2(ms[h] - m_new)
            p = jnp.exp2(lg - m_new).astype(BF16)
            pv = jnp.dot(vt_ref[vs, pl.ds(r0, kc)], p, preferred_element_type=F32)
            acc_ref[vs, :] = acc_ref[vs, :] * alpha + pv
            new_ms.append(m_new)
        return tuple(new_ms)

    def pair_body(j, carry):
        ms, tie_base = carry
        c = 2 * j
        tie_base = produce(c + 1, 1, tie_base)
        ms = consume(c, 0, ms)
        tie_base = produce(c + 2, 0, tie_base)
        ms = consume(c + 1, 1, ms)
        return ms, tie_base

    row = lambda v: jnp.full((1, tq), v, F32)
    ms0 = tuple(row(NEG) for _ in range(B_HEADS))
    n_pairs = (n_chunk - 1) // 2
    ms, tie_base = lax.fori_loop(0, n_pairs, pair_body, (ms0, produce(0, 0, row(0.0))))
    c0 = 2 * n_pairs

    @pl.when(c0 == n_chunk - 1)
    def _():
        consume(c0, 0, ms)

    @pl.when(c0 != n_chunk - 1)
    def _():
        produce(c0 + 1, 1, tie_base)
        consume(c0 + 1, 1, consume(c0, 0, ms))

    for h in range(B_HEADS):
        a = acc_ref[h * V_ROWS:(h + 1) * V_ROWS, :]
        o_ref[h * B_HEAD_DIM:(h + 1) * B_HEAD_DIM, :] = (
            a[:B_HEAD_DIM, :] / a[B_HEAD_DIM:B_HEAD_DIM + 1, :]).astype(o_ref.dtype)


def _dsa(qit, wit, ki, qt, k, vt_aug, bsz, seq, n_sel):
    tq = min(DSA_TQ, seq)
    kc = min(DSA_KC, seq)
    skc = min(DSA_SKC, seq)
    nq = seq // tq
    once = pl.Buffered(1)
    kern = functools.partial(_dsa_kernel, seq=seq, n_sel=n_sel, tq=tq, kc=kc, skc=skc,
                             sub=min(DSA_SUB, skc), rb=min(DSA_RB, skc))
    q_tile = lambda rows: pl.BlockSpec((rows, tq), lambda b, i: (0, b * nq + i))
    return pl.pallas_call(
        kern,
        grid=(bsz, nq),
        in_specs=[
            q_tile(IDX_HEADS * IDX_DIM),
            q_tile(SUBLANES),
            pl.BlockSpec((seq, LANES), lambda b, i: (b, 0), pipeline_mode=once),
            q_tile(B_WIDTH),
            pl.BlockSpec((seq, B_WIDTH), lambda b, i: (b, 0), pipeline_mode=once),
            pl.BlockSpec((B_HEADS * V_ROWS, seq), lambda b, i: (0, b), pipeline_mode=once),
        ],
        out_specs=q_tile(B_WIDTH),
        out_shape=jax.ShapeDtypeStruct((B_WIDTH, bsz * seq), BF16),
        scratch_shapes=[pltpu.VMEM((seq, tq), F32), pltpu.VMEM((skc, tq), F32), pltpu.VMEM((skc, tq), I32),
                        pltpu.VMEM((B_HEADS * V_ROWS, tq), F32), pltpu.VMEM((kc, tq), F32),
                        pltpu.VMEM((kc, tq), F32), pltpu.VMEM((B_HEADS, kc, tq), F32),
                        pltpu.VMEM((B_HEADS, kc, tq), F32)],
        compiler_params=_params(2),
        name="dsa",
    )(qit, wit, ki, qt, k, vt_aug)


def _proj_a_kernel(x_ref, g_ref, w_ref, ws_ref, bs_ref, ya_ref, *, tm):
    h = _rms(x_ref[...], g_ref[...]).astype(BF16)
    p = jnp.dot(h, w_ref[...], preferred_element_type=F32)
    u = jax.nn.gelu(p[:, :A_WIDTH])
    v = jax.nn.gelu(p[:, A_WIDTH:])
    mu = jnp.mean(v, axis=-1, keepdims=True)
    var = jnp.mean(jnp.square(v - mu), axis=-1, keepdims=True)
    v = ((v - mu) * lax.rsqrt(var + EPS)).astype(BF16)
    causal = (lax.broadcasted_iota(I32, (CHUNK, CHUNK), 1) <= lax.broadcasted_iota(I32, (CHUNK, CHUNK), 0))
    bs = bs_ref[...]
    for gidx in range(A_GROUPS):
        wsm = jnp.where(causal, ws_ref[gidx], 0.0).astype(BF16)
        cs = slice(gidx * A_GROUP_DIM, (gidx + 1) * A_GROUP_DIM)
        for n in range(tm // CHUNK):
            rs = slice(n * CHUNK, (n + 1) * CHUNK)
            vm = jnp.dot(wsm, v[rs, cs], preferred_element_type=F32) + bs[:, gidx:gidx + 1]
            ya_ref[rs, cs] = (u[rs, cs] * vm).astype(ya_ref.dtype)


def _proj_a(x2, g, w, ws, bs_t):
    t = x2.shape[0]
    tm = min(TM_PROJ, t)
    return pl.pallas_call(
        functools.partial(_proj_a_kernel, tm=tm),
        grid=(t // tm,),
        in_specs=[pl.BlockSpec((tm, D_MODEL), lambda i: (i, 0)), _full((1, D_MODEL)),
                  _full((D_MODEL, 2 * A_WIDTH)), _full((A_GROUPS, CHUNK, CHUNK)), _full((CHUNK, A_GROUPS))],
        out_specs=pl.BlockSpec((tm, A_WIDTH), lambda i: (i, 0)),
        out_shape=jax.ShapeDtypeStruct((t, A_WIDTH), BF16),
        compiler_params=_params(),
        name="proj_a",
    )(x2, g, w, ws, bs_t)


def _proj_c_kernel(x_ref, xh_ref, g_ref, w_ref, cw_ref, yc_ref, *, tm, tiles_per_seq):
    g = g_ref[...]
    w = w_ref[...]
    p = jnp.dot(_rms(x_ref[...], g).astype(BF16), w, preferred_element_type=F32)
    ph = jnp.dot(_rms(xh_ref[...], g).astype(BF16), w[:, C_WIDTH:], preferred_element_type=F32)
    first = (pl.program_id(0) % tiles_per_seq) == 0
    zh = ph[:, :C_WIDTH] * ph[:, C_WIDTH:] * jnp.where(first, 0.0, 1.0)
    z = p[:, C_WIDTH:2 * C_WIDTH] * p[:, 2 * C_WIDTH:]
    row = lax.broadcasted_iota(I32, (tm, C_WIDTH), 0)
    z1 = jnp.where(row == 0, zh[SUBLANES - 1:SUBLANES, :], pltpu.roll(z, 1, 0))
    z2 = jnp.where(row == 0, zh[SUBLANES - 2:SUBLANES - 1, :],
                   jnp.where(row == 1, zh[SUBLANES - 1:SUBLANES, :], pltpu.roll(z, 2, 0)))
    cw = cw_ref[...]
    y = cw[0:1, :] * z2 + cw[1:2, :] * z1 + cw[2:3, :] * z
    yc_ref[...] = (p[:, :C_WIDTH] * y).astype(yc_ref.dtype)


def _proj_c(x2, g, w, cw, seq):
    t = x2.shape[0]
    tm = min(TM_PROJ, seq)
    per8 = tm // SUBLANES
    return pl.pallas_call(
        functools.partial(_proj_c_kernel, tm=tm, tiles_per_seq=seq // tm),
        grid=(t // tm,),
        in_specs=[pl.BlockSpec((tm, D_MODEL), lambda i: (i, 0)),
                  pl.BlockSpec((SUBLANES, D_MODEL), lambda i: (jnp.maximum(i * per8 - 1, 0), 0)),
                  _full((1, D_MODEL)), _full((D_MODEL, 3 * C_WIDTH)), _full((CONV_WIDTH, C_WIDTH))],
        out_specs=pl.BlockSpec((tm, C_WIDTH), lambda i: (i, 0)),
        out_shape=jax.ShapeDtypeStruct((t, C_WIDTH), BF16),
        compiler_params=_params(),
        name="proj_c",
    )(x2, x2, g, w, cw)


def _merge_kernel(x_ref, g_ref, wg_ref, ya_ref, ybt_ref, yc_ref, wa_ref, wb_ref, wc_ref, wo_ref, o_ref):
    x = x_ref[...]
    h = _rms(x, g_ref[...]).astype(BF16)
    projected = (
        jnp.dot(ya_ref[...], wa_ref[...], preferred_element_type=F32),
        lax.dot_general(ybt_ref[...], wb_ref[...], (((0,), (0,)), ((), ())), preferred_element_type=F32),
        jnp.dot(yc_ref[...], wc_ref[...], preferred_element_type=F32),
    )
    merged = None
    for j, proj in enumerate(projected):
        gate = jnp.dot(h, wg_ref[:, j * D_MODEL:(j + 1) * D_MODEL], preferred_element_type=F32)
        term = jax.nn.sigmoid(gate) * proj
        merged = term if merged is None else merged + term
    o_ref[...] = x + jnp.dot(merged.astype(BF16), wo_ref[...], preferred_element_type=F32)


def _merge(x2, g, wg, ya, ybt, yc, wa, wb, wc, wo):
    t = x2.shape[0]
    tm = min(TM_MERGE, t)
    tok = lambda n: pl.BlockSpec((tm, n), lambda i: (i, 0))
    wproj = _full((A_WIDTH, D_MODEL))
    return pl.pallas_call(
        _merge_kernel,
        grid=(t // tm,),
        in_specs=[tok(D_MODEL), _full((1, D_MODEL)), _full((D_MODEL, N_BRANCH * D_MODEL)),
                  tok(A_WIDTH), pl.BlockSpec((B_WIDTH, tm), lambda i: (0, i)), tok(C_WIDTH),
                  wproj, wproj, wproj, _full((D_MODEL, D_MODEL))],
        out_specs=tok(D_MODEL),
        out_shape=jax.ShapeDtypeStruct((t, D_MODEL), F32),
        compiler_params=_params(),
        name="merge",
    )(x2, g, wg, ya, ybt, yc, wa, wb, wc, wo)


def _router_kernel(x_ref, g_ref, w_ref, b_ref, h_ref, eid_ref, wt_ref):
    h = _rms(x_ref[...], g_ref[...])
    h_ref[...] = h
    logits = jnp.dot(h, w_ref[...], preferred_element_type=F32, precision=lax.Precision.HIGHEST) + b_ref[...]
    lane = lax.broadcasted_iota(I32, logits.shape, 1)
    lane_f = lane.astype(F32)
    ninf = -jnp.inf

    def first_max(vals):
        vmax = jnp.max(vals, axis=1, keepdims=True)
        idx = jnp.min(jnp.where(vals == vmax, lane_f, float(LANES)), axis=1, keepdims=True)
        return vmax, idx

    gl = jnp.where(lane < N_GROUPS, logits, ninf)
    gmax, grp = first_max(gl)
    gsum = jnp.sum(jnp.where(lane < N_GROUPS, jnp.exp(logits - gmax), 0.0), axis=1, keepdims=True)
    gw = 1.0 / gsum
    e_lane = lane - N_GROUPS
    in_grp = (e_lane >= 0) & (e_lane < N_EXPERTS) & ((e_lane >> 3).astype(F32) == grp)
    el = jnp.where(in_grp, logits, ninf)
    v1, i1 = first_max(el)
    v2, i2 = first_max(jnp.where(lane_f == i1, ninf, el))
    e2 = jnp.exp(v2 - v1)
    den = 1.0 + e2
    w1 = (1.0 / den) * gw
    w2 = (e2 / den) * gw
    eid1 = (i1 - N_GROUPS).astype(I32)
    eid2 = (i2 - N_GROUPS).astype(I32)
    eid_ref[...] = jnp.where(lane == 0, eid1, jnp.where(lane == 1, eid2, 0))
    wt_ref[...] = jnp.where(lane == 0, w1, jnp.where(lane == 1, w2, 0.0))


def _router(x2, g, w, b):
    t = x2.shape[0]
    tm = min(TM_ROUTE, t)
    tok = lambda n: pl.BlockSpec((tm, n), lambda i: (i, 0))
    return pl.pallas_call(
        _router_kernel,
        grid=(t // tm,),
        in_specs=[tok(D_MODEL), _full((1, D_MODEL)), _full((D_MODEL, LANES)), _full((1, LANES))],
        out_specs=[tok(D_MODEL), tok(LANES), tok(LANES)],
        out_shape=[jax.ShapeDtypeStruct((t, D_MODEL), F32), jax.ShapeDtypeStruct((t, LANES), I32),
                   jax.ShapeDtypeStruct((t, LANES), F32)],
        compiler_params=_params(),
        name="router",
    )(x2, g, w, b)


def _rank_kernel(eid_ref, rank_ref, cnt_ref, *, tm):
    @pl.when(pl.program_id(0) == 0)
    def _():
        cnt_ref[...] = jnp.zeros_like(cnt_ref)

    eid = eid_ref[...]
    lane = lax.broadcasted_iota(I32, (tm, LANES), 1)
    o1 = lane == eid[:, 0:1]
    o2 = lane == eid[:, 1:2]
    both = jnp.where(o1 | o2, 1.0, 0.0)
    lower = lax.broadcasted_iota(I32, (tm, tm), 1) < lax.broadcasted_iota(I32, (tm, tm), 0)
    before = jnp.dot(jnp.where(lower, 1.0, 0.0).astype(BF16), both.astype(BF16),
                     preferred_element_type=F32) + cnt_ref[...]
    r1 = jnp.sum(jnp.where(o1, before, 0.0), axis=1, keepdims=True).astype(I32)
    r2 = jnp.sum(jnp.where(o2, before, 0.0), axis=1, keepdims=True).astype(I32)
    rank_ref[...] = jnp.where(lane == 0, r1, jnp.where(lane == 1, r2, 0))
    cnt_ref[...] = cnt_ref[...] + jnp.sum(both, axis=0, keepdims=True)


def _rank(eid):
    t = eid.shape[0]
    tm = min(TM_RANK, t)
    return pl.pallas_call(
        functools.partial(_rank_kernel, tm=tm),
        grid=(t // tm,),
        in_specs=[pl.BlockSpec((tm, LANES), lambda i: (i, 0))],
        out_specs=[pl.BlockSpec((tm, LANES), lambda i: (i, 0)), _full((1, LANES))],
        out_shape=[jax.ShapeDtypeStruct((t, LANES), I32), jax.ShapeDtypeStruct((1, LANES), F32)],
        compiler_params=_params(),
        name="rank",
    )(eid)


def _dest_kernel(eid_ref, rank_ref, start_ref, d1_ref, d2_ref):
    eid = eid_ref[...]
    lane = lax.broadcasted_iota(I32, eid.shape, 1)
    start = start_ref[...]
    s1 = jnp.sum(jnp.where(lane == eid[:, 0:1], start, 0.0), axis=1, keepdims=True).astype(I32)
    s2 = jnp.sum(jnp.where(lane == eid[:, 1:2], start, 0.0), axis=1, keepdims=True).astype(I32)
    dest = jnp.where(lane == 0, s1, jnp.where(lane == 1, s2, 0)) + rank_ref[...]
    dest_t = jnp.transpose(dest.astype(F32))[:SUBLANES, :].astype(I32)
    d1_ref[...] = dest_t[0:1, :]
    d2_ref[...] = dest_t[1:2, :]


def _dest(eid, rank, start_row):
    t = eid.shape[0]
    tm = min(TM_ROUTE, t)
    tok = pl.BlockSpec((tm, LANES), lambda i: (i, 0))
    return pl.pallas_call(
        _dest_kernel,
        grid=(t // tm,),
        in_specs=[tok, tok, _full((1, LANES))],
        out_specs=[pl.BlockSpec((1, tm), lambda i: (0, i))] * 2,
        out_shape=[jax.ShapeDtypeStruct((1, t), I32)] * 2,
        compiler_params=_params(),
        name="dest",
    )(eid, rank, start_row)


def _row_copy(src_ref, src_row, dst_ref, dst_row, sem):
    return pltpu.make_async_copy(src_ref.at[pl.ds(src_row, 1)], dst_ref.at[pl.ds(dst_row, 1)], sem)


def _rows_copy(src_ref, dst_ref, n, sem):
    return pltpu.make_async_copy(src_ref.at[pl.ds(0, n)], dst_ref.at[pl.ds(0, n)], sem)


def _scatter_kernel(d1_ref, d2_ref, h_ref, xs_in_ref, xs_ref, sem, *, tm):
    del xs_in_ref

    def issue(t, _):
        for d_ref in (d1_ref, d2_ref):
            _row_copy(h_ref, t, xs_ref, d_ref[0, t], sem).start()
        return 0

    lax.fori_loop(0, tm, issue, 0)
    for _ in range(2):
        _rows_copy(h_ref, xs_ref, tm, sem).wait()


def _scatter_rows(dests, h, n_rows):
    t = h.shape[0]
    tm = min(TM_MOVE, t)
    xs0 = jnp.zeros((n_rows, D_MODEL), F32)
    return pl.pallas_call(
        functools.partial(_scatter_kernel, tm=tm),
        grid=(t // tm,),
        in_specs=[pl.BlockSpec((1, tm), lambda i: (0, i), memory_space=pltpu.SMEM)] * 2
                 + [pl.BlockSpec((tm, D_MODEL), lambda i: (i, 0)), pl.BlockSpec(memory_space=pl.ANY)],
        out_specs=pl.BlockSpec(memory_space=pl.ANY),
        out_shape=jax.ShapeDtypeStruct((n_rows, D_MODEL), F32),
        scratch_shapes=[pltpu.SemaphoreType.DMA(())],
        input_output_aliases={3: 0},
        compiler_params=_params(),
        name="scatter_rows",
    )(*dests, h, xs0)


def _expert_kernel(be_ref, nused_ref, xs_ref, wg_ref, wu_ref, wd_ref, y_ref, wg_bf, wu_bf, wd_bf):
    b = pl.program_id(0)

    @pl.when((b == 0) | (be_ref[b] != be_ref[jnp.maximum(b - 1, 0)]))
    def _():
        wg_bf[...] = wg_ref[...].astype(BF16)
        wu_bf[...] = wu_ref[...].astype(BF16)
        wd_bf[...] = wd_ref[...].astype(BF16)

    @pl.when(b < nused_ref[0])
    def _():
        xb = xs_ref[...].astype(BF16)
        gate = jnp.dot(xb, wg_bf[...], preferred_element_type=F32)
        up = jnp.dot(xb, wu_bf[...], preferred_element_type=F32)
        hid = (jax.nn.silu(gate) * up).astype(BF16)
        y_ref[...] = jnp.dot(hid, wd_bf[...], preferred_element_type=F32)

    @pl.when(b >= nused_ref[0])
    def _():
        y_ref[...] = jnp.zeros_like(y_ref)


def _experts(block_exp, n_used, xs, wg, wu, wd, layer):
    n_rows = xs.shape[0]
    w_in_spec = pl.BlockSpec((None, None, D_MODEL, D_FF_EXPERT), lambda b, be, nu: (layer, be[b], 0, 0))
    grid_spec = pltpu.PrefetchScalarGridSpec(
        num_scalar_prefetch=2,
        grid=(n_rows // EXP_BLOCK,),
        in_specs=[pl.BlockSpec((EXP_BLOCK, D_MODEL), lambda b, be, nu: (b, 0)), w_in_spec, w_in_spec,
                  pl.BlockSpec((None, None, D_FF_EXPERT, D_MODEL), lambda b, be, nu: (layer, be[b], 0, 0))],
        out_specs=pl.BlockSpec((EXP_BLOCK, D_MODEL), lambda b, be, nu: (b, 0)),
        scratch_shapes=[pltpu.VMEM((D_MODEL, D_FF_EXPERT), BF16), pltpu.VMEM((D_MODEL, D_FF_EXPERT), BF16),
                        pltpu.VMEM((D_FF_EXPERT, D_MODEL), BF16)],
    )
    return pl.pallas_call(
        _expert_kernel,
        grid_spec=grid_spec,
        out_shape=jax.ShapeDtypeStruct((n_rows, D_MODEL), F32),
        compiler_params=_params(),
        name="experts",
    )(block_exp, n_used, xs, wg, wu, wd)


def _combine_kernel(d1_ref, d2_ref, d1_next_ref, d2_next_ref, x_ref, wt_ref, g_ref, y_ref, o_ref, buf_ref, sems,
                    *, tm, steps, final_norm):
    i = pl.program_id(0)

    def gather(idx_refs, slot):
        def issue(t, _):
            for j, idx_ref in enumerate(idx_refs):
                _row_copy(y_ref, idx_ref[0, t], buf_ref.at[slot], j * tm + t, sems.at[slot]).start()
            return 0

        lax.fori_loop(0, tm, issue, 0)

    @pl.when(i == 0)
    def _():
        gather((d1_ref, d2_ref), 0)

    @pl.when(i + 1 < steps)
    def _():
        gather((d1_next_ref, d2_next_ref), (i + 1) % 2)

    slot = i % 2
    _rows_copy(y_ref, buf_ref.at[slot], 2 * tm, sems.at[slot]).wait()
    wt = wt_ref[...]
    rows = buf_ref[slot]
    out = x_ref[...] + (wt[:, 0:1] * rows[:tm, :] + wt[:, 1:2] * rows[tm:, :])
    if final_norm:
        out = _rms(out, g_ref[...])
    o_ref[...] = out


def _combine(dests, x2, wt, g_final, y, final_norm):
    t = x2.shape[0]
    tm = min(TM_MOVE, t)
    steps = t // tm
    tok = lambda n: pl.BlockSpec((tm, n), lambda i: (i, 0))
    idx = lambda f: pl.BlockSpec((1, tm), f, memory_space=pltpu.SMEM)
    return pl.pallas_call(
        functools.partial(_combine_kernel, tm=tm, steps=steps, final_norm=final_norm),
        grid=(steps,),
        in_specs=[idx(lambda i: (0, i))] * 2 + [idx(lambda i: (0, jnp.minimum(i + 1, steps - 1)))] * 2
                 + [tok(D_MODEL), tok(LANES), _full((1, D_MODEL)), pl.BlockSpec(memory_space=pl.ANY)],
        out_specs=tok(D_MODEL),
        out_shape=jax.ShapeDtypeStruct((t, D_MODEL), F32),
        scratch_shapes=[pltpu.VMEM((2, 2 * tm, D_MODEL), F32), pltpu.SemaphoreType.DMA((2,))],
        compiler_params=_params(),
        name="combine",
    )(*dests, *dests, x2, wt, g_final, y)


W_IN_SIZES = (A_WIDTH, A_WIDTH, B_WIDTH, B_WIDTH, B_WIDTH, IDX_HEADS * IDX_DIM, IDX_DIM, IDX_HEADS,
              C_WIDTH, C_WIDTH, C_WIDTH, N_BRANCH * D_MODEL)
W_IN_CUTS = tuple(int(c) for c in np.cumsum((0,) + W_IN_SIZES))
W_BT_ROWS = 2 * B_WIDTH + IDX_HEADS * IDX_DIM + WI_ROWS


def _repack_kernel(w_ref, wa_ref, wb_ref, wbt_ref, wc_ref, wg_ref):
    x = w_ref[...]
    u, va, q, k, vb, qi, ki, wi, cb, cc, cx, gates = [
        x[:, a:b] for a, b in zip(W_IN_CUTS[:-1], W_IN_CUTS[1:])]
    for j, part in enumerate((u, va)):
        wa_ref[:, j * A_WIDTH:(j + 1) * A_WIDTH] = part.astype(BF16)
    wb_ref[:, :B_WIDTH] = k.astype(BF16)
    ki_pad = jnp.concatenate([ki, jnp.zeros((ki.shape[0], LANES - IDX_DIM), F32)], axis=1)
    wb_ref[:, B_WIDTH:] = ki_pad.astype(BF16)
    row = 0
    for part in (q, vb, qi):
        for j in range(part.shape[1] // LANES):
            wbt_ref[row:row + LANES, :] = jnp.transpose(part[:, j * LANES:(j + 1) * LANES]).astype(BF16)
            row += LANES
    lane = lax.broadcasted_iota(I32, (x.shape[0], LANES), 1)
    wi_wide = jnp.where(lane < IDX_HEADS, x[:, W_IN_CUTS[7]:W_IN_CUTS[7] + LANES], 0.0)
    wbt_ref[row:row + WI_ROWS, :] = jnp.transpose(wi_wide)[:WI_ROWS, :].astype(BF16)
    for j, part in enumerate((cb, cc, cx)):
        wc_ref[:, j * C_WIDTH:(j + 1) * C_WIDTH] = part.astype(BF16)
    wg_ref[...] = gates.astype(BF16)


def _repack_w_in(w_in):
    depth, d, d_in = w_in.shape
    rows = lambda n: pl.BlockSpec((None, LANES, n), lambda l, i: (l, i, 0))
    outs = [2 * A_WIDTH, B_WIDTH + LANES, None, 3 * C_WIDTH, N_BRANCH * D_MODEL]
    shapes = [jax.ShapeDtypeStruct((depth, W_BT_ROWS, d) if n is None else (depth, d, n), BF16) for n in outs]
    specs = [pl.BlockSpec((None, W_BT_ROWS, LANES), lambda l, i: (l, 0, i)) if n is None else rows(n)
             for n in outs]
    return pl.pallas_call(
        _repack_kernel,
        grid=(depth, d // LANES),
        in_specs=[rows(d_in)],
        out_specs=specs,
        out_shape=shapes,
        compiler_params=_params(2),
        name="repack_w_in",
    )(w_in)


def _token_mixers(x2, g, w_parts, ws, bs, conv_w, wpa, wpb, wpc, w_out, tabs, bsz, seq):
    w_a, w_b, w_bt, w_c, w_g = w_parts
    n_sel = min(TOPK_MAX, seq // 4)
    k, ki, qt, vt_aug, qit, wit = _proj_b(x2, g, w_b, w_bt, *tabs)
    ybt = _dsa(qit, wit, ki, qt, k, vt_aug, bsz, seq, n_sel)
    ya = _proj_a(x2, g, w_a, ws.astype(BF16), bs.T)
    yc = _proj_c(x2, g, w_c, conv_w, seq)
    bf = lambda w: w.astype(BF16)
    return _merge(x2, g, w_g, ya, ybt, yc, bf(wpa), bf(wpb), bf(wpc), bf(w_out))


def _hier_moe(x2, g, rg_w, rg_b, re_w, re_b, w_gate, w_up, w_down, g_final, layer):
    t = x2.shape[0]
    pad_c = LANES - N_GROUPS - N_EXPERTS
    w_r = jnp.pad(jnp.concatenate([rg_w, re_w], axis=1), ((0, 0), (0, pad_c)))
    b_r = jnp.pad(jnp.concatenate([rg_b, re_b]), (0, pad_c))[None, :]
    h, eid, wt = _router(x2, g, w_r, b_r)
    rank, counts = _rank(eid)
    cnt = counts[0, :N_EXPERTS].astype(I32)
    padded = ((cnt + EXP_BLOCK - 1) // EXP_BLOCK) * EXP_BLOCK
    pend = jnp.cumsum(padded)
    pstart = pend - padded
    n_blocks = (2 * t) // EXP_BLOCK + N_EXPERTS
    n_rows = n_blocks * EXP_BLOCK
    block_row = jnp.arange(n_blocks, dtype=I32) * EXP_BLOCK
    block_exp = jnp.minimum(jnp.sum((pend[None, :] <= block_row[:, None]).astype(I32), axis=1),
                            N_EXPERTS - 1)
    n_used = (pend[-1:] // EXP_BLOCK).astype(I32)
    start_row = jnp.pad(pstart.astype(F32), (0, LANES - N_EXPERTS))[None, :]
    dests = _dest(eid, rank, start_row)
    xs = _scatter_rows(dests, h, n_rows)
    y = _experts(block_exp, n_used, xs, w_gate, w_up, w_down, layer)
    return _combine(dests, x2, wt, g_final, y, layer == w_gate.shape[0] - 1)


def kernel(x, positions, norm_mix_g, norm_ffn_g, norm_final_g, w_in, gmlp_ws, gmlp_b, conv_w, w_proj_a,
           w_proj_b, w_proj_c, w_out, router_group_w, router_group_b, router_expert_w, router_expert_b,
           expert_w_gate, expert_w_up, expert_w_down):
    bsz, seq, d = x.shape
    depth = w_in.shape[0]
    x2 = x.reshape(bsz * seq, d)
    pos = positions.astype(F32).reshape(bsz * seq)
    tabs = (_rope_tables(pos[:, None], B_HEAD_DIM, B_ROT), _rope_tables(pos[:, None], IDX_DIM, IDX_ROT),
            _rope_tables_t(pos[None, :], B_ROT), _rope_tables_t(pos[None, :], IDX_ROT))
    g_final = norm_final_g[None, :]
    w_parts = _repack_w_in(w_in)
    for l in range(depth):
        x2 = _token_mixers(x2, norm_mix_g[l][None, :], [w[l] for w in w_parts], gmlp_ws[l], gmlp_b[l], conv_w[l],
                           w_proj_a[l], w_proj_b[l], w_proj_c[l], w_out[l], tabs, bsz, seq)
        x2 = _hier_moe(x2, norm_ffn_g[l][None, :], router_group_w[l], router_group_b[l],
                       router_expert_w[l], router_expert_b[l], expert_w_gate, expert_w_up,
                       expert_w_down, g_final, l)
    return x2.reshape(bsz, seq, d)
```

```python
import functools

import numpy as np
import jax
import jax.numpy as jnp
from jax import lax
from jax.experimental import pallas as pl
from jax.experimental.pallas import tpu as pltpu

D_MODEL = 1024
CHUNK = 128
A_GROUPS = 4
A_GROUP_DIM = 128
A_WIDTH = A_GROUPS * A_GROUP_DIM
B_HEADS = 4
B_HEAD_DIM = 128
B_WIDTH = B_HEADS * B_HEAD_DIM
B_ROT = B_HEAD_DIM // 4
IDX_HEADS = 4
IDX_DIM = 64
IDX_ROT = IDX_DIM // 4
TOPK_MAX = 256
C_WIDTH = 512
CONV_WIDTH = 3
N_BRANCH = 3
N_GROUPS = 4
EXP_PER_GROUP = 8
N_EXPERTS = N_GROUPS * EXP_PER_GROUP
D_FF_EXPERT = 512
ROPE_THETA = 500000.0
EPS = 1e-6
NEG = -1e30

LANES = 128
SUBLANES = 8
VMEM_LIMIT = 56 * 1024 * 1024

TM_PROJ = 512
TM_MERGE = 256
TM_ROUTE = 512
TM_RANK = 256
TM_MOVE = 256
ISSUE_UNROLL = 4
EXP_BLOCK = 512
DSA_TQ = 128
DSA_KC = 256
DSA_SKC = 512
DSA_SUB = 256
DSA_RB = 512
V_ROWS = B_HEAD_DIM + 16
LOG2E = 1.4426950408889634

F32 = jnp.float32
BF16 = jnp.bfloat16
I32 = jnp.int32
INT_MIN = -2 ** 31


def _monotone_key_of(value):
    bits = int(np.array(value, np.float32).view(np.int32))
    return bits ^ ((bits >> 31) & 0x7FFFFFFF)


NEG_KEY = _monotone_key_of(NEG)


def _params(n_axes=1, semantics=None):
    return pltpu.CompilerParams(
        dimension_semantics=semantics or ("arbitrary",) * n_axes,
        vmem_limit_bytes=VMEM_LIMIT)


def _rms(x, g):
    ms = jnp.mean(x * x, axis=-1, keepdims=True)
    return x * lax.rsqrt(ms + EPS) * g


def _full(shape):
    nd = len(shape)
    return pl.BlockSpec(shape, lambda *_: (0,) * nd)


def _rope_table_kernel(pos_ref, invf_ref, mrot_ref, ma_ref, mb_ref, c_ref, sa_ref, sb_ref):
    ang = pos_ref[...] * invf_ref[...]
    c = jnp.cos(ang)
    s = jnp.sin(ang)
    c_ref[...] = jnp.where(mrot_ref[...] > 0, c, 1.0)
    sa_ref[...] = jnp.where(ma_ref[...] > 0, -s, 0.0)
    sb_ref[...] = jnp.where(mb_ref[...] > 0, s, 0.0)


def _rope_tables(pos_col, head_dim, rot):
    t = pos_col.shape[0]
    half = rot // 2
    inv_freq = jnp.float32(ROPE_THETA) ** (-jnp.arange(half, dtype=F32) * 2.0 / rot)
    lane = np.arange(LANES) % head_dim
    mrot = lane < rot
    ma = lane < half
    mb = (lane >= half) & (lane < rot)
    invf = jnp.where(jnp.asarray(mrot), inv_freq[np.where(mrot, lane % half, 0)], 0.0)[None, :]
    row = lambda m: jnp.asarray(m.astype(np.float32))[None, :]
    tm = min(1024, t)
    spec_row = _full((1, LANES))
    out_spec = pl.BlockSpec((tm, LANES), lambda i: (i, 0))
    return pl.pallas_call(
        _rope_table_kernel,
        grid=(t // tm,),
        in_specs=[pl.BlockSpec((tm, 1), lambda i: (i, 0)), spec_row, spec_row, spec_row, spec_row],
        out_specs=[out_spec] * 3,
        out_shape=[jax.ShapeDtypeStruct((t, LANES), F32)] * 3,
        compiler_params=_params(),
        name="rope_tables",
    )(pos_col, invf, row(mrot), row(ma), row(mb))


def _rope_table_t_kernel(pos_ref, invf_ref, c_ref, s_ref):
    ang = invf_ref[...] * pos_ref[...]
    c_ref[...] = jnp.cos(ang)
    s_ref[...] = jnp.sin(ang)


def _rope_tables_t(pos_row, rot):
    t = pos_row.shape[1]
    half = rot // 2
    inv_freq = jnp.float32(ROPE_THETA) ** (-jnp.arange(half, dtype=F32) * 2.0 / rot)
    tm = min(2048, t)
    out_spec = pl.BlockSpec((half, tm), lambda i: (0, i))
    return pl.pallas_call(
        _rope_table_t_kernel,
        grid=(t // tm,),
        in_specs=[pl.BlockSpec((1, tm), lambda i: (0, i)), _full((half, 1))],
        out_specs=[out_spec] * 2,
        out_shape=[jax.ShapeDtypeStruct((half, t), F32)] * 2,
        compiler_params=_params(),
        name="rope_tables_t",
    )(pos_row, inv_freq[:, None])


def _rope_t(x, c, s):
    half = c.shape[0]
    x1, x2 = x[:half, :], x[half:2 * half, :]
    return jnp.concatenate([x1 * c - x2 * s, x1 * s + x2 * c, x[2 * half:, :]], axis=0)


def _rope(x, c, sa, sb, half):
    parts = []
    for j in range(x.shape[1] // LANES):
        xj = x[:, j * LANES:(j + 1) * LANES]
        parts.append(xj * c + pltpu.roll(xj, LANES - half, 1) * sa + pltpu.roll(xj, half, 1) * sb)
    return parts


WI_ROWS = 16


def _proj_b_kernel(x_ref, g_ref, w_ref, wt_ref, cb_ref, sab_ref, sbb_ref, ci_ref, sai_ref, sbi_ref,
                   cbt_ref, sbt_ref, cit_ref, sit_ref, k_ref, ki_ref, qt_ref, vt_ref, qit_ref, wit_ref):
    h = _rms(x_ref[...], g_ref[...]).astype(BF16)
    tm = h.shape[0]
    p = jnp.dot(h, w_ref[...], preferred_element_type=F32)
    pt = lax.dot_general(wt_ref[...], h, (((1,), (1,)), ((), ())), preferred_element_type=F32)
    tb = (cb_ref[...], sab_ref[...], sbb_ref[...])
    ti = (ci_ref[...], sai_ref[...], sbi_ref[...])
    for j, kj in enumerate(_rope(p[:, :B_WIDTH], *tb, B_ROT // 2)):
        k_ref[:, j * LANES:(j + 1) * LANES] = kj.astype(BF16)
    ki_ref[...] = _rope(p[:, B_WIDTH:B_WIDTH + LANES], *ti, IDX_ROT // 2)[0].astype(BF16)
    cbt, sbt = cbt_ref[...], sbt_ref[...]
    for hh in range(B_HEADS):
        hs = slice(hh * B_HEAD_DIM, (hh + 1) * B_HEAD_DIM)
        qt_ref[hs, :] = (_rope_t(pt[hs, :], cbt, sbt) * (B_HEAD_DIM ** -0.5 * LOG2E)).astype(BF16)
    o = B_WIDTH
    for hh in range(B_HEADS):
        vt_ref[hh * V_ROWS:hh * V_ROWS + B_HEAD_DIM, :] = (
            pt[o + hh * B_HEAD_DIM:o + (hh + 1) * B_HEAD_DIM, :].astype(BF16))
        vt_ref[hh * V_ROWS + B_HEAD_DIM:(hh + 1) * V_ROWS, :] = jnp.ones((V_ROWS - B_HEAD_DIM, tm), BF16)
    o += B_WIDTH
    cit, sit = cit_ref[...], sit_ref[...]
    for hh in range(IDX_HEADS):
        hs = slice(hh * IDX_DIM, (hh + 1) * IDX_DIM)
        qit_ref[hs, :] = _rope_t(pt[o + hh * IDX_DIM:o + (hh + 1) * IDX_DIM, :], cit, sit).astype(BF16)
    o += IDX_HEADS * IDX_DIM
    wit_ref[...] = pt[o:o + SUBLANES, :] * ((IDX_HEADS ** -0.5) * (IDX_DIM ** -0.5))


def _proj_b(x2, g, w, wt, tabs_b, tabs_i, tabs_bt, tabs_it):
    t = x2.shape[0]
    tm = min(TM_PROJ, t)
    tok = lambda n: pl.BlockSpec((tm, n), lambda i: (i, 0))
    feat = lambda n: pl.BlockSpec((n, tm), lambda i: (0, i))
    outs_tok = [(B_WIDTH, BF16), (LANES, BF16)]
    outs_feat = [(B_WIDTH, BF16), (B_HEADS * V_ROWS, BF16), (IDX_HEADS * IDX_DIM, BF16), (SUBLANES, F32)]
    return pl.pallas_call(
        _proj_b_kernel,
        grid=(t // tm,),
        in_specs=[tok(D_MODEL), _full((1, D_MODEL)), _full(w.shape), _full(wt.shape)] + [tok(LANES)] * 6
                 + [feat(B_ROT // 2)] * 2 + [feat(IDX_ROT // 2)] * 2,
        out_specs=[tok(n) for n, _ in outs_tok] + [feat(n) for n, _ in outs_feat],
        out_shape=[jax.ShapeDtypeStruct((t, n), d) for n, d in outs_tok]
                  + [jax.ShapeDtypeStruct((n, t), d) for n, d in outs_feat],
        compiler_params=_params(),
        name="proj_b",
    )(x2, g, w, wt, *tabs_b, *tabs_i, *tabs_bt, *tabs_it)


def _ordered_key(score):
    bits = lax.bitcast_convert_type(score, I32)
    key = bits ^ ((bits >> 31) & 0x7FFFFFFF)
    return jnp.where(key == -1, 0, key)


def _score_of_key(key):
    return lax.bitcast_convert_type(key ^ ((key >> 31) & 0x7FFFFFFF), F32)


def _ce_desc(vals, i, j):
    a, b = vals[i], vals[j]
    vals[i] = jnp.maximum(a, b)
    vals[j] = jnp.minimum(a, b)


def _sort_desc(vals):
    n = len(vals)
    p = 1
    while p < n:
        k = p
        while k >= 1:
            for j in range(k % p, n - k, 2 * k):
                for i in range(min(k, n - j - k)):
                    if (i + j) // (2 * p) == (i + j + k) // (2 * p):
                        _ce_desc(vals, i + j, i + j + k)
            k //= 2
        p *= 2


def _bitonic_merge_desc(vals):
    n = len(vals)
    j = n // 2
    while j >= 1:
        for i in range(n):
            l = i ^ j
            if l > i:
                _ce_desc(vals, i, l)
        j //= 2


def _dsa_kernel(qit_ref, wit_ref, ki_ref, qt_ref, k_ref, vt_ref, o_ref, sc_ref, top_ref, topk_ref, acc_ref,
                bias0_ref, bias1_ref, lg0_ref, lg1_ref,
                *, seq, n_sel, tq, kc, skc, sub, rb):
    i = pl.program_id(1)
    q0 = i * tq
    n_full = q0 // skc
    n_rows = (n_full + 1) * skc
    n_pad = (seq - 1 - q0) - lax.broadcasted_iota(I32, (1, tq), 1)
    n_chunk = (q0 + tq + kc - 1) // kc
    wit = wit_ref[...]

    n_top = skc // SUBLANES
    top_ref[...] = jnp.full((skc, tq), -jnp.inf, F32)

    def score_batch(c, masked):
        vals = []
        for s in range(skc // sub):
            r0 = pl.multiple_of(c * skc + s * sub, sub)
            kic = ki_ref[pl.ds(r0, sub), :IDX_DIM]
            sc = None
            for h in range(IDX_HEADS):
                a = jnp.dot(kic, qit_ref[h * IDX_DIM:(h + 1) * IDX_DIM, :], preferred_element_type=F32)
                term = jnp.maximum(a, 0.0) * wit[h:h + 1, :]
                sc = term if sc is None else sc + term
            if masked:
                qpos = q0 + lax.broadcasted_iota(I32, (sub, tq), 1)
                krow = r0 + lax.broadcasted_iota(I32, (sub, tq), 0)
                sc = jnp.where(krow <= qpos, sc, -jnp.inf)
            sc_ref[pl.ds(r0, sub), :] = sc
            vals += [sc[SUBLANES * i:SUBLANES * (i + 1), :] for i in range(sub // SUBLANES)]
        _sort_desc(vals)
        return vals

    def keep_top(batches):
        top = [top_ref[SUBLANES * i:SUBLANES * (i + 1), :] for i in range(n_top)]
        for vals in batches:
            top = [jnp.maximum(top[i], vals[n_top - 1 - i]) for i in range(n_top)]
            _bitonic_merge_desc(top)
        for i in range(n_top):
            top_ref[SUBLANES * i:SUBLANES * (i + 1), :] = top[i]

    def score_pair(j, _):
        keep_top([score_batch(2 * j, False), score_batch(2 * j + 1, False)])
        return 0

    lax.fori_loop(0, n_full // 2, score_pair, 0)

    @pl.when(n_full % 2 == 1)
    def _():
        keep_top([score_batch(n_full - 1, False)])

    keep_top([score_batch(n_full, True)])

    def search(count):
        def bit_body(b, carry):
            ans_u, = carry
            cand_u = ans_u | jnp.left_shift(jnp.int32(1), 31 - b)
            ok = count(cand_u ^ INT_MIN, False) >= n_sel
            return (jnp.where(ok, cand_u, ans_u),)

        ans_u, = lax.fori_loop(0, 32, bit_body, (jnp.zeros((1, tq), I32),))
        thr = ans_u ^ INT_MIN
        return thr, (n_sel - count(thr, True)).astype(F32)

    def pads(cand, strict):
        return jnp.where((NEG_KEY > cand) if strict else (NEG_KEY >= cand), n_pad, 0)

    topk_ref[...] = _ordered_key(top_ref[...])

    def count_top(cand, strict):
        t = topk_ref[...]
        hit = (t > cand) if strict else (t >= cand)
        return jnp.sum(jnp.where(hit, 1.0, 0.0), axis=0, keepdims=True).astype(I32) + pads(cand, strict)

    def search_top():
        def digit_body(b, carry):
            ans_u, = carry
            shift = 30 - 2 * b
            for d in (1, 2, 3):
                cand_u = carry[0] | jnp.left_shift(jnp.int32(d), shift)
                ans_u = jnp.where(count_top(cand_u ^ INT_MIN, False) >= n_sel, cand_u, ans_u)
            return (ans_u,)

        ans_u, = lax.fori_loop(0, 16, digit_body, (jnp.zeros((1, tq), I32),))
        thr = ans_u ^ INT_MIN
        return thr, (n_sel - count_top(thr, True)).astype(F32)

    def count_all(cand, strict):
        acc_rows = 4 * SUBLANES
        cand_f = _score_of_key(cand)

        def body(r, acc):
            r0 = pl.multiple_of(r * rb, rb)
            blk = sc_ref[pl.ds(r0, rb), :]
            for j in range(rb // acc_rows):
                part = blk[j * acc_rows:(j + 1) * acc_rows, :]
                hit = (part > cand_f) if strict else (part >= cand_f)
                acc = acc + jnp.where(hit, 1, 0)
            return acc

        acc = lax.fori_loop(0, n_rows // rb, body, jnp.zeros((acc_rows, tq), I32))
        return jnp.sum(acc.astype(F32), axis=0, keepdims=True).astype(I32) + pads(cand, strict)

    thr, need = search_top()
    last = topk_ref[skc - SUBLANES:skc, :]
    overflow = jnp.max(jnp.where(last > thr, 1.0, 0.0))
    thr, need = lax.cond(overflow > 0.0, lambda: search(count_all), lambda: (thr, need))
    thr_f = _score_of_key(thr)

    acc_ref[...] = jnp.zeros_like(acc_ref)
    bias_refs = (bias0_ref, bias1_ref)
    lg_refs = (lg0_ref, lg1_ref)
    hk = kc // 2
    lower = (lax.broadcasted_iota(I32, (hk, hk), 1) < lax.broadcasted_iota(I32, (hk, hk), 0))
    lower = jnp.where(lower, 1.0, 0.0).astype(BF16)

    def produce(c, slot, tie_base):
        r0 = pl.multiple_of(c * kc, kc)
        sc = sc_ref[pl.ds(r0, kc), :]
        tie = sc == thr_f
        tie_f = jnp.where(tie, 1.0, 0.0)
        tie_b = tie_f.astype(BF16)
        b0 = jnp.dot(lower, tie_b[:hk, :], preferred_element_type=F32)
        n0 = b0[hk - 1:hk, :] + tie_f[hk - 1:hk, :]
        b1 = jnp.dot(lower, tie_b[hk:, :], preferred_element_type=F32) + n0
        before = jnp.concatenate([b0, b1], axis=0)
        sel = (sc > thr_f) | (tie & (before < need - tie_base))
        bias_refs[slot][...] = jnp.where(sel, 0.0, NEG)
        for h in range(B_HEADS):
            hs = slice(h * B_HEAD_DIM, (h + 1) * B_HEAD_DIM)
            lg_refs[slot][h] = jnp.dot(k_ref[pl.ds(r0, kc), hs], qt_ref[hs, :], preferred_element_type=F32)
        return tie_base + b1[hk - 1:hk, :] + tie_f[kc - 1:kc, :]

    def consume(c, slot, ms):
        r0 = pl.multiple_of(c * kc, kc)
        bias = bias_refs[slot][...]
        new_ms = []
        for h in range(B_HEADS):
            vs = slice(h * V_ROWS, (h + 1) * V_ROWS)
            lg = lg_refs[slot][h] + bias
            m_new = jnp.maximum(ms[h], jnp.max(lg, axis=0, keepdims=True))
            alpha = jnp.exp2(ms[h] - m_new)
            p = jnp.exp2(lg - m_new).astype(BF16)
            pv = jnp.dot(vt_ref[vs, pl.ds(r0, kc)], p, preferred_element_type=F32)
            acc_ref[vs, :] = acc_ref[vs, :] * alpha + pv
            new_ms.append(m_new)
        return tuple(new_ms)

    def pair_body(j, carry):
        ms, tie_base = carry
        c = 2 * j
        tie_base = produce(c + 1, 1, tie_base)
        ms = consume(c, 0, ms)
        tie_base = produce(c + 2, 0, tie_base)
        ms = consume(c + 1, 1, ms)
        return ms, tie_base

    row = lambda v: jnp.full((1, tq), v, F32)
    ms0 = tuple(row(NEG) for _ in range(B_HEADS))
    n_pairs = (n_chunk - 1) // 2
    ms, tie_base = lax.fori_loop(0, n_pairs, pair_body, (ms0, produce(0, 0, row(0.0))))
    c0 = 2 * n_pairs

    @pl.when(c0 == n_chunk - 1)
    def _():
        consume(c0, 0, ms)

    @pl.when(c0 != n_chunk - 1)
    def _():
        produce(c0 + 1, 1, tie_base)
        consume(c0 + 1, 1, consume(c0, 0, ms))

    for h in range(B_HEADS):
        a = acc_ref[h * V_ROWS:(h + 1) * V_ROWS, :]
        o_ref[h * B_HEAD_DIM:(h + 1) * B_HEAD_DIM, :] = (
            a[:B_HEAD_DIM, :] / a[B_HEAD_DIM:B_HEAD_DIM + 1, :]).astype(o_ref.dtype)


def _dsa(qit, wit, ki, qt, k, vt_aug, bsz, seq, n_sel):
    tq = min(DSA_TQ, seq)
    kc = min(DSA_KC, seq)
    skc = min(DSA_SKC, seq)
    nq = seq // tq
    once = pl.Buffered(1)
    kern = functools.partial(_dsa_kernel, seq=seq, n_sel=n_sel, tq=tq, kc=kc, skc=skc,
                             sub=min(DSA_SUB, skc), rb=min(DSA_RB, skc))
    q_tile = lambda rows: pl.BlockSpec((rows, tq), lambda b, i: (0, b * nq + i))
    return pl.pallas_call(
        kern,
        grid=(bsz, nq),
        in_specs=[
            q_tile(IDX_HEADS * IDX_DIM),
            q_tile(SUBLANES),
            pl.BlockSpec((seq, LANES), lambda b, i: (b, 0), pipeline_mode=once),
            q_tile(B_WIDTH),
            pl.BlockSpec((seq, B_WIDTH), lambda b, i: (b, 0), pipeline_mode=once),
            pl.BlockSpec((B_HEADS * V_ROWS, seq), lambda b, i: (0, b), pipeline_mode=once),
        ],
        out_specs=q_tile(B_WIDTH),
        out_shape=jax.ShapeDtypeStruct((B_WIDTH, bsz * seq), BF16),
        scratch_shapes=[pltpu.VMEM((seq, tq), F32), pltpu.VMEM((skc, tq), F32), pltpu.VMEM((skc, tq), I32),
                        pltpu.VMEM((B_HEADS * V_ROWS, tq), F32), pltpu.VMEM((kc, tq), F32),
                        pltpu.VMEM((kc, tq), F32), pltpu.VMEM((B_HEADS, kc, tq), F32),
                        pltpu.VMEM((B_HEADS, kc, tq), F32)],
        compiler_params=_params(2),
        name="dsa",
    )(qit, wit, ki, qt, k, vt_aug)


def _proj_a_kernel(x_ref, g_ref, w_ref, ws_ref, bs_ref, ya_ref, *, tm):
    h = _rms(x_ref[...], g_ref[...]).astype(BF16)
    p = jnp.dot(h, w_ref[...], preferred_element_type=F32)
    u = jax.nn.gelu(p[:, :A_WIDTH])
    v = jax.nn.gelu(p[:, A_WIDTH:])
    mu = jnp.mean(v, axis=-1, keepdims=True)
    var = jnp.mean(jnp.square(v - mu), axis=-1, keepdims=True)
    v = ((v - mu) * lax.rsqrt(var + EPS)).astype(BF16)
    causal = (lax.broadcasted_iota(I32, (CHUNK, CHUNK), 1) <= lax.broadcasted_iota(I32, (CHUNK, CHUNK), 0))
    bs = bs_ref[...]
    for gidx in range(A_GROUPS):
        wsm = jnp.where(causal, ws_ref[gidx], 0.0).astype(BF16)
        cs = slice(gidx * A_GROUP_DIM, (gidx + 1) * A_GROUP_DIM)
        for n in range(tm // CHUNK):
            rs = slice(n * CHUNK, (n + 1) * CHUNK)
            vm = jnp.dot(wsm, v[rs, cs], preferred_element_type=F32) + bs[:, gidx:gidx + 1]
            ya_ref[rs, cs] = (u[rs, cs] * vm).astype(ya_ref.dtype)


def _proj_a(x2, g, w, ws, bs_t):
    t = x2.shape[0]
    tm = min(TM_PROJ, t)
    return pl.pallas_call(
        functools.partial(_proj_a_kernel, tm=tm),
        grid=(t // tm,),
        in_specs=[pl.BlockSpec((tm, D_MODEL), lambda i: (i, 0)), _full((1, D_MODEL)),
                  _full((D_MODEL, 2 * A_WIDTH)), _full((A_GROUPS, CHUNK, CHUNK)), _full((CHUNK, A_GROUPS))],
        out_specs=pl.BlockSpec((tm, A_WIDTH), lambda i: (i, 0)),
        out_shape=jax.ShapeDtypeStruct((t, A_WIDTH), BF16),
        compiler_params=_params(),
        name="proj_a",
    )(x2, g, w, ws, bs_t)


def _proj_c_kernel(x_ref, xh_ref, g_ref, w_ref, cw_ref, yc_ref, *, tm, tiles_per_seq):
    g = g_ref[...]
    w = w_ref[...]
    p = jnp.dot(_rms(x_ref[...], g).astype(BF16), w, preferred_element_type=F32)
    ph = jnp.dot(_rms(xh_ref[...], g).astype(BF16), w[:, C_WIDTH:], preferred_element_type=F32)
    first = (pl.program_id(0) % tiles_per_seq) == 0
    zh = ph[:, :C_WIDTH] * ph[:, C_WIDTH:] * jnp.where(first, 0.0, 1.0)
    z = p[:, C_WIDTH:2 * C_WIDTH] * p[:, 2 * C_WIDTH:]
    row = lax.broadcasted_iota(I32, (tm, C_WIDTH), 0)
    z1 = jnp.where(row == 0, zh[SUBLANES - 1:SUBLANES, :], pltpu.roll(z, 1, 0))
    z2 = jnp.where(row == 0, zh[SUBLANES - 2:SUBLANES - 1, :],
                   jnp.where(row == 1, zh[SUBLANES - 1:SUBLANES, :], pltpu.roll(z, 2, 0)))
    cw = cw_ref[...]
    y = cw[0:1, :] * z2 + cw[1:2, :] * z1 + cw[2:3, :] * z
    yc_ref[...] = (p[:, :C_WIDTH] * y).astype(yc_ref.dtype)


def _proj_c(x2, g, w, cw, seq):
    t = x2.shape[0]
    tm = min(TM_PROJ, seq)
    per8 = tm // SUBLANES
    return pl.pallas_call(
        functools.partial(_proj_c_kernel, tm=tm, tiles_per_seq=seq // tm),
        grid=(t // tm,),
        in_specs=[pl.BlockSpec((tm, D_MODEL), lambda i: (i, 0)),
                  pl.BlockSpec((SUBLANES, D_MODEL), lambda i: (jnp.maximum(i * per8 - 1, 0), 0)),
                  _full((1, D_MODEL)), _full((D_MODEL, 3 * C_WIDTH)), _full((CONV_WIDTH, C_WIDTH))],
        out_specs=pl.BlockSpec((tm, C_WIDTH), lambda i: (i, 0)),
        out_shape=jax.ShapeDtypeStruct((t, C_WIDTH), BF16),
        compiler_params=_params(),
        name="proj_c",
    )(x2, x2, g, w, cw)


def _merge_kernel(x_ref, g_ref, wg_ref, ya_ref, ybt_ref, yc_ref, wa_ref, wb_ref, wc_ref, wo_ref, o_ref):
    x = x_ref[...]
    h = _rms(x, g_ref[...]).astype(BF16)
    projected = (
        jnp.dot(ya_ref[...], wa_ref[...], preferred_element_type=F32),
        lax.dot_general(ybt_ref[...], wb_ref[...], (((0,), (0,)), ((), ())), preferred_element_type=F32),
        jnp.dot(yc_ref[...], wc_ref[...], preferred_element_type=F32),
    )
    merged = None
    for j, proj in enumerate(projected):
        gate = jnp.dot(h, wg_ref[:, j * D_MODEL:(j + 1) * D_MODEL], preferred_element_type=F32)
        term = jax.nn.sigmoid(gate) * proj
        merged = term if merged is None else merged + term
    o_ref[...] = x + jnp.dot(merged.astype(BF16), wo_ref[...], preferred_element_type=F32)


def _merge(x2, g, wg, ya, ybt, yc, wa, wb, wc, wo):
    t = x2.shape[0]
    tm = min(TM_MERGE, t)
    tok = lambda n: pl.BlockSpec((tm, n), lambda i: (i, 0))
    wproj = _full((A_WIDTH, D_MODEL))
    return pl.pallas_call(
        _merge_kernel,
        grid=(t // tm,),
        in_specs=[tok(D_MODEL), _full((1, D_MODEL)), _full((D_MODEL, N_BRANCH * D_MODEL)),
                  tok(A_WIDTH), pl.BlockSpec((B_WIDTH, tm), lambda i: (0, i)), tok(C_WIDTH),
                  wproj, wproj, wproj, _full((D_MODEL, D_MODEL))],
        out_specs=tok(D_MODEL),
        out_shape=jax.ShapeDtypeStruct((t, D_MODEL), F32),
        compiler_params=_params(),
        name="merge",
    )(x2, g, wg, ya, ybt, yc, wa, wb, wc, wo)


def _router_kernel(x_ref, g_ref, w_ref, b_ref, h_ref, eid_ref, wt_ref):
    h = _rms(x_ref[...], g_ref[...])
    h_ref[...] = h
    logits = jnp.dot(h, w_ref[...], preferred_element_type=F32, precision=lax.Precision.HIGHEST) + b_ref[...]
    lane = lax.broadcasted_iota(I32, logits.shape, 1)
    lane_f = lane.astype(F32)
    ninf = -jnp.inf

    def first_max(vals):
        vmax = jnp.max(vals, axis=1, keepdims=True)
        idx = jnp.min(jnp.where(vals == vmax, lane_f, float(LANES)), axis=1, keepdims=True)
        return vmax, idx

    gl = jnp.where(lane < N_GROUPS, logits, ninf)
    gmax, grp = first_max(gl)
    gsum = jnp.sum(jnp.where(lane < N_GROUPS, jnp.exp(logits - gmax), 0.0), axis=1, keepdims=True)
    gw = 1.0 / gsum
    e_lane = lane - N_GROUPS
    in_grp = (e_lane >= 0) & (e_lane < N_EXPERTS) & ((e_lane >> 3).astype(F32) == grp)
    el = jnp.where(in_grp, logits, ninf)
    v1, i1 = first_max(el)
    v2, i2 = first_max(jnp.where(lane_f == i1, ninf, el))
    e2 = jnp.exp(v2 - v1)
    den = 1.0 + e2
    w1 = (1.0 / den) * gw
    w2 = (e2 / den) * gw
    eid1 = (i1 - N_GROUPS).astype(I32)
    eid2 = (i2 - N_GROUPS).astype(I32)
    eid_ref[...] = jnp.where(lane == 0, eid1, jnp.where(lane == 1, eid2, 0))
    wt_ref[...] = jnp.where(lane == 0, w1, jnp.where(lane == 1, w2, 0.0))


def _router(x2, g, w, b):
    t = x2.shape[0]
    tm = min(TM_ROUTE, t)
    tok = lambda n: pl.BlockSpec((tm, n), lambda i: (i, 0))
    return pl.pallas_call(
        _router_kernel,
        grid=(t // tm,),
        in_specs=[tok(D_MODEL), _full((1, D_MODEL)), _full((D_MODEL, LANES)), _full((1, LANES))],
        out_specs=[tok(D_MODEL), tok(LANES), tok(LANES)],
        out_shape=[jax.ShapeDtypeStruct((t, D_MODEL), F32), jax.ShapeDtypeStruct((t, LANES), I32),
                   jax.ShapeDtypeStruct((t, LANES), F32)],
        compiler_params=_params(),
        name="router",
    )(x2, g, w, b)


def _rank_kernel(eid_ref, rank_ref, cnt_ref, *, tm):
    @pl.when(pl.program_id(0) == 0)
    def _():
        cnt_ref[...] = jnp.zeros_like(cnt_ref)

    eid = eid_ref[...]
    lane = lax.broadcasted_iota(I32, (tm, LANES), 1)
    o1 = lane == eid[:, 0:1]
    o2 = lane == eid[:, 1:2]
    both = jnp.where(o1 | o2, 1.0, 0.0)
    lower = lax.broadcasted_iota(I32, (tm, tm), 1) < lax.broadcasted_iota(I32, (tm, tm), 0)
    before = jnp.dot(jnp.where(lower, 1.0, 0.0).astype(BF16), both.astype(BF16),
                     preferred_element_type=F32) + cnt_ref[...]
    r1 = jnp.sum(jnp.where(o1, before, 0.0), axis=1, keepdims=True).astype(I32)
    r2 = jnp.sum(jnp.where(o2, before, 0.0), axis=1, keepdims=True).astype(I32)
    rank_ref[...] = jnp.where(lane == 0, r1, jnp.where(lane == 1, r2, 0))
    cnt_ref[...] = cnt_ref[...] + jnp.sum(both, axis=0, keepdims=True)


def _rank(eid):
    t = eid.shape[0]
    tm = min(TM_RANK, t)
    return pl.pallas_call(
        functools.partial(_rank_kernel, tm=tm),
        grid=(t // tm,),
        in_specs=[pl.BlockSpec((tm, LANES), lambda i: (i, 0))],
        out_specs=[pl.BlockSpec((tm, LANES), lambda i: (i, 0)), _full((1, LANES))],
        out_shape=[jax.ShapeDtypeStruct((t, LANES), I32), jax.ShapeDtypeStruct((1, LANES), F32)],
        compiler_params=_params(),
        name="rank",
    )(eid)


def _dest_kernel(eid_ref, rank_ref, start_ref, d1_ref, d2_ref):
    eid = eid_ref[...]
    lane = lax.broadcasted_iota(I32, eid.shape, 1)
    start = start_ref[...]
    s1 = jnp.sum(jnp.where(lane == eid[:, 0:1], start, 0.0), axis=1, keepdims=True).astype(I32)
    s2 = jnp.sum(jnp.where(lane == eid[:, 1:2], start, 0.0), axis=1, keepdims=True).astype(I32)
    dest = jnp.where(lane == 0, s1, jnp.where(lane == 1, s2, 0)) + rank_ref[...]
    dest_t = jnp.transpose(dest.astype(F32))[:SUBLANES, :].astype(I32)
    d1_ref[...] = dest_t[0:1, :]
    d2_ref[...] = dest_t[1:2, :]


def _dest(eid, rank, start_row):
    t = eid.shape[0]
    tm = min(TM_ROUTE, t)
    tok = pl.BlockSpec((tm, LANES), lambda i: (i, 0))
    return pl.pallas_call(
        _dest_kernel,
        grid=(t // tm,),
        in_specs=[tok, tok, _full((1, LANES))],
        out_specs=[pl.BlockSpec((1, tm), lambda i: (0, i))] * 2,
        out_shape=[jax.ShapeDtypeStruct((1, t), I32)] * 2,
        compiler_params=_params(),
        name="dest",
    )(eid, rank, start_row)


def _row_copy(src_ref, src_row, dst_ref, dst_row, sem):
    return pltpu.make_async_copy(src_ref.at[pl.ds(src_row, 1)], dst_ref.at[pl.ds(dst_row, 1)], sem)


def _rows_copy(src_ref, dst_ref, n, sem):
    return pltpu.make_async_copy(src_ref.at[pl.ds(0, n)], dst_ref.at[pl.ds(0, n)], sem)


def _scatter_kernel(d1_ref, d2_ref, h_ref, xs_in_ref, xs_ref, sem, *, tm):
    del xs_in_ref

    def issue(t, _):
        for d_ref in (d1_ref, d2_ref):
            _row_copy(h_ref, t, xs_ref, d_ref[0, t], sem).start()
        return 0

    lax.fori_loop(0, tm, issue, 0, unroll=ISSUE_UNROLL)
    for _ in range(2):
        _rows_copy(h_ref, xs_ref, tm, sem).wait()


def _scatter_rows(dests, h, n_rows):
    t = h.shape[0]
    tm = min(TM_MOVE, t)
    xs0 = jnp.zeros((n_rows, D_MODEL), F32)
    return pl.pallas_call(
        functools.partial(_scatter_kernel, tm=tm),
        grid=(t // tm,),
        in_specs=[pl.BlockSpec((1, tm), lambda i: (0, i), memory_space=pltpu.SMEM)] * 2
                 + [pl.BlockSpec((tm, D_MODEL), lambda i: (i, 0)), pl.BlockSpec(memory_space=pl.ANY)],
        out_specs=pl.BlockSpec(memory_space=pl.ANY),
        out_shape=jax.ShapeDtypeStruct((n_rows, D_MODEL), F32),
        scratch_shapes=[pltpu.SemaphoreType.DMA(())],
        input_output_aliases={3: 0},
        compiler_params=_params(),
        name="scatter_rows",
    )(*dests, h, xs0)


def _expert_kernel(be_ref, nused_ref, xs_ref, wg_ref, wu_ref, wd_ref, y_ref, wg_bf, wu_bf, wd_bf):
    b = pl.program_id(0)

    @pl.when((b == 0) | (be_ref[b] != be_ref[jnp.maximum(b - 1, 0)]))
    def _():
        wg_bf[...] = wg_ref[...].astype(BF16)
        wu_bf[...] = wu_ref[...].astype(BF16)
        wd_bf[...] = wd_ref[...].astype(BF16)

    @pl.when(b < nused_ref[0])
    def _():
        xb = xs_ref[...].astype(BF16)
        gate = jnp.dot(xb, wg_bf[...], preferred_element_type=F32)
        up = jnp.dot(xb, wu_bf[...], preferred_element_type=F32)
        hid = (jax.nn.silu(gate) * up).astype(BF16)
        y_ref[...] = jnp.dot(hid, wd_bf[...], preferred_element_type=F32)

    @pl.when(b >= nused_ref[0])
    def _():
        y_ref[...] = jnp.zeros_like(y_ref)


def _experts(block_exp, n_used, xs, wg, wu, wd, layer):
    n_rows = xs.shape[0]
    w_in_spec = pl.BlockSpec((None, None, D_MODEL, D_FF_EXPERT), lambda b, be, nu: (layer, be[b], 0, 0))
    grid_spec = pltpu.PrefetchScalarGridSpec(
        num_scalar_prefetch=2,
        grid=(n_rows // EXP_BLOCK,),
        in_specs=[pl.BlockSpec((EXP_BLOCK, D_MODEL), lambda b, be, nu: (b, 0)), w_in_spec, w_in_spec,
                  pl.BlockSpec((None, None, D_FF_EXPERT, D_MODEL), lambda b, be, nu: (layer, be[b], 0, 0))],
        out_specs=pl.BlockSpec((EXP_BLOCK, D_MODEL), lambda b, be, nu: (b, 0)),
        scratch_shapes=[pltpu.VMEM((D_MODEL, D_FF_EXPERT), BF16), pltpu.VMEM((D_MODEL, D_FF_EXPERT), BF16),
                        pltpu.VMEM((D_FF_EXPERT, D_MODEL), BF16)],
    )
    return pl.pallas_call(
        _expert_kernel,
        grid_spec=grid_spec,
        out_shape=jax.ShapeDtypeStruct((n_rows, D_MODEL), F32),
        compiler_params=_params(),
        name="experts",
    )(block_exp, n_used, xs, wg, wu, wd)


def _combine_kernel(d1_ref, d2_ref, d1_next_ref, d2_next_ref, x_ref, wt_ref, g_ref, y_ref, o_ref, buf_ref, sems,
                    *, tm, steps, final_norm):
    i = pl.program_id(0)

    def gather(idx_refs, slot):
        def issue(t, _):
            for j, idx_ref in enumerate(idx_refs):
                _row_copy(y_ref, idx_ref[0, t], buf_ref.at[slot], j * tm + t, sems.at[slot]).start()
            return 0

        lax.fori_loop(0, tm, issue, 0, unroll=ISSUE_UNROLL)

    @pl.when(i == 0)
    def _():
        gather((d1_ref, d2_ref), 0)

    @pl.when(i + 1 < steps)
    def _():
        gather((d1_next_ref, d2_next_ref), (i + 1) % 2)

    slot = i % 2
    _rows_copy(y_ref, buf_ref.at[slot], 2 * tm, sems.at[slot]).wait()
    wt = wt_ref[...]
    rows = buf_ref[slot]
    out = x_ref[...] + (wt[:, 0:1] * rows[:tm, :] + wt[:, 1:2] * rows[tm:, :])
    if final_norm:
        out = _rms(out, g_ref[...])
    o_ref[...] = out


def _combine(dests, x2, wt, g_final, y, final_norm):
    t = x2.shape[0]
    tm = min(TM_MOVE, t)
    steps = t // tm
    tok = lambda n: pl.BlockSpec((tm, n), lambda i: (i, 0))
    idx = lambda f: pl.BlockSpec((1, tm), f, memory_space=pltpu.SMEM)
    return pl.pallas_call(
        functools.partial(_combine_kernel, tm=tm, steps=steps, final_norm=final_norm),
        grid=(steps,),
        in_specs=[idx(lambda i: (0, i))] * 2 + [idx(lambda i: (0, jnp.minimum(i + 1, steps - 1)))] * 2
                 + [tok(D_MODEL), tok(LANES), _full((1, D_MODEL)), pl.BlockSpec(memory_space=pl.ANY)],
        out_specs=tok(D_MODEL),
        out_shape=jax.ShapeDtypeStruct((t, D_MODEL), F32),
        scratch_shapes=[pltpu.VMEM((2, 2 * tm, D_MODEL), F32), pltpu.SemaphoreType.DMA((2,))],
        compiler_params=_params(),
        name="combine",
    )(*dests, *dests, x2, wt, g_final, y)


W_IN_SIZES = (A_WIDTH, A_WIDTH, B_WIDTH, B_WIDTH, B_WIDTH, IDX_HEADS * IDX_DIM, IDX_DIM, IDX_HEADS,
              C_WIDTH, C_WIDTH, C_WIDTH, N_BRANCH * D_MODEL)
W_IN_CUTS = tuple(int(c) for c in np.cumsum((0,) + W_IN_SIZES))
W_BT_ROWS = 2 * B_WIDTH + IDX_HEADS * IDX_DIM + WI_ROWS


def _repack_kernel(w_ref, wa_ref, wb_ref, wbt_ref, wc_ref, wg_ref):
    x = w_ref[...]
    u, va, q, k, vb, qi, ki, wi, cb, cc, cx, gates = [
        x[:, a:b] for a, b in zip(W_IN_CUTS[:-1], W_IN_CUTS[1:])]
    for j, part in enumerate((u, va)):
        wa_ref[:, j * A_WIDTH:(j + 1) * A_WIDTH] = part.astype(BF16)
    wb_ref[:, :B_WIDTH] = k.astype(BF16)
    ki_pad = jnp.concatenate([ki, jnp.zeros((ki.shape[0], LANES - IDX_DIM), F32)], axis=1)
    wb_ref[:, B_WIDTH:] = ki_pad.astype(BF16)
    row = 0
    for part in (q, vb, qi):
        for j in range(part.shape[1] // LANES):
            wbt_ref[row:row + LANES, :] = jnp.transpose(part[:, j * LANES:(j + 1) * LANES]).astype(BF16)
            row += LANES
    lane = lax.broadcasted_iota(I32, (x.shape[0], LANES), 1)
    wi_wide = jnp.where(lane < IDX_HEADS, x[:, W_IN_CUTS[7]:W_IN_CUTS[7] + LANES], 0.0)
    wbt_ref[row:row + WI_ROWS, :] = jnp.transpose(wi_wide)[:WI_ROWS, :].astype(BF16)
    for j, part in enumerate((cb, cc, cx)):
        wc_ref[:, j * C_WIDTH:(j + 1) * C_WIDTH] = part.astype(BF16)
    wg_ref[...] = gates.astype(BF16)


def _repack_w_in(w_in):
    depth, d, d_in = w_in.shape
    rows = lambda n: pl.BlockSpec((None, LANES, n), lambda l, i: (l, i, 0))
    outs = [2 * A_WIDTH, B_WIDTH + LANES, None, 3 * C_WIDTH, N_BRANCH * D_MODEL]
    shapes = [jax.ShapeDtypeStruct((depth, W_BT_ROWS, d) if n is None else (depth, d, n), BF16) for n in outs]
    specs = [pl.BlockSpec((None, W_BT_ROWS, LANES), lambda l, i: (l, 0, i)) if n is None else rows(n)
             for n in outs]
    return pl.pallas_call(
        _repack_kernel,
        grid=(depth, d // LANES),
        in_specs=[rows(d_in)],
        out_specs=specs,
        out_shape=shapes,
        compiler_params=_params(2),
        name="repack_w_in",
    )(w_in)


def _token_mixers(x2, g, w_parts, ws, bs, conv_w, wpa, wpb, wpc, w_out, tabs, bsz, seq):
    w_a, w_b, w_bt, w_c, w_g = w_parts
    n_sel = min(TOPK_MAX, seq // 4)
    k, ki, qt, vt_aug, qit, wit = _proj_b(x2, g, w_b, w_bt, *tabs)
    ybt = _dsa(qit, wit, ki, qt, k, vt_aug, bsz, seq, n_sel)
    ya = _proj_a(x2, g, w_a, ws.astype(BF16), bs.T)
    yc = _proj_c(x2, g, w_c, conv_w, seq)
    bf = lambda w: w.astype(BF16)
    return _merge(x2, g, w_g, ya, ybt, yc, bf(wpa), bf(wpb), bf(wpc), bf(w_out))


def _hier_moe(x2, g, rg_w, rg_b, re_w, re_b, w_gate, w_up, w_down, g_final, layer):
    t = x2.shape[0]
    pad_c = LANES - N_GROUPS - N_EXPERTS
    w_r = jnp.pad(jnp.concatenate([rg_w, re_w], axis=1), ((0, 0), (0, pad_c)))
    b_r = jnp.pad(jnp.concatenate([rg_b, re_b]), (0, pad_c))[None, :]
    h, eid, wt = _router(x2, g, w_r, b_r)
    rank, counts = _rank(eid)
    cnt = counts[0, :N_EXPERTS].astype(I32)
    padded = ((cnt + EXP_BLOCK - 1) // EXP_BLOCK) * EXP_BLOCK
    pend = jnp.cumsum(padded)
    pstart = pend - padded
    n_blocks = (2 * t) // EXP_BLOCK + N_EXPERTS
    n_rows = n_blocks * EXP_BLOCK
    block_row = jnp.arange(n_blocks, dtype=I32) * EXP_BLOCK
    block_exp = jnp.minimum(jnp.sum((pend[None, :] <= block_row[:, None]).astype(I32), axis=1),
                            N_EXPERTS - 1)
    n_used = (pend[-1:] // EXP_BLOCK).astype(I32)
    start_row = jnp.pad(pstart.astype(F32), (0, LANES - N_EXPERTS))[None, :]
    dests = _dest(eid, rank, start_row)
    xs = _scatter_rows(dests, h, n_rows)
    y = _experts(block_exp, n_used, xs, w_gate, w_up, w_down, layer)
    return _combine(dests, x2, wt, g_final, y, layer == w_gate.shape[0] - 1)


def kernel(x, positions, norm_mix_g, norm_ffn_g, norm_final_g, w_in, gmlp_ws, gmlp_b, conv_w, w_proj_a,
           w_proj_b, w_proj_c, w_out, router_group_w, router_group_b, router_expert_w, router_expert_b,
           expert_w_gate, expert_w_up, expert_w_down):
    bsz, seq, d = x.shape
    depth = w_in.shape[0]
    x2 = x.reshape(bsz * seq, d)
    pos = positions.astype(F32).reshape(bsz * seq)
    tabs = (_rope_tables(pos[:, None], B_HEAD_DIM, B_ROT), _rope_tables(pos[:, None], IDX_DIM, IDX_ROT),
            _rope_tables_t(pos[None, :], B_ROT), _rope_tables_t(pos[None, :], IDX_ROT))
    g_final = norm_final_g[None, :]
    w_parts = _repack_w_in(w_in)
    for l in range(depth):
        x2 = _token_mixers(x2, norm_mix_g[l][None, :], [w[l] for w in w_parts], gmlp_ws[l], gmlp_b[l], conv_w[l],
                           w_proj_a[l], w_proj_b[l], w_proj_c[l], w_out[l], tabs, bsz, seq)
        x2 = _hier_moe(x2, norm_ffn_g[l][None, :], router_group_w[l], router_group_b[l],
                       router_expert_w[l], router_expert_b[l], expert_w_gate, expert_w_up,
                       expert_w_down, g_final, l)
    return x2.reshape(bsz, seq, d)
```

```python
import functools

import numpy as np
import jax
import jax.numpy as jnp
from jax import lax
from jax.experimental import pallas as pl
from jax.experimental.pallas import tpu as pltpu

D_MODEL = 1024
CHUNK = 128
A_GROUPS = 4
A_GROUP_DIM = 128
A_WIDTH = A_GROUPS * A_GROUP_DIM
B_HEADS = 4
B_HEAD_DIM = 128
B_WIDTH = B_HEADS * B_HEAD_DIM
B_ROT = B_HEAD_DIM // 4
IDX_HEADS = 4
IDX_DIM = 64
IDX_ROT = IDX_DIM // 4
TOPK_MAX = 256
C_WIDTH = 512
CONV_WIDTH = 3
N_BRANCH = 3
N_GROUPS = 4
EXP_PER_GROUP = 8
N_EXPERTS = N_GROUPS * EXP_PER_GROUP
D_FF_EXPERT = 512
ROPE_THETA = 500000.0
EPS = 1e-6
NEG = -1e30

LANES = 128
SUBLANES = 8
VMEM_LIMIT = 56 * 1024 * 1024

TM_PROJ = 512
TM_MERGE = 256
TM_ROUTE = 512
TM_RANK = 256
TM_MOVE = 512
ISSUE_UNROLL = 4
EXP_BLOCK = 512
DSA_TQ = 128
DSA_KC = 256
DSA_SKC = 512
DSA_SUB = 256
DSA_RB = 512
V_ROWS = B_HEAD_DIM + 16
LOG2E = 1.4426950408889634

F32 = jnp.float32
BF16 = jnp.bfloat16
I32 = jnp.int32
INT_MIN = -2 ** 31


def _monotone_key_of(value):
    bits = int(np.array(value, np.float32).view(np.int32))
    return bits ^ ((bits >> 31) & 0x7FFFFFFF)


NEG_KEY = _monotone_key_of(NEG)


def _params(n_axes=1, semantics=None):
    return pltpu.CompilerParams(
        dimension_semantics=semantics or ("arbitrary",) * n_axes,
        vmem_limit_bytes=VMEM_LIMIT)


def _rms(x, g):
    ms = jnp.mean(x * x, axis=-1, keepdims=True)
    return x * lax.rsqrt(ms + EPS) * g


def _full(shape):
    nd = len(shape)
    return pl.BlockSpec(shape, lambda *_: (0,) * nd)


def _rope_table_kernel(pos_ref, invf_ref, mrot_ref, ma_ref, mb_ref, c_ref, sa_ref, sb_ref):
    ang = pos_ref[...] * invf_ref[...]
    c = jnp.cos(ang)
    s = jnp.sin(ang)
    c_ref[...] = jnp.where(mrot_ref[...] > 0, c, 1.0)
    sa_ref[...] = jnp.where(ma_ref[...] > 0, -s, 0.0)
    sb_ref[...] = jnp.where(mb_ref[...] > 0, s, 0.0)


def _rope_tables(pos_col, head_dim, rot):
    t = pos_col.shape[0]
    half = rot // 2
    inv_freq = jnp.float32(ROPE_THETA) ** (-jnp.arange(half, dtype=F32) * 2.0 / rot)
    lane = np.arange(LANES) % head_dim
    mrot = lane < rot
    ma = lane < half
    mb = (lane >= half) & (lane < rot)
    invf = jnp.where(jnp.asarray(mrot), inv_freq[np.where(mrot, lane % half, 0)], 0.0)[None, :]
    row = lambda m: jnp.asarray(m.astype(np.float32))[None, :]
    tm = min(1024, t)
    spec_row = _full((1, LANES))
    out_spec = pl.BlockSpec((tm, LANES), lambda i: (i, 0))
    return pl.pallas_call(
        _rope_table_kernel,
        grid=(t // tm,),
        in_specs=[pl.BlockSpec((tm, 1), lambda i: (i, 0)), spec_row, spec_row, spec_row, spec_row],
        out_specs=[out_spec] * 3,
        out_shape=[jax.ShapeDtypeStruct((t, LANES), F32)] * 3,
        compiler_params=_params(),
        name="rope_tables",
    )(pos_col, invf, row(mrot), row(ma), row(mb))


def _rope_table_t_kernel(pos_ref, invf_ref, c_ref, s_ref):
    ang = invf_ref[...] * pos_ref[...]
    c_ref[...] = jnp.cos(ang)
    s_ref[...] = jnp.sin(ang)


def _rope_tables_t(pos_row, rot):
    t = pos_row.shape[1]
    half = rot // 2
    inv_freq = jnp.float32(ROPE_THETA) ** (-jnp.arange(half, dtype=F32) * 2.0 / rot)
    tm = min(2048, t)
    out_spec = pl.BlockSpec((half, tm), lambda i: (0, i))
    return pl.pallas_call(
        _rope_table_t_kernel,
        grid=(t // tm,),
        in_specs=[pl.BlockSpec((1, tm), lambda i: (0, i)), _full((half, 1))],
        out_specs=[out_spec] * 2,
        out_shape=[jax.ShapeDtypeStruct((half, t), F32)] * 2,
        compiler_params=_params(),
        name="rope_tables_t",
    )(pos_row, inv_freq[:, None])


def _rope_t(x, c, s):
    half = c.shape[0]
    x1, x2 = x[:half, :], x[half:2 * half, :]
    return jnp.concatenate([x1 * c - x2 * s, x1 * s + x2 * c, x[2 * half:, :]], axis=0)


def _rope(x, c, sa, sb, half):
    parts = []
    for j in range(x.shape[1] // LANES):
        xj = x[:, j * LANES:(j + 1) * LANES]
        parts.append(xj * c + pltpu.roll(xj, LANES - half, 1) * sa + pltpu.roll(xj, half, 1) * sb)
    return parts


WI_ROWS = 16


def _proj_b_kernel(x_ref, g_ref, w_ref, wt_ref, cb_ref, sab_ref, sbb_ref, ci_ref, sai_ref, sbi_ref,
                   cbt_ref, sbt_ref, cit_ref, sit_ref, k_ref, ki_ref, qt_ref, vt_ref, qit_ref, wit_ref):
    h = _rms(x_ref[...], g_ref[...]).astype(BF16)
    tm = h.shape[0]
    p = jnp.dot(h, w_ref[...], preferred_element_type=F32)
    pt = lax.dot_general(wt_ref[...], h, (((1,), (1,)), ((), ())), preferred_element_type=F32)
    tb = (cb_ref[...], sab_ref[...], sbb_ref[...])
    ti = (ci_ref[...], sai_ref[...], sbi_ref[...])
    for j, kj in enumerate(_rope(p[:, :B_WIDTH], *tb, B_ROT // 2)):
        k_ref[:, j * LANES:(j + 1) * LANES] = kj.astype(BF16)
    ki_ref[...] = _rope(p[:, B_WIDTH:B_WIDTH + LANES], *ti, IDX_ROT // 2)[0].astype(BF16)
    cbt, sbt = cbt_ref[...], sbt_ref[...]
    for hh in range(B_HEADS):
        hs = slice(hh * B_HEAD_DIM, (hh + 1) * B_HEAD_DIM)
        qt_ref[hs, :] = (_rope_t(pt[hs, :], cbt, sbt) * (B_HEAD_DIM ** -0.5 * LOG2E)).astype(BF16)
    o = B_WIDTH
    for hh in range(B_HEADS):
        vt_ref[hh * V_ROWS:hh * V_ROWS + B_HEAD_DIM, :] = (
            pt[o + hh * B_HEAD_DIM:o + (hh + 1) * B_HEAD_DIM, :].astype(BF16))
        vt_ref[hh * V_ROWS + B_HEAD_DIM:(hh + 1) * V_ROWS, :] = jnp.ones((V_ROWS - B_HEAD_DIM, tm), BF16)
    o += B_WIDTH
    cit, sit = cit_ref[...], sit_ref[...]
    for hh in range(IDX_HEADS):
        hs = slice(hh * IDX_DIM, (hh + 1) * IDX_DIM)
        qit_ref[hs, :] = _rope_t(pt[o + hh * IDX_DIM:o + (hh + 1) * IDX_DIM, :], cit, sit).astype(BF16)
    o += IDX_HEADS * IDX_DIM
    wit_ref[...] = pt[o:o + SUBLANES, :] * ((IDX_HEADS ** -0.5) * (IDX_DIM ** -0.5))


def _proj_b(x2, g, w, wt, tabs_b, tabs_i, tabs_bt, tabs_it):
    t = x2.shape[0]
    tm = min(TM_PROJ, t)
    tok = lambda n: pl.BlockSpec((tm, n), lambda i: (i, 0))
    feat = lambda n: pl.BlockSpec((n, tm), lambda i: (0, i))
    outs_tok = [(B_WIDTH, BF16), (LANES, BF16)]
    outs_feat = [(B_WIDTH, BF16), (B_HEADS * V_ROWS, BF16), (IDX_HEADS * IDX_DIM, BF16), (SUBLANES, F32)]
    return pl.pallas_call(
        _proj_b_kernel,
        grid=(t // tm,),
        in_specs=[tok(D_MODEL), _full((1, D_MODEL)), _full(w.shape), _full(wt.shape)] + [tok(LANES)] * 6
                 + [feat(B_ROT // 2)] * 2 + [feat(IDX_ROT // 2)] * 2,
        out_specs=[tok(n) for n, _ in outs_tok] + [feat(n) for n, _ in outs_feat],
        out_shape=[jax.ShapeDtypeStruct((t, n), d) for n, d in outs_tok]
                  + [jax.ShapeDtypeStruct((n, t), d) for n, d in outs_feat],
        compiler_params=_params(),
        name="proj_b",
    )(x2, g, w, wt, *tabs_b, *tabs_i, *tabs_bt, *tabs_it)


def _ordered_key(score):
    bits = lax.bitcast_convert_type(score, I32)
    key = bits ^ ((bits >> 31) & 0x7FFFFFFF)
    return jnp.where(key == -1, 0, key)


def _score_of_key(key):
    return lax.bitcast_convert_type(key ^ ((key >> 31) & 0x7FFFFFFF), F32)


def _ce_desc(vals, i, j):
    a, b = vals[i], vals[j]
    vals[i] = jnp.maximum(a, b)
    vals[j] = jnp.minimum(a, b)


def _sort_desc(vals):
    n = len(vals)
    p = 1
    while p < n:
        k = p
        while k >= 1:
            for j in range(k % p, n - k, 2 * k):
                for i in range(min(k, n - j - k)):
                    if (i + j) // (2 * p) == (i + j + k) // (2 * p):
                        _ce_desc(vals, i + j, i + j + k)
            k //= 2
        p *= 2


def _bitonic_merge_desc(vals):
    n = len(vals)
    j = n // 2
    while j >= 1:
        for i in range(n):
            l = i ^ j
            if l > i:
                _ce_desc(vals, i, l)
        j //= 2


def _dsa_kernel(qit_ref, wit_ref, ki_ref, qt_ref, k_ref, vt_ref, o_ref, sc_ref, top_ref, topk_ref, acc_ref,
                bias0_ref, bias1_ref, lg0_ref, lg1_ref,
                *, seq, n_sel, tq, kc, skc, sub, rb):
    i = pl.program_id(1)
    q0 = i * tq
    n_full = q0 // skc
    n_rows = (n_full + 1) * skc
    n_pad = (seq - 1 - q0) - lax.broadcasted_iota(I32, (1, tq), 1)
    n_chunk = (q0 + tq + kc - 1) // kc
    wit = wit_ref[...]

    n_top = skc // SUBLANES
    top_ref[...] = jnp.full((skc, tq), -jnp.inf, F32)

    def score_batch(c, masked):
        vals = []
        for s in range(skc // sub):
            r0 = pl.multiple_of(c * skc + s * sub, sub)
            kic = ki_ref[pl.ds(r0, sub), :IDX_DIM]
            sc = None
            for h in range(IDX_HEADS):
                a = jnp.dot(kic, qit_ref[h * IDX_DIM:(h + 1) * IDX_DIM, :], preferred_element_type=F32)
                term = jnp.maximum(a, 0.0) * wit[h:h + 1, :]
                sc = term if sc is None else sc + term
            if masked:
                qpos = q0 + lax.broadcasted_iota(I32, (sub, tq), 1)
                krow = r0 + lax.broadcasted_iota(I32, (sub, tq), 0)
                sc = jnp.where(krow <= qpos, sc, -jnp.inf)
            sc_ref[pl.ds(r0, sub), :] = sc
            vals += [sc[SUBLANES * i:SUBLANES * (i + 1), :] for i in range(sub // SUBLANES)]
        _sort_desc(vals)
        return vals

    def keep_top(batches):
        top = [top_ref[SUBLANES * i:SUBLANES * (i + 1), :] for i in range(n_top)]
        for vals in batches:
            top = [jnp.maximum(top[i], vals[n_top - 1 - i]) for i in range(n_top)]
            _bitonic_merge_desc(top)
        for i in range(n_top):
            top_ref[SUBLANES * i:SUBLANES * (i + 1), :] = top[i]

    def score_pair(j, _):
        keep_top([score_batch(2 * j, False), score_batch(2 * j + 1, False)])
        return 0

    lax.fori_loop(0, n_full // 2, score_pair, 0)

    @pl.when(n_full % 2 == 1)
    def _():
        keep_top([score_batch(n_full - 1, False)])

    keep_top([score_batch(n_full, True)])

    def search(count):
        def bit_body(b, carry):
            ans_u, = carry
            cand_u = ans_u | jnp.left_shift(jnp.int32(1), 31 - b)
            ok = count(cand_u ^ INT_MIN, False) >= n_sel
            return (jnp.where(ok, cand_u, ans_u),)

        ans_u, = lax.fori_loop(0, 32, bit_body, (jnp.zeros((1, tq), I32),))
        thr = ans_u ^ INT_MIN
        return thr, (n_sel - count(thr, True)).astype(F32)

    def pads(cand, strict):
        return jnp.where((NEG_KEY > cand) if strict else (NEG_KEY >= cand), n_pad, 0)

    topk_ref[...] = _ordered_key(top_ref[...])

    def count_top(cand, strict):
        t = topk_ref[...]
        hit = (t > cand) if strict else (t >= cand)
        return jnp.sum(jnp.where(hit, 1.0, 0.0), axis=0, keepdims=True).astype(I32) + pads(cand, strict)

    def search_top():
        def digit_body(b, carry):
            ans_u, = carry
            shift = 30 - 2 * b
            for d in (1, 2, 3):
                cand_u = carry[0] | jnp.left_shift(jnp.int32(d), shift)
                ans_u = jnp.where(count_top(cand_u ^ INT_MIN, False) >= n_sel, cand_u, ans_u)
            return (ans_u,)

        ans_u, = lax.fori_loop(0, 16, digit_body, (jnp.zeros((1, tq), I32),))
        thr = ans_u ^ INT_MIN
        return thr, (n_sel - count_top(thr, True)).astype(F32)

    def count_all(cand, strict):
        acc_rows = 4 * SUBLANES
        cand_f = _score_of_key(cand)

        def body(r, acc):
            r0 = pl.multiple_of(r * rb, rb)
            blk = sc_ref[pl.ds(r0, rb), :]
            for j in range(rb // acc_rows):
                part = blk[j * acc_rows:(j + 1) * acc_rows, :]
                hit = (part > cand_f) if strict else (part >= cand_f)
                acc = acc + jnp.where(hit, 1, 0)
            return acc

        acc = lax.fori_loop(0, n_rows // rb, body, jnp.zeros((acc_rows, tq), I32))
        return jnp.sum(acc.astype(F32), axis=0, keepdims=True).astype(I32) + pads(cand, strict)

    thr, need = search_top()
    last = topk_ref[skc - SUBLANES:skc, :]
    overflow = jnp.max(jnp.where(last > thr, 1.0, 0.0))
    thr, need = lax.cond(overflow > 0.0, lambda: search(count_all), lambda: (thr, need))
    thr_f = _score_of_key(thr)

    acc_ref[...] = jnp.zeros_like(acc_ref)
    bias_refs = (bias0_ref, bias1_ref)
    lg_refs = (lg0_ref, lg1_ref)
    hk = kc // 2
    lower = (lax.broadcasted_iota(I32, (hk, hk), 1) < lax.broadcasted_iota(I32, (hk, hk), 0))
    lower = jnp.where(lower, 1.0, 0.0).astype(BF16)

    def produce(c, slot, tie_base):
        r0 = pl.multiple_of(c * kc, kc)
        sc = sc_ref[pl.ds(r0, kc), :]
        tie = sc == thr_f
        tie_f = jnp.where(tie, 1.0, 0.0)
        tie_b = tie_f.astype(BF16)
        b0 = jnp.dot(lower, tie_b[:hk, :], preferred_element_type=F32)
        n0 = b0[hk - 1:hk, :] + tie_f[hk - 1:hk, :]
        b1 = jnp.dot(lower, tie_b[hk:, :], preferred_element_type=F32) + n0
        before = jnp.concatenate([b0, b1], axis=0)
        sel = (sc > thr_f) | (tie & (before < need - tie_base))
        bias_refs[slot][...] = jnp.where(sel, 0.0, NEG)
        for h in range(B_HEADS):
            hs = slice(h * B_HEAD_DIM, (h + 1) * B_HEAD_DIM)
            lg_refs[slot][h] = jnp.dot(k_ref[pl.ds(r0, kc), hs], qt_ref[hs, :], preferred_element_type=F32)
        return tie_base + b1[hk - 1:hk, :] + tie_f[kc - 1:kc, :]

    def consume(c, slot, ms):
        r0 = pl.multiple_of(c * kc, kc)
        bias = bias_refs[slot][...]
        new_ms = []
        for h in range(B_HEADS):
            vs = slice(h * V_ROWS, (h + 1) * V_ROWS)
            lg = lg_refs[slot][h] + bias
            m_new = jnp.maximum(ms[h], jnp.max(lg, axis=0, keepdims=True))
            alpha = jnp.exp2(ms[h] - m_new)
            p = jnp.exp2(lg - m_new).astype(BF16)
            pv = jnp.dot(vt_ref[vs, pl.ds(r0, kc)], p, preferred_element_type=F32)
            acc_ref[vs, :] = acc_ref[vs, :] * alpha + pv
            new_ms.append(m_new)
        return tuple(new_ms)

    def pair_body(j, carry):
        ms, tie_base = carry
        c = 2 * j
        tie_base = produce(c + 1, 1, tie_base)
        ms = consume(c, 0, ms)
        tie_base = produce(c + 2, 0, tie_base)
        ms = consume(c + 1, 1, ms)
        return ms, tie_base

    row = lambda v: jnp.full((1, tq), v, F32)
    ms0 = tuple(row(NEG) for _ in range(B_HEADS))
    n_pairs = (n_chunk - 1) // 2
    ms, tie_base = lax.fori_loop(0, n_pairs, pair_body, (ms0, produce(0, 0, row(0.0))))
    c0 = 2 * n_pairs

    @pl.when(c0 == n_chunk - 1)
    def _():
        consume(c0, 0, ms)

    @pl.when(c0 != n_chunk - 1)
    def _():
        produce(c0 + 1, 1, tie_base)
        consume(c0 + 1, 1, consume(c0, 0, ms))

    for h in range(B_HEADS):
        a = acc_ref[h * V_ROWS:(h + 1) * V_ROWS, :]
        o_ref[h * B_HEAD_DIM:(h + 1) * B_HEAD_DIM, :] = (
            a[:B_HEAD_DIM, :] / a[B_HEAD_DIM:B_HEAD_DIM + 1, :]).astype(o_ref.dtype)


def _dsa(qit, wit, ki, qt, k, vt_aug, bsz, seq, n_sel):
    tq = min(DSA_TQ, seq)
    kc = min(DSA_KC, seq)
    skc = min(DSA_SKC, seq)
    nq = seq // tq
    once = pl.Buffered(1)
    kern = functools.partial(_dsa_kernel, seq=seq, n_sel=n_sel, tq=tq, kc=kc, skc=skc,
                             sub=min(DSA_SUB, skc), rb=min(DSA_RB, skc))
    q_tile = lambda rows: pl.BlockSpec((rows, tq), lambda b, i: (0, b * nq + i))
    return pl.pallas_call(
        kern,
        grid=(bsz, nq),
        in_specs=[
            q_tile(IDX_HEADS * IDX_DIM),
            q_tile(SUBLANES),
            pl.BlockSpec((seq, LANES), lambda b, i: (b, 0), pipeline_mode=once),
            q_tile(B_WIDTH),
            pl.BlockSpec((seq, B_WIDTH), lambda b, i: (b, 0), pipeline_mode=once),
            pl.BlockSpec((B_HEADS * V_ROWS, seq), lambda b, i: (0, b), pipeline_mode=once),
        ],
        out_specs=q_tile(B_WIDTH),
        out_shape=jax.ShapeDtypeStruct((B_WIDTH, bsz * seq), BF16),
        scratch_shapes=[pltpu.VMEM((seq, tq), F32), pltpu.VMEM((skc, tq), F32), pltpu.VMEM((skc, tq), I32),
                        pltpu.VMEM((B_HEADS * V_ROWS, tq), F32), pltpu.VMEM((kc, tq), F32),
                        pltpu.VMEM((kc, tq), F32), pltpu.VMEM((B_HEADS, kc, tq), F32),
                        pltpu.VMEM((B_HEADS, kc, tq), F32)],
        compiler_params=_params(2),
        name="dsa",
    )(qit, wit, ki, qt, k, vt_aug)


def _proj_a_kernel(x_ref, g_ref, w_ref, ws_ref, bs_ref, ya_ref, *, tm):
    h = _rms(x_ref[...], g_ref[...]).astype(BF16)
    p = jnp.dot(h, w_ref[...], preferred_element_type=F32)
    u = jax.nn.gelu(p[:, :A_WIDTH])
    v = jax.nn.gelu(p[:, A_WIDTH:])
    mu = jnp.mean(v, axis=-1, keepdims=True)
    var = jnp.mean(jnp.square(v - mu), axis=-1, keepdims=True)
    v = ((v - mu) * lax.rsqrt(var + EPS)).astype(BF16)
    causal = (lax.broadcasted_iota(I32, (CHUNK, CHUNK), 1) <= lax.broadcasted_iota(I32, (CHUNK, CHUNK), 0))
    bs = bs_ref[...]
    for gidx in range(A_GROUPS):
        wsm = jnp.where(causal, ws_ref[gidx], 0.0).astype(BF16)
        cs = slice(gidx * A_GROUP_DIM, (gidx + 1) * A_GROUP_DIM)
        for n in range(tm // CHUNK):
            rs = slice(n * CHUNK, (n + 1) * CHUNK)
            vm = jnp.dot(wsm, v[rs, cs], preferred_element_type=F32) + bs[:, gidx:gidx + 1]
            ya_ref[rs, cs] = (u[rs, cs] * vm).astype(ya_ref.dtype)


def _proj_a(x2, g, w, ws, bs_t):
    t = x2.shape[0]
    tm = min(TM_PROJ, t)
    return pl.pallas_call(
        functools.partial(_proj_a_kernel, tm=tm),
        grid=(t // tm,),
        in_specs=[pl.BlockSpec((tm, D_MODEL), lambda i: (i, 0)), _full((1, D_MODEL)),
                  _full((D_MODEL, 2 * A_WIDTH)), _full((A_GROUPS, CHUNK, CHUNK)), _full((CHUNK, A_GROUPS))],
        out_specs=pl.BlockSpec((tm, A_WIDTH), lambda i: (i, 0)),
        out_shape=jax.ShapeDtypeStruct((t, A_WIDTH), BF16),
        compiler_params=_params(),
        name="proj_a",
    )(x2, g, w, ws, bs_t)


def _proj_c_kernel(x_ref, xh_ref, g_ref, w_ref, cw_ref, yc_ref, *, tm, tiles_per_seq):
    g = g_ref[...]
    w = w_ref[...]
    p = jnp.dot(_rms(x_ref[...], g).astype(BF16), w, preferred_element_type=F32)
    ph = jnp.dot(_rms(xh_ref[...], g).astype(BF16), w[:, C_WIDTH:], preferred_element_type=F32)
    first = (pl.program_id(0) % tiles_per_seq) == 0
    zh = ph[:, :C_WIDTH] * ph[:, C_WIDTH:] * jnp.where(first, 0.0, 1.0)
    z = p[:, C_WIDTH:2 * C_WIDTH] * p[:, 2 * C_WIDTH:]
    row = lax.broadcasted_iota(I32, (tm, C_WIDTH), 0)
    z1 = jnp.where(row == 0, zh[SUBLANES - 1:SUBLANES, :], pltpu.roll(z, 1, 0))
    z2 = jnp.where(row == 0, zh[SUBLANES - 2:SUBLANES - 1, :],
                   jnp.where(row == 1, zh[SUBLANES - 1:SUBLANES, :], pltpu.roll(z, 2, 0)))
    cw = cw_ref[...]
    y = cw[0:1, :] * z2 + cw[1:2, :] * z1 + cw[2:3, :] * z
    yc_ref[...] = (p[:, :C_WIDTH] * y).astype(yc_ref.dtype)


def _proj_c(x2, g, w, cw, seq):
    t = x2.shape[0]
    tm = min(TM_PROJ, seq)
    per8 = tm // SUBLANES
    return pl.pallas_call(
        functools.partial(_proj_c_kernel, tm=tm, tiles_per_seq=seq // tm),
        grid=(t // tm,),
        in_specs=[pl.BlockSpec((tm, D_MODEL), lambda i: (i, 0)),
                  pl.BlockSpec((SUBLANES, D_MODEL), lambda i: (jnp.maximum(i * per8 - 1, 0), 0)),
                  _full((1, D_MODEL)), _full((D_MODEL, 3 * C_WIDTH)), _full((CONV_WIDTH, C_WIDTH))],
        out_specs=pl.BlockSpec((tm, C_WIDTH), lambda i: (i, 0)),
        out_shape=jax.ShapeDtypeStruct((t, C_WIDTH), BF16),
        compiler_params=_params(),
        name="proj_c",
    )(x2, x2, g, w, cw)


def _merge_kernel(x_ref, g_ref, wg_ref, ya_ref, ybt_ref, yc_ref, wa_ref, wb_ref, wc_ref, wo_ref, o_ref):
    x = x_ref[...]
    h = _rms(x, g_ref[...]).astype(BF16)
    projected = (
        jnp.dot(ya_ref[...], wa_ref[...], preferred_element_type=F32),
        lax.dot_general(ybt_ref[...], wb_ref[...], (((0,), (0,)), ((), ())), preferred_element_type=F32),
        jnp.dot(yc_ref[...], wc_ref[...], preferred_element_type=F32),
    )
    merged = None
    for j, proj in enumerate(projected):
        gate = jnp.dot(h, wg_ref[:, j * D_MODEL:(j + 1) * D_MODEL], preferred_element_type=F32)
        term = jax.nn.sigmoid(gate) * proj
        merged = term if merged is None else merged + term
    o_ref[...] = x + jnp.dot(merged.astype(BF16), wo_ref[...], preferred_element_type=F32)


def _merge(x2, g, wg, ya, ybt, yc, wa, wb, wc, wo):
    t = x2.shape[0]
    tm = min(TM_MERGE, t)
    tok = lambda n: pl.BlockSpec((tm, n), lambda i: (i, 0))
    wproj = _full((A_WIDTH, D_MODEL))
    return pl.pallas_call(
        _merge_kernel,
        grid=(t // tm,),
        in_specs=[tok(D_MODEL), _full((1, D_MODEL)), _full((D_MODEL, N_BRANCH * D_MODEL)),
                  tok(A_WIDTH), pl.BlockSpec((B_WIDTH, tm), lambda i: (0, i)), tok(C_WIDTH),
                  wproj, wproj, wproj, _full((D_MODEL, D_MODEL))],
        out_specs=tok(D_MODEL),
        out_shape=jax.ShapeDtypeStruct((t, D_MODEL), F32),
        compiler_params=_params(),
        name="merge",
    )(x2, g, wg, ya, ybt, yc, wa, wb, wc, wo)


def _router_kernel(x_ref, g_ref, w_ref, b_ref, h_ref, eid_ref, wt_ref):
    h = _rms(x_ref[...], g_ref[...])
    h_ref[...] = h
    logits = jnp.dot(h, w_ref[...], preferred_element_type=F32, precision=lax.Precision.HIGHEST) + b_ref[...]
    lane = lax.broadcasted_iota(I32, logits.shape, 1)
    lane_f = lane.astype(F32)
    ninf = -jnp.inf

    def first_max(vals):
        vmax = jnp.max(vals, axis=1, keepdims=True)
        idx = jnp.min(jnp.where(vals == vmax, lane_f, float(LANES)), axis=1, keepdims=True)
        return vmax, idx

    gl = jnp.where(lane < N_GROUPS, logits, ninf)
    gmax, grp = first_max(gl)
    gsum = jnp.sum(jnp.where(lane < N_GROUPS, jnp.exp(logits - gmax), 0.0), axis=1, keepdims=True)
    gw = 1.0 / gsum
    e_lane = lane - N_GROUPS
    in_grp = (e_lane >= 0) & (e_lane < N_EXPERTS) & ((e_lane >> 3).astype(F32) == grp)
    el = jnp.where(in_grp, logits, ninf)
    v1, i1 = first_max(el)
    v2, i2 = first_max(jnp.where(lane_f == i1, ninf, el))
    e2 = jnp.exp(v2 - v1)
    den = 1.0 + e2
    w1 = (1.0 / den) * gw
    w2 = (e2 / den) * gw
    eid1 = (i1 - N_GROUPS).astype(I32)
    eid2 = (i2 - N_GROUPS).astype(I32)
    eid_ref[...] = jnp.where(lane == 0, eid1, jnp.where(lane == 1, eid2, 0))
    wt_ref[...] = jnp.where(lane == 0, w1, jnp.where(lane == 1, w2, 0.0))


def _router(x2, g, w, b):
    t = x2.shape[0]
    tm = min(TM_ROUTE, t)
    tok = lambda n: pl.BlockSpec((tm, n), lambda i: (i, 0))
    return pl.pallas_call(
        _router_kernel,
        grid=(t // tm,),
        in_specs=[tok(D_MODEL), _full((1, D_MODEL)), _full((D_MODEL, LANES)), _full((1, LANES))],
        out_specs=[tok(D_MODEL), tok(LANES), tok(LANES)],
        out_shape=[jax.ShapeDtypeStruct((t, D_MODEL), F32), jax.ShapeDtypeStruct((t, LANES), I32),
                   jax.ShapeDtypeStruct((t, LANES), F32)],
        compiler_params=_params(),
        name="router",
    )(x2, g, w, b)


def _rank_kernel(eid_ref, rank_ref, cnt_ref, *, tm):
    @pl.when(pl.program_id(0) == 0)
    def _():
        cnt_ref[...] = jnp.zeros_like(cnt_ref)

    eid = eid_ref[...]
    lane = lax.broadcasted_iota(I32, (tm, LANES), 1)
    o1 = lane == eid[:, 0:1]
    o2 = lane == eid[:, 1:2]
    both = jnp.where(o1 | o2, 1.0, 0.0)
    lower = lax.broadcasted_iota(I32, (tm, tm), 1) < lax.broadcasted_iota(I32, (tm, tm), 0)
    before = jnp.dot(jnp.where(lower, 1.0, 0.0).astype(BF16), both.astype(BF16),
                     preferred_element_type=F32) + cnt_ref[...]
    r1 = jnp.sum(jnp.where(o1, before, 0.0), axis=1, keepdims=True).astype(I32)
    r2 = jnp.sum(jnp.where(o2, before, 0.0), axis=1, keepdims=True).astype(I32)
    rank_ref[...] = jnp.where(lane == 0, r1, jnp.where(lane == 1, r2, 0))
    cnt_ref[...] = cnt_ref[...] + jnp.sum(both, axis=0, keepdims=True)


def _rank(eid):
    t = eid.shape[0]
    tm = min(TM_RANK, t)
    return pl.pallas_call(
        functools.partial(_rank_kernel, tm=tm),
        grid=(t // tm,),
        in_specs=[pl.BlockSpec((tm, LANES), lambda i: (i, 0))],
        out_specs=[pl.BlockSpec((tm, LANES), lambda i: (i, 0)), _full((1, LANES))],
        out_shape=[jax.ShapeDtypeStruct((t, LANES), I32), jax.ShapeDtypeStruct((1, LANES), F32)],
        compiler_params=_params(),
        name="rank",
    )(eid)


def _dest_kernel(eid_ref, rank_ref, start_ref, d1_ref, d2_ref):
    eid = eid_ref[...]
    lane = lax.broadcasted_iota(I32, eid.shape, 1)
    start = start_ref[...]
    s1 = jnp.sum(jnp.where(lane == eid[:, 0:1], start, 0.0), axis=1, keepdims=True).astype(I32)
    s2 = jnp.sum(jnp.where(lane == eid[:, 1:2], start, 0.0), axis=1, keepdims=True).astype(I32)
    dest = jnp.where(lane == 0, s1, jnp.where(lane == 1, s2, 0)) + rank_ref[...]
    dest_t = jnp.transpose(dest.astype(F32))[:SUBLANES, :].astype(I32)
    d1_ref[...] = dest_t[0:1, :]
    d2_ref[...] = dest_t[1:2, :]


def _dest(eid, rank, start_row):
    t = eid.shape[0]
    tm = min(TM_ROUTE, t)
    tok = pl.BlockSpec((tm, LANES), lambda i: (i, 0))
    return pl.pallas_call(
        _dest_kernel,
        grid=(t // tm,),
        in_specs=[tok, tok, _full((1, LANES))],
        out_specs=[pl.BlockSpec((1, tm), lambda i: (0, i))] * 2,
        out_shape=[jax.ShapeDtypeStruct((1, t), I32)] * 2,
        compiler_params=_params(),
        name="dest",
    )(eid, rank, start_row)


def _row_copy(src_ref, src_row, dst_ref, dst_row, sem):
    return pltpu.make_async_copy(src_ref.at[pl.ds(src_row, 1)], dst_ref.at[pl.ds(dst_row, 1)], sem)


def _rows_copy(src_ref, dst_ref, n, sem):
    return pltpu.make_async_copy(src_ref.at[pl.ds(0, n)], dst_ref.at[pl.ds(0, n)], sem)


def _scatter_kernel(d1_ref, d2_ref, h_ref, xs_in_ref, xs_ref, sem, *, tm):
    del xs_in_ref

    def issue(t, _):
        for d_ref in (d1_ref, d2_ref):
            _row_copy(h_ref, t, xs_ref, d_ref[0, t], sem).start()
        return 0

    lax.fori_loop(0, tm, issue, 0, unroll=ISSUE_UNROLL)
    for _ in range(2):
        _rows_copy(h_ref, xs_ref, tm, sem).wait()


def _scatter_rows(dests, h, n_rows):
    t = h.shape[0]
    tm = min(TM_MOVE, t)
    xs0 = jnp.zeros((n_rows, D_MODEL), F32)
    return pl.pallas_call(
        functools.partial(_scatter_kernel, tm=tm),
        grid=(t // tm,),
        in_specs=[pl.BlockSpec((1, tm), lambda i: (0, i), memory_space=pltpu.SMEM)] * 2
                 + [pl.BlockSpec((tm, D_MODEL), lambda i: (i, 0)), pl.BlockSpec(memory_space=pl.ANY)],
        out_specs=pl.BlockSpec(memory_space=pl.ANY),
        out_shape=jax.ShapeDtypeStruct((n_rows, D_MODEL), F32),
        scratch_shapes=[pltpu.SemaphoreType.DMA(())],
        input_output_aliases={3: 0},
        compiler_params=_params(),
        name="scatter_rows",
    )(*dests, h, xs0)


def _expert_kernel(be_ref, nused_ref, xs_ref, wg_ref, wu_ref, wd_ref, y_ref, wg_bf, wu_bf, wd_bf):
    b = pl.program_id(0)

    @pl.when((b == 0) | (be_ref[b] != be_ref[jnp.maximum(b - 1, 0)]))
    def _():
        wg_bf[...] = wg_ref[...].astype(BF16)
        wu_bf[...] = wu_ref[...].astype(BF16)
        wd_bf[...] = wd_ref[...].astype(BF16)

    @pl.when(b < nused_ref[0])
    def _():
        xb = xs_ref[...].astype(BF16)
        gate = jnp.dot(xb, wg_bf[...], preferred_element_type=F32)
        up = jnp.dot(xb, wu_bf[...], preferred_element_type=F32)
        hid = (jax.nn.silu(gate) * up).astype(BF16)
        y_ref[...] = jnp.dot(hid, wd_bf[...], preferred_element_type=F32)

    @pl.when(b >= nused_ref[0])
    def _():
        y_ref[...] = jnp.zeros_like(y_ref)


def _experts(block_exp, n_used, xs, wg, wu, wd, layer):
    n_rows = xs.shape[0]
    w_in_spec = pl.BlockSpec((None, None, D_MODEL, D_FF_EXPERT), lambda b, be, nu: (layer, be[b], 0, 0))
    grid_spec = pltpu.PrefetchScalarGridSpec(
        num_scalar_prefetch=2,
        grid=(n_rows // EXP_BLOCK,),
        in_specs=[pl.BlockSpec((EXP_BLOCK, D_MODEL), lambda b, be, nu: (b, 0)), w_in_spec, w_in_spec,
                  pl.BlockSpec((None, None, D_FF_EXPERT, D_MODEL), lambda b, be, nu: (layer, be[b], 0, 0))],
        out_specs=pl.BlockSpec((EXP_BLOCK, D_MODEL), lambda b, be, nu: (b, 0)),
        scratch_shapes=[pltpu.VMEM((D_MODEL, D_FF_EXPERT), BF16), pltpu.VMEM((D_MODEL, D_FF_EXPERT), BF16),
                        pltpu.VMEM((D_FF_EXPERT, D_MODEL), BF16)],
    )
    return pl.pallas_call(
        _expert_kernel,
        grid_spec=grid_spec,
        out_shape=jax.ShapeDtypeStruct((n_rows, D_MODEL), F32),
        compiler_params=_params(),
        name="experts",
    )(block_exp, n_used, xs, wg, wu, wd)


def _combine_kernel(d1_ref, d2_ref, d1_next_ref, d2_next_ref, x_ref, wt_ref, g_ref, y_ref, o_ref, buf_ref, sems,
                    *, tm, steps, final_norm):
    i = pl.program_id(0)

    def gather(idx_refs, slot):
        def issue(t, _):
            for j, idx_ref in enumerate(idx_refs):
                _row_copy(y_ref, idx_ref[0, t], buf_ref.at[slot], j * tm + t, sems.at[slot]).start()
            return 0

        lax.fori_loop(0, tm, issue, 0, unroll=ISSUE_UNROLL)

    @pl.when(i == 0)
    def _():
        gather((d1_ref, d2_ref), 0)

    @pl.when(i + 1 < steps)
    def _():
        gather((d1_next_ref, d2_next_ref), (i + 1) % 2)

    slot = i % 2
    _rows_copy(y_ref, buf_ref.at[slot], 2 * tm, sems.at[slot]).wait()
    wt = wt_ref[...]
    rows = buf_ref[slot]
    out = x_ref[...] + (wt[:, 0:1] * rows[:tm, :] + wt[:, 1:2] * rows[tm:, :])
    if final_norm:
        out = _rms(out, g_ref[...])
    o_ref[...] = out


def _combine(dests, x2, wt, g_final, y, final_norm):
    t = x2.shape[0]
    tm = min(TM_MOVE, t)
    steps = t // tm
    tok = lambda n: pl.BlockSpec((tm, n), lambda i: (i, 0))
    idx = lambda f: pl.BlockSpec((1, tm), f, memory_space=pltpu.SMEM)
    return pl.pallas_call(
        functools.partial(_combine_kernel, tm=tm, steps=steps, final_norm=final_norm),
        grid=(steps,),
        in_specs=[idx(lambda i: (0, i))] * 2 + [idx(lambda i: (0, jnp.minimum(i + 1, steps - 1)))] * 2
                 + [tok(D_MODEL), tok(LANES), _full((1, D_MODEL)), pl.BlockSpec(memory_space=pl.ANY)],
        out_specs=tok(D_MODEL),
        out_shape=jax.ShapeDtypeStruct((t, D_MODEL), F32),
        scratch_shapes=[pltpu.VMEM((2, 2 * tm, D_MODEL), F32), pltpu.SemaphoreType.DMA((2,))],
        compiler_params=_params(),
        name="combine",
    )(*dests, *dests, x2, wt, g_final, y)


W_IN_SIZES = (A_WIDTH, A_WIDTH, B_WIDTH, B_WIDTH, B_WIDTH, IDX_HEADS * IDX_DIM, IDX_DIM, IDX_HEADS,
              C_WIDTH, C_WIDTH, C_WIDTH, N_BRANCH * D_MODEL)
W_IN_CUTS = tuple(int(c) for c in np.cumsum((0,) + W_IN_SIZES))
W_BT_ROWS = 2 * B_WIDTH + IDX_HEADS * IDX_DIM + WI_ROWS


def _repack_kernel(w_ref, wa_ref, wb_ref, wbt_ref, wc_ref, wg_ref):
    x = w_ref[...]
    u, va, q, k, vb, qi, ki, wi, cb, cc, cx, gates = [
        x[:, a:b] for a, b in zip(W_IN_CUTS[:-1], W_IN_CUTS[1:])]
    for j, part in enumerate((u, va)):
        wa_ref[:, j * A_WIDTH:(j + 1) * A_WIDTH] = part.astype(BF16)
    wb_ref[:, :B_WIDTH] = k.astype(BF16)
    ki_pad = jnp.concatenate([ki, jnp.zeros((ki.shape[0], LANES - IDX_DIM), F32)], axis=1)
    wb_ref[:, B_WIDTH:] = ki_pad.astype(BF16)
    row = 0
    for part in (q, vb, qi):
        for j in range(part.shape[1] // LANES):
            wbt_ref[row:row + LANES, :] = jnp.transpose(part[:, j * LANES:(j + 1) * LANES]).astype(BF16)
            row += LANES
    lane = lax.broadcasted_iota(I32, (x.shape[0], LANES), 1)
    wi_wide = jnp.where(lane < IDX_HEADS, x[:, W_IN_CUTS[7]:W_IN_CUTS[7] + LANES], 0.0)
    wbt_ref[row:row + WI_ROWS, :] = jnp.transpose(wi_wide)[:WI_ROWS, :].astype(BF16)
    for j, part in enumerate((cb, cc, cx)):
        wc_ref[:, j * C_WIDTH:(j + 1) * C_WIDTH] = part.astype(BF16)
    wg_ref[...] = gates.astype(BF16)


def _repack_w_in(w_in):
    depth, d, d_in = w_in.shape
    rows = lambda n: pl.BlockSpec((None, LANES, n), lambda l, i: (l, i, 0))
    outs = [2 * A_WIDTH, B_WIDTH + LANES, None, 3 * C_WIDTH, N_BRANCH * D_MODEL]
    shapes = [jax.ShapeDtypeStruct((depth, W_BT_ROWS, d) if n is None else (depth, d, n), BF16) for n in outs]
    specs = [pl.BlockSpec((None, W_BT_ROWS, LANES), lambda l, i: (l, 0, i)) if n is None else rows(n)
             for n in outs]
    return pl.pallas_call(
        _repack_kernel,
        grid=(depth, d // LANES),
        in_specs=[rows(d_in)],
        out_specs=specs,
        out_shape=shapes,
        compiler_params=_params(2),
        name="repack_w_in",
    )(w_in)


def _token_mixers(x2, g, w_parts, ws, bs, conv_w, wpa, wpb, wpc, w_out, tabs, bsz, seq):
    w_a, w_b, w_bt, w_c, w_g = w_parts
    n_sel = min(TOPK_MAX, seq // 4)
    k, ki, qt, vt_aug, qit, wit = _proj_b(x2, g, w_b, w_bt, *tabs)
    ybt = _dsa(qit, wit, ki, qt, k, vt_aug, bsz, seq, n_sel)
    ya = _proj_a(x2, g, w_a, ws.astype(BF16), bs.T)
    yc = _proj_c(x2, g, w_c, conv_w, seq)
    bf = lambda w: w.astype(BF16)
    return _merge(x2, g, w_g, ya, ybt, yc, bf(wpa), bf(wpb), bf(wpc), bf(w_out))


def _hier_moe(x2, g, rg_w, rg_b, re_w, re_b, w_gate, w_up, w_down, g_final, layer):
    t = x2.shape[0]
    pad_c = LANES - N_GROUPS - N_EXPERTS
    w_r = jnp.pad(jnp.concatenate([rg_w, re_w], axis=1), ((0, 0), (0, pad_c)))
    b_r = jnp.pad(jnp.concatenate([rg_b, re_b]), (0, pad_c))[None, :]
    h, eid, wt = _router(x2, g, w_r, b_r)
    rank, counts = _rank(eid)
    cnt = counts[0, :N_EXPERTS].astype(I32)
    padded = ((cnt + EXP_BLOCK - 1) // EXP_BLOCK) * EXP_BLOCK
    pend = jnp.cumsum(padded)
    pstart = pend - padded
    n_blocks = (2 * t) // EXP_BLOCK + N_EXPERTS
    n_rows = n_blocks * EXP_BLOCK
    block_row = jnp.arange(n_blocks, dtype=I32) * EXP_BLOCK
    block_exp = jnp.minimum(jnp.sum((pend[None, :] <= block_row[:, None]).astype(I32), axis=1),
                            N_EXPERTS - 1)
    n_used = (pend[-1:] // EXP_BLOCK).astype(I32)
    start_row = jnp.pad(pstart.astype(F32), (0, LANES - N_EXPERTS))[None, :]
    dests = _dest(eid, rank, start_row)
    xs = _scatter_rows(dests, h, n_rows)
    y = _experts(block_exp, n_used, xs, w_gate, w_up, w_down, layer)
    return _combine(dests, x2, wt, g_final, y, layer == w_gate.shape[0] - 1)


def kernel(x, positions, norm_mix_g, norm_ffn_g, norm_final_g, w_in, gmlp_ws, gmlp_b, conv_w, w_proj_a,
           w_proj_b, w_proj_c, w_out, router_group_w, router_group_b, router_expert_w, router_expert_b,
           expert_w_gate, expert_w_up, expert_w_down):
    bsz, seq, d = x.shape
    depth = w_in.shape[0]
    x2 = x.reshape(bsz * seq, d)
    pos = positions.astype(F32).reshape(bsz * seq)
    tabs = (_rope_tables(pos[:, None], B_HEAD_DIM, B_ROT), _rope_tables(pos[:, None], IDX_DIM, IDX_ROT),
            _rope_tables_t(pos[None, :], B_ROT), _rope_tables_t(pos[None, :], IDX_ROT))
    g_final = norm_final_g[None, :]
    w_parts = _repack_w_in(w_in)
    for l in range(depth):
        x2 = _token_mixers(x2, norm_mix_g[l][None, :], [w[l] for w in w_parts], gmlp_ws[l], gmlp_b[l], conv_w[l],
                           w_proj_a[l], w_proj_b[l], w_proj_c[l], w_out[l], tabs, bsz, seq)
        x2 = _hier_moe(x2, norm_ffn_g[l][None, :], router_group_w[l], router_group_b[l],
                       router_expert_w[l], router_expert_b[l], expert_w_gate, expert_w_up,
                       expert_w_down, g_final, l)
    return x2.reshape(bsz, seq, d)
```

```python
import functools

import numpy as np
import jax
import jax.numpy as jnp
from jax import lax
from jax.experimental import pallas as pl
from jax.experimental.pallas import tpu as pltpu

D_MODEL = 1024
CHUNK = 128
A_GROUPS = 4
A_GROUP_DIM = 128
A_WIDTH = A_GROUPS * A_GROUP_DIM
B_HEADS = 4
B_HEAD_DIM = 128
B_WIDTH = B_HEADS * B_HEAD_DIM
B_ROT = B_HEAD_DIM // 4
IDX_HEADS = 4
IDX_DIM = 64
IDX_ROT = IDX_DIM // 4
TOPK_MAX = 256
C_WIDTH = 512
CONV_WIDTH = 3
N_BRANCH = 3
N_GROUPS = 4
EXP_PER_GROUP = 8
N_EXPERTS = N_GROUPS * EXP_PER_GROUP
D_FF_EXPERT = 512
ROPE_THETA = 500000.0
EPS = 1e-6
NEG = -1e30

LANES = 128
SUBLANES = 8
VMEM_LIMIT = 56 * 1024 * 1024

TM_PROJ = 512
TM_MERGE = 256
TM_ROUTE = 512
TM_RANK = 256
TM_MOVE = 512
ISSUE_UNROLL = 4
EXP_BLOCK = 512
DSA_TQ = 128
DSA_KC = 256
DSA_SKC = 512
DSA_SUB = 256
DSA_RB = 512
V_ROWS = B_HEAD_DIM + 16
LOG2E = 1.4426950408889634

F32 = jnp.float32
BF16 = jnp.bfloat16
I32 = jnp.int32
INT_MIN = -2 ** 31


def _monotone_key_of(value):
    bits = int(np.array(value, np.float32).view(np.int32))
    return bits ^ ((bits >> 31) & 0x7FFFFFFF)


NEG_KEY = _monotone_key_of(NEG)


def _params(n_axes=1, semantics=None):
    return pltpu.CompilerParams(
        dimension_semantics=semantics or ("arbitrary",) * n_axes,
        vmem_limit_bytes=VMEM_LIMIT)


def _rms(x, g):
    ms = jnp.mean(x * x, axis=-1, keepdims=True)
    return x * lax.rsqrt(ms + EPS) * g


def _full(shape):
    nd = len(shape)
    return pl.BlockSpec(shape, lambda *_: (0,) * nd)


def _rope_table_kernel(pos_ref, invf_ref, mrot_ref, ma_ref, mb_ref, c_ref, sa_ref, sb_ref):
    ang = pos_ref[...] * invf_ref[...]
    c = jnp.cos(ang)
    s = jnp.sin(ang)
    c_ref[...] = jnp.where(mrot_ref[...] > 0, c, 1.0)
    sa_ref[...] = jnp.where(ma_ref[...] > 0, -s, 0.0)
    sb_ref[...] = jnp.where(mb_ref[...] > 0, s, 0.0)


def _rope_tables(pos_col, head_dim, rot):
    t = pos_col.shape[0]
    half = rot // 2
    inv_freq = jnp.float32(ROPE_THETA) ** (-jnp.arange(half, dtype=F32) * 2.0 / rot)
    lane = np.arange(LANES) % head_dim
    mrot = lane < rot
    ma = lane < half
    mb = (lane >= half) & (lane < rot)
    invf = jnp.where(jnp.asarray(mrot), inv_freq[np.where(mrot, lane % half, 0)], 0.0)[None, :]
    row = lambda m: jnp.asarray(m.astype(np.float32))[None, :]
    tm = min(1024, t)
    spec_row = _full((1, LANES))
    out_spec = pl.BlockSpec((tm, LANES), lambda i: (i, 0))
    return pl.pallas_call(
        _rope_table_kernel,
        grid=(t // tm,),
        in_specs=[pl.BlockSpec((tm, 1), lambda i: (i, 0)), spec_row, spec_row, spec_row, spec_row],
        out_specs=[out_spec] * 3,
        out_shape=[jax.ShapeDtypeStruct((t, LANES), F32)] * 3,
        compiler_params=_params(),
        name="rope_tables",
    )(pos_col, invf, row(mrot), row(ma), row(mb))


def _rope_table_t_kernel(pos_ref, invf_ref, c_ref, s_ref):
    ang = invf_ref[...] * pos_ref[...]
    c_ref[...] = jnp.cos(ang)
    s_ref[...] = jnp.sin(ang)


def _rope_tables_t(pos_row, rot):
    t = pos_row.shape[1]
    half = rot // 2
    inv_freq = jnp.float32(ROPE_THETA) ** (-jnp.arange(half, dtype=F32) * 2.0 / rot)
    tm = min(2048, t)
    out_spec = pl.BlockSpec((half, tm), lambda i: (0, i))
    return pl.pallas_call(
        _rope_table_t_kernel,
        grid=(t // tm,),
        in_specs=[pl.BlockSpec((1, tm), lambda i: (0, i)), _full((half, 1))],
        out_specs=[out_spec] * 2,
        out_shape=[jax.ShapeDtypeStruct((half, t), F32)] * 2,
        compiler_params=_params(),
        name="rope_tables_t",
    )(pos_row, inv_freq[:, None])


def _rope_t(x, c, s):
    half = c.shape[0]
    x1, x2 = x[:half, :], x[half:2 * half, :]
    return jnp.concatenate([x1 * c - x2 * s, x1 * s + x2 * c, x[2 * half:, :]], axis=0)


def _rope(x, c, sa, sb, half):
    parts = []
    for j in range(x.shape[1] // LANES):
        xj = x[:, j * LANES:(j + 1) * LANES]
        parts.append(xj * c + pltpu.roll(xj, LANES - half, 1) * sa + pltpu.roll(xj, half, 1) * sb)
    return parts


WI_ROWS = 16


def _proj_b_kernel(x_ref, g_ref, w_ref, wt_ref, cb_ref, sab_ref, sbb_ref, ci_ref, sai_ref, sbi_ref,
                   cbt_ref, sbt_ref, cit_ref, sit_ref, k_ref, ki_ref, qt_ref, vt_ref, qit_ref, wit_ref):
    h = _rms(x_ref[...], g_ref[...]).astype(BF16)
    tm = h.shape[0]
    p = jnp.dot(h, w_ref[...], preferred_element_type=F32)
    pt = lax.dot_general(wt_ref[...], h, (((1,), (1,)), ((), ())), preferred_element_type=F32)
    tb = (cb_ref[...], sab_ref[...], sbb_ref[...])
    ti = (ci_ref[...], sai_ref[...], sbi_ref[...])
    for j, kj in enumerate(_rope(p[:, :B_WIDTH], *tb, B_ROT // 2)):
        k_ref[:, j * LANES:(j + 1) * LANES] = kj.astype(BF16)
    ki_ref[...] = _rope(p[:, B_WIDTH:B_WIDTH + LANES], *ti, IDX_ROT // 2)[0].astype(BF16)
    cbt, sbt = cbt_ref[...], sbt_ref[...]
    for hh in range(B_HEADS):
        hs = slice(hh * B_HEAD_DIM, (hh + 1) * B_HEAD_DIM)
        qt_ref[hs, :] = (_rope_t(pt[hs, :], cbt, sbt) * (B_HEAD_DIM ** -0.5 * LOG2E)).astype(BF16)
    o = B_WIDTH
    for hh in range(B_HEADS):
        vt_ref[hh * V_ROWS:hh * V_ROWS + B_HEAD_DIM, :] = (
            pt[o + hh * B_HEAD_DIM:o + (hh + 1) * B_HEAD_DIM, :].astype(BF16))
        vt_ref[hh * V_ROWS + B_HEAD_DIM:(hh + 1) * V_ROWS, :] = jnp.ones((V_ROWS - B_HEAD_DIM, tm), BF16)
    o += B_WIDTH
    cit, sit = cit_ref[...], sit_ref[...]
    for hh in range(IDX_HEADS):
        hs = slice(hh * IDX_DIM, (hh + 1) * IDX_DIM)
        qit_ref[hs, :] = _rope_t(pt[o + hh * IDX_DIM:o + (hh + 1) * IDX_DIM, :], cit, sit).astype(BF16)
    o += IDX_HEADS * IDX_DIM
    wit_ref[...] = pt[o:o + SUBLANES, :] * ((IDX_HEADS ** -0.5) * (IDX_DIM ** -0.5))


def _proj_b(x2, g, w, wt, tabs_b, tabs_i, tabs_bt, tabs_it):
    t = x2.shape[0]
    tm = min(TM_PROJ, t)
    tok = lambda n: pl.BlockSpec((tm, n), lambda i: (i, 0))
    feat = lambda n: pl.BlockSpec((n, tm), lambda i: (0, i))
    outs_tok = [(B_WIDTH, BF16), (LANES, BF16)]
    outs_feat = [(B_WIDTH, BF16), (B_HEADS * V_ROWS, BF16), (IDX_HEADS * IDX_DIM, BF16), (SUBLANES, F32)]
    return pl.pallas_call(
        _proj_b_kernel,
        grid=(t // tm,),
        in_specs=[tok(D_MODEL), _full((1, D_MODEL)), _full(w.shape), _full(wt.shape)] + [tok(LANES)] * 6
                 + [feat(B_ROT // 2)] * 2 + [feat(IDX_ROT // 2)] * 2,
        out_specs=[tok(n) for n, _ in outs_tok] + [feat(n) for n, _ in outs_feat],
        out_shape=[jax.ShapeDtypeStruct((t, n), d) for n, d in outs_tok]
                  + [jax.ShapeDtypeStruct((n, t), d) for n, d in outs_feat],
        compiler_params=_params(),
        name="proj_b",
    )(x2, g, w, wt, *tabs_b, *tabs_i, *tabs_bt, *tabs_it)


def _ordered_key(score):
    bits = lax.bitcast_convert_type(score, I32)
    key = bits ^ ((bits >> 31) & 0x7FFFFFFF)
    return jnp.where(key == -1, 0, key)


def _score_of_key(key):
    return lax.bitcast_convert_type(key ^ ((key >> 31) & 0x7FFFFFFF), F32)


def _ce_desc(vals, i, j):
    a, b = vals[i], vals[j]
    vals[i] = jnp.maximum(a, b)
    vals[j] = jnp.minimum(a, b)


def _sort_desc(vals):
    n = len(vals)
    p = 1
    while p < n:
        k = p
        while k >= 1:
            for j in range(k % p, n - k, 2 * k):
                for i in range(min(k, n - j - k)):
                    if (i + j) // (2 * p) == (i + j + k) // (2 * p):
                        _ce_desc(vals, i + j, i + j + k)
            k //= 2
        p *= 2


def _bitonic_merge_desc(vals):
    n = len(vals)
    j = n // 2
    while j >= 1:
        for i in range(n):
            l = i ^ j
            if l > i:
                _ce_desc(vals, i, l)
        j //= 2


def _dsa_kernel(qit_ref, wit_ref, ki_ref, qt_ref, k_ref, vt_ref, o_ref, sc_ref, top_ref, topk_ref, acc_ref,
                bias0_ref, bias1_ref, lg0_ref, lg1_ref,
                *, seq, n_sel, tq, kc, skc, sub, rb):
    i = pl.program_id(1)
    q0 = i * tq
    n_full = q0 // skc
    n_rows = (n_full + 1) * skc
    n_pad = (seq - 1 - q0) - lax.broadcasted_iota(I32, (1, tq), 1)
    n_chunk = (q0 + tq + kc - 1) // kc
    wit = wit_ref[...]

    n_top = skc // SUBLANES
    top_ref[...] = jnp.full((skc, tq), -jnp.inf, F32)

    def score_batch(c, masked):
        vals = []
        for s in range(skc // sub):
            r0 = pl.multiple_of(c * skc + s * sub, sub)
            kic = ki_ref[pl.ds(r0, sub), :IDX_DIM]
            sc = None
            for h in range(IDX_HEADS):
                a = jnp.dot(kic, qit_ref[h * IDX_DIM:(h + 1) * IDX_DIM, :], preferred_element_type=F32)
                term = jnp.maximum(a, 0.0) * wit[h:h + 1, :]
                sc = term if sc is None else sc + term
            if masked:
                qpos = q0 + lax.broadcasted_iota(I32, (sub, tq), 1)
                krow = r0 + lax.broadcasted_iota(I32, (sub, tq), 0)
                sc = jnp.where(krow <= qpos, sc, -jnp.inf)
            sc_ref[pl.ds(r0, sub), :] = sc
            vals += [sc[SUBLANES * i:SUBLANES * (i + 1), :] for i in range(sub // SUBLANES)]
        _sort_desc(vals)
        return vals

    def keep_top(batches):
        top = [top_ref[SUBLANES * i:SUBLANES * (i + 1), :] for i in range(n_top)]
        for vals in batches:
            top = [jnp.maximum(top[i], vals[n_top - 1 - i]) for i in range(n_top)]
            _bitonic_merge_desc(top)
        for i in range(n_top):
            top_ref[SUBLANES * i:SUBLANES * (i + 1), :] = top[i]

    def score_pair(j, _):
        keep_top([score_batch(2 * j, False), score_batch(2 * j + 1, False)])
        return 0

    lax.fori_loop(0, n_full // 2, score_pair, 0)

    @pl.when(n_full % 2 == 1)
    def _():
        keep_top([score_batch(n_full - 1, False)])

    keep_top([score_batch(n_full, True)])

    def search(count):
        def bit_body(b, carry):
            ans_u, = carry
            cand_u = ans_u | jnp.left_shift(jnp.int32(1), 31 - b)
            ok = count(cand_u ^ INT_MIN, False) >= n_sel
            return (jnp.where(ok, cand_u, ans_u),)

        ans_u, = lax.fori_loop(0, 32, bit_body, (jnp.zeros((1, tq), I32),))
        thr = ans_u ^ INT_MIN
        return thr, (n_sel - count(thr, True)).astype(F32)

    def pads(cand, strict):
        return jnp.where((NEG_KEY > cand) if strict else (NEG_KEY >= cand), n_pad, 0)

    topk_ref[...] = _ordered_key(top_ref[...])

    def count_top(cand, strict):
        t = topk_ref[...]
        hit = (t > cand) if strict else (t >= cand)
        return jnp.sum(jnp.where(hit, 1.0, 0.0), axis=0, keepdims=True).astype(I32) + pads(cand, strict)

    def search_top():
        def digit_body(b, carry):
            ans_u, = carry
            shift = 30 - 2 * b
            for d in (1, 2, 3):
                cand_u = carry[0] | jnp.left_shift(jnp.int32(d), shift)
                ans_u = jnp.where(count_top(cand_u ^ INT_MIN, False) >= n_sel, cand_u, ans_u)
            return (ans_u,)

        ans_u, = lax.fori_loop(0, 16, digit_body, (jnp.zeros((1, tq), I32),))
        thr = ans_u ^ INT_MIN
        return thr, (n_sel - count_top(thr, True)).astype(F32)

    def count_all(cand, strict):
        acc_rows = 4 * SUBLANES
        cand_f = _score_of_key(cand)

        def body(r, acc):
            r0 = pl.multiple_of(r * rb, rb)
            blk = sc_ref[pl.ds(r0, rb), :]
            for j in range(rb // acc_rows):
                part = blk[j * acc_rows:(j + 1) * acc_rows, :]
                hit = (part > cand_f) if strict else (part >= cand_f)
                acc = acc + jnp.where(hit, 1, 0)
            return acc

        acc = lax.fori_loop(0, n_rows // rb, body, jnp.zeros((acc_rows, tq), I32))
        return jnp.sum(acc.astype(F32), axis=0, keepdims=True).astype(I32) + pads(cand, strict)

    thr, need = search_top()
    last = topk_ref[skc - SUBLANES:skc, :]
    overflow = jnp.max(jnp.where(last > thr, 1.0, 0.0))
    thr, need = lax.cond(overflow > 0.0, lambda: search(count_all), lambda: (thr, need))
    thr_f = _score_of_key(thr)

    acc_ref[...] = jnp.zeros_like(acc_ref)
    bias_refs = (bias0_ref, bias1_ref)
    lg_refs = (lg0_ref, lg1_ref)
    hk = kc // 2
    lower = (lax.broadcasted_iota(I32, (hk, hk), 1) < lax.broadcasted_iota(I32, (hk, hk), 0))
    lower = jnp.where(lower, 1.0, 0.0).astype(BF16)

    def produce(c, slot, tie_base):
        r0 = pl.multiple_of(c * kc, kc)
        sc = sc_ref[pl.ds(r0, kc), :]
        tie = sc == thr_f
        tie_f = jnp.where(tie, 1.0, 0.0)
        tie_b = tie_f.astype(BF16)
        b0 = jnp.dot(lower, tie_b[:hk, :], preferred_element_type=F32)
        n0 = b0[hk - 1:hk, :] + tie_f[hk - 1:hk, :]
        b1 = jnp.dot(lower, tie_b[hk:, :], preferred_element_type=F32) + n0
        before = jnp.concatenate([b0, b1], axis=0)
        sel = (sc > thr_f) | (tie & (before < need - tie_base))
        bias_refs[slot][...] = jnp.where(sel, 0.0, NEG)
        for h in range(B_HEADS):
            hs = slice(h * B_HEAD_DIM, (h + 1) * B_HEAD_DIM)
            lg_refs[slot][h] = jnp.dot(k_ref[pl.ds(r0, kc), hs], qt_ref[hs, :], preferred_element_type=F32)
        return tie_base + b1[hk - 1:hk, :] + tie_f[kc - 1:kc, :]

    def consume(c, slot, ms):
        r0 = pl.multiple_of(c * kc, kc)
        bias = bias_refs[slot][...]
        new_ms = []
        for h in range(B_HEADS):
            vs = slice(h * V_ROWS, (h + 1) * V_ROWS)
            lg = lg_refs[slot][h] + bias
            m_new = jnp.maximum(ms[h], jnp.max(lg, axis=0, keepdims=True))
            alpha = jnp.exp2(ms[h] - m_new)
            p = jnp.exp2(lg - m_new).astype(BF16)
            pv = jnp.dot(vt_ref[vs, pl.ds(r0, kc)], p, preferred_element_type=F32)
            acc_ref[vs, :] = acc_ref[vs, :] * alpha + pv
            new_ms.append(m_new)
        return tuple(new_ms)

    def pair_body(j, carry):
        ms, tie_base = carry
        c = 2 * j
        tie_base = produce(c + 1, 1, tie_base)
        ms = consume(c, 0, ms)
        tie_base = produce(c + 2, 0, tie_base)
        ms = consume(c + 1, 1, ms)
        return ms, tie_base

    row = lambda v: jnp.full((1, tq), v, F32)
    ms0 = tuple(row(NEG) for _ in range(B_HEADS))
    n_pairs = (n_chunk - 1) // 2
    ms, tie_base = lax.fori_loop(0, n_pairs, pair_body, (ms0, produce(0, 0, row(0.0))))
    c0 = 2 * n_pairs

    @pl.when(c0 == n_chunk - 1)
    def _():
        consume(c0, 0, ms)

    @pl.when(c0 != n_chunk - 1)
    def _():
        produce(c0 + 1, 1, tie_base)
        consume(c0 + 1, 1, consume(c0, 0, ms))

    for h in range(B_HEADS):
        a = acc_ref[h * V_ROWS:(h + 1) * V_ROWS, :]
        o_ref[h * B_HEAD_DIM:(h + 1) * B_HEAD_DIM, :] = (
            a[:B_HEAD_DIM, :] / a[B_HEAD_DIM:B_HEAD_DIM + 1, :]).astype(o_ref.dtype)


def _dsa(qit, wit, ki, qt, k, vt_aug, bsz, seq, n_sel):
    tq = min(DSA_TQ, seq)
    kc = min(DSA_KC, seq)
    skc = min(DSA_SKC, seq)
    nq = seq // tq
    once = pl.Buffered(1)
    kern = functools.partial(_dsa_kernel, seq=seq, n_sel=n_sel, tq=tq, kc=kc, skc=skc,
                             sub=min(DSA_SUB, skc), rb=min(DSA_RB, skc))
    q_tile = lambda rows: pl.BlockSpec((rows, tq), lambda b, i: (0, b * nq + i))
    return pl.pallas_call(
        kern,
        grid=(bsz, nq),
        in_specs=[
            q_tile(IDX_HEADS * IDX_DIM),
            q_tile(SUBLANES),
            pl.BlockSpec((seq, LANES), lambda b, i: (b, 0), pipeline_mode=once),
            q_tile(B_WIDTH),
            pl.BlockSpec((seq, B_WIDTH), lambda b, i: (b, 0), pipeline_mode=once),
            pl.BlockSpec((B_HEADS * V_ROWS, seq), lambda b, i: (0, b), pipeline_mode=once),
        ],
        out_specs=q_tile(B_WIDTH),
        out_shape=jax.ShapeDtypeStruct((B_WIDTH, bsz * seq), BF16),
        scratch_shapes=[pltpu.VMEM((seq, tq), F32), pltpu.VMEM((skc, tq), F32), pltpu.VMEM((skc, tq), I32),
                        pltpu.VMEM((B_HEADS * V_ROWS, tq), F32), pltpu.VMEM((kc, tq), F32),
                        pltpu.VMEM((kc, tq), F32), pltpu.VMEM((B_HEADS, kc, tq), F32),
                        pltpu.VMEM((B_HEADS, kc, tq), F32)],
        compiler_params=_params(2),
        name="dsa",
    )(qit, wit, ki, qt, k, vt_aug)


def _proj_a_kernel(x_ref, g_ref, w_ref, ws_ref, bs_ref, ya_ref, *, tm):
    h = _rms(x_ref[...], g_ref[...]).astype(BF16)
    p = jnp.dot(h, w_ref[...], preferred_element_type=F32)
    u = jax.nn.gelu(p[:, :A_WIDTH])
    v = jax.nn.gelu(p[:, A_WIDTH:])
    mu = jnp.mean(v, axis=-1, keepdims=True)
    var = jnp.mean(jnp.square(v - mu), axis=-1, keepdims=True)
    v = ((v - mu) * lax.rsqrt(var + EPS)).astype(BF16)
    causal = (lax.broadcasted_iota(I32, (CHUNK, CHUNK), 1) <= lax.broadcasted_iota(I32, (CHUNK, CHUNK), 0))
    bs = bs_ref[...]
    for gidx in range(A_GROUPS):
        wsm = jnp.where(causal, ws_ref[gidx], 0.0).astype(BF16)
        cs = slice(gidx * A_GROUP_DIM, (gidx + 1) * A_GROUP_DIM)
        for n in range(tm // CHUNK):
            rs = slice(n * CHUNK, (n + 1) * CHUNK)
            vm = jnp.dot(wsm, v[rs, cs], preferred_element_type=F32) + bs[:, gidx:gidx + 1]
            ya_ref[rs, cs] = (u[rs, cs] * vm).astype(ya_ref.dtype)


def _proj_a(x2, g, w, ws, bs_t):
    t = x2.shape[0]
    tm = min(TM_PROJ, t)
    return pl.pallas_call(
        functools.partial(_proj_a_kernel, tm=tm),
        grid=(t // tm,),
        in_specs=[pl.BlockSpec((tm, D_MODEL), lambda i: (i, 0)), _full((1, D_MODEL)),
                  _full((D_MODEL, 2 * A_WIDTH)), _full((A_GROUPS, CHUNK, CHUNK)), _full((CHUNK, A_GROUPS))],
        out_specs=pl.BlockSpec((tm, A_WIDTH), lambda i: (i, 0)),
        out_shape=jax.ShapeDtypeStruct((t, A_WIDTH), BF16),
        compiler_params=_params(),
        name="proj_a",
    )(x2, g, w, ws, bs_t)


def _proj_c_kernel(x_ref, xh_ref, g_ref, w_ref, cw_ref, yc_ref, *, tm, tiles_per_seq):
    g = g_ref[...]
    w = w_ref[...]
    p = jnp.dot(_rms(x_ref[...], g).astype(BF16), w, preferred_element_type=F32)
    ph = jnp.dot(_rms(xh_ref[...], g).astype(BF16), w[:, C_WIDTH:], preferred_element_type=F32)
    first = (pl.program_id(0) % tiles_per_seq) == 0
    zh = ph[:, :C_WIDTH] * ph[:, C_WIDTH:] * jnp.where(first, 0.0, 1.0)
    z = p[:, C_WIDTH:2 * C_WIDTH] * p[:, 2 * C_WIDTH:]
    row = lax.broadcasted_iota(I32, (tm, C_WIDTH), 0)
    z1 = jnp.where(row == 0, zh[SUBLANES - 1:SUBLANES, :], pltpu.roll(z, 1, 0))
    z2 = jnp.where(row == 0, zh[SUBLANES - 2:SUBLANES - 1, :],
                   jnp.where(row == 1, zh[SUBLANES - 1:SUBLANES, :], pltpu.roll(z, 2, 0)))
    cw = cw_ref[...]
    y = cw[0:1, :] * z2 + cw[1:2, :] * z1 + cw[2:3, :] * z
    yc_ref[...] = (p[:, :C_WIDTH] * y).astype(yc_ref.dtype)


def _proj_c(x2, g, w, cw, seq):
    t = x2.shape[0]
    tm = min(TM_PROJ, seq)
    per8 = tm // SUBLANES
    return pl.pallas_call(
        functools.partial(_proj_c_kernel, tm=tm, tiles_per_seq=seq // tm),
        grid=(t // tm,),
        in_specs=[pl.BlockSpec((tm, D_MODEL), lambda i: (i, 0)),
                  pl.BlockSpec((SUBLANES, D_MODEL), lambda i: (jnp.maximum(i * per8 - 1, 0), 0)),
                  _full((1, D_MODEL)), _full((D_MODEL, 3 * C_WIDTH)), _full((CONV_WIDTH, C_WIDTH))],
        out_specs=pl.BlockSpec((tm, C_WIDTH), lambda i: (i, 0)),
        out_shape=jax.ShapeDtypeStruct((t, C_WIDTH), BF16),
        compiler_params=_params(),
        name="proj_c",
    )(x2, x2, g, w, cw)


def _merge_kernel(x_ref, g_ref, wg_ref, ya_ref, ybt_ref, yc_ref, wa_ref, wb_ref, wc_ref, wo_ref, o_ref):
    x = x_ref[...]
    h = _rms(x, g_ref[...]).astype(BF16)
    projected = (
        jnp.dot(ya_ref[...], wa_ref[...], preferred_element_type=F32),
        lax.dot_general(ybt_ref[...], wb_ref[...], (((0,), (0,)), ((), ())), preferred_element_type=F32),
        jnp.dot(yc_ref[...], wc_ref[...], preferred_element_type=F32),
    )
    merged = None
    for j, proj in enumerate(projected):
        gate = jnp.dot(h, wg_ref[:, j * D_MODEL:(j + 1) * D_MODEL], preferred_element_type=F32)
        term = jax.nn.sigmoid(gate) * proj
        merged = term if merged is None else merged + term
    o_ref[...] = x + jnp.dot(merged.astype(BF16), wo_ref[...], preferred_element_type=F32)


def _merge(x2, g, wg, ya, ybt, yc, wa, wb, wc, wo):
    t = x2.shape[0]
    tm = min(TM_MERGE, t)
    tok = lambda n: pl.BlockSpec((tm, n), lambda i: (i, 0))
    wproj = _full((A_WIDTH, D_MODEL))
    return pl.pallas_call(
        _merge_kernel,
        grid=(t // tm,),
        in_specs=[tok(D_MODEL), _full((1, D_MODEL)), _full((D_MODEL, N_BRANCH * D_MODEL)),
                  tok(A_WIDTH), pl.BlockSpec((B_WIDTH, tm), lambda i: (0, i)), tok(C_WIDTH),
                  wproj, wproj, wproj, _full((D_MODEL, D_MODEL))],
        out_specs=tok(D_MODEL),
        out_shape=jax.ShapeDtypeStruct((t, D_MODEL), F32),
        compiler_params=_params(),
        name="merge",
    )(x2, g, wg, ya, ybt, yc, wa, wb, wc, wo)


def _router_kernel(x_ref, g_ref, w_ref, b_ref, h_ref, eid_ref, wt_ref):
    h = _rms(x_ref[...], g_ref[...])
    h_ref[...] = h
    logits = jnp.dot(h, w_ref[...], preferred_element_type=F32, precision=lax.Precision.HIGHEST) + b_ref[...]
    lane = lax.broadcasted_iota(I32, logits.shape, 1)
    lane_f = lane.astype(F32)
    ninf = -jnp.inf

    def first_max(vals):
        vmax = jnp.max(vals, axis=1, keepdims=True)
        idx = jnp.min(jnp.where(vals == vmax, lane_f, float(LANES)), axis=1, keepdims=True)
        return vmax, idx

    gl = jnp.where(lane < N_GROUPS, logits, ninf)
    gmax, grp = first_max(gl)
    gsum = jnp.sum(jnp.where(lane < N_GROUPS, jnp.exp(logits - gmax), 0.0), axis=1, keepdims=True)
    gw = 1.0 / gsum
    e_lane = lane - N_GROUPS
    in_grp = (e_lane >= 0) & (e_lane < N_EXPERTS) & ((e_lane >> 3).astype(F32) == grp)
    el = jnp.where(in_grp, logits, ninf)
    v1, i1 = first_max(el)
    v2, i2 = first_max(jnp.where(lane_f == i1, ninf, el))
    e2 = jnp.exp(v2 - v1)
    den = 1.0 + e2
    w1 = (1.0 / den) * gw
    w2 = (e2 / den) * gw
    eid1 = (i1 - N_GROUPS).astype(I32)
    eid2 = (i2 - N_GROUPS).astype(I32)
    eid_ref[...] = jnp.where(lane == 0, eid1, jnp.where(lane == 1, eid2, 0))
    wt_ref[...] = jnp.where(lane == 0, w1, jnp.where(lane == 1, w2, 0.0))


def _router(x2, g, w, b):
    t = x2.shape[0]
    tm = min(TM_ROUTE, t)
    tok = lambda n: pl.BlockSpec((tm, n), lambda i: (i, 0))
    return pl.pallas_call(
        _router_kernel,
        grid=(t // tm,),
        in_specs=[tok(D_MODEL), _full((1, D_MODEL)), _full((D_MODEL, LANES)), _full((1, LANES))],
        out_specs=[tok(D_MODEL), tok(LANES), tok(LANES)],
        out_shape=[jax.ShapeDtypeStruct((t, D_MODEL), F32), jax.ShapeDtypeStruct((t, LANES), I32),
                   jax.ShapeDtypeStruct((t, LANES), F32)],
        compiler_params=_params(),
        name="router",
    )(x2, g, w, b)


def _rank_kernel(eid_ref, rank_ref, cnt_ref, *, tm):
    @pl.when(pl.program_id(0) == 0)
    def _():
        cnt_ref[...] = jnp.zeros_like(cnt_ref)

    eid = eid_ref[...]
    lane = lax.broadcasted_iota(I32, (tm, LANES), 1)
    o1 = lane == eid[:, 0:1]
    o2 = lane == eid[:, 1:2]
    both = jnp.where(o1 | o2, 1.0, 0.0)
    lower = lax.broadcasted_iota(I32, (tm, tm), 1) < lax.broadcasted_iota(I32, (tm, tm), 0)
    before = jnp.dot(jnp.where(lower, 1.0, 0.0).astype(BF16), both.astype(BF16),
                     preferred_element_type=F32) + cnt_ref[...]
    r1 = jnp.sum(jnp.where(o1, before, 0.0), axis=1, keepdims=True).astype(I32)
    r2 = jnp.sum(jnp.where(o2, before, 0.0), axis=1, keepdims=True).astype(I32)
    rank_ref[...] = jnp.where(lane == 0, r1, jnp.where(lane == 1, r2, 0))
    cnt_ref[...] = cnt_ref[...] + jnp.sum(both, axis=0, keepdims=True)


def _rank(eid):
    t = eid.shape[0]
    tm = min(TM_RANK, t)
    return pl.pallas_call(
        functools.partial(_rank_kernel, tm=tm),
        grid=(t // tm,),
        in_specs=[pl.BlockSpec((tm, LANES), lambda i: (i, 0))],
        out_specs=[pl.BlockSpec((tm, LANES), lambda i: (i, 0)), _full((1, LANES))],
        out_shape=[jax.ShapeDtypeStruct((t, LANES), I32), jax.ShapeDtypeStruct((1, LANES), F32)],
        compiler_params=_params(),
        name="rank",
    )(eid)


def _dest_kernel(eid_ref, rank_ref, start_ref, d1_ref, d2_ref):
    eid = eid_ref[...]
    lane = lax.broadcasted_iota(I32, eid.shape, 1)
    start = start_ref[...]
    s1 = jnp.sum(jnp.where(lane == eid[:, 0:1], start, 0.0), axis=1, keepdims=True).astype(I32)
    s2 = jnp.sum(jnp.where(lane == eid[:, 1:2], start, 0.0), axis=1, keepdims=True).astype(I32)
    dest = jnp.where(lane == 0, s1, jnp.where(lane == 1, s2, 0)) + rank_ref[...]
    dest_t = jnp.transpose(dest.astype(F32))[:SUBLANES, :].astype(I32)
    d1_ref[...] = dest_t[0:1, :]
    d2_ref[...] = dest_t[1:2, :]


def _dest(eid, rank, start_row):
    t = eid.shape[0]
    tm = min(TM_ROUTE, t)
    tok = pl.BlockSpec((tm, LANES), lambda i: (i, 0))
    return pl.pallas_call(
        _dest_kernel,
        grid=(t // tm,),
        in_specs=[tok, tok, _full((1, LANES))],
        out_specs=[pl.BlockSpec((1, tm), lambda i: (0, i))] * 2,
        out_shape=[jax.ShapeDtypeStruct((1, t), I32)] * 2,
        compiler_params=_params(),
        name="dest",
    )(eid, rank, start_row)


def _row_copy(src_ref, src_row, dst_ref, dst_row, sem):
    return pltpu.make_async_copy(src_ref.at[pl.ds(src_row, 1)], dst_ref.at[pl.ds(dst_row, 1)], sem)


def _rows_copy(src_ref, dst_ref, n, sem):
    return pltpu.make_async_copy(src_ref.at[pl.ds(0, n)], dst_ref.at[pl.ds(0, n)], sem)


def _scatter_kernel(d1_ref, d2_ref, h_ref, xs_in_ref, xs_ref, sem, *, tm):
    del xs_in_ref

    def issue(t, _):
        for p, d_ref in enumerate((d1_ref, d2_ref)):
            _row_copy(h_ref, t, xs_ref, d_ref[0, t], sem).start(priority=p)
        return 0

    lax.fori_loop(0, tm, issue, 0, unroll=ISSUE_UNROLL)
    for _ in range(2):
        _rows_copy(h_ref, xs_ref, tm, sem).wait()


def _scatter_rows(dests, h, n_rows):
    t = h.shape[0]
    tm = min(TM_MOVE, t)
    xs0 = jnp.zeros((n_rows, D_MODEL), F32)
    return pl.pallas_call(
        functools.partial(_scatter_kernel, tm=tm),
        grid=(t // tm,),
        in_specs=[pl.BlockSpec((1, tm), lambda i: (0, i), memory_space=pltpu.SMEM)] * 2
                 + [pl.BlockSpec((tm, D_MODEL), lambda i: (i, 0)), pl.BlockSpec(memory_space=pl.ANY)],
        out_specs=pl.BlockSpec(memory_space=pl.ANY),
        out_shape=jax.ShapeDtypeStruct((n_rows, D_MODEL), F32),
        scratch_shapes=[pltpu.SemaphoreType.DMA(())],
        input_output_aliases={3: 0},
        compiler_params=_params(),
        name="scatter_rows",
    )(*dests, h, xs0)


def _expert_kernel(be_ref, nused_ref, xs_ref, wg_ref, wu_ref, wd_ref, y_ref, wg_bf, wu_bf, wd_bf):
    b = pl.program_id(0)

    @pl.when((b == 0) | (be_ref[b] != be_ref[jnp.maximum(b - 1, 0)]))
    def _():
        wg_bf[...] = wg_ref[...].astype(BF16)
        wu_bf[...] = wu_ref[...].astype(BF16)
        wd_bf[...] = wd_ref[...].astype(BF16)

    @pl.when(b < nused_ref[0])
    def _():
        xb = xs_ref[...].astype(BF16)
        gate = jnp.dot(xb, wg_bf[...], preferred_element_type=F32)
        up = jnp.dot(xb, wu_bf[...], preferred_element_type=F32)
        hid = (jax.nn.silu(gate) * up).astype(BF16)
        y_ref[...] = jnp.dot(hid, wd_bf[...], preferred_element_type=F32)

    @pl.when(b >= nused_ref[0])
    def _():
        y_ref[...] = jnp.zeros_like(y_ref)


def _experts(block_exp, n_used, xs, wg, wu, wd, layer):
    n_rows = xs.shape[0]
    w_in_spec = pl.BlockSpec((None, None, D_MODEL, D_FF_EXPERT), lambda b, be, nu: (layer, be[b], 0, 0))
    grid_spec = pltpu.PrefetchScalarGridSpec(
        num_scalar_prefetch=2,
        grid=(n_rows // EXP_BLOCK,),
        in_specs=[pl.BlockSpec((EXP_BLOCK, D_MODEL), lambda b, be, nu: (b, 0)), w_in_spec, w_in_spec,
                  pl.BlockSpec((None, None, D_FF_EXPERT, D_MODEL), lambda b, be, nu: (layer, be[b], 0, 0))],
        out_specs=pl.BlockSpec((EXP_BLOCK, D_MODEL), lambda b, be, nu: (b, 0)),
        scratch_shapes=[pltpu.VMEM((D_MODEL, D_FF_EXPERT), BF16), pltpu.VMEM((D_MODEL, D_FF_EXPERT), BF16),
                        pltpu.VMEM((D_FF_EXPERT, D_MODEL), BF16)],
    )
    return pl.pallas_call(
        _expert_kernel,
        grid_spec=grid_spec,
        out_shape=jax.ShapeDtypeStruct((n_rows, D_MODEL), F32),
        compiler_params=_params(),
        name="experts",
    )(block_exp, n_used, xs, wg, wu, wd)


def _combine_kernel(d1_ref, d2_ref, d1_next_ref, d2_next_ref, x_ref, wt_ref, g_ref, y_ref, o_ref, buf_ref, sems,
                    *, tm, steps, final_norm):
    i = pl.program_id(0)

    def gather(idx_refs, slot):
        def issue(t, _):
            for j, idx_ref in enumerate(idx_refs):
                _row_copy(y_ref, idx_ref[0, t], buf_ref.at[slot], j * tm + t, sems.at[slot]).start(priority=j)
            return 0

        lax.fori_loop(0, tm, issue, 0, unroll=ISSUE_UNROLL)

    @pl.when(i == 0)
    def _():
        gather((d1_ref, d2_ref), 0)

    @pl.when(i + 1 < steps)
    def _():
        gather((d1_next_ref, d2_next_ref), (i + 1) % 2)

    slot = i % 2
    _rows_copy(y_ref, buf_ref.at[slot], 2 * tm, sems.at[slot]).wait()
    wt = wt_ref[...]
    rows = buf_ref[slot]
    out = x_ref[...] + (wt[:, 0:1] * rows[:tm, :] + wt[:, 1:2] * rows[tm:, :])
    if final_norm:
        out = _rms(out, g_ref[...])
    o_ref[...] = out


def _combine(dests, x2, wt, g_final, y, final_norm):
    t = x2.shape[0]
    tm = min(TM_MOVE, t)
    steps = t // tm
    tok = lambda n: pl.BlockSpec((tm, n), lambda i: (i, 0))
    idx = lambda f: pl.BlockSpec((1, tm), f, memory_space=pltpu.SMEM)
    return pl.pallas_call(
        functools.partial(_combine_kernel, tm=tm, steps=steps, final_norm=final_norm),
        grid=(steps,),
        in_specs=[idx(lambda i: (0, i))] * 2 + [idx(lambda i: (0, jnp.minimum(i + 1, steps - 1)))] * 2
                 + [tok(D_MODEL), tok(LANES), _full((1, D_MODEL)), pl.BlockSpec(memory_space=pl.ANY)],
        out_specs=tok(D_MODEL),
        out_shape=jax.ShapeDtypeStruct((t, D_MODEL), F32),
        scratch_shapes=[pltpu.VMEM((2, 2 * tm, D_MODEL), F32), pltpu.SemaphoreType.DMA((2,))],
        compiler_params=_params(),
        name="combine",
    )(*dests, *dests, x2, wt, g_final, y)


W_IN_SIZES = (A_WIDTH, A_WIDTH, B_WIDTH, B_WIDTH, B_WIDTH, IDX_HEADS * IDX_DIM, IDX_DIM, IDX_HEADS,
              C_WIDTH, C_WIDTH, C_WIDTH, N_BRANCH * D_MODEL)
W_IN_CUTS = tuple(int(c) for c in np.cumsum((0,) + W_IN_SIZES))
W_BT_ROWS = 2 * B_WIDTH + IDX_HEADS * IDX_DIM + WI_ROWS


def _repack_kernel(w_ref, wa_ref, wb_ref, wbt_ref, wc_ref, wg_ref):
    x = w_ref[...]
    u, va, q, k, vb, qi, ki, wi, cb, cc, cx, gates = [
        x[:, a:b] for a, b in zip(W_IN_CUTS[:-1], W_IN_CUTS[1:])]
    for j, part in enumerate((u, va)):
        wa_ref[:, j * A_WIDTH:(j + 1) * A_WIDTH] = part.astype(BF16)
    wb_ref[:, :B_WIDTH] = k.astype(BF16)
    ki_pad = jnp.concatenate([ki, jnp.zeros((ki.shape[0], LANES - IDX_DIM), F32)], axis=1)
    wb_ref[:, B_WIDTH:] = ki_pad.astype(BF16)
    row = 0
    for part in (q, vb, qi):
        for j in range(part.shape[1] // LANES):
            wbt_ref[row:row + LANES, :] = jnp.transpose(part[:, j * LANES:(j + 1) * LANES]).astype(BF16)
            row += LANES
    lane = lax.broadcasted_iota(I32, (x.shape[0], LANES), 1)
    wi_wide = jnp.where(lane < IDX_HEADS, x[:, W_IN_CUTS[7]:W_IN_CUTS[7] + LANES], 0.0)
    wbt_ref[row:row + WI_ROWS, :] = jnp.transpose(wi_wide)[:WI_ROWS, :].astype(BF16)
    for j, part in enumerate((cb, cc, cx)):
        wc_ref[:, j * C_WIDTH:(j + 1) * C_WIDTH] = part.astype(BF16)
    wg_ref[...] = gates.astype(BF16)


def _repack_w_in(w_in):
    depth, d, d_in = w_in.shape
    rows = lambda n: pl.BlockSpec((None, LANES, n), lambda l, i: (l, i, 0))
    outs = [2 * A_WIDTH, B_WIDTH + LANES, None, 3 * C_WIDTH, N_BRANCH * D_MODEL]
    shapes = [jax.ShapeDtypeStruct((depth, W_BT_ROWS, d) if n is None else (depth, d, n), BF16) for n in outs]
    specs = [pl.BlockSpec((None, W_BT_ROWS, LANES), lambda l, i: (l, 0, i)) if n is None else rows(n)
             for n in outs]
    return pl.pallas_call(
        _repack_kernel,
        grid=(depth, d // LANES),
        in_specs=[rows(d_in)],
        out_specs=specs,
        out_shape=shapes,
        compiler_params=_params(2),
        name="repack_w_in",
    )(w_in)


def _token_mixers(x2, g, w_parts, ws, bs, conv_w, wpa, wpb, wpc, w_out, tabs, bsz, seq):
    w_a, w_b, w_bt, w_c, w_g = w_parts
    n_sel = min(TOPK_MAX, seq // 4)
    k, ki, qt, vt_aug, qit, wit = _proj_b(x2, g, w_b, w_bt, *tabs)
    ybt = _dsa(qit, wit, ki, qt, k, vt_aug, bsz, seq, n_sel)
    ya = _proj_a(x2, g, w_a, ws.astype(BF16), bs.T)
    yc = _proj_c(x2, g, w_c, conv_w, seq)
    bf = lambda w: w.astype(BF16)
    return _merge(x2, g, w_g, ya, ybt, yc, bf(wpa), bf(wpb), bf(wpc), bf(w_out))


def _hier_moe(x2, g, rg_w, rg_b, re_w, re_b, w_gate, w_up, w_down, g_final, layer):
    t = x2.shape[0]
    pad_c = LANES - N_GROUPS - N_EXPERTS
    w_r = jnp.pad(jnp.concatenate([rg_w, re_w], axis=1), ((0, 0), (0, pad_c)))
    b_r = jnp.pad(jnp.concatenate([rg_b, re_b]), (0, pad_c))[None, :]
    h, eid, wt = _router(x2, g, w_r, b_r)
    rank, counts = _rank(eid)
    cnt = counts[0, :N_EXPERTS].astype(I32)
    padded = ((cnt + EXP_BLOCK - 1) // EXP_BLOCK) * EXP_BLOCK
    pend = jnp.cumsum(padded)
    pstart = pend - padded
    n_blocks = (2 * t) // EXP_BLOCK + N_EXPERTS
    n_rows = n_blocks * EXP_BLOCK
    block_row = jnp.arange(n_blocks, dtype=I32) * EXP_BLOCK
    block_exp = jnp.minimum(jnp.sum((pend[None, :] <= block_row[:, None]).astype(I32), axis=1),
                            N_EXPERTS - 1)
    n_used = (pend[-1:] // EXP_BLOCK).astype(I32)
    start_row = jnp.pad(pstart.astype(F32), (0, LANES - N_EXPERTS))[None, :]
    dests = _dest(eid, rank, start_row)
    xs = _scatter_rows(dests, h, n_rows)
    y = _experts(block_exp, n_used, xs, w_gate, w_up, w_down, layer)
    return _combine(dests, x2, wt, g_final, y, layer == w_gate.shape[0] - 1)


def kernel(x, positions, norm_mix_g, norm_ffn_g, norm_final_g, w_in, gmlp_ws, gmlp_b, conv_w, w_proj_a,
           w_proj_b, w_proj_c, w_out, router_group_w, router_group_b, router_expert_w, router_expert_b,
           expert_w_gate, expert_w_up, expert_w_down):
    bsz, seq, d = x.shape
    depth = w_in.shape[0]
    x2 = x.reshape(bsz * seq, d)
    pos = positions.astype(F32).reshape(bsz * seq)
    tabs = (_rope_tables(pos[:, None], B_HEAD_DIM, B_ROT), _rope_tables(pos[:, None], IDX_DIM, IDX_ROT),
            _rope_tables_t(pos[None, :], B_ROT), _rope_tables_t(pos[None, :], IDX_ROT))
    g_final = norm_final_g[None, :]
    w_parts = _repack_w_in(w_in)
    for l in range(depth):
        x2 = _token_mixers(x2, norm_mix_g[l][None, :], [w[l] for w in w_parts], gmlp_ws[l], gmlp_b[l], conv_w[l],
                           w_proj_a[l], w_proj_b[l], w_proj_c[l], w_out[l], tabs, bsz, seq)
        x2 = _hier_moe(x2, norm_ffn_g[l][None, :], router_group_w[l], router_group_b[l],
                       router_expert_w[l], router_expert_b[l], expert_w_gate, expert_w_up,
                       expert_w_down, g_final, l)
    return x2.reshape(bsz, seq, d)
```

```python
import functools

import numpy as np
import jax
import jax.numpy as jnp
from jax import lax
from jax.experimental import pallas as pl
from jax.experimental.pallas import tpu as pltpu

D_MODEL = 1024
CHUNK = 128
A_GROUPS = 4
A_GROUP_DIM = 128
A_WIDTH = A_GROUPS * A_GROUP_DIM
B_HEADS = 4
B_HEAD_DIM = 128
B_WIDTH = B_HEADS * B_HEAD_DIM
B_ROT = B_HEAD_DIM // 4
IDX_HEADS = 4
IDX_DIM = 64
IDX_ROT = IDX_DIM // 4
TOPK_MAX = 256
C_WIDTH = 512
CONV_WIDTH = 3
N_BRANCH = 3
N_GROUPS = 4
EXP_PER_GROUP = 8
N_EXPERTS = N_GROUPS * EXP_PER_GROUP
D_FF_EXPERT = 512
ROPE_THETA = 500000.0
EPS = 1e-6
NEG = -1e30

LANES = 128
SUBLANES = 8
VMEM_LIMIT = 56 * 1024 * 1024

TM_PROJ = 512
TM_MERGE = 256
TM_ROUTE = 512
TM_RANK = 256
TM_MOVE = 512
ISSUE_UNROLL = 4
EXP_BLOCK = 512
DSA_TQ = 128
DSA_KC = 256
DSA_SKC = 512
DSA_SUB = 256
DSA_RB = 512
V_ROWS = B_HEAD_DIM + 16
LOG2E = 1.4426950408889634

F32 = jnp.float32
BF16 = jnp.bfloat16
I32 = jnp.int32
INT_MIN = -2 ** 31


def _monotone_key_of(value):
    bits = int(np.array(value, np.float32).view(np.int32))
    return bits ^ ((bits >> 31) & 0x7FFFFFFF)


NEG_KEY = _monotone_key_of(NEG)


def _params(n_axes=1, semantics=None):
    return pltpu.CompilerParams(
        dimension_semantics=semantics or ("arbitrary",) * n_axes,
        vmem_limit_bytes=VMEM_LIMIT)


def _rms(x, g):
    ms = jnp.mean(x * x, axis=-1, keepdims=True)
    return x * lax.rsqrt(ms + EPS) * g


def _full(shape):
    nd = len(shape)
    return pl.BlockSpec(shape, lambda *_: (0,) * nd)


def _rope_table_kernel(pos_ref, invf_ref, mrot_ref, ma_ref, mb_ref, c_ref, sa_ref, sb_ref):
    ang = pos_ref[...] * invf_ref[...]
    c = jnp.cos(ang)
    s = jnp.sin(ang)
    c_ref[...] = jnp.where(mrot_ref[...] > 0, c, 1.0)
    sa_ref[...] = jnp.where(ma_ref[...] > 0, -s, 0.0)
    sb_ref[...] = jnp.where(mb_ref[...] > 0, s, 0.0)


def _rope_tables(pos_col, head_dim, rot):
    t = pos_col.shape[0]
    half = rot // 2
    inv_freq = jnp.float32(ROPE_THETA) ** (-jnp.arange(half, dtype=F32) * 2.0 / rot)
    lane = np.arange(LANES) % head_dim
    mrot = lane < rot
    ma = lane < half
    mb = (lane >= half) & (lane < rot)
    invf = jnp.where(jnp.asarray(mrot), inv_freq[np.where(mrot, lane % half, 0)], 0.0)[None, :]
    row = lambda m: jnp.asarray(m.astype(np.float32))[None, :]
    tm = min(1024, t)
    spec_row = _full((1, LANES))
    out_spec = pl.BlockSpec((tm, LANES), lambda i: (i, 0))
    return pl.pallas_call(
        _rope_table_kernel,
        grid=(t // tm,),
        in_specs=[pl.BlockSpec((tm, 1), lambda i: (i, 0)), spec_row, spec_row, spec_row, spec_row],
        out_specs=[out_spec] * 3,
        out_shape=[jax.ShapeDtypeStruct((t, LANES), F32)] * 3,
        compiler_params=_params(),
        name="rope_tables",
    )(pos_col, invf, row(mrot), row(ma), row(mb))


def _rope_table_t_kernel(pos_ref, invf_ref, c_ref, s_ref):
    ang = invf_ref[...] * pos_ref[...]
    c_ref[...] = jnp.cos(ang)
    s_ref[...] = jnp.sin(ang)


def _rope_tables_t(pos_row, rot):
    t = pos_row.shape[1]
    half = rot // 2
    inv_freq = jnp.float32(ROPE_THETA) ** (-jnp.arange(half, dtype=F32) * 2.0 / rot)
    tm = min(2048, t)
    out_spec = pl.BlockSpec((half, tm), lambda i: (0, i))
    return pl.pallas_call(
        _rope_table_t_kernel,
        grid=(t // tm,),
        in_specs=[pl.BlockSpec((1, tm), lambda i: (0, i)), _full((half, 1))],
        out_specs=[out_spec] * 2,
        out_shape=[jax.ShapeDtypeStruct((half, t), F32)] * 2,
        compiler_params=_params(),
        name="rope_tables_t",
    )(pos_row, inv_freq[:, None])


def _rope_t(x, c, s):
    half = c.shape[0]
    x1, x2 = x[:half, :], x[half:2 * half, :]
    return jnp.concatenate([x1 * c - x2 * s, x1 * s + x2 * c, x[2 * half:, :]], axis=0)


def _rope(x, c, sa, sb, half):
    parts = []
    for j in range(x.shape[1] // LANES):
        xj = x[:, j * LANES:(j + 1) * LANES]
        parts.append(xj * c + pltpu.roll(xj, LANES - half, 1) * sa + pltpu.roll(xj, half, 1) * sb)
    return parts


WI_ROWS = 16


def _proj_b_kernel(x_ref, g_ref, w_ref, wt_ref, cb_ref, sab_ref, sbb_ref, ci_ref, sai_ref, sbi_ref,
                   cbt_ref, sbt_ref, cit_ref, sit_ref, k_ref, ki_ref, qt_ref, vt_ref, qit_ref, wit_ref):
    h = _rms(x_ref[...], g_ref[...]).astype(BF16)
    tm = h.shape[0]
    p = jnp.dot(h, w_ref[...], preferred_element_type=F32)
    pt = lax.dot_general(wt_ref[...], h, (((1,), (1,)), ((), ())), preferred_element_type=F32)
    tb = (cb_ref[...], sab_ref[...], sbb_ref[...])
    ti = (ci_ref[...], sai_ref[...], sbi_ref[...])
    for j, kj in enumerate(_rope(p[:, :B_WIDTH], *tb, B_ROT // 2)):
        k_ref[:, j * LANES:(j + 1) * LANES] = kj.astype(BF16)
    ki_ref[...] = _rope(p[:, B_WIDTH:B_WIDTH + LANES], *ti, IDX_ROT // 2)[0].astype(BF16)
    cbt, sbt = cbt_ref[...], sbt_ref[...]
    for hh in range(B_HEADS):
        hs = slice(hh * B_HEAD_DIM, (hh + 1) * B_HEAD_DIM)
        qt_ref[hs, :] = (_rope_t(pt[hs, :], cbt, sbt) * (B_HEAD_DIM ** -0.5 * LOG2E)).astype(BF16)
    o = B_WIDTH
    for hh in range(B_HEADS):
        vt_ref[hh * V_ROWS:hh * V_ROWS + B_HEAD_DIM, :] = (
            pt[o + hh * B_HEAD_DIM:o + (hh + 1) * B_HEAD_DIM, :].astype(BF16))
        vt_ref[hh * V_ROWS + B_HEAD_DIM:(hh + 1) * V_ROWS, :] = jnp.ones((V_ROWS - B_HEAD_DIM, tm), BF16)
    o += B_WIDTH
    cit, sit = cit_ref[...], sit_ref[...]
    for hh in range(IDX_HEADS):
        hs = slice(hh * IDX_DIM, (hh + 1) * IDX_DIM)
        qit_ref[hs, :] = _rope_t(pt[o + hh * IDX_DIM:o + (hh + 1) * IDX_DIM, :], cit, sit).astype(BF16)
    o += IDX_HEADS * IDX_DIM
    wit_ref[...] = pt[o:o + SUBLANES, :] * ((IDX_HEADS ** -0.5) * (IDX_DIM ** -0.5))


def _proj_b(x2, g, w, wt, tabs_b, tabs_i, tabs_bt, tabs_it):
    t = x2.shape[0]
    tm = min(TM_PROJ, t)
    tok = lambda n: pl.BlockSpec((tm, n), lambda i: (i, 0))
    feat = lambda n: pl.BlockSpec((n, tm), lambda i: (0, i))
    outs_tok = [(B_WIDTH, BF16), (LANES, BF16)]
    outs_feat = [(B_WIDTH, BF16), (B_HEADS * V_ROWS, BF16), (IDX_HEADS * IDX_DIM, BF16), (SUBLANES, F32)]
    return pl.pallas_call(
        _proj_b_kernel,
        grid=(t // tm,),
        in_specs=[tok(D_MODEL), _full((1, D_MODEL)), _full(w.shape), _full(wt.shape)] + [tok(LANES)] * 6
                 + [feat(B_ROT // 2)] * 2 + [feat(IDX_ROT // 2)] * 2,
        out_specs=[tok(n) for n, _ in outs_tok] + [feat(n) for n, _ in outs_feat],
        out_shape=[jax.ShapeDtypeStruct((t, n), d) for n, d in outs_tok]
                  + [jax.ShapeDtypeStruct((n, t), d) for n, d in outs_feat],
        compiler_params=_params(),
        name="proj_b",
    )(x2, g, w, wt, *tabs_b, *tabs_i, *tabs_bt, *tabs_it)


def _ordered_key(score):
    bits = lax.bitcast_convert_type(score, I32)
    key = bits ^ ((bits >> 31) & 0x7FFFFFFF)
    return jnp.where(key == -1, 0, key)


def _score_of_key(key):
    return lax.bitcast_convert_type(key ^ ((key >> 31) & 0x7FFFFFFF), F32)


def _ce_desc(vals, i, j):
    a, b = vals[i], vals[j]
    vals[i] = jnp.maximum(a, b)
    vals[j] = jnp.minimum(a, b)


def _sort_desc(vals):
    n = len(vals)
    p = 1
    while p < n:
        k = p
        while k >= 1:
            for j in range(k % p, n - k, 2 * k):
                for i in range(min(k, n - j - k)):
                    if (i + j) // (2 * p) == (i + j + k) // (2 * p):
                        _ce_desc(vals, i + j, i + j + k)
            k //= 2
        p *= 2


def _bitonic_merge_desc(vals):
    n = len(vals)
    j = n // 2
    while j >= 1:
        for i in range(n):
            l = i ^ j
            if l > i:
                _ce_desc(vals, i, l)
        j //= 2


def _dsa_kernel(qit_ref, wit_ref, ki_ref, qt_ref, k_ref, vt_ref, o_ref, sc_ref, top_ref, topk_ref, acc_ref,
                bias0_ref, bias1_ref, lg0_ref, lg1_ref,
                *, seq, n_sel, tq, kc, skc, sub, rb):
    i = pl.program_id(1)
    q0 = i * tq
    n_full = q0 // skc
    n_rows = (n_full + 1) * skc
    n_pad = (seq - 1 - q0) - lax.broadcasted_iota(I32, (1, tq), 1)
    n_chunk = (q0 + tq + kc - 1) // kc
    wit = wit_ref[...]

    n_top = skc // SUBLANES
    top_ref[...] = jnp.full((skc, tq), -jnp.inf, F32)

    def score_batch(c, masked):
        vals = []
        for s in range(skc // sub):
            r0 = pl.multiple_of(c * skc + s * sub, sub)
            kic = ki_ref[pl.ds(r0, sub), :IDX_DIM]
            sc = None
            for h in range(IDX_HEADS):
                a = jnp.dot(kic, qit_ref[h * IDX_DIM:(h + 1) * IDX_DIM, :], preferred_element_type=F32)
                term = jnp.maximum(a, 0.0) * wit[h:h + 1, :]
                sc = term if sc is None else sc + term
            if masked:
                qpos = q0 + lax.broadcasted_iota(I32, (sub, tq), 1)
                krow = r0 + lax.broadcasted_iota(I32, (sub, tq), 0)
                sc = jnp.where(krow <= qpos, sc, -jnp.inf)
            sc_ref[pl.ds(r0, sub), :] = sc
            vals += [sc[SUBLANES * i:SUBLANES * (i + 1), :] for i in range(sub // SUBLANES)]
        _sort_desc(vals)
        return vals

    def keep_top(batches):
        top = [top_ref[SUBLANES * i:SUBLANES * (i + 1), :] for i in range(n_top)]
        for vals in batches:
            top = [jnp.maximum(top[i], vals[n_top - 1 - i]) for i in range(n_top)]
            _bitonic_merge_desc(top)
        for i in range(n_top):
            top_ref[SUBLANES * i:SUBLANES * (i + 1), :] = top[i]

    def score_pair(j, _):
        keep_top([score_batch(2 * j, False), score_batch(2 * j + 1, False)])
        return 0

    lax.fori_loop(0, n_full // 2, score_pair, 0)

    @pl.when(n_full % 2 == 1)
    def _():
        keep_top([score_batch(n_full - 1, False)])

    keep_top([score_batch(n_full, True)])

    def search(count):
        def bit_body(b, carry):
            ans_u, = carry
            cand_u = ans_u | jnp.left_shift(jnp.int32(1), 31 - b)
            ok = count(cand_u ^ INT_MIN, False) >= n_sel
            return (jnp.where(ok, cand_u, ans_u),)

        ans_u, = lax.fori_loop(0, 32, bit_body, (jnp.zeros((1, tq), I32),))
        thr = ans_u ^ INT_MIN
        return thr, (n_sel - count(thr, True)).astype(F32)

    def pads(cand, strict):
        return jnp.where((NEG_KEY > cand) if strict else (NEG_KEY >= cand), n_pad, 0)

    topk_ref[...] = _ordered_key(top_ref[...])

    def count_top(cand, strict):
        t = topk_ref[...]
        hit = (t > cand) if strict else (t >= cand)
        return jnp.sum(jnp.where(hit, 1.0, 0.0), axis=0, keepdims=True).astype(I32) + pads(cand, strict)

    def search_top():
        def digit_body(b, carry):
            ans_u, = carry
            shift = 30 - 2 * b
            for d in (1, 2, 3):
                cand_u = carry[0] | jnp.left_shift(jnp.int32(d), shift)
                ans_u = jnp.where(count_top(cand_u ^ INT_MIN, False) >= n_sel, cand_u, ans_u)
            return (ans_u,)

        ans_u, = lax.fori_loop(0, 16, digit_body, (jnp.zeros((1, tq), I32),))
        thr = ans_u ^ INT_MIN
        return thr, (n_sel - count_top(thr, True)).astype(F32)

    def count_all(cand, strict):
        acc_rows = 4 * SUBLANES
        cand_f = _score_of_key(cand)

        def body(r, acc):
            r0 = pl.multiple_of(r * rb, rb)
            blk = sc_ref[pl.ds(r0, rb), :]
            for j in range(rb // acc_rows):
                part = blk[j * acc_rows:(j + 1) * acc_rows, :]
                hit = (part > cand_f) if strict else (part >= cand_f)
                acc = acc + jnp.where(hit, 1, 0)
            return acc

        acc = lax.fori_loop(0, n_rows // rb, body, jnp.zeros((acc_rows, tq), I32))
        return jnp.sum(acc.astype(F32), axis=0, keepdims=True).astype(I32) + pads(cand, strict)

    thr, need = search_top()
    last = topk_ref[skc - SUBLANES:skc, :]
    overflow = jnp.max(jnp.where(last > thr, 1.0, 0.0))
    thr, need = lax.cond(overflow > 0.0, lambda: search(count_all), lambda: (thr, need))
    thr_f = _score_of_key(thr)

    acc_ref[...] = jnp.zeros_like(acc_ref)
    bias_refs = (bias0_ref, bias1_ref)
    lg_refs = (lg0_ref, lg1_ref)
    hk = kc // 2
    lower = (lax.broadcasted_iota(I32, (hk, hk), 1) < lax.broadcasted_iota(I32, (hk, hk), 0))
    lower = jnp.where(lower, 1.0, 0.0).astype(BF16)

    def produce(c, slot, tie_base):
        r0 = pl.multiple_of(c * kc, kc)
        sc = sc_ref[pl.ds(r0, kc), :]
        tie = sc == thr_f
        tie_f = jnp.where(tie, 1.0, 0.0)
        tie_b = tie_f.astype(BF16)
        b0 = jnp.dot(lower, tie_b[:hk, :], preferred_element_type=F32)
        n0 = b0[hk - 1:hk, :] + tie_f[hk - 1:hk, :]
        b1 = jnp.dot(lower, tie_b[hk:, :], preferred_element_type=F32) + n0
        before = jnp.concatenate([b0, b1], axis=0)
        sel = (sc > thr_f) | (tie & (before < need - tie_base))
        bias_refs[slot][...] = jnp.where(sel, 0.0, NEG)
        for h in range(B_HEADS):
            hs = slice(h * B_HEAD_DIM, (h + 1) * B_HEAD_DIM)
            lg_refs[slot][h] = jnp.dot(k_ref[pl.ds(r0, kc), hs], qt_ref[hs, :], preferred_element_type=F32)
        return tie_base + b1[hk - 1:hk, :] + tie_f[kc - 1:kc, :]

    def consume(c, slot, ms):
        r0 = pl.multiple_of(c * kc, kc)
        bias = bias_refs[slot][...]
        new_ms = []
        for h in range(B_HEADS):
            vs = slice(h * V_ROWS, (h + 1) * V_ROWS)
            lg = lg_refs[slot][h] + bias
            m_new = jnp.maximum(ms[h], jnp.max(lg, axis=0, keepdims=True))
            alpha = jnp.exp2(ms[h] - m_new)
            p = jnp.exp2(lg - m_new).astype(BF16)
            pv = jnp.dot(vt_ref[vs, pl.ds(r0, kc)], p, preferred_element_type=F32)
            acc_ref[vs, :] = acc_ref[vs, :] * alpha + pv
            new_ms.append(m_new)
        return tuple(new_ms)

    def pair_body(j, carry):
        ms, tie_base = carry
        c = 2 * j
        tie_base = produce(c + 1, 1, tie_base)
        ms = consume(c, 0, ms)
        tie_base = produce(c + 2, 0, tie_base)
        ms = consume(c + 1, 1, ms)
        return ms, tie_base

    row = lambda v: jnp.full((1, tq), v, F32)
    ms0 = tuple(row(NEG) for _ in range(B_HEADS))
    n_pairs = (n_chunk - 1) // 2
    ms, tie_base = lax.fori_loop(0, n_pairs, pair_body, (ms0, produce(0, 0, row(0.0))))
    c0 = 2 * n_pairs

    @pl.when(c0 == n_chunk - 1)
    def _():
        consume(c0, 0, ms)

    @pl.when(c0 != n_chunk - 1)
    def _():
        produce(c0 + 1, 1, tie_base)
        consume(c0 + 1, 1, consume(c0, 0, ms))

    for h in range(B_HEADS):
        a = acc_ref[h * V_ROWS:(h + 1) * V_ROWS, :]
        o_ref[h * B_HEAD_DIM:(h + 1) * B_HEAD_DIM, :] = (
            a[:B_HEAD_DIM, :] / a[B_HEAD_DIM:B_HEAD_DIM + 1, :]).astype(o_ref.dtype)


def _dsa(qit, wit, ki, qt, k, vt_aug, bsz, seq, n_sel):
    tq = min(DSA_TQ, seq)
    kc = min(DSA_KC, seq)
    skc = min(DSA_SKC, seq)
    nq = seq // tq
    once = pl.Buffered(1)
    kern = functools.partial(_dsa_kernel, seq=seq, n_sel=n_sel, tq=tq, kc=kc, skc=skc,
                             sub=min(DSA_SUB, skc), rb=min(DSA_RB, skc))
    q_tile = lambda rows: pl.BlockSpec((rows, tq), lambda b, i: (0, b * nq + i))
    return pl.pallas_call(
        kern,
        grid=(bsz, nq),
        in_specs=[
            q_tile(IDX_HEADS * IDX_DIM),
            q_tile(SUBLANES),
            pl.BlockSpec((seq, LANES), lambda b, i: (b, 0), pipeline_mode=once),
            q_tile(B_WIDTH),
            pl.BlockSpec((seq, B_WIDTH), lambda b, i: (b, 0), pipeline_mode=once),
            pl.BlockSpec((B_HEADS * V_ROWS, seq), lambda b, i: (0, b), pipeline_mode=once),
        ],
        out_specs=q_tile(B_WIDTH),
        out_shape=jax.ShapeDtypeStruct((B_WIDTH, bsz * seq), BF16),
        scratch_shapes=[pltpu.VMEM((seq, tq), F32), pltpu.VMEM((skc, tq), F32), pltpu.VMEM((skc, tq), I32),
                        pltpu.VMEM((B_HEADS * V_ROWS, tq), F32), pltpu.VMEM((kc, tq), F32),
                        pltpu.VMEM((kc, tq), F32), pltpu.VMEM((B_HEADS, kc, tq), F32),
                        pltpu.VMEM((B_HEADS, kc, tq), F32)],
        compiler_params=_params(2),
        name="dsa",
    )(qit, wit, ki, qt, k, vt_aug)


def _proj_a_kernel(x_ref, g_ref, w_ref, ws_ref, bs_ref, ya_ref, p_ref, *, tm):
    h = _rms(x_ref[...], g_ref[...]).astype(BF16)
    p_ref[...] = jnp.dot(h, w_ref[...], preferred_element_type=F32)
    causal = (lax.broadcasted_iota(I32, (CHUNK, CHUNK), 1) <= lax.broadcasted_iota(I32, (CHUNK, CHUNK), 0))
    wsm = [jnp.where(causal, ws_ref[gidx], 0.0).astype(BF16) for gidx in range(A_GROUPS)]
    bs = bs_ref[...]
    for n in range(tm // CHUNK):
        rs = slice(n * CHUNK, (n + 1) * CHUNK)
        u = jax.nn.gelu(p_ref[rs, :A_WIDTH])
        v = jax.nn.gelu(p_ref[rs, A_WIDTH:])
        mu = jnp.mean(v, axis=-1, keepdims=True)
        var = jnp.mean(jnp.square(v - mu), axis=-1, keepdims=True)
        v = ((v - mu) * lax.rsqrt(var + EPS)).astype(BF16)
        for gidx in range(A_GROUPS):
            cs = slice(gidx * A_GROUP_DIM, (gidx + 1) * A_GROUP_DIM)
            vm = jnp.dot(wsm[gidx], v[:, cs], preferred_element_type=F32) + bs[:, gidx:gidx + 1]
            ya_ref[rs, cs] = (u[:, cs] * vm).astype(ya_ref.dtype)


def _proj_a(x2, g, w, ws, bs_t):
    t = x2.shape[0]
    tm = min(TM_PROJ, t)
    return pl.pallas_call(
        functools.partial(_proj_a_kernel, tm=tm),
        grid=(t // tm,),
        in_specs=[pl.BlockSpec((tm, D_MODEL), lambda i: (i, 0)), _full((1, D_MODEL)),
                  _full((D_MODEL, 2 * A_WIDTH)), _full((A_GROUPS, CHUNK, CHUNK)), _full((CHUNK, A_GROUPS))],
        out_specs=pl.BlockSpec((tm, A_WIDTH), lambda i: (i, 0)),
        out_shape=jax.ShapeDtypeStruct((t, A_WIDTH), BF16),
        scratch_shapes=[pltpu.VMEM((tm, 2 * A_WIDTH), F32)],
        compiler_params=_params(),
        name="proj_a",
    )(x2, g, w, ws, bs_t)


def _proj_c_kernel(x_ref, xh_ref, g_ref, w_ref, cw_ref, yc_ref, *, tm, tiles_per_seq):
    g = g_ref[...]
    w = w_ref[...]
    p = jnp.dot(_rms(x_ref[...], g).astype(BF16), w, preferred_element_type=F32)
    ph = jnp.dot(_rms(xh_ref[...], g).astype(BF16), w[:, C_WIDTH:], preferred_element_type=F32)
    first = (pl.program_id(0) % tiles_per_seq) == 0
    zh = ph[:, :C_WIDTH] * ph[:, C_WIDTH:] * jnp.where(first, 0.0, 1.0)
    z = p[:, C_WIDTH:2 * C_WIDTH] * p[:, 2 * C_WIDTH:]
    row = lax.broadcasted_iota(I32, (tm, C_WIDTH), 0)
    z1 = jnp.where(row == 0, zh[SUBLANES - 1:SUBLANES, :], pltpu.roll(z, 1, 0))
    z2 = jnp.where(row == 0, zh[SUBLANES - 2:SUBLANES - 1, :],
                   jnp.where(row == 1, zh[SUBLANES - 1:SUBLANES, :], pltpu.roll(z, 2, 0)))
    cw = cw_ref[...]
    y = cw[0:1, :] * z2 + cw[1:2, :] * z1 + cw[2:3, :] * z
    yc_ref[...] = (p[:, :C_WIDTH] * y).astype(yc_ref.dtype)


def _proj_c(x2, g, w, cw, seq):
    t = x2.shape[0]
    tm = min(TM_PROJ, seq)
    per8 = tm // SUBLANES
    return pl.pallas_call(
        functools.partial(_proj_c_kernel, tm=tm, tiles_per_seq=seq // tm),
        grid=(t // tm,),
        in_specs=[pl.BlockSpec((tm, D_MODEL), lambda i: (i, 0)),
                  pl.BlockSpec((SUBLANES, D_MODEL), lambda i: (jnp.maximum(i * per8 - 1, 0), 0)),
                  _full((1, D_MODEL)), _full((D_MODEL, 3 * C_WIDTH)), _full((CONV_WIDTH, C_WIDTH))],
        out_specs=pl.BlockSpec((tm, C_WIDTH), lambda i: (i, 0)),
        out_shape=jax.ShapeDtypeStruct((t, C_WIDTH), BF16),
        compiler_params=_params(),
        name="proj_c",
    )(x2, x2, g, w, cw)


def _merge_kernel(x_ref, g_ref, wg_ref, ya_ref, ybt_ref, yc_ref, wa_ref, wb_ref, wc_ref, wo_ref, o_ref):
    x = x_ref[...]
    h = _rms(x, g_ref[...]).astype(BF16)
    projected = (
        jnp.dot(ya_ref[...], wa_ref[...], preferred_element_type=F32),
        lax.dot_general(ybt_ref[...], wb_ref[...], (((0,), (0,)), ((), ())), preferred_element_type=F32),
        jnp.dot(yc_ref[...], wc_ref[...], preferred_element_type=F32),
    )
    merged = None
    for j, proj in enumerate(projected):
        gate = jnp.dot(h, wg_ref[:, j * D_MODEL:(j + 1) * D_MODEL], preferred_element_type=F32)
        term = jax.nn.sigmoid(gate) * proj
        merged = term if merged is None else merged + term
    o_ref[...] = x + jnp.dot(merged.astype(BF16), wo_ref[...], preferred_element_type=F32)


def _merge(x2, g, wg, ya, ybt, yc, wa, wb, wc, wo):
    t = x2.shape[0]
    tm = min(TM_MERGE, t)
    tok = lambda n: pl.BlockSpec((tm, n), lambda i: (i, 0))
    wproj = _full((A_WIDTH, D_MODEL))
    return pl.pallas_call(
        _merge_kernel,
        grid=(t // tm,),
        in_specs=[tok(D_MODEL), _full((1, D_MODEL)), _full((D_MODEL, N_BRANCH * D_MODEL)),
                  tok(A_WIDTH), pl.BlockSpec((B_WIDTH, tm), lambda i: (0, i)), tok(C_WIDTH),
                  wproj, wproj, wproj, _full((D_MODEL, D_MODEL))],
        out_specs=tok(D_MODEL),
        out_shape=jax.ShapeDtypeStruct((t, D_MODEL), F32),
        compiler_params=_params(),
        name="merge",
    )(x2, g, wg, ya, ybt, yc, wa, wb, wc, wo)


def _router_kernel(x_ref, g_ref, w_ref, b_ref, h_ref, eid_ref, wt_ref):
    h = _rms(x_ref[...], g_ref[...])
    h_ref[...] = h
    logits = jnp.dot(h, w_ref[...], preferred_element_type=F32, precision=lax.Precision.HIGHEST) + b_ref[...]
    lane = lax.broadcasted_iota(I32, logits.shape, 1)
    lane_f = lane.astype(F32)
    ninf = -jnp.inf

    def first_max(vals):
        vmax = jnp.max(vals, axis=1, keepdims=True)
        idx = jnp.min(jnp.where(vals == vmax, lane_f, float(LANES)), axis=1, keepdims=True)
        return vmax, idx

    gl = jnp.where(lane < N_GROUPS, logits, ninf)
    gmax, grp = first_max(gl)
    gsum = jnp.sum(jnp.where(lane < N_GROUPS, jnp.exp(logits - gmax), 0.0), axis=1, keepdims=True)
    gw = 1.0 / gsum
    e_lane = lane - N_GROUPS
    in_grp = (e_lane >= 0) & (e_lane < N_EXPERTS) & ((e_lane >> 3).astype(F32) == grp)
    el = jnp.where(in_grp, logits, ninf)
    v1, i1 = first_max(el)
    v2, i2 = first_max(jnp.where(lane_f == i1, ninf, el))
    e2 = jnp.exp(v2 - v1)
    den = 1.0 + e2
    w1 = (1.0 / den) * gw
    w2 = (e2 / den) * gw
    eid1 = (i1 - N_GROUPS).astype(I32)
    eid2 = (i2 - N_GROUPS).astype(I32)
    eid_ref[...] = jnp.where(lane == 0, eid1, jnp.where(lane == 1, eid2, 0))
    wt_ref[...] = jnp.where(lane == 0, w1, jnp.where(lane == 1, w2, 0.0))


def _router(x2, g, w, b):
    t = x2.shape[0]
    tm = min(TM_ROUTE, t)
    tok = lambda n: pl.BlockSpec((tm, n), lambda i: (i, 0))
    return pl.pallas_call(
        _router_kernel,
        grid=(t // tm,),
        in_specs=[tok(D_MODEL), _full((1, D_MODEL)), _full((D_MODEL, LANES)), _full((1, LANES))],
        out_specs=[tok(D_MODEL), tok(LANES), tok(LANES)],
        out_shape=[jax.ShapeDtypeStruct((t, D_MODEL), F32), jax.ShapeDtypeStruct((t, LANES), I32),
                   jax.ShapeDtypeStruct((t, LANES), F32)],
        compiler_params=_params(),
        name="router",
    )(x2, g, w, b)


def _rank_kernel(eid_ref, rank_ref, cnt_ref, *, tm):
    @pl.when(pl.program_id(0) == 0)
    def _():
        cnt_ref[...] = jnp.zeros_like(cnt_ref)

    eid = eid_ref[...]
    lane = lax.broadcasted_iota(I32, (tm, LANES), 1)
    o1 = lane == eid[:, 0:1]
    o2 = lane == eid[:, 1:2]
    both = jnp.where(o1 | o2, 1.0, 0.0)
    lower = lax.broadcasted_iota(I32, (tm, tm), 1) < lax.broadcasted_iota(I32, (tm, tm), 0)
    before = jnp.dot(jnp.where(lower, 1.0, 0.0).astype(BF16), both.astype(BF16),
                     preferred_element_type=F32) + cnt_ref[...]
    r1 = jnp.sum(jnp.where(o1, before, 0.0), axis=1, keepdims=True).astype(I32)
    r2 = jnp.sum(jnp.where(o2, before, 0.0), axis=1, keepdims=True).astype(I32)
    rank_ref[...] = jnp.where(lane == 0, r1, jnp.where(lane == 1, r2, 0))
    cnt_ref[...] = cnt_ref[...] + jnp.sum(both, axis=0, keepdims=True)


def _rank(eid):
    t = eid.shape[0]
    tm = min(TM_RANK, t)
    return pl.pallas_call(
        functools.partial(_rank_kernel, tm=tm),
        grid=(t // tm,),
        in_specs=[pl.BlockSpec((tm, LANES), lambda i: (i, 0))],
        out_specs=[pl.BlockSpec((tm, LANES), lambda i: (i, 0)), _full((1, LANES))],
        out_shape=[jax.ShapeDtypeStruct((t, LANES), I32), jax.ShapeDtypeStruct((1, LANES), F32)],
        compiler_params=_params(),
        name="rank",
    )(eid)


def _dest_kernel(eid_ref, rank_ref, start_ref, d1_ref, d2_ref):
    eid = eid_ref[...]
    lane = lax.broadcasted_iota(I32, eid.shape, 1)
    start = start_ref[...]
    s1 = jnp.sum(jnp.where(lane == eid[:, 0:1], start, 0.0), axis=1, keepdims=True).astype(I32)
    s2 = jnp.sum(jnp.where(lane == eid[:, 1:2], start, 0.0), axis=1, keepdims=True).astype(I32)
    dest = jnp.where(lane == 0, s1, jnp.where(lane == 1, s2, 0)) + rank_ref[...]
    dest_t = jnp.transpose(dest.astype(F32))[:SUBLANES, :].astype(I32)
    d1_ref[...] = dest_t[0:1, :]
    d2_ref[...] = dest_t[1:2, :]


def _dest(eid, rank, start_row):
    t = eid.shape[0]
    tm = min(TM_ROUTE, t)
    tok = pl.BlockSpec((tm, LANES), lambda i: (i, 0))
    return pl.pallas_call(
        _dest_kernel,
        grid=(t // tm,),
        in_specs=[tok, tok, _full((1, LANES))],
        out_specs=[pl.BlockSpec((1, tm), lambda i: (0, i))] * 2,
        out_shape=[jax.ShapeDtypeStruct((1, t), I32)] * 2,
        compiler_params=_params(),
        name="dest",
    )(eid, rank, start_row)


def _row_copy(src_ref, src_row, dst_ref, dst_row, sem):
    return pltpu.make_async_copy(src_ref.at[pl.ds(src_row, 1)], dst_ref.at[pl.ds(dst_row, 1)], sem)


def _rows_copy(src_ref, dst_ref, n, sem):
    return pltpu.make_async_copy(src_ref.at[pl.ds(0, n)], dst_ref.at[pl.ds(0, n)], sem)


def _scatter_kernel(d1_ref, d2_ref, h_ref, xs_in_ref, xs_ref, sem, *, tm):
    del xs_in_ref

    def issue(t, _):
        for p, d_ref in enumerate((d1_ref, d2_ref)):
            _row_copy(h_ref, t, xs_ref, d_ref[0, t], sem).start(priority=p)
        return 0

    lax.fori_loop(0, tm, issue, 0, unroll=ISSUE_UNROLL)
    for _ in range(2):
        _rows_copy(h_ref, xs_ref, tm, sem).wait()


def _scatter_rows(dests, h, n_rows):
    t = h.shape[0]
    tm = min(TM_MOVE, t)
    xs0 = jnp.zeros((n_rows, D_MODEL), F32)
    return pl.pallas_call(
        functools.partial(_scatter_kernel, tm=tm),
        grid=(t // tm,),
        in_specs=[pl.BlockSpec((1, tm), lambda i: (0, i), memory_space=pltpu.SMEM)] * 2
                 + [pl.BlockSpec((tm, D_MODEL), lambda i: (i, 0)), pl.BlockSpec(memory_space=pl.ANY)],
        out_specs=pl.BlockSpec(memory_space=pl.ANY),
        out_shape=jax.ShapeDtypeStruct((n_rows, D_MODEL), F32),
        scratch_shapes=[pltpu.SemaphoreType.DMA(())],
        input_output_aliases={3: 0},
        compiler_params=_params(),
        name="scatter_rows",
    )(*dests, h, xs0)


def _expert_kernel(be_ref, nused_ref, xs_ref, wg_ref, wu_ref, wd_ref, y_ref, wg_bf, wu_bf, wd_bf):
    b = pl.program_id(0)

    @pl.when((b == 0) | (be_ref[b] != be_ref[jnp.maximum(b - 1, 0)]))
    def _():
        wg_bf[...] = wg_ref[...].astype(BF16)
        wu_bf[...] = wu_ref[...].astype(BF16)
        wd_bf[...] = wd_ref[...].astype(BF16)

    @pl.when(b < nused_ref[0])
    def _():
        xb = xs_ref[...].astype(BF16)
        gate = jnp.dot(xb, wg_bf[...], preferred_element_type=F32)
        up = jnp.dot(xb, wu_bf[...], preferred_element_type=F32)
        hid = (jax.nn.silu(gate) * up).astype(BF16)
        y_ref[...] = jnp.dot(hid, wd_bf[...], preferred_element_type=F32)

    @pl.when(b >= nused_ref[0])
    def _():
        y_ref[...] = jnp.zeros_like(y_ref)


def _experts(block_exp, n_used, xs, wg, wu, wd, layer):
    n_rows = xs.shape[0]
    w_in_spec = pl.BlockSpec((None, None, D_MODEL, D_FF_EXPERT), lambda b, be, nu: (layer, be[b], 0, 0))
    grid_spec = pltpu.PrefetchScalarGridSpec(
        num_scalar_prefetch=2,
        grid=(n_rows // EXP_BLOCK,),
        in_specs=[pl.BlockSpec((EXP_BLOCK, D_MODEL), lambda b, be, nu: (b, 0)), w_in_spec, w_in_spec,
                  pl.BlockSpec((None, None, D_FF_EXPERT, D_MODEL), lambda b, be, nu: (layer, be[b], 0, 0))],
        out_specs=pl.BlockSpec((EXP_BLOCK, D_MODEL), lambda b, be, nu: (b, 0)),
        scratch_shapes=[pltpu.VMEM((D_MODEL, D_FF_EXPERT), BF16), pltpu.VMEM((D_MODEL, D_FF_EXPERT), BF16),
                        pltpu.VMEM((D_FF_EXPERT, D_MODEL), BF16)],
    )
    return pl.pallas_call(
        _expert_kernel,
        grid_spec=grid_spec,
        out_shape=jax.ShapeDtypeStruct((n_rows, D_MODEL), F32),
        compiler_params=_params(),
        name="experts",
    )(block_exp, n_used, xs, wg, wu, wd)


def _combine_kernel(d1_ref, d2_ref, d1_next_ref, d2_next_ref, x_ref, wt_ref, g_ref, y_ref, o_ref, buf_ref, sems,
                    *, tm, steps, final_norm):
    i = pl.program_id(0)

    def gather(idx_refs, slot):
        def issue(t, _):
            for j, idx_ref in enumerate(idx_refs):
                _row_copy(y_ref, idx_ref[0, t], buf_ref.at[slot], j * tm + t, sems.at[slot]).start(priority=j)
            return 0

        lax.fori_loop(0, tm, issue, 0, unroll=ISSUE_UNROLL)

    @pl.when(i == 0)
    def _():
        gather((d1_ref, d2_ref), 0)

    @pl.when(i + 1 < steps)
    def _():
        gather((d1_next_ref, d2_next_ref), (i + 1) % 2)

    slot = i % 2
    _rows_copy(y_ref, buf_ref.at[slot], 2 * tm, sems.at[slot]).wait()
    wt = wt_ref[...]
    rows = buf_ref[slot]
    out = x_ref[...] + (wt[:, 0:1] * rows[:tm, :] + wt[:, 1:2] * rows[tm:, :])
    if final_norm:
        out = _rms(out, g_ref[...])
    o_ref[...] = out


def _combine(dests, x2, wt, g_final, y, final_norm):
    t = x2.shape[0]
    tm = min(TM_MOVE, t)
    steps = t // tm
    tok = lambda n: pl.BlockSpec((tm, n), lambda i: (i, 0))
    idx = lambda f: pl.BlockSpec((1, tm), f, memory_space=pltpu.SMEM)
    return pl.pallas_call(
        functools.partial(_combine_kernel, tm=tm, steps=steps, final_norm=final_norm),
        grid=(steps,),
        in_specs=[idx(lambda i: (0, i))] * 2 + [idx(lambda i: (0, jnp.minimum(i + 1, steps - 1)))] * 2
                 + [tok(D_MODEL), tok(LANES), _full((1, D_MODEL)), pl.BlockSpec(memory_space=pl.ANY)],
        out_specs=tok(D_MODEL),
        out_shape=jax.ShapeDtypeStruct((t, D_MODEL), F32),
        scratch_shapes=[pltpu.VMEM((2, 2 * tm, D_MODEL), F32), pltpu.SemaphoreType.DMA((2,))],
        compiler_params=_params(),
        name="combine",
    )(*dests, *dests, x2, wt, g_final, y)


W_IN_SIZES = (A_WIDTH, A_WIDTH, B_WIDTH, B_WIDTH, B_WIDTH, IDX_HEADS * IDX_DIM, IDX_DIM, IDX_HEADS,
              C_WIDTH, C_WIDTH, C_WIDTH, N_BRANCH * D_MODEL)
W_IN_CUTS = tuple(int(c) for c in np.cumsum((0,) + W_IN_SIZES))
W_BT_ROWS = 2 * B_WIDTH + IDX_HEADS * IDX_DIM + WI_ROWS


def _repack_kernel(w_ref, wa_ref, wb_ref, wbt_ref, wc_ref, wg_ref):
    x = w_ref[...]
    u, va, q, k, vb, qi, ki, wi, cb, cc, cx, gates = [
        x[:, a:b] for a, b in zip(W_IN_CUTS[:-1], W_IN_CUTS[1:])]
    for j, part in enumerate((u, va)):
        wa_ref[:, j * A_WIDTH:(j + 1) * A_WIDTH] = part.astype(BF16)
    wb_ref[:, :B_WIDTH] = k.astype(BF16)
    ki_pad = jnp.concatenate([ki, jnp.zeros((ki.shape[0], LANES - IDX_DIM), F32)], axis=1)
    wb_ref[:, B_WIDTH:] = ki_pad.astype(BF16)
    row = 0
    for part in (q, vb, qi):
        for j in range(part.shape[1] // LANES):
            wbt_ref[row:row + LANES, :] = jnp.transpose(part[:, j * LANES:(j + 1) * LANES]).astype(BF16)
            row += LANES
    lane = lax.broadcasted_iota(I32, (x.shape[0], LANES), 1)
    wi_wide = jnp.where(lane < IDX_HEADS, x[:, W_IN_CUTS[7]:W_IN_CUTS[7] + LANES], 0.0)
    wbt_ref[row:row + WI_ROWS, :] = jnp.transpose(wi_wide)[:WI_ROWS, :].astype(BF16)
    for j, part in enumerate((cb, cc, cx)):
        wc_ref[:, j * C_WIDTH:(j + 1) * C_WIDTH] = part.astype(BF16)
    wg_ref[...] = gates.astype(BF16)


def _repack_w_in(w_in):
    depth, d, d_in = w_in.shape
    rows = lambda n: pl.BlockSpec((None, LANES, n), lambda l, i: (l, i, 0))
    outs = [2 * A_WIDTH, B_WIDTH + LANES, None, 3 * C_WIDTH, N_BRANCH * D_MODEL]
    shapes = [jax.ShapeDtypeStruct((depth, W_BT_ROWS, d) if n is None else (depth, d, n), BF16) for n in outs]
    specs = [pl.BlockSpec((None, W_BT_ROWS, LANES), lambda l, i: (l, 0, i)) if n is None else rows(n)
             for n in outs]
    return pl.pallas_call(
        _repack_kernel,
        grid=(depth, d // LANES),
        in_specs=[rows(d_in)],
        out_specs=specs,
        out_shape=shapes,
        compiler_params=_params(2),
        name="repack_w_in",
    )(w_in)


def _token_mixers(x2, g, w_parts, ws, bs, conv_w, wpa, wpb, wpc, w_out, tabs, bsz, seq):
    w_a, w_b, w_bt, w_c, w_g = w_parts
    n_sel = min(TOPK_MAX, seq // 4)
    k, ki, qt, vt_aug, qit, wit = _proj_b(x2, g, w_b, w_bt, *tabs)
    ybt = _dsa(qit, wit, ki, qt, k, vt_aug, bsz, seq, n_sel)
    ya = _proj_a(x2, g, w_a, ws.astype(BF16), bs.T)
    yc = _proj_c(x2, g, w_c, conv_w, seq)
    bf = lambda w: w.astype(BF16)
    return _merge(x2, g, w_g, ya, ybt, yc, bf(wpa), bf(wpb), bf(wpc), bf(w_out))


def _hier_moe(x2, g, rg_w, rg_b, re_w, re_b, w_gate, w_up, w_down, g_final, layer):
    t = x2.shape[0]
    pad_c = LANES - N_GROUPS - N_EXPERTS
    w_r = jnp.pad(jnp.concatenate([rg_w, re_w], axis=1), ((0, 0), (0, pad_c)))
    b_r = jnp.pad(jnp.concatenate([rg_b, re_b]), (0, pad_c))[None, :]
    h, eid, wt = _router(x2, g, w_r, b_r)
    rank, counts = _rank(eid)
    cnt = counts[0, :N_EXPERTS].astype(I32)
    padded = ((cnt + EXP_BLOCK - 1) // EXP_BLOCK) * EXP_BLOCK
    pend = jnp.cumsum(padded)
    pstart = pend - padded
    n_blocks = (2 * t) // EXP_BLOCK + N_EXPERTS
    n_rows = n_blocks * EXP_BLOCK
    block_row = jnp.arange(n_blocks, dtype=I32) * EXP_BLOCK
    block_exp = jnp.minimum(jnp.sum((pend[None, :] <= block_row[:, None]).astype(I32), axis=1),
                            N_EXPERTS - 1)
    n_used = (pend[-1:] // EXP_BLOCK).astype(I32)
    start_row = jnp.pad(pstart.astype(F32), (0, LANES - N_EXPERTS))[None, :]
    dests = _dest(eid, rank, start_row)
    xs = _scatter_rows(dests, h, n_rows)
    y = _experts(block_exp, n_used, xs, w_gate, w_up, w_down, layer)
    return _combine(dests, x2, wt, g_final, y, layer == w_gate.shape[0] - 1)


def kernel(x, positions, norm_mix_g, norm_ffn_g, norm_final_g, w_in, gmlp_ws, gmlp_b, conv_w, w_proj_a,
           w_proj_b, w_proj_c, w_out, router_group_w, router_group_b, router_expert_w, router_expert_b,
           expert_w_gate, expert_w_up, expert_w_down):
    bsz, seq, d = x.shape
    depth = w_in.shape[0]
    x2 = x.reshape(bsz * seq, d)
    pos = positions.astype(F32).reshape(bsz * seq)
    tabs = (_rope_tables(pos[:, None], B_HEAD_DIM, B_ROT), _rope_tables(pos[:, None], IDX_DIM, IDX_ROT),
            _rope_tables_t(pos[None, :], B_ROT), _rope_tables_t(pos[None, :], IDX_ROT))
    g_final = norm_final_g[None, :]
    w_parts = _repack_w_in(w_in)
    for l in range(depth):
        x2 = _token_mixers(x2, norm_mix_g[l][None, :], [w[l] for w in w_parts], gmlp_ws[l], gmlp_b[l], conv_w[l],
                           w_proj_a[l], w_proj_b[l], w_proj_c[l], w_out[l], tabs, bsz, seq)
        x2 = _hier_moe(x2, norm_ffn_g[l][None, :], router_group_w[l], router_group_b[l],
                       router_expert_w[l], router_expert_b[l], expert_w_gate, expert_w_up,
                       expert_w_down, g_final, l)
    return x2.reshape(bsz, seq, d)
```
